```python
import jax
import jax.numpy as jnp
from jax import lax
import numpy as np

D_MODEL = 1024
BATCH = 8
SEQ = 4096
DEPTH = 4

GRID_W = 64
CTX_LEN = 256
N_MOD = 6

POOL_GROUPS = 4
POOL_WINDOWS = (2, 4, 8, 16)
POOL_WIDTH = D_MODEL // 4
POOL_GROUP_DIM = POOL_WIDTH // POOL_GROUPS

NA_HEAD_DIM = 64
NA_HEADS = (D_MODEL - POOL_WIDTH) // NA_HEAD_DIM
NA_WIDTH = NA_HEADS * NA_HEAD_DIM
WIN_H = 8
WIN_W = 16
COL_BLOCK = 16
COL_BAND = COL_BLOCK + WIN_W
N_COL_BLOCKS = GRID_W // COL_BLOCK
AB_IN = POOL_WIDTH + 3 * NA_WIDTH
AB_OUT = POOL_WIDTH + NA_WIDTH

GLA_HEADS = 4
GLA_DK = D_MODEL // 2 // GLA_HEADS
GLA_DV = D_MODEL // GLA_HEADS
GATE_RANK = 16
GATE_NORM = 16.0
GLA_CHUNK = 64
GLA_QK = GLA_HEADS * GLA_DK
GLA_V = GLA_HEADS * GLA_DV
GLA_IN = 2 * GLA_QK + 2 * GLA_V + 2 * GATE_RANK
ROPE_BASE = 10000.0

N_EXPERTS = 32
TOP_K = 4
D_EXPERT = D_MODEL
SWIGLU_LIMIT = 7.0
SWIGLU_ALPHA = 1.702
MOE_BLOCK = 256

DEEPNORM_ALPHA = (2.0 * DEPTH) ** 0.25
DEEPNORM_BETA = (8.0 * DEPTH) ** -0.25
N_EVEN = (DEPTH + 1) // 2
N_ODD = DEPTH // 2
LN_EPS = 1e-5
RMS_EPS = 1e-6
NEG_INF = -1e30

kernel_name = 'hybrid_pool_natten_gla_moe_dit'


def layer_norm(x, g, b):
    xf = x.astype(jnp.float32)
    mu = jnp.mean(xf, axis=-1, keepdims=True)
    var = jnp.mean(jnp.square(xf - mu), axis=-1, keepdims=True)
    return ((xf - mu) * lax.rsqrt(var + LN_EPS) * g + b).astype(x.dtype)


def multiscale_pool(u):
    T = u.shape[1]
    uf = u.astype(jnp.float32)
    csum = jnp.concatenate([jnp.zeros_like(uf[:, :1]), jnp.cumsum(uf, axis=1)], axis=1)
    w = jnp.array(POOL_WINDOWS, dtype=jnp.int32)
    t = jnp.arange(T)[:, None]
    lo = jnp.clip(t - w // 2, 0, T)
    hi = jnp.clip(t - w // 2 + w, 0, T)
    g = jnp.arange(POOL_GROUPS)[None, :]
    mean = (csum[:, hi, g] - csum[:, lo, g]) / (hi - lo).astype(jnp.float32)[None, :, :, None]
    return (mean - uf).astype(u.dtype)


def axial_rope_tables(T, dim):
    t = jnp.arange(T)
    row = (t // GRID_W).astype(jnp.float32)
    col = (t % GRID_W).astype(jnp.float32)
    nf = dim // 4
    freqs = ROPE_BASE ** (-jnp.arange(nf, dtype=jnp.float32) / nf)
    ang = jnp.concatenate([row[:, None] * freqs, col[:, None] * freqs], axis=-1)
    return jnp.cos(ang), jnp.sin(ang)


def apply_rope(x, cos, sin):
    x1, x2 = jnp.split(x, 2, axis=-1)
    cs = cos[None, :, None, :]
    sn = sin[None, :, None, :]
    return jnp.concatenate([x1 * cs - x2 * sn, x1 * sn + x2 * cs], axis=-1).astype(x.dtype)


def neighborhood_attention(q, k, v, kc, vc, rpb):
    B, T, H, Dh = q.shape
    rows = T // GRID_W
    kh = min(WIN_H, rows)
    scale = Dh ** -0.5

    def grid(a):
        return a.reshape(B, rows, GRID_W, H, Dh).transpose(0, 3, 1, 2, 4)

    qg = grid(q).reshape(B, H, rows, N_COL_BLOCKS, COL_BLOCK, Dh)
    kg, vg = grid(k), grid(v)
    r = jnp.arange(rows)
    row_start = jnp.clip(r - kh // 2, 0, rows - kh)
    key_rows = row_start[:, None] + jnp.arange(kh)
    jb = jnp.arange(N_COL_BLOCKS)
    band_start = jnp.clip(jb * COL_BLOCK - WIN_W // 2, 0, GRID_W - COL_BAND)
    key_cols = band_start[:, None] + jnp.arange(COL_BAND)
    ridx = key_rows[:, :, None, None]
    cidx = key_cols[None, None, :, :]
    kb = kg[:, :, ridx, cidx]
    vb = vg[:, :, ridx, cidx]

    qcol = jb[:, None] * COL_BLOCK + jnp.arange(COL_BLOCK)
    win_start = jnp.clip(qcol - WIN_W // 2, 0, GRID_W - WIN_W)
    kcol = key_cols[:, None, :]
    col_ok = (kcol >= win_start[..., None]) & (kcol < win_start[..., None] + WIN_W)
    dcol = jnp.clip(kcol - qcol[..., None] + WIN_W - 1, 0, 2 * WIN_W - 2)
    drow = key_rows - r[:, None] + WIN_H - 1
    bias = rpb[:, drow[:, None, None, :, None], dcol[None, :, :, None, :]]

    s_band = jnp.einsum('bhrjqd,bhrkjcd->bhrjqkc', qg, kb).astype(jnp.float32) * scale + bias
    s_band = jnp.where(col_ok[:, :, None, :], s_band, NEG_INF)
    s_ctx = jnp.einsum('bhrjqd,bnhd->bhrjqn', qg, kc).astype(jnp.float32) * scale
    nb = kh * COL_BAND
    logits = jnp.concatenate([s_band.reshape(B, H, rows, N_COL_BLOCKS, COL_BLOCK, nb), s_ctx], axis=-1)
    p = jax.nn.softmax(logits, axis=-1).astype(v.dtype)
    p_band = p[..., :nb].reshape(B, H, rows, N_COL_BLOCKS, COL_BLOCK, kh, COL_BAND)
    p_ctx = p[..., nb:]
    o = jnp.einsum('bhrjqkc,bhrkjcd->bhrjqd', p_band, vb) + jnp.einsum('bhrjqn,bnhd->bhrjqd', p_ctx, vc)
    return o.reshape(B, H, rows, GRID_W, Dh).transpose(0, 2, 3, 1, 4).reshape(B, T, H * Dh)


def context_self_attention(q, k, v):
    B, N, H, Dh = q.shape
    s = jnp.einsum('bnhd,bmhd->bhnm', q, k).astype(jnp.float32) * Dh ** -0.5
    p = jax.nn.softmax(s, axis=-1).astype(v.dtype)
    return jnp.einsum('bhnm,bmhd->bnhd', p, v).reshape(B, N, H * Dh)


def pool_na_mixer(hx, hc, w_in, pool_w, pool_scale, rpb, w_out, with_ctx_out):
    def split_heads(p):
        u = p[..., :POOL_WIDTH]
        q, k, v = jnp.split(p[..., POOL_WIDTH:], 3, axis=-1)
        shp = p.shape[:2] + (NA_HEADS, NA_HEAD_DIM)
        return u, q.reshape(shp), k.reshape(shp), v.reshape(shp)

    def pool_branch(u):
        ug = u.reshape(u.shape[:2] + (POOL_GROUPS, POOL_GROUP_DIM))
        y = jnp.einsum('btgd,gde->btge', multiscale_pool(ug), pool_w)
        return y.reshape(u.shape) * pool_scale

    ux, qx, kx, vx = split_heads(hx @ w_in)
    uc, qc, kc, vc = split_heads(hc @ w_in)
    ax = neighborhood_attention(qx, kx, vx, kc, vc, rpb)
    ox = jnp.concatenate([pool_branch(ux), ax], axis=-1) @ w_out
    if not with_ctx_out:
        return ox, None
    ac = context_self_attention(qc, kc, vc)
    oc = jnp.concatenate([pool_branch(uc), ac], axis=-1) @ w_out
    return ox, oc


def gla_chunked(q, k, v, log_a, s0):
    B, H, T, dk = q.shape
    dv = v.shape[-1]
    n = T // GLA_CHUNK

    def chunks(a):
        return a.reshape(B, H, n, GLA_CHUNK, a.shape[-1]).transpose(2, 0, 1, 3, 4)

    lower = jnp.tril(jnp.ones((GLA_CHUNK, GLA_CHUNK), dtype=bool))[:, :, None]

    def step(S, inp):
        qc, kc, vc, gc = inp
        b = jnp.cumsum(gc.astype(jnp.float32), axis=2)
        b_last = b[:, :, -1:, :]
        o_inter = jnp.einsum('bhtd,bhde->bhte', qc * jnp.exp(b), S)
        diff = b[:, :, :, None, :] - b[:, :, None, :, :]
        decay = jnp.where(lower, jnp.exp(jnp.where(lower, diff, 0.0)), 0.0)
        att = jnp.einsum('bhtd,bhsd,bhtsd->bhts', qc, kc, decay)
        o = o_inter + jnp.einsum('bhts,bhse->bhte', att, vc)
        S_new = S * jnp.exp(b_last)[:, :, 0, :, None] + jnp.einsum('bhsd,bhse->bhde', kc * jnp.exp(b_last - b), vc)
        return S_new, o

    S, o = lax.scan(step, s0, (chunks(q), chunks(k), chunks(v), chunks(log_a)))
    return o.transpose(1, 2, 0, 3, 4).reshape(B, H, T, dv).astype(v.dtype), S


def bidir_gla_mixer(hx, hc, w_in, w_gate, b_gate, norm_g, w_out, cos, sin, with_ctx_out):
    def project(h, rotary):
        B, T, _ = h.shape
        p = h @ w_in
        q, k, v, r, g = jnp.split(p, [GLA_QK, 2 * GLA_QK, 2 * GLA_QK + GLA_V, 2 * GLA_QK + 2 * GLA_V], axis=-1)
        q = q.reshape(B, T, GLA_HEADS, GLA_DK)
        k = k.reshape(B, T, GLA_HEADS, GLA_DK)
        if rotary:
            q = apply_rope(q, cos, sin)
            k = apply_rope(k, cos, sin)
        q = q * GLA_DK ** -0.5
        v = v.reshape(B, T, GLA_HEADS, GLA_DV)
        g = g.reshape(B, T, 2, GATE_RANK)
        la = jax.nn.log_sigmoid((jnp.einsum('btzr,zre->btze', g, w_gate) + b_gate).astype(jnp.float32)) / GATE_NORM
        la = la.reshape(B, T, 2, GLA_HEADS, GLA_DK)
        heads = lambda a: a.transpose(0, 2, 1, 3)
        return heads(q), heads(k), heads(v), r, heads(la[:, :, 0]), heads(la[:, :, 1])

    def flip(a):
        return jnp.flip(a, axis=2)

    def output(o, r):
        B, H, T, dv = o.shape
        of = o.astype(jnp.float32)
        of = of * lax.rsqrt(jnp.mean(of * of, axis=-1, keepdims=True) + RMS_EPS) * norm_g
        of = of.transpose(0, 2, 1, 3).reshape(B, T, H * dv)
        return (of.astype(r.dtype) * jax.nn.silu(r)) @ w_out

    qc, kc, vc, rc, af_c, ab_c = project(hc, False)
    qx, kx, vx, rx, af_x, ab_x = project(hx, True)
    B = hx.shape[0]
    zero = jnp.zeros((B, GLA_HEADS, GLA_DK, GLA_DV), jnp.float32)
    of_c, s_f = gla_chunked(qc, kc, vc, af_c, zero)
    ob_c, s_b = gla_chunked(flip(qc), flip(kc), flip(vc), flip(ab_c), zero)
    of_x, _ = gla_chunked(qx, kx, vx, af_x, s_f)
    ob_x, _ = gla_chunked(flip(qx), flip(kx), flip(vx), flip(ab_x), s_b)
    ox = output(of_x + flip(ob_x), rx)
    if not with_ctx_out:
        return ox, None
    oc = output(of_c + flip(ob_c), rc)
    return ox, oc


def moe_ffn(h, router_w, router_b, w1, b1, w2, b2):
    lead = h.shape[:-1]
    d = h.shape[-1]
    xf = h.reshape(-1, d)
    n_tok = xf.shape[0]
    m = n_tok * TOP_K
    logits = (xf @ router_w + router_b).astype(jnp.float32)
    top_logit, top_idx = lax.top_k(logits, TOP_K)
    gates = jax.nn.softmax(top_logit, axis=-1).reshape(m)
    flat_e = top_idx.reshape(m)
    order = jnp.argsort(flat_e)
    e_sorted = flat_e[order]
    tok_sorted = order // TOP_K
    sizes = jnp.bincount(flat_e, length=N_EXPERTS)
    starts = jnp.cumsum(sizes) - sizes
    padded = (sizes + MOE_BLOCK - 1) // MOE_BLOCK * MOE_BLOCK
    pad_ends = jnp.cumsum(padded)
    pad_starts = pad_ends - padded
    dest = pad_starts[e_sorted] + jnp.arange(m) - starts[e_sorted]
    n_blocks = (m + N_EXPERTS * (MOE_BLOCK - 1)) // MOE_BLOCK + 1
    x_pad = jnp.zeros((n_blocks * MOE_BLOCK, d), xf.dtype).at[dest].set(xf[tok_sorted])
    block_start = jnp.arange(n_blocks) * MOE_BLOCK
    block_expert = jnp.minimum(jnp.searchsorted(pad_ends, block_start, side='right'), N_EXPERTS - 1)

    def expert_block(args):
        xb, e = args
        hid = xb @ w1[e] + b1[e]
        x_glu, x_lin = jnp.split(hid, 2, axis=-1)
        x_glu = jnp.minimum(x_glu, SWIGLU_LIMIT)
        x_lin = jnp.clip(x_lin, -SWIGLU_LIMIT, SWIGLU_LIMIT)
        act = x_glu * jax.nn.sigmoid(SWIGLU_ALPHA * x_glu) * (x_lin + 1.0)
        return act @ w2[e] + b2[e]

    y_pad = lax.map(expert_block, (x_pad.reshape(n_blocks, MOE_BLOCK, d), block_expert))
    y_sorted = y_pad.reshape(n_blocks * MOE_BLOCK, d)[dest] * gates[order][:, None]
    y = jax.ops.segment_sum(y_sorted, tok_sorted, num_segments=n_tok)
    return y.reshape(lead + (d,)).astype(h.dtype)


def setup_inputs(seed: int = 0) -> dict:
    key = jax.random.key(seed)
    ks = jax.random.split(key, 24)
    D = D_MODEL

    def nrm(k, shape, scale):
        return jax.random.normal(k, shape, jnp.float32) * scale

    return {
        'x': nrm(ks[0], (BATCH, SEQ, D), 1.0),
        'c': nrm(ks[1], (BATCH, D), 1.0),
        'ctx': nrm(ks[2], (BATCH, CTX_LEN, D), 1.0),
        'c_ctx': nrm(ks[3], (D,), 1.0),
        'ada_w': nrm(ks[4], (DEPTH, D, N_MOD * D), 0.5 * D ** -0.5),
        'ada_b': nrm(ks[5], (DEPTH, N_MOD * D), 0.02),
        'ln_g': 1.0 + nrm(ks[6], (DEPTH, 2, D), 0.02),
        'ln_b': nrm(ks[7], (DEPTH, 2, D), 0.02),
        'ab_w_in': nrm(ks[8], (N_EVEN, D, AB_IN), D ** -0.5),
        'ab_pool_w': nrm(ks[9], (N_EVEN, POOL_GROUPS, POOL_GROUP_DIM, POOL_GROUP_DIM), POOL_GROUP_DIM ** -0.5),
        'ab_pool_scale': 1.0 + nrm(ks[10], (N_EVEN, POOL_WIDTH), 0.1),
        'ab_rpb': nrm(ks[11], (N_EVEN, NA_HEADS, 2 * WIN_H - 1, 2 * WIN_W - 1), 0.1),
        'ab_w_out': nrm(ks[12], (N_EVEN, AB_OUT, D), DEEPNORM_BETA * AB_OUT ** -0.5),
        'gla_w_in': nrm(ks[13], (N_ODD, D, GLA_IN), D ** -0.5),
        'gla_w_gate': nrm(ks[14], (N_ODD, 2, GATE_RANK, GLA_QK), GATE_RANK ** -0.5),
        'gla_b_gate': nrm(ks[15], (N_ODD, 2, GLA_QK), 0.5),
        'gla_norm_g': 1.0 + nrm(ks[16], (N_ODD, GLA_DV), 0.02),
        'gla_w_out': nrm(ks[17], (N_ODD, GLA_V, D), DEEPNORM_BETA * GLA_V ** -0.5),
        'router_w': nrm(ks[18], (DEPTH, D, N_EXPERTS), D ** -0.5),
        'router_b': nrm(ks[19], (DEPTH, N_EXPERTS), 0.01),
        'exp_w1': nrm(ks[20], (DEPTH, N_EXPERTS, D, 2 * D_EXPERT), D ** -0.5),
        'exp_b1': nrm(ks[21], (DEPTH, N_EXPERTS, 2 * D_EXPERT), 0.01),
        'exp_w2': nrm(ks[22], (DEPTH, N_EXPERTS, D_EXPERT, D), DEEPNORM_BETA * D_EXPERT ** -0.5),
        'exp_b2': nrm(ks[23], (DEPTH, N_EXPERTS, D), 0.01),
    }


def reference(x, c, ctx, c_ctx, ada_w, ada_b, ln_g, ln_b, ab_w_in, ab_pool_w, ab_pool_scale, ab_rpb,
              ab_w_out, gla_w_in, gla_w_gate, gla_b_gate, gla_norm_g, gla_w_out, router_w, router_b,
              exp_w1, exp_b1, exp_w2, exp_b2):
    T = x.shape[1]
    n_ctx = ctx.shape[1]
    cos, sin = axial_rope_tables(T, GLA_DK)
    sc = jax.nn.silu(c)
    scc = jax.nn.silu(c_ctx)
    for i in range(DEPTH):
        last = i == DEPTH - 1
        j = i // 2
        sh1, s1, g1, sh2, s2, g2 = jnp.split((sc @ ada_w[i] + ada_b[i])[:, None, :], N_MOD, axis=-1)
        ch1, cs1, cg1, ch2, cs2, cg2 = jnp.split(scc @ ada_w[i] + ada_b[i], N_MOD, axis=-1)
        hx = x * (1.0 + s1) + sh1
        hc = ctx * (1.0 + cs1) + ch1
        if i % 2 == 0:
            ox, oc = pool_na_mixer(hx, hc, ab_w_in[j], ab_pool_w[j], ab_pool_scale[j], ab_rpb[j],
                                   ab_w_out[j], not last)
        else:
            ox, oc = bidir_gla_mixer(hx, hc, gla_w_in[j], gla_w_gate[j], gla_b_gate[j], gla_norm_g[j],
                                     gla_w_out[j], cos, sin, not last)
        x = layer_norm(DEEPNORM_ALPHA * x + g1 * ox, ln_g[i, 0], ln_b[i, 0])
        hx = x * (1.0 + s2) + sh2
        if last:
            y = moe_ffn(hx, router_w[i], router_b[i], exp_w1[i], exp_b1[i], exp_w2[i], exp_b2[i])
            x = layer_norm(DEEPNORM_ALPHA * x + g2 * y, ln_g[i, 1], ln_b[i, 1])
        else:
            ctx = layer_norm(DEEPNORM_ALPHA * ctx + cg1 * oc, ln_g[i, 0], ln_b[i, 0])
            hc = ctx * (1.0 + cs2) + ch2
            y = moe_ffn(jnp.concatenate([hc, hx], axis=1), router_w[i], router_b[i], exp_w1[i], exp_b1[i],
                        exp_w2[i], exp_b2[i])
            x = layer_norm(DEEPNORM_ALPHA * x + g2 * y[:, n_ctx:], ln_g[i, 1], ln_b[i, 1])
            ctx = layer_norm(DEEPNORM_ALPHA * ctx + cg2 * y[:, :n_ctx], ln_g[i, 1], ln_b[i, 1])
    return x
```

```python
import functools
import math

import numpy as np
import jax
import jax.numpy as jnp
from jax import lax
from jax.experimental import pallas as pl
from jax.experimental.pallas import tpu as pltpu

F32 = jnp.float32
BF16 = jnp.bfloat16
HIGHEST = lax.Precision.HIGHEST

D_MODEL = 1024
GRID_W = 64
N_MOD = 6
POOL_WINDOWS = (2, 4, 8, 16)
POOL_WIDTH = D_MODEL // 4
POOL_GROUP_DIM = POOL_WIDTH // len(POOL_WINDOWS)
POOL_HALO = max(POOL_WINDOWS) // 2
NA_HEAD_DIM = 64
NA_HEADS = (D_MODEL - POOL_WIDTH) // NA_HEAD_DIM
NA_WIDTH = NA_HEADS * NA_HEAD_DIM
WIN_H = 8
WIN_W = 16
GLA_HEADS = 4
GLA_DK = D_MODEL // 2 // GLA_HEADS
GLA_DV = D_MODEL // GLA_HEADS
GATE_RANK = 16
GATE_NORM = 16.0
GLA_CHUNK = 64
GLA_SUB = 16
GLA_QK = GLA_HEADS * GLA_DK
GLA_V = GLA_HEADS * GLA_DV
ROPE_BASE = 10000.0
TOP_K = 4
SWIGLU_LIMIT = 7.0
SWIGLU_ALPHA = 1.702
MOE_BLOCK = 256
LN_EPS = 1e-5
RMS_EPS = 1e-6
NEG_INF = -1e30

LANES = 128
TOK_TILE = 256
NA_QROWS = 4
NA_KROWS = 12
VMEM_LIMIT = 48 * 1024 * 1024


def _cparams(sem):
    return pltpu.CompilerParams(dimension_semantics=sem, vmem_limit_bytes=VMEM_LIMIT)


def _row_tile(n):
    return 2 * TOK_TILE if n % (2 * TOK_TILE) == 0 else TOK_TILE


def _mods_kernel(c_ref, w_ref, b_ref, o_ref):
    cv = c_ref[...]
    sc = cv * jax.nn.sigmoid(cv)
    o_ref[0] = jnp.dot(sc, w_ref[0], precision=HIGHEST, preferred_element_type=F32) + b_ref[0]


def _mods(cc, ada_w, ada_b):
    depth, d, n = ada_w.shape
    r = cc.shape[0]
    tn = n // 4
    return pl.pallas_call(
        _mods_kernel,
        grid=(depth, n // tn),
        in_specs=[pl.BlockSpec((r, d), lambda i, j: (0, 0)),
                  pl.BlockSpec((1, d, tn), lambda i, j: (i, 0, j)),
                  pl.BlockSpec((1, 1, tn), lambda i, j: (i, 0, j))],
        out_specs=pl.BlockSpec((1, r, tn), lambda i, j: (i, 0, j)),
        out_shape=jax.ShapeDtypeStruct((depth, r, n), F32),
        compiler_params=_cparams(("arbitrary", "arbitrary")),
        name="mods",
    )(cc, ada_w, ada_b.reshape(depth, 1, n))


def _tab_row(g, tpb):
    return (g // tpb) * 2 + (g % tpb == tpb - 1).astype(jnp.int32)


def _modulate_kernel(tpb, x_ref, tab_ref, h_ref):
    t = tab_ref[_tab_row(pl.program_id(0), tpb)]
    h_ref[...] = (x_ref[...] * (1.0 + t[0:1]) + t[1:2]).astype(h_ref.dtype)


def _modulate(z, tab, tpb):
    n, d = z.shape
    return pl.pallas_call(
        functools.partial(_modulate_kernel, tpb),
        grid=(n // TOK_TILE,),
        in_specs=[pl.BlockSpec((TOK_TILE, d), lambda i: (i, 0)),
                  pl.BlockSpec(tab.shape, lambda i: (0, 0, 0))],
        out_specs=pl.BlockSpec((TOK_TILE, d), lambda i: (i, 0)),
        out_shape=jax.ShapeDtypeStruct((n, d), BF16),
        compiler_params=_cparams(("arbitrary",)),
        name="modulate",
    )(z, tab)


def _proj_kernel(splits, x_ref, w_ref, *out_refs):
    x = x_ref[...]
    for (a, b), o_ref in zip(splits, out_refs):
        o_ref[...] = jnp.dot(x, w_ref[:, a:b], preferred_element_type=F32).astype(o_ref.dtype)


def _proj(h, w, splits, dtypes):
    n, k = h.shape
    m = w.shape[1]
    tm = _row_tile(n)
    return pl.pallas_call(
        functools.partial(_proj_kernel, tuple(splits)),
        grid=(n // tm,),
        in_specs=[pl.BlockSpec((tm, k), lambda i: (i, 0)),
                  pl.BlockSpec((k, m), lambda i: (0, 0))],
        out_specs=[pl.BlockSpec((tm, b - a), lambda i: (i, 0)) for a, b in splits],
        out_shape=[jax.ShapeDtypeStruct((n, b - a), dt) for (a, b), dt in zip(splits, dtypes)],
        compiler_params=_cparams(("arbitrary",)),
        name="proj",
    )(h, w)


def _pool_kernel(n_lat, t_lat, t_ctx, prev_ref, cur_ref, next_ref, w_ref, scale_ref, o_ref, halo_ref):
    j = pl.program_id(1)
    is_ctx = j == n_lat
    has_prev = jnp.logical_and(j > 0, jnp.logical_not(is_ctx))
    has_next = j < n_lat - 1
    cur = cur_ref[0]
    hl = POOL_HALO
    halo_ref[0:hl] = jnp.where(has_prev, prev_ref[0, TOK_TILE - hl:TOK_TILE], 0.0)
    halo_ref[hl:hl + TOK_TILE] = cur
    halo_ref[hl + TOK_TILE:2 * hl + TOK_TILE] = jnp.where(has_next, next_ref[0, 0:hl], 0.0)

    shape = cur.shape
    lane = lax.broadcasted_iota(jnp.int32, shape, 1)
    group = lane // POOL_GROUP_DIM
    half = jnp.ones(shape, jnp.int32)
    for gi, wdw in enumerate(POOL_WINDOWS):
        half = jnp.where(group == gi, wdw // 2, half)
    acc = jnp.zeros(shape, F32)
    for off in range(-hl, hl):
        v = halo_ref[hl + off:hl + off + TOK_TILE]
        inside = (half >= -off) if off < 0 else (half > off)
        acc = acc + jnp.where(inside, v, 0.0)
    pos0 = jnp.where(is_ctx, 0, j * TOK_TILE)
    seq = jnp.where(is_ctx, t_ctx, t_lat)
    t = pos0 + lax.broadcasted_iota(jnp.int32, shape, 0)
    cnt = jnp.minimum(t + half, seq) - jnp.maximum(t - half, 0)
    pooled = acc / cnt.astype(F32) - cur
    y = jnp.dot(pooled.astype(BF16), w_ref[...], preferred_element_type=F32) * scale_ref[...]
    o_ref[0] = y.astype(o_ref.dtype)


def _pool(u, w_blk, scale, n_lat, t_lat, t_ctx):
    b, l, pw = u.shape
    tpb = l // TOK_TILE
    blk = (1, TOK_TILE, pw)
    return pl.pallas_call(
        functools.partial(_pool_kernel, n_lat, t_lat, t_ctx),
        grid=(b, tpb),
        in_specs=[pl.BlockSpec(blk, lambda bi, j: (bi, jnp.maximum(j - 1, 0), 0)),
                  pl.BlockSpec(blk, lambda bi, j: (bi, j, 0)),
                  pl.BlockSpec(blk, lambda bi, j: (bi, jnp.minimum(j + 1, tpb - 1), 0)),
                  pl.BlockSpec((pw, pw), lambda bi, j: (0, 0)),
                  pl.BlockSpec((1, pw), lambda bi, j: (0, 0))],
        out_specs=pl.BlockSpec(blk, lambda bi, j: (bi, j, 0)),
        out_shape=jax.ShapeDtypeStruct((b, l, pw), BF16),
        scratch_shapes=[pltpu.VMEM((TOK_TILE + 2 * POOL_HALO, pw), F32)],
        compiler_params=_cparams(("arbitrary", "arbitrary")),
        name="pool",
    )(u, u, u, w_blk, scale)


def _na_bias_tables(rpb, rows, n_ctx):
    n_i = rows // NA_QROWS
    a = np.repeat(np.arange(NA_QROWS), GRID_W)
    cq = np.tile(np.arange(GRID_W), NA_QROWS)
    kr = np.repeat(np.arange(NA_KROWS), GRID_W)
    ck = np.tile(np.arange(GRID_W), NA_KROWS)
    tabs = []
    for i in (0, 1, n_i - 1):
        start = int(np.clip(NA_QROWS * i - WIN_H // 2, 0, rows - NA_KROWS))
        r = NA_QROWS * i + a
        krow = start + kr
        rs = np.clip(r - WIN_H // 2, 0, rows - WIN_H)
        ok_row = (krow[None, :] >= rs[:, None]) & (krow[None, :] < rs[:, None] + WIN_H)
        ws = np.clip(cq - WIN_W // 2, 0, GRID_W - WIN_W)
        ok_col = (ck[None, :] >= ws[:, None]) & (ck[None, :] < ws[:, None] + WIN_W)
        drow = np.clip(krow[None, :] - r[:, None] + WIN_H - 1, 0, 2 * WIN_H - 2)
        dcol = np.clip(ck[None, :] - cq[:, None] + WIN_W - 1, 0, 2 * WIN_W - 2)
        tabs.append(jnp.where((ok_row & ok_col)[None], rpb[:, drow, dcol], NEG_INF))
    tabs.append(jnp.full_like(tabs[0], NEG_INF))
    band = jnp.stack(tabs).astype(F32)
    return jnp.concatenate([band, jnp.zeros(band.shape[:3] + (n_ctx,), F32)], axis=-1)


def _na_kernel(q_ref, k0_ref, k1_ref, k2_ref, kc_ref, v0_ref, v1_ref, v2_ref, vc_ref, bias_ref, o_ref):
    q = q_ref[0]
    kcat = jnp.concatenate([k0_ref[0], k1_ref[0], k2_ref[0], kc_ref[0]], axis=0)
    vcat = jnp.concatenate([v0_ref[0], v1_ref[0], v2_ref[0], vc_ref[0]], axis=0)
    lane = lax.broadcasted_iota(jnp.int32, q.shape, 1)
    first = lane < NA_HEAD_DIM
    outs = []
    for hh in range(2):
        qm = jnp.where(first if hh == 0 else jnp.logical_not(first), q, jnp.zeros_like(q))
        s = lax.dot_general(qm, kcat, (((1,), (1,)), ((), ())), preferred_element_type=F32)
        s = s * NA_HEAD_DIM ** -0.5 + bias_ref[0, hh]
        m = jnp.max(s, axis=-1, keepdims=True)
        p = jnp.exp(s - m)
        l = jnp.sum(p, axis=-1, keepdims=True)
        outs.append(jnp.dot(p.astype(BF16), vcat, preferred_element_type=F32) / l)
    o_ref[0] = jnp.where(first, outs[0], outs[1]).astype(o_ref.dtype)


def _na(qkv, bias, n_lat):
    b, l, _ = qkv.shape
    tpb = l // TOK_TILE
    n_pairs = NA_WIDTH // LANES
    blk = (1, TOK_TILE, LANES)

    def kstart(i):
        return jnp.clip(i - 1, 0, n_lat - NA_KROWS // NA_QROWS)

    def btype(i):
        return jnp.where(i == 0, 0, jnp.where(i == n_lat - 1, 2, jnp.where(i == n_lat, 3, 1)))

    def kv_spec(col0, j):
        return pl.BlockSpec(blk, lambda hp, i, bi: (bi, kstart(i) + j, col0 + hp))

    def ctx_spec(col0):
        return pl.BlockSpec(blk, lambda hp, i, bi: (bi, n_lat, col0 + hp))

    nk = bias.shape[-1]
    return pl.pallas_call(
        _na_kernel,
        grid=(n_pairs, tpb, b),
        in_specs=[pl.BlockSpec(blk, lambda hp, i, bi: (bi, i, hp)),
                  kv_spec(n_pairs, 0), kv_spec(n_pairs, 1), kv_spec(n_pairs, 2), ctx_spec(n_pairs),
                  kv_spec(2 * n_pairs, 0), kv_spec(2 * n_pairs, 1), kv_spec(2 * n_pairs, 2), ctx_spec(2 * n_pairs),
                  pl.BlockSpec((1, 2, TOK_TILE, nk), lambda hp, i, bi: (btype(i), hp, 0, 0))],
        out_specs=pl.BlockSpec(blk, lambda hp, i, bi: (bi, i, hp)),
        out_shape=jax.ShapeDtypeStruct((b, l, NA_WIDTH), BF16),
        compiler_params=_cparams(("arbitrary", "arbitrary", "arbitrary")),
        name="na",
    )(qkv, qkv, qkv, qkv, qkv, qkv, qkv, qkv, qkv, bias)


def _log_sigmoid(z):
    return jnp.minimum(z, 0.0) - jnp.log(1.0 + jnp.exp(-jnp.abs(z)))


def _gla_prep_kernel(rev, q_ref, k_ref, g_ref, cos_ref, sin_ref, wg_ref, bg_ref, tri_ref,
                     qe_ref, kd_ref, a_ref, gd_ref, b_scr, qr_scr, kr_scr):
    tg = q_ref.shape[0]
    gcol = GATE_RANK if rev else 0
    gg = g_ref[:, gcol:gcol + GATE_RANK]
    z = jnp.dot(gg, wg_ref[...], precision=HIGHEST, preferred_element_type=F32) + bg_ref[...]
    la = _log_sigmoid(z) * (1.0 / GATE_NORM)
    b = jnp.dot(tri_ref[...], la, precision=HIGHEST, preferred_element_type=F32)
    cosv = cos_ref[...]
    sinv = sin_ref[...]
    q = q_ref[...]
    k = k_ref[...]
    qr = (q * cosv + pltpu.roll(q, GLA_DK // 2, 1) * sinv) * GLA_DK ** -0.5
    kr = k * cosv + pltpu.roll(k, GLA_DK // 2, 1) * sinv
    qe_ref[...] = (qr * jnp.exp(b)).astype(qe_ref.dtype)
    for c in range(tg // GLA_CHUNK):
        r0 = c * GLA_CHUNK
        last = r0 if rev else r0 + GLA_CHUNK - 1
        tot = b[last:last + 1]
        kd_ref[r0:r0 + GLA_CHUNK] = (kr[r0:r0 + GLA_CHUNK] * jnp.exp(tot - b[r0:r0 + GLA_CHUNK])).astype(kd_ref.dtype)
        gd_ref[0, 0, c:c + 1] = jnp.exp(tot)
    b_scr[...] = b
    qr_scr[...] = qr
    kr_scr[...] = kr

    n_sub = GLA_CHUNK // GLA_SUB
    colio = lax.broadcasted_iota(jnp.int32, (GLA_SUB, GLA_CHUNK), 1)
    rowio = lax.broadcasted_iota(jnp.int32, (GLA_SUB, GLA_CHUNK), 0)

    def sub_block(sb, carry):
        r0 = pl.multiple_of(sb * GLA_SUB, GLA_SUB)
        c0 = pl.multiple_of((sb // n_sub) * GLA_CHUNK, GLA_CHUNK)
        d0 = (sb % n_sub) * GLA_SUB
        bq = b_scr[pl.ds(r0, GLA_SUB), :]
        qq = qr_scr[pl.ds(r0, GLA_SUB), :]
        ref_row = jnp.minimum(r0 + GLA_SUB, tg - 1) if rev else jnp.maximum(r0 - 1, 0)
        rb = b_scr[pl.ds(ref_row, 1), :]
        bc = b_scr[pl.ds(c0, GLA_CHUNK), :]
        kc = kr_scr[pl.ds(c0, GLA_CHUNK), :]
        qt = qq * jnp.exp(jnp.minimum(bq - rb, 0.0))
        kt = kc * jnp.exp(jnp.minimum(rb - bc, 0.0))
        off = lax.dot_general(qt.astype(BF16), kt.astype(BF16), (((1,), (1,)), ((), ())),
                              preferred_element_type=F32)
        dacc = jnp.zeros((GLA_SUB, GLA_CHUNK), F32)
        for s in range(GLA_SUB):
            ks = kr_scr[pl.ds(r0 + s, 1), :]
            bs = b_scr[pl.ds(r0 + s, 1), :]
            col = jnp.sum(qq * ks * jnp.exp(jnp.minimum(bq - bs, 0.0)), axis=1, keepdims=True)
            dacc = jnp.where(colio == d0 + s, col, dacc)
        dcol = colio - d0
        in_diag = jnp.logical_and(dcol >= 0, dcol < GLA_SUB)
        if rev:
            ok_off = colio >= d0 + GLA_SUB
            ok_diag = jnp.logical_and(in_diag, dcol >= rowio)
        else:
            ok_off = colio < d0
            ok_diag = jnp.logical_and(in_diag, dcol <= rowio)
        a_blk = jnp.where(ok_off, off, jnp.where(ok_diag, dacc, 0.0))
        a_ref[0, pl.ds(r0, GLA_SUB), :] = a_blk.astype(a_ref.dtype)
        return carry

    lax.fori_loop(0, tg // GLA_SUB, sub_block, 0)


def _gla_prep(rev, qk, g, cos2, sin2, w_gate, b_gate, tri, tpb):
    n = qk.shape[0]
    tg = TOK_TILE
    nt = n // tg
    return pl.pallas_call(
        functools.partial(_gla_prep_kernel, rev),
        grid=(GLA_HEADS, nt),
        in_specs=[pl.BlockSpec((tg, GLA_DK), lambda hd, t: (t, hd)),
                  pl.BlockSpec((tg, GLA_DK), lambda hd, t: (t, GLA_HEADS + hd)),
                  pl.BlockSpec((tg, 2 * GATE_RANK), lambda hd, t: (t, 0)),
                  pl.BlockSpec((tg, GLA_DK), lambda hd, t: (t % tpb, 0)),
                  pl.BlockSpec((tg, GLA_DK), lambda hd, t: (t % tpb, 0)),
                  pl.BlockSpec((GATE_RANK, GLA_DK), lambda hd, t: (0, hd)),
                  pl.BlockSpec((1, GLA_DK), lambda hd, t: (0, hd)),
                  pl.BlockSpec((tg, tg), lambda hd, t: (0, 0))],
        out_specs=[pl.BlockSpec((tg, GLA_DK), lambda hd, t: (t, hd)),
                   pl.BlockSpec((tg, GLA_DK), lambda hd, t: (t, hd)),
                   pl.BlockSpec((1, tg, GLA_CHUNK), lambda hd, t: (hd, t, 0)),
                   pl.BlockSpec((1, 1, tg // GLA_CHUNK, GLA_DK), lambda hd, t: (hd, t, 0, 0))],
        out_shape=[jax.ShapeDtypeStruct((n, GLA_QK), BF16),
                   jax.ShapeDtypeStruct((n, GLA_QK), BF16),
                   jax.ShapeDtypeStruct((GLA_HEADS, n, GLA_CHUNK), BF16),
                   jax.ShapeDtypeStruct((GLA_HEADS, nt, tg // GLA_CHUNK, GLA_DK), F32)],
        scratch_shapes=[pltpu.VMEM((tg, GLA_DK), F32)] * 3,
        compiler_params=_cparams(("arbitrary", "arbitrary")),
        name="gla_prep_bwd" if rev else "gla_prep_fwd",
    )(qk, qk, g, cos2, sin2, w_gate, b_gate, tri)


def _gla_scan_kernel(rev, qe_ref, kd_ref, a_ref, gd_ref, v_ref, o_ref, st_ref):
    @pl.when(pl.program_id(1) == 0)
    def _():
        st_ref[...] = jnp.zeros_like(st_ref)

    n_chunks = qe_ref.shape[0] // GLA_CHUNK
    for hd in range(GLA_HEADS):
        st = st_ref[hd]
        kcols = slice(hd * GLA_DK, (hd + 1) * GLA_DK)
        vcols = slice(hd * GLA_DV, (hd + 1) * GLA_DV)
        for cc in range(n_chunks):
            c = n_chunks - 1 - cc if rev else cc
            rows = slice(c * GLA_CHUNK, (c + 1) * GLA_CHUNK)
            v_c = v_ref[rows, vcols]
            o = lax.dot_general(qe_ref[rows, kcols], st.astype(BF16), (((1,), (1,)), ((), ())),
                                preferred_element_type=F32)
            o = o + jnp.dot(a_ref[hd, rows, :], v_c, preferred_element_type=F32)
            o_ref[rows, vcols] = o
            upd = lax.dot_general(v_c, kd_ref[rows, kcols], (((0,), (0,)), ((), ())),
                                  preferred_element_type=F32)
            st = st * gd_ref[hd, 0, c:c + 1, :] + upd
        st_ref[hd] = st


def _gla_scan(rev, qe, kd, a, gd, v, batch, tpb):
    n = qe.shape[0]
    tg = TOK_TILE
    n_lat = tpb - 1

    def tile(bi, s):
        lat = n_lat - s if rev else s - 1
        return bi * tpb + jnp.where(s == 0, n_lat, lat)

    return pl.pallas_call(
        functools.partial(_gla_scan_kernel, rev),
        grid=(batch, tpb),
        in_specs=[pl.BlockSpec((tg, GLA_QK), lambda bi, s: (tile(bi, s), 0)),
                  pl.BlockSpec((tg, GLA_QK), lambda bi, s: (tile(bi, s), 0)),
                  pl.BlockSpec((GLA_HEADS, tg, GLA_CHUNK), lambda bi, s: (0, tile(bi, s), 0)),
                  pl.BlockSpec((GLA_HEADS, 1, tg // GLA_CHUNK, GLA_DK), lambda bi, s: (0, tile(bi, s), 0, 0)),
                  pl.BlockSpec((tg, GLA_V), lambda bi, s: (tile(bi, s), 0))],
        out_specs=pl.BlockSpec((tg, GLA_V), lambda bi, s: (tile(bi, s), 0)),
        out_shape=jax.ShapeDtypeStruct((n, GLA_V), F32),
        scratch_shapes=[pltpu.VMEM((GLA_HEADS, GLA_DV, GLA_DK), F32)],
        compiler_params=_cparams(("arbitrary", "arbitrary")),
        name="gla_scan_bwd" if rev else "gla_scan_fwd",
    )(qe, kd, a, gd, v)


def _gla_out_kernel(of_ref, ob_ref, r_ref, g_ref, o_ref):
    o = of_ref[...] + ob_ref[...]
    r = r_ref[...]
    gate = r * jax.nn.sigmoid(r)
    gn = g_ref[...]
    for hd in range(GLA_HEADS):
        cols = slice(hd * GLA_DV, (hd + 1) * GLA_DV)
        oh = o[:, cols]
        ms = jnp.mean(oh * oh, axis=-1, keepdims=True)
        o_ref[:, cols] = (oh * lax.rsqrt(ms + RMS_EPS) * gn * gate[:, cols]).astype(o_ref.dtype)


def _gla_out(o_f, o_b, r, norm_g):
    n, dv = o_f.shape
    tm = _row_tile(n)
    spec = pl.BlockSpec((tm, dv), lambda i: (i, 0))
    return pl.pallas_call(
        _gla_out_kernel,
        grid=(n // tm,),
        in_specs=[spec, spec, spec, pl.BlockSpec((1, GLA_DV), lambda i: (0, 0))],
        out_specs=spec,
        out_shape=jax.ShapeDtypeStruct((n, dv), BF16),
        compiler_params=_cparams(("arbitrary",)),
        name="gla_out",
    )(o_f, o_b, r, norm_g)


def _residual_ln(x, a, t, ln, alpha):
    y = alpha * x + t[0:1] * a
    mu = jnp.mean(y, axis=-1, keepdims=True)
    yc = y - mu
    var = jnp.mean(yc * yc, axis=-1, keepdims=True)
    xn = yc * lax.rsqrt(var + LN_EPS) * ln[0:1] + ln[1:2]
    return xn, xn * (1.0 + t[1:2]) + t[2:3]


def _top4_softmax(lt):
    e = lt.shape[0]
    io = lax.broadcasted_iota(jnp.int32, lt.shape, 0)
    work = lt
    idxs, vals = [], []
    for _ in range(TOP_K):
        m = jnp.max(work, axis=0, keepdims=True)
        ik = jnp.min(jnp.where(work == m, io, e), axis=0, keepdims=True)
        idxs.append(ik)
        vals.append(m)
        work = jnp.where(io == ik, -jnp.inf, work)
    ex = [jnp.exp(v - vals[0]) for v in vals]
    den = ex[0] + ex[1] + ex[2] + ex[3]
    return jnp.concatenate(idxs, axis=0), jnp.concatenate([x / den for x in ex], axis=0)


def _post_kernel(n_act, tpb, alpha, *refs):
    acts = refs[:n_act]
    ws = refs[n_act:2 * n_act]
    x_ref, tab_ref, ln_ref, rw_ref, rb_ref, xo_ref, h_ref, idx_ref, gate_ref = refs[2 * n_act:]
    tm = x_ref.shape[0]
    a = jnp.dot(acts[0][...], ws[0][...], preferred_element_type=F32)
    for k in range(1, n_act):
        a = a + jnp.dot(acts[k][...], ws[k][...], preferred_element_type=F32)
    ln = ln_ref[...]
    for s in range(tm // TOK_TILE):
        rows = slice(s * TOK_TILE, (s + 1) * TOK_TILE)
        t = tab_ref[_tab_row(pl.program_id(0) * (tm // TOK_TILE) + s, tpb)]
        xn, h = _residual_ln(x_ref[rows], a[rows], t, ln, alpha)
        xo_ref[rows] = xn
        h_ref[rows] = h.astype(h_ref.dtype)
        lt = lax.dot_general(rw_ref[...], h, (((1,), (1,)), ((), ())), precision=HIGHEST,
                             preferred_element_type=F32) + rb_ref[...]
        idx, gates = _top4_softmax(lt)
        idx_ref[:, rows] = idx
        gate_ref[:, rows] = gates


def _post(acts, ws, x, tab, ln, rw_t, rb, tpb, alpha):
    n, d = x.shape
    tm = _row_tile(n)
    e = rw_t.shape[0]
    row = lambda i: (i, 0)
    fixed = lambda i: (0, 0)
    return pl.pallas_call(
        functools.partial(_post_kernel, len(acts), tpb, alpha),
        grid=(n // tm,),
        in_specs=([pl.BlockSpec((tm, a.shape[1]), row) for a in acts]
                  + [pl.BlockSpec(w.shape, fixed) for w in ws]
                  + [pl.BlockSpec((tm, d), row),
                     pl.BlockSpec(tab.shape, lambda i: (0, 0, 0)),
                     pl.BlockSpec(ln.shape, fixed),
                     pl.BlockSpec((e, d), fixed),
                     pl.BlockSpec((e, 1), fixed)]),
        out_specs=[pl.BlockSpec((tm, d), row), pl.BlockSpec((tm, d), row),
                   pl.BlockSpec((TOP_K, tm), lambda i: (0, i)), pl.BlockSpec((TOP_K, tm), lambda i: (0, i))],
        out_shape=[jax.ShapeDtypeStruct((n, d), F32), jax.ShapeDtypeStruct((n, d), BF16),
                   jax.ShapeDtypeStruct((TOP_K, n), jnp.int32), jax.ShapeDtypeStruct((TOP_K, n), F32)],
        compiler_params=_cparams(("arbitrary",)),
        name="post",
    )(*acts, *ws, x, tab, ln, rw_t, rb)


def _rank_kernel(idx_ref, tri_ref, rank_ref, cnt_ref, carry_ref):
    @pl.when(pl.program_id(0) == 0)
    def _():
        carry_ref[...] = jnp.zeros_like(carry_ref)

    idx = idx_ref[...]
    e = carry_ref.shape[0]
    tr = idx.shape[1]
    io = lax.broadcasted_iota(jnp.int32, (e, tr), 0)
    chosen = jnp.zeros((e, tr), F32)
    for k in range(TOP_K):
        chosen = chosen + (idx[k:k + 1] == io).astype(F32)
    cum = jnp.dot(chosen.astype(BF16), tri_ref[...], preferred_element_type=F32)
    base = carry_ref[:, 0:1]
    excl = base + cum - chosen
    ranks = [jnp.sum(jnp.where(idx[k:k + 1] == io, excl, 0.0), axis=0, keepdims=True) for k in range(TOP_K)]
    rank_ref[...] = jnp.concatenate(ranks, axis=0).astype(jnp.int32)
    carry_ref[...] = carry_ref[...] + jnp.sum(chosen, axis=1, keepdims=True)
    cnt_ref[...] = carry_ref[...]


def _rank(idx_t, n_experts):
    n = idx_t.shape[1]
    tr = _row_tile(n)
    tri = (np.arange(tr)[:, None] <= np.arange(tr)[None, :]).astype(np.float32)
    return pl.pallas_call(
        _rank_kernel,
        grid=(n // tr,),
        in_specs=[pl.BlockSpec((TOP_K, tr), lambda i: (0, i)),
                  pl.BlockSpec((tr, tr), lambda i: (0, 0))],
        out_specs=[pl.BlockSpec((TOP_K, tr), lambda i: (0, i)),
                   pl.BlockSpec((n_experts, LANES), lambda i: (0, 0))],
        out_shape=[jax.ShapeDtypeStruct((TOP_K, n), jnp.int32),
                   jax.ShapeDtypeStruct((n_experts, LANES), F32)],
        scratch_shapes=[pltpu.VMEM((n_experts, LANES), F32)],
        compiler_params=_cparams(("arbitrary",)),
        name="rank",
    )(idx_t, jnp.asarray(tri, BF16))


def _expert_kernel(be_ref, nb_ref, x_ref, w1_ref, b1_ref, w2_ref, b2_ref, o_ref):
    i = pl.program_id(0)

    @pl.when(i < nb_ref[0])
    def _():
        hid = jnp.dot(x_ref[...], w1_ref[0], preferred_element_type=F32) + b1_ref[0]
        half = hid.shape[1] // 2
        glu = jnp.minimum(hid[:, :half], SWIGLU_LIMIT)
        lin = jnp.clip(hid[:, half:], -SWIGLU_LIMIT, SWIGLU_LIMIT)
        act = glu * jax.nn.sigmoid(SWIGLU_ALPHA * glu) * (lin + 1.0)
        o_ref[...] = jnp.dot(act.astype(BF16), w2_ref[0], preferred_element_type=F32) + b2_ref[0]

    @pl.when(i >= nb_ref[0])
    def _():
        o_ref[...] = jnp.zeros_like(o_ref)


def _experts(block_expert, n_used, x_pad, w1, b1, w2, b2):
    n_pad, d = x_pad.shape
    e, _, dh2 = w1.shape
    n_blocks = n_pad // MOE_BLOCK
    grid_spec = pltpu.PrefetchScalarGridSpec(
        num_scalar_prefetch=2,
        grid=(n_blocks,),
        in_specs=[pl.BlockSpec((MOE_BLOCK, d), lambda i, be, nb: (i, 0)),
                  pl.BlockSpec((1, d, dh2), lambda i, be, nb: (be[i], 0, 0)),
                  pl.BlockSpec((1, 1, dh2), lambda i, be, nb: (be[i], 0, 0)),
                  pl.BlockSpec((1, dh2 // 2, d), lambda i, be, nb: (be[i], 0, 0)),
                  pl.BlockSpec((1, 1, d), lambda i, be, nb: (be[i], 0, 0))],
        out_specs=pl.BlockSpec((MOE_BLOCK, d), lambda i, be, nb: (i, 0)),
    )
    return pl.pallas_call(
        _expert_kernel,
        grid_spec=grid_spec,
        out_shape=jax.ShapeDtypeStruct((n_pad, d), F32),
        compiler_params=_cparams(("arbitrary",)),
        name="experts",
    )(block_expert, n_used, x_pad, w1, b1.reshape(e, 1, dh2), w2, b2.reshape(e, 1, d))


def _combine_kernel(tpb, alpha, y_ref, gate_ref, x_ref, tab_ref, ln_ref, xo_ref, h_ref):
    g = gate_ref[...]
    y = y_ref[0] * g[:, 0:1]
    for k in range(1, TOP_K):
        y = y + y_ref[k] * g[:, k:k + 1]
    t = tab_ref[_tab_row(pl.program_id(0), tpb)]
    xn, h = _residual_ln(x_ref[...], y, t, ln_ref[...], alpha)
    xo_ref[...] = xn
    h_ref[...] = h.astype(h_ref.dtype)


def _combine(y_g, gates, x, tab, ln, tpb, alpha):
    n, d = x.shape
    tm = TOK_TILE
    row = lambda i: (i, 0)
    return pl.pallas_call(
        functools.partial(_combine_kernel, tpb, alpha),
        grid=(n // tm,),
        in_specs=[pl.BlockSpec((TOP_K, tm, d), lambda i: (0, i, 0)),
                  pl.BlockSpec((tm, TOP_K), row),
                  pl.BlockSpec((tm, d), row),
                  pl.BlockSpec(tab.shape, lambda i: (0, 0, 0)),
                  pl.BlockSpec(ln.shape, lambda i: (0, 0))],
        out_specs=[pl.BlockSpec((tm, d), row), pl.BlockSpec((tm, d), row)],
        out_shape=[jax.ShapeDtypeStruct((n, d), F32), jax.ShapeDtypeStruct((n, d), BF16)],
        compiler_params=_cparams(("arbitrary",)),
        name="combine",
    )(y_g, gates, x, tab, ln)


def _moe(h, idx_t, gates_t, w1, b1, w2, b2):
    n, d = h.shape
    e = w1.shape[0]
    m = n * TOP_K
    rank_t, cnt = _rank(idx_t, e)
    sizes = cnt[:, 0].astype(jnp.int32)
    padded = (sizes + MOE_BLOCK - 1) // MOE_BLOCK * MOE_BLOCK
    pad_ends = jnp.cumsum(padded)
    pad_starts = pad_ends - padded
    dest_t = pad_starts[idx_t] + rank_t
    n_blocks = (m + e * (MOE_BLOCK - 1)) // MOE_BLOCK + 1
    block_start = jnp.arange(n_blocks, dtype=jnp.int32) * MOE_BLOCK
    block_expert = jnp.minimum(jnp.searchsorted(pad_ends, block_start, side='right'), e - 1).astype(jnp.int32)
    n_used = (pad_ends[-1:] // MOE_BLOCK).astype(jnp.int32)
    tok = jnp.broadcast_to(jnp.arange(n, dtype=jnp.int32)[None, :], (TOP_K, n))
    src = jnp.zeros((n_blocks * MOE_BLOCK,), jnp.int32).at[dest_t.reshape(-1)].set(tok.reshape(-1))
    x_pad = jnp.take(h, src, axis=0)
    y_pad = _experts(block_expert, n_used, x_pad, w1, b1, w2, b2)
    return jnp.take(y_pad, dest_t, axis=0)


def _rope_tables(t_lat, n_ctx):
    t = jnp.arange(t_lat)
    row = (t // GRID_W).astype(F32)
    col = (t % GRID_W).astype(F32)
    nf = GLA_DK // 4
    freqs = ROPE_BASE ** (-jnp.arange(nf, dtype=F32) / nf)
    ang = jnp.concatenate([row[:, None] * freqs, col[:, None] * freqs], axis=-1)
    cos, sin = jnp.cos(ang), jnp.sin(ang)
    cos2 = jnp.concatenate([cos, cos], axis=-1)
    sin2 = jnp.concatenate([-sin, sin], axis=-1)
    return (jnp.concatenate([cos2, jnp.ones((n_ctx, GLA_DK), F32)], axis=0),
            jnp.concatenate([sin2, jnp.zeros((n_ctx, GLA_DK), F32)], axis=0))


def _chunk_tri(tg, rev):
    t = np.arange(tg)
    same = (t[:, None] // GLA_CHUNK) == (t[None, :] // GLA_CHUNK)
    side = (t[None, :] >= t[:, None]) if rev else (t[None, :] <= t[:, None])
    return jnp.asarray((same & side).astype(np.float32))


def _table(mods, rows, batch):
    lat = jnp.stack([mods[:batch, r] for r in rows], axis=1)
    ctx = jnp.broadcast_to(jnp.stack([mods[batch, r] for r in rows], axis=0)[None], lat.shape)
    tab = jnp.stack([lat, ctx], axis=1).reshape(2 * batch, len(rows), -1)
    return jnp.pad(tab, ((0, 0), (0, 8 - len(rows)), (0, 0)))


@jax.jit
def _forward(x, c, ctx, c_ctx, ada_w, ada_b, ln_g, ln_b, ab_w_in, ab_pool_w, ab_pool_scale, ab_rpb,
             ab_w_out, gla_w_in, gla_w_gate, gla_b_gate, gla_norm_g, gla_w_out, router_w, router_b,
             exp_w1, exp_b1, exp_w2, exp_b2):
    batch, t_lat, d = x.shape
    n_ctx = ctx.shape[1]
    depth = ada_w.shape[0]
    assert d == D_MODEL and n_ctx == TOK_TILE and t_lat % TOK_TILE == 0
    rows = t_lat // GRID_W
    assert rows % NA_QROWS == 0 and rows >= NA_KROWS + NA_QROWS
    n_lat = t_lat // TOK_TILE
    tpb = n_lat + 1
    l = t_lat + n_ctx
    n = batch * l
    alpha = (2.0 * depth) ** 0.25

    cc = jnp.concatenate([c, c_ctx[None], jnp.zeros((16 - batch - 1, d), F32)], axis=0)
    mods = _mods(cc, ada_w, ada_b).reshape(depth, 16, N_MOD, d)

    z = jnp.concatenate([x, ctx], axis=1).reshape(n, d)
    h = _modulate(z, _table(mods[0], (1, 0), batch), tpb)
    cos2, sin2 = _rope_tables(t_lat, n_ctx)

    for i in range(depth):
        j = i // 2
        last = i == depth - 1
        tab1 = _table(mods[i], (2, 4, 3), batch)
        ln1 = jnp.stack([ln_g[i, 0], ln_b[i, 0]])
        ln2 = jnp.stack([ln_g[i, 1], ln_b[i, 1]])
        rw_t = router_w[i].T
        rb = router_b[i][:, None]
        if i % 2 == 0:
            u, qkv = _proj(h, ab_w_in[j].astype(BF16), [(0, POOL_WIDTH), (POOL_WIDTH, POOL_WIDTH + 3 * NA_WIDTH)],
                           [F32, BF16])
            w_blk = jax.scipy.linalg.block_diag(*[ab_pool_w[j, g] for g in range(len(POOL_WINDOWS))])
            pooled = _pool(u.reshape(batch, l, POOL_WIDTH), w_blk.astype(BF16), ab_pool_scale[j][None, :],
                           n_lat, t_lat, n_ctx)
            bias = _na_bias_tables(ab_rpb[j], rows, n_ctx)
            attn = _na(qkv.reshape(batch, l, 3 * NA_WIDTH), bias, n_lat)
            w_out = ab_w_out[j].astype(BF16)
            acts = [pooled.reshape(n, POOL_WIDTH), attn.reshape(n, NA_WIDTH)]
            ws = [w_out[:POOL_WIDTH], w_out[POOL_WIDTH:]]
        else:
            qk, v, r, g = _proj(h, gla_w_in[j].astype(BF16),
                                [(0, 2 * GLA_QK), (2 * GLA_QK, 2 * GLA_QK + GLA_V),
                                 (2 * GLA_QK + GLA_V, 2 * GLA_QK + 2 * GLA_V),
                                 (2 * GLA_QK + 2 * GLA_V, 2 * GLA_QK + 2 * GLA_V + 2 * GATE_RANK)],
                                [F32, BF16, F32, F32])
            o_dirs = []
            for rev in (False, True):
                dr = int(rev)
                qe, kd, a, gd = _gla_prep(rev, qk, g, cos2, sin2, gla_w_gate[j, dr], gla_b_gate[j, dr][None, :],
                                          _chunk_tri(TOK_TILE, rev), tpb)
                o_dirs.append(_gla_scan(rev, qe, kd, a, gd, v, batch, tpb))
            acts = [_gla_out(o_dirs[0], o_dirs[1], r, gla_norm_g[j][None, :])]
            ws = [gla_w_out[j].astype(BF16)]
        z, h, idx_t, gates_t = _post(acts, ws, z, tab1, ln1, rw_t, rb, tpb, alpha)
        y_g = _moe(h, idx_t, gates_t, exp_w1[i].astype(BF16), exp_b1[i], exp_w2[i].astype(BF16), exp_b2[i])
        nxt = mods[i + 1] if not last else mods[i]
        tab2 = _table(jnp.concatenate([mods[i][:, 5:6], nxt[:, 1:2], nxt[:, 0:1]], axis=1), (0, 1, 2), batch)
        z, h = _combine(y_g, gates_t.T, z, tab2, ln2, tpb, alpha)
    return z.reshape(batch, l, d)[:, :t_lat]


def kernel(x, c, ctx, c_ctx, ada_w, ada_b, ln_g, ln_b, ab_w_in, ab_pool_w, ab_pool_scale, ab_rpb, ab_w_out,
           gla_w_in, gla_w_gate, gla_b_gate, gla_norm_g, gla_w_out, router_w, router_b, exp_w1, exp_b1, exp_w2,
           exp_b2):
    return _forward(x, c, ctx, c_ctx, ada_w, ada_b, ln_g, ln_b, ab_w_in, ab_pool_w, ab_pool_scale, ab_rpb,
                    ab_w_out, gla_w_in, gla_w_gate, gla_b_gate, gla_norm_g, gla_w_out, router_w, router_b,
                    exp_w1, exp_b1, exp_w2, exp_b2)
```

```python
import functools
import math

import numpy as np
import jax
import jax.numpy as jnp
from jax import lax
from jax.experimental import pallas as pl
from jax.experimental.pallas import tpu as pltpu
from jax.experimental.pallas import tpu_sc as plsc

F32 = jnp.float32
BF16 = jnp.bfloat16
HIGHEST = lax.Precision.HIGHEST

D_MODEL = 1024
GRID_W = 64
N_MOD = 6
POOL_WINDOWS = (2, 4, 8, 16)
POOL_WIDTH = D_MODEL // 4
POOL_GROUP_DIM = POOL_WIDTH // len(POOL_WINDOWS)
POOL_HALO = max(POOL_WINDOWS) // 2
NA_HEAD_DIM = 64
NA_HEADS = (D_MODEL - POOL_WIDTH) // NA_HEAD_DIM
NA_WIDTH = NA_HEADS * NA_HEAD_DIM
WIN_H = 8
WIN_W = 16
GLA_HEADS = 4
GLA_DK = D_MODEL // 2 // GLA_HEADS
GLA_DV = D_MODEL // GLA_HEADS
GATE_RANK = 16
GATE_NORM = 16.0
GLA_CHUNK = 64
GLA_SUB = 16
GLA_QK = GLA_HEADS * GLA_DK
GLA_V = GLA_HEADS * GLA_DV
ROPE_BASE = 10000.0
TOP_K = 4
SWIGLU_LIMIT = 7.0
SWIGLU_ALPHA = 1.702
MOE_BLOCK = 256
LN_EPS = 1e-5
RMS_EPS = 1e-6
NEG_INF = -1e30

LANES = 128
TOK_TILE = 256
NA_QROWS = 4
NA_KROWS = 12
VMEM_LIMIT = 48 * 1024 * 1024
EXPERT_VMEM_LIMIT = 56 * 1024 * 1024
SC_WINDOW = 128
SC_SUBROW = 256


def _cparams(sem):
    return pltpu.CompilerParams(dimension_semantics=sem, vmem_limit_bytes=VMEM_LIMIT)


def _row_tile(n):
    return 2 * TOK_TILE if n % (2 * TOK_TILE) == 0 else TOK_TILE


def _mods_kernel(c_ref, w_ref, b_ref, o_ref):
    cv = c_ref[...]
    sc = cv * jax.nn.sigmoid(cv)
    o_ref[0] = jnp.dot(sc, w_ref[0], precision=HIGHEST, preferred_element_type=F32) + b_ref[0]


def _mods(cc, ada_w, ada_b):
    depth, d, n = ada_w.shape
    r = cc.shape[0]
    tn = n // 4
    return pl.pallas_call(
        _mods_kernel,
        grid=(depth, n // tn),
        in_specs=[pl.BlockSpec((r, d), lambda i, j: (0, 0)),
                  pl.BlockSpec((1, d, tn), lambda i, j: (i, 0, j)),
                  pl.BlockSpec((1, 1, tn), lambda i, j: (i, 0, j))],
        out_specs=pl.BlockSpec((1, r, tn), lambda i, j: (i, 0, j)),
        out_shape=jax.ShapeDtypeStruct((depth, r, n), F32),
        compiler_params=_cparams(("arbitrary", "arbitrary")),
        name="mods",
    )(cc, ada_w, ada_b.reshape(depth, 1, n))


def _tab_row(g, tpb):
    return (g // tpb) * 2 + (g % tpb == tpb - 1).astype(jnp.int32)


def _modulate_kernel(tpb, x_ref, tab_ref, h_ref):
    t = tab_ref[_tab_row(pl.program_id(0), tpb)]
    h_ref[...] = (x_ref[...] * (1.0 + t[0:1]) + t[1:2]).astype(h_ref.dtype)


def _modulate(z, tab, tpb):
    n, d = z.shape
    return pl.pallas_call(
        functools.partial(_modulate_kernel, tpb),
        grid=(n // TOK_TILE,),
        in_specs=[pl.BlockSpec((TOK_TILE, d), lambda i: (i, 0)),
                  pl.BlockSpec(tab.shape, lambda i: (0, 0, 0))],
        out_specs=pl.BlockSpec((TOK_TILE, d), lambda i: (i, 0)),
        out_shape=jax.ShapeDtypeStruct((n, d), BF16),
        compiler_params=_cparams(("arbitrary",)),
        name="modulate",
    )(z, tab)


def _proj_kernel(splits, x_ref, w_ref, *out_refs):
    x = x_ref[...]
    for (a, b), o_ref in zip(splits, out_refs):
        o_ref[...] = jnp.dot(x, w_ref[:, a:b], preferred_element_type=F32).astype(o_ref.dtype)


def _proj(h, w, splits, dtypes):
    n, k = h.shape
    m = w.shape[1]
    tm = _row_tile(n)
    return pl.pallas_call(
        functools.partial(_proj_kernel, tuple(splits)),
        grid=(n // tm,),
        in_specs=[pl.BlockSpec((tm, k), lambda i: (i, 0)),
                  pl.BlockSpec((k, m), lambda i: (0, 0))],
        out_specs=[pl.BlockSpec((tm, b - a), lambda i: (i, 0)) for a, b in splits],
        out_shape=[jax.ShapeDtypeStruct((n, b - a), dt) for (a, b), dt in zip(splits, dtypes)],
        compiler_params=_cparams(("arbitrary",)),
        name="proj",
    )(h, w)


def _pool_kernel(n_lat, t_lat, t_ctx, prev_ref, cur_ref, next_ref, w_ref, scale_ref, o_ref, halo_ref):
    j = pl.program_id(1)
    is_ctx = j == n_lat
    has_prev = jnp.logical_and(j > 0, jnp.logical_not(is_ctx))
    has_next = j < n_lat - 1
    cur = cur_ref[0]
    hl = POOL_HALO
    halo_ref[0:hl] = jnp.where(has_prev, prev_ref[0, TOK_TILE - hl:TOK_TILE], 0.0)
    halo_ref[hl:hl + TOK_TILE] = cur
    halo_ref[hl + TOK_TILE:2 * hl + TOK_TILE] = jnp.where(has_next, next_ref[0, 0:hl], 0.0)

    shape = cur.shape
    lane = lax.broadcasted_iota(jnp.int32, shape, 1)
    group = lane // POOL_GROUP_DIM
    half = jnp.ones(shape, jnp.int32)
    for gi, wdw in enumerate(POOL_WINDOWS):
        half = jnp.where(group == gi, wdw // 2, half)
    acc = jnp.zeros(shape, F32)
    for off in range(-hl, hl):
        v = halo_ref[hl + off:hl + off + TOK_TILE]
        inside = (half >= -off) if off < 0 else (half > off)
        acc = acc + jnp.where(inside, v, 0.0)
    pos0 = jnp.where(is_ctx, 0, j * TOK_TILE)
    seq = jnp.where(is_ctx, t_ctx, t_lat)
    t = pos0 + lax.broadcasted_iota(jnp.int32, shape, 0)
    cnt = jnp.minimum(t + half, seq) - jnp.maximum(t - half, 0)
    pooled = acc / cnt.astype(F32) - cur
    y = jnp.dot(pooled.astype(BF16), w_ref[...], preferred_element_type=F32) * scale_ref[...]
    o_ref[0] = y.astype(o_ref.dtype)


def _pool(u, w_blk, scale, n_lat, t_lat, t_ctx):
    b, l, pw = u.shape
    tpb = l // TOK_TILE
    blk = (1, TOK_TILE, pw)
    return pl.pallas_call(
        functools.partial(_pool_kernel, n_lat, t_lat, t_ctx),
        grid=(b, tpb),
        in_specs=[pl.BlockSpec(blk, lambda bi, j: (bi, jnp.maximum(j - 1, 0), 0)),
                  pl.BlockSpec(blk, lambda bi, j: (bi, j, 0)),
                  pl.BlockSpec(blk, lambda bi, j: (bi, jnp.minimum(j + 1, tpb - 1), 0)),
                  pl.BlockSpec((pw, pw), lambda bi, j: (0, 0)),
                  pl.BlockSpec((1, pw), lambda bi, j: (0, 0))],
        out_specs=pl.BlockSpec(blk, lambda bi, j: (bi, j, 0)),
        out_shape=jax.ShapeDtypeStruct((b, l, pw), BF16),
        scratch_shapes=[pltpu.VMEM((TOK_TILE + 2 * POOL_HALO, pw), F32)],
        compiler_params=_cparams(("arbitrary", "arbitrary")),
        name="pool",
    )(u, u, u, w_blk, scale)


def _na_bias_tables(rpb, rows, n_ctx):
    n_i = rows // NA_QROWS
    a = np.repeat(np.arange(NA_QROWS), GRID_W)
    cq = np.tile(np.arange(GRID_W), NA_QROWS)
    kr = np.repeat(np.arange(NA_KROWS), GRID_W)
    ck = np.tile(np.arange(GRID_W), NA_KROWS)
    tabs = []
    for i in (0, 1, n_i - 1):
        start = int(np.clip(NA_QROWS * i - WIN_H // 2, 0, rows - NA_KROWS))
        r = NA_QROWS * i + a
        krow = start + kr
        rs = np.clip(r - WIN_H // 2, 0, rows - WIN_H)
        ok_row = (krow[None, :] >= rs[:, None]) & (krow[None, :] < rs[:, None] + WIN_H)
        ws = np.clip(cq - WIN_W // 2, 0, GRID_W - WIN_W)
        ok_col = (ck[None, :] >= ws[:, None]) & (ck[None, :] < ws[:, None] + WIN_W)
        drow = np.clip(krow[None, :] - r[:, None] + WIN_H - 1, 0, 2 * WIN_H - 2)
        dcol = np.clip(ck[None, :] - cq[:, None] + WIN_W - 1, 0, 2 * WIN_W - 2)
        tabs.append(jnp.where((ok_row & ok_col)[None], rpb[:, drow, dcol], NEG_INF))
    tabs.append(jnp.full_like(tabs[0], NEG_INF))
    band = jnp.stack(tabs).astype(F32)
    return jnp.concatenate([band, jnp.zeros(band.shape[:3] + (n_ctx,), F32)], axis=-1)


def _na_kernel(q_ref, k0_ref, k1_ref, k2_ref, kc_ref, v0_ref, v1_ref, v2_ref, vc_ref, bias_ref, o_ref):
    q = q_ref[0]
    kcat = jnp.concatenate([k0_ref[0], k1_ref[0], k2_ref[0], kc_ref[0]], axis=0)
    vcat = jnp.concatenate([v0_ref[0], v1_ref[0], v2_ref[0], vc_ref[0]], axis=0)
    lane = lax.broadcasted_iota(jnp.int32, q.shape, 1)
    first = lane < NA_HEAD_DIM
    outs = []
    for hh in range(2):
        qm = jnp.where(first if hh == 0 else jnp.logical_not(first), q, jnp.zeros_like(q))
        s = lax.dot_general(qm, kcat, (((1,), (1,)), ((), ())), preferred_element_type=F32)
        s = s * NA_HEAD_DIM ** -0.5 + bias_ref[0, hh]
        m = jnp.max(s, axis=-1, keepdims=True)
        p = jnp.exp(s - m)
        l = jnp.sum(p, axis=-1, keepdims=True)
        outs.append(jnp.dot(p.astype(BF16), vcat, preferred_element_type=F32) / l)
    o_ref[0] = jnp.where(first, outs[0], outs[1]).astype(o_ref.dtype)


def _na(qkv, bias, n_lat):
    b, l, _ = qkv.shape
    tpb = l // TOK_TILE
    n_pairs = NA_WIDTH // LANES
    blk = (1, TOK_TILE, LANES)

    def kstart(i):
        return jnp.clip(i - 1, 0, n_lat - NA_KROWS // NA_QROWS)

    def btype(i):
        return jnp.where(i == 0, 0, jnp.where(i == n_lat - 1, 2, jnp.where(i == n_lat, 3, 1)))

    def kv_spec(col0, j):
        return pl.BlockSpec(blk, lambda hp, i, bi: (bi, kstart(i) + j, col0 + hp))

    def ctx_spec(col0):
        return pl.BlockSpec(blk, lambda hp, i, bi: (bi, n_lat, col0 + hp))

    nk = bias.shape[-1]
    return pl.pallas_call(
        _na_kernel,
        grid=(n_pairs, tpb, b),
        in_specs=[pl.BlockSpec(blk, lambda hp, i, bi: (bi, i, hp)),
                  kv_spec(n_pairs, 0), kv_spec(n_pairs, 1), kv_spec(n_pairs, 2), ctx_spec(n_pairs),
                  kv_spec(2 * n_pairs, 0), kv_spec(2 * n_pairs, 1), kv_spec(2 * n_pairs, 2), ctx_spec(2 * n_pairs),
                  pl.BlockSpec((1, 2, TOK_TILE, nk), lambda hp, i, bi: (btype(i), hp, 0, 0))],
        out_specs=pl.BlockSpec(blk, lambda hp, i, bi: (bi, i, hp)),
        out_shape=jax.ShapeDtypeStruct((b, l, NA_WIDTH), BF16),
        compiler_params=_cparams(("arbitrary", "arbitrary", "arbitrary")),
        name="na",
    )(qkv, qkv, qkv, qkv, qkv, qkv, qkv, qkv, qkv, bias)


def _log_sigmoid(z):
    return jnp.minimum(z, 0.0) - jnp.log(1.0 + jnp.exp(-jnp.abs(z)))


def _gla_prep_kernel(rev, q_ref, k_ref, g_ref, cos_ref, sin_ref, wg_ref, bg_ref, tri_ref,
                     qe_ref, kd_ref, a_ref, gd_ref, b_scr, qr_scr, kr_scr):
    tg = q_ref.shape[0]
    gcol = GATE_RANK if rev else 0
    gg = g_ref[:, gcol:gcol + GATE_RANK]
    z = jnp.dot(gg, wg_ref[...], precision=HIGHEST, preferred_element_type=F32) + bg_ref[...]
    la = _log_sigmoid(z) * (1.0 / GATE_NORM)
    b = jnp.dot(tri_ref[...], la, precision=HIGHEST, preferred_element_type=F32)
    cosv = cos_ref[...]
    sinv = sin_ref[...]
    q = q_ref[...]
    k = k_ref[...]
    qr = (q * cosv + pltpu.roll(q, GLA_DK // 2, 1) * sinv) * GLA_DK ** -0.5
    kr = k * cosv + pltpu.roll(k, GLA_DK // 2, 1) * sinv
    qe_ref[...] = (qr * jnp.exp(b)).astype(qe_ref.dtype)
    for c in range(tg // GLA_CHUNK):
        r0 = c * GLA_CHUNK
        last = r0 if rev else r0 + GLA_CHUNK - 1
        tot = b[last:last + 1]
        kd_ref[r0:r0 + GLA_CHUNK] = (kr[r0:r0 + GLA_CHUNK] * jnp.exp(tot - b[r0:r0 + GLA_CHUNK])).astype(kd_ref.dtype)
        gd_ref[0, 0, c:c + 1] = jnp.exp(tot)
    b_scr[...] = b
    qr_scr[...] = qr
    kr_scr[...] = kr

    n_sub = GLA_CHUNK // GLA_SUB
    colio = lax.broadcasted_iota(jnp.int32, (GLA_SUB, GLA_CHUNK), 1)
    rowio = lax.broadcasted_iota(jnp.int32, (GLA_SUB, GLA_CHUNK), 0)

    def sub_block(sb, carry):
        r0 = pl.multiple_of(sb * GLA_SUB, GLA_SUB)
        c0 = pl.multiple_of((sb // n_sub) * GLA_CHUNK, GLA_CHUNK)
        d0 = (sb % n_sub) * GLA_SUB
        bq = b_scr[pl.ds(r0, GLA_SUB), :]
        qq = qr_scr[pl.ds(r0, GLA_SUB), :]
        ref_row = jnp.minimum(r0 + GLA_SUB, tg - 1) if rev else jnp.maximum(r0 - 1, 0)
        rb = b_scr[pl.ds(ref_row, 1), :]
        bc = b_scr[pl.ds(c0, GLA_CHUNK), :]
        kc = kr_scr[pl.ds(c0, GLA_CHUNK), :]
        qt = qq * jnp.exp(jnp.minimum(bq - rb, 0.0))
        kt = kc * jnp.exp(jnp.minimum(rb - bc, 0.0))
        off = lax.dot_general(qt.astype(BF16), kt.astype(BF16), (((1,), (1,)), ((), ())),
                              preferred_element_type=F32)
        dacc = jnp.zeros((GLA_SUB, GLA_CHUNK), F32)
        for s in range(GLA_SUB):
            ks = kr_scr[pl.ds(r0 + s, 1), :]
            bs = b_scr[pl.ds(r0 + s, 1), :]
            col = jnp.sum(qq * ks * jnp.exp(jnp.minimum(bq - bs, 0.0)), axis=1, keepdims=True)
            dacc = jnp.where(colio == d0 + s, col, dacc)
        dcol = colio - d0
        in_diag = jnp.logical_and(dcol >= 0, dcol < GLA_SUB)
        if rev:
            ok_off = colio >= d0 + GLA_SUB
            ok_diag = jnp.logical_and(in_diag, dcol >= rowio)
        else:
            ok_off = colio < d0
            ok_diag = jnp.logical_and(in_diag, dcol <= rowio)
        a_blk = jnp.where(ok_off, off, jnp.where(ok_diag, dacc, 0.0))
        a_ref[0, pl.ds(r0, GLA_SUB), :] = a_blk.astype(a_ref.dtype)
        return carry

    lax.fori_loop(0, tg // GLA_SUB, sub_block, 0)


def _gla_prep(rev, qk, g, cos2, sin2, w_gate, b_gate, tri, tpb):
    n = qk.shape[0]
    tg = TOK_TILE
    nt = n // tg
    return pl.pallas_call(
        functools.partial(_gla_prep_kernel, rev),
        grid=(GLA_HEADS, nt),
        in_specs=[pl.BlockSpec((tg, GLA_DK), lambda hd, t: (t, hd)),
                  pl.BlockSpec((tg, GLA_DK), lambda hd, t: (t, GLA_HEADS + hd)),
                  pl.BlockSpec((tg, 2 * GATE_RANK), lambda hd, t: (t, 0)),
                  pl.BlockSpec((tg, GLA_DK), lambda hd, t: (t % tpb, 0)),
                  pl.BlockSpec((tg, GLA_DK), lambda hd, t: (t % tpb, 0)),
                  pl.BlockSpec((GATE_RANK, GLA_DK), lambda hd, t: (0, hd)),
                  pl.BlockSpec((1, GLA_DK), lambda hd, t: (0, hd)),
                  pl.BlockSpec((tg, tg), lambda hd, t: (0, 0))],
        out_specs=[pl.BlockSpec((tg, GLA_DK), lambda hd, t: (t, hd)),
                   pl.BlockSpec((tg, GLA_DK), lambda hd, t: (t, hd)),
                   pl.BlockSpec((1, tg, GLA_CHUNK), lambda hd, t: (hd, t, 0)),
                   pl.BlockSpec((1, 1, tg // GLA_CHUNK, GLA_DK), lambda hd, t: (hd, t, 0, 0))],
        out_shape=[jax.ShapeDtypeStruct((n, GLA_QK), BF16),
                   jax.ShapeDtypeStruct((n, GLA_QK), BF16),
                   jax.ShapeDtypeStruct((GLA_HEADS, n, GLA_CHUNK), BF16),
                   jax.ShapeDtypeStruct((GLA_HEADS, nt, tg // GLA_CHUNK, GLA_DK), F32)],
        scratch_shapes=[pltpu.VMEM((tg, GLA_DK), F32)] * 3,
        compiler_params=_cparams(("arbitrary", "arbitrary")),
        name="gla_prep_bwd" if rev else "gla_prep_fwd",
    )(qk, qk, g, cos2, sin2, w_gate, b_gate, tri)


def _gla_scan_kernel(rev, qe_ref, kd_ref, a_ref, gd_ref, v_ref, o_ref, st_ref):
    @pl.when(pl.program_id(1) == 0)
    def _():
        st_ref[...] = jnp.zeros_like(st_ref)

    n_chunks = qe_ref.shape[0] // GLA_CHUNK
    for hd in range(GLA_HEADS):
        st = st_ref[hd]
        kcols = slice(hd * GLA_DK, (hd + 1) * GLA_DK)
        vcols = slice(hd * GLA_DV, (hd + 1) * GLA_DV)
        for cc in range(n_chunks):
            c = n_chunks - 1 - cc if rev else cc
            rows = slice(c * GLA_CHUNK, (c + 1) * GLA_CHUNK)
            v_c = v_ref[rows, vcols]
            o = lax.dot_general(qe_ref[rows, kcols], st.astype(BF16), (((1,), (1,)), ((), ())),
                                preferred_element_type=F32)
            o = o + jnp.dot(a_ref[hd, rows, :], v_c, preferred_element_type=F32)
            o_ref[rows, vcols] = o
            upd = lax.dot_general(v_c, kd_ref[rows, kcols], (((0,), (0,)), ((), ())),
                                  preferred_element_type=F32)
            st = st * gd_ref[hd, 0, c:c + 1, :] + upd
        st_ref[hd] = st


def _gla_scan(rev, qe, kd, a, gd, v, batch, tpb):
    n = qe.shape[0]
    tg = TOK_TILE
    n_lat = tpb - 1

    def tile(bi, s):
        lat = n_lat - s if rev else s - 1
        return bi * tpb + jnp.where(s == 0, n_lat, lat)

    return pl.pallas_call(
        functools.partial(_gla_scan_kernel, rev),
        grid=(batch, tpb),
        in_specs=[pl.BlockSpec((tg, GLA_QK), lambda bi, s: (tile(bi, s), 0)),
                  pl.BlockSpec((tg, GLA_QK), lambda bi, s: (tile(bi, s), 0)),
                  pl.BlockSpec((GLA_HEADS, tg, GLA_CHUNK), lambda bi, s: (0, tile(bi, s), 0)),
                  pl.BlockSpec((GLA_HEADS, 1, tg // GLA_CHUNK, GLA_DK), lambda bi, s: (0, tile(bi, s), 0, 0)),
                  pl.BlockSpec((tg, GLA_V), lambda bi, s: (tile(bi, s), 0))],
        out_specs=pl.BlockSpec((tg, GLA_V), lambda bi, s: (tile(bi, s), 0)),
        out_shape=jax.ShapeDtypeStruct((n, GLA_V), F32),
        scratch_shapes=[pltpu.VMEM((GLA_HEADS, GLA_DV, GLA_DK), F32)],
        compiler_params=_cparams(("arbitrary", "arbitrary")),
        name="gla_scan_bwd" if rev else "gla_scan_fwd",
    )(qe, kd, a, gd, v)


def _gla_out_kernel(of_ref, ob_ref, r_ref, g_ref, o_ref):
    o = of_ref[...] + ob_ref[...]
    r = r_ref[...]
    gate = r * jax.nn.sigmoid(r)
    gn = g_ref[...]
    for hd in range(GLA_HEADS):
        cols = slice(hd * GLA_DV, (hd + 1) * GLA_DV)
        oh = o[:, cols]
        ms = jnp.mean(oh * oh, axis=-1, keepdims=True)
        o_ref[:, cols] = (oh * lax.rsqrt(ms + RMS_EPS) * gn * gate[:, cols]).astype(o_ref.dtype)


def _gla_out(o_f, o_b, r, norm_g):
    n, dv = o_f.shape
    tm = _row_tile(n)
    spec = pl.BlockSpec((tm, dv), lambda i: (i, 0))
    return pl.pallas_call(
        _gla_out_kernel,
        grid=(n // tm,),
        in_specs=[spec, spec, spec, pl.BlockSpec((1, GLA_DV), lambda i: (0, 0))],
        out_specs=spec,
        out_shape=jax.ShapeDtypeStruct((n, dv), BF16),
        compiler_params=_cparams(("arbitrary",)),
        name="gla_out",
    )(o_f, o_b, r, norm_g)


def _residual_ln(x, a, t, ln, alpha):
    y = alpha * x + t[0:1] * a
    mu = jnp.mean(y, axis=-1, keepdims=True)
    yc = y - mu
    var = jnp.mean(yc * yc, axis=-1, keepdims=True)
    xn = yc * lax.rsqrt(var + LN_EPS) * ln[0:1] + ln[1:2]
    return xn, xn * (1.0 + t[1:2]) + t[2:3]


def _top4_softmax(lt):
    e = lt.shape[0]
    io = lax.broadcasted_iota(jnp.int32, lt.shape, 0)
    work = lt
    idxs, vals = [], []
    for _ in range(TOP_K):
        m = jnp.max(work, axis=0, keepdims=True)
        ik = jnp.min(jnp.where(work == m, io, e), axis=0, keepdims=True)
        idxs.append(ik)
        vals.append(m)
        work = jnp.where(io == ik, -jnp.inf, work)
    ex = [jnp.exp(v - vals[0]) for v in vals]
    den = ex[0] + ex[1] + ex[2] + ex[3]
    return jnp.concatenate(idxs, axis=0), jnp.concatenate([x / den for x in ex], axis=0)


def _post_kernel(n_act, tpb, alpha, *refs):
    acts = refs[:n_act]
    ws = refs[n_act:2 * n_act]
    x_ref, tab_ref, ln_ref, rw_ref, rb_ref, xo_ref, h_ref, idx_ref, gate_ref = refs[2 * n_act:]
    tm = x_ref.shape[0]
    a = jnp.dot(acts[0][...], ws[0][...], preferred_element_type=F32)
    for k in range(1, n_act):
        a = a + jnp.dot(acts[k][...], ws[k][...], preferred_element_type=F32)
    ln = ln_ref[...]
    for s in range(tm // TOK_TILE):
        rows = slice(s * TOK_TILE, (s + 1) * TOK_TILE)
        t = tab_ref[_tab_row(pl.program_id(0) * (tm // TOK_TILE) + s, tpb)]
        xn, h = _residual_ln(x_ref[rows], a[rows], t, ln, alpha)
        xo_ref[rows] = xn
        h_ref[rows] = h.astype(h_ref.dtype)
        lt = lax.dot_general(rw_ref[...], h, (((1,), (1,)), ((), ())), precision=HIGHEST,
                             preferred_element_type=F32) + rb_ref[...]
        idx, gates = _top4_softmax(lt)
        idx_ref[:, rows] = idx
        gate_ref[:, rows] = gates


def _post(acts, ws, x, tab, ln, rw_t, rb, tpb, alpha):
    n, d = x.shape
    tm = _row_tile(n)
    e = rw_t.shape[0]
    row = lambda i: (i, 0)
    fixed = lambda i: (0, 0)
    return pl.pallas_call(
        functools.partial(_post_kernel, len(acts), tpb, alpha),
        grid=(n // tm,),
        in_specs=([pl.BlockSpec((tm, a.shape[1]), row) for a in acts]
                  + [pl.BlockSpec(w.shape, fixed) for w in ws]
                  + [pl.BlockSpec((tm, d), row),
                     pl.BlockSpec(tab.shape, lambda i: (0, 0, 0)),
                     pl.BlockSpec(ln.shape, fixed),
                     pl.BlockSpec((e, d), fixed),
                     pl.BlockSpec((e, 1), fixed)]),
        out_specs=[pl.BlockSpec((tm, d), row), pl.BlockSpec((tm, d), row),
                   pl.BlockSpec((TOP_K, tm), lambda i: (0, i)), pl.BlockSpec((TOP_K, tm), lambda i: (0, i))],
        out_shape=[jax.ShapeDtypeStruct((n, d), F32), jax.ShapeDtypeStruct((n, d), F32),
                   jax.ShapeDtypeStruct((TOP_K, n), jnp.int32), jax.ShapeDtypeStruct((TOP_K, n), F32)],
        compiler_params=_cparams(("arbitrary",)),
        name="post",
    )(*acts, *ws, x, tab, ln, rw_t, rb)


def _rank_kernel(idx_ref, tri_ref, rank_ref, cnt_ref, carry_ref):
    @pl.when(pl.program_id(0) == 0)
    def _():
        carry_ref[...] = jnp.zeros_like(carry_ref)

    idx = idx_ref[...]
    e = carry_ref.shape[0]
    tr = idx.shape[1]
    io = lax.broadcasted_iota(jnp.int32, (e, tr), 0)
    chosen = jnp.zeros((e, tr), F32)
    for k in range(TOP_K):
        chosen = chosen + (idx[k:k + 1] == io).astype(F32)
    cum = jnp.dot(chosen.astype(BF16), tri_ref[...], preferred_element_type=F32)
    base = carry_ref[:, 0:1]
    excl = base + cum - chosen
    ranks = [jnp.sum(jnp.where(idx[k:k + 1] == io, excl, 0.0), axis=0, keepdims=True) for k in range(TOP_K)]
    rank_ref[...] = jnp.concatenate(ranks, axis=0).astype(jnp.int32)
    carry_ref[...] = carry_ref[...] + jnp.sum(chosen, axis=1, keepdims=True)
    cnt_ref[...] = carry_ref[...]


def _rank(idx_t, n_experts):
    n = idx_t.shape[1]
    tr = _row_tile(n)
    tri = (np.arange(tr)[:, None] <= np.arange(tr)[None, :]).astype(np.float32)
    return pl.pallas_call(
        _rank_kernel,
        grid=(n // tr,),
        in_specs=[pl.BlockSpec((TOP_K, tr), lambda i: (0, i)),
                  pl.BlockSpec((tr, tr), lambda i: (0, 0))],
        out_specs=[pl.BlockSpec((TOP_K, tr), lambda i: (0, i)),
                   pl.BlockSpec((n_experts, LANES), lambda i: (0, 0))],
        out_shape=[jax.ShapeDtypeStruct((TOP_K, n), jnp.int32),
                   jax.ShapeDtypeStruct((n_experts, LANES), F32)],
        scratch_shapes=[pltpu.VMEM((n_experts, LANES), F32)],
        compiler_params=_cparams(("arbitrary",)),
        name="rank",
    )(idx_t, jnp.asarray(tri, BF16))


def _expert_kernel(be_ref, nb_ref, x_ref, w1_ref, b1_ref, w2_ref, b2_ref, o_ref, w1b_ref, w2b_ref):
    i = pl.program_id(0)
    used = i < nb_ref[0]
    new_expert = jnp.logical_or(i == 0, be_ref[i] != be_ref[jnp.maximum(i - 1, 0)])

    @pl.when(jnp.logical_and(used, new_expert))
    def _():
        w1b_ref[...] = w1_ref[0].astype(BF16)
        w2b_ref[...] = w2_ref[0].astype(BF16)

    @pl.when(used)
    def _():
        hid = jnp.dot(x_ref[...].astype(BF16), w1b_ref[...], preferred_element_type=F32) + b1_ref[0]
        half = hid.shape[1] // 2
        glu = jnp.minimum(hid[:, :half], SWIGLU_LIMIT)
        lin = jnp.clip(hid[:, half:], -SWIGLU_LIMIT, SWIGLU_LIMIT)
        act = glu * jax.nn.sigmoid(SWIGLU_ALPHA * glu) * (lin + 1.0)
        o_ref[...] = jnp.dot(act.astype(BF16), w2b_ref[...], preferred_element_type=F32) + b2_ref[0]

    @pl.when(i >= nb_ref[0])
    def _():
        o_ref[...] = jnp.zeros_like(o_ref)


def _experts(block_expert, n_used, x_pad, layer, w1, b1, w2, b2):
    n_pad, d = x_pad.shape
    depth, e, _, dh2 = w1.shape
    n_blocks = n_pad // MOE_BLOCK
    grid_spec = pltpu.PrefetchScalarGridSpec(
        num_scalar_prefetch=2,
        grid=(n_blocks,),
        in_specs=[pl.BlockSpec((MOE_BLOCK, d), lambda i, be, nb: (i, 0)),
                  pl.BlockSpec((None, 1, d, dh2), lambda i, be, nb: (layer, be[i], 0, 0)),
                  pl.BlockSpec((None, 1, 1, dh2), lambda i, be, nb: (layer, be[i], 0, 0)),
                  pl.BlockSpec((None, 1, dh2 // 2, d), lambda i, be, nb: (layer, be[i], 0, 0)),
                  pl.BlockSpec((None, 1, 1, d), lambda i, be, nb: (layer, be[i], 0, 0))],
        out_specs=pl.BlockSpec((MOE_BLOCK, d), lambda i, be, nb: (i, 0)),
        scratch_shapes=[pltpu.VMEM((d, dh2), BF16), pltpu.VMEM((dh2 // 2, d), BF16)],
    )
    return pl.pallas_call(
        _expert_kernel,
        grid_spec=grid_spec,
        out_shape=jax.ShapeDtypeStruct((n_pad, d), F32),
        compiler_params=pltpu.CompilerParams(dimension_semantics=("arbitrary",),
                                             vmem_limit_bytes=EXPERT_VMEM_LIMIT),
        name="experts",
    )(block_expert, n_used, x_pad, w1, b1.reshape(depth, e, 1, dh2), w2, b2.reshape(depth, e, 1, d))


def _combine_kernel(tpb, alpha, y_ref, gate_ref, x_ref, tab_ref, ln_ref, xo_ref, h_ref):
    g = gate_ref[...]
    y = y_ref[0] * g[:, 0:1]
    for k in range(1, TOP_K):
        y = y + y_ref[k] * g[:, k:k + 1]
    t = tab_ref[_tab_row(pl.program_id(0), tpb)]
    xn, h = _residual_ln(x_ref[...], y, t, ln_ref[...], alpha)
    xo_ref[...] = xn
    h_ref[...] = h.astype(h_ref.dtype)


def _combine(y_g, gates, x, tab, ln, tpb, alpha):
    n, d = x.shape
    tm = TOK_TILE
    row = lambda i: (i, 0)
    return pl.pallas_call(
        functools.partial(_combine_kernel, tpb, alpha),
        grid=(n // tm,),
        in_specs=[pl.BlockSpec((TOP_K, tm, d), lambda i: (0, i, 0)),
                  pl.BlockSpec((tm, TOP_K), row),
                  pl.BlockSpec((tm, d), row),
                  pl.BlockSpec(tab.shape, lambda i: (0, 0, 0)),
                  pl.BlockSpec(ln.shape, lambda i: (0, 0))],
        out_specs=[pl.BlockSpec((tm, d), row), pl.BlockSpec((tm, d), row)],
        out_shape=[jax.ShapeDtypeStruct((n, d), F32), jax.ShapeDtypeStruct((n, d), BF16)],
        compiler_params=_cparams(("arbitrary",)),
        name="combine",
    )(y_g, gates, x, tab, ln)


def _sc_mesh():
    return plsc.VectorSubcoreMesh(core_axis_name="c", subcore_axis_name="s")


def _sc_scatter_rows(x, idx, n_out):
    r, c = x.shape

    @functools.partial(pl.kernel, out_type=jax.ShapeDtypeStruct((n_out, c), x.dtype), mesh=_sc_mesh())
    def scatter(x_hbm, i0, i1, i2, i3, o_hbm):
        def body(x_vmem, *idx_vmem):
            for i_vmem in idx_vmem:
                pltpu.sync_copy(x_vmem, o_hbm.at[i_vmem.at[0]])

        ispec = pl.BlockSpec((1, SC_WINDOW), lambda i: (0, i))
        pltpu.emit_pipeline(
            body, grid=(r // SC_WINDOW,),
            in_specs=[pl.BlockSpec((SC_WINDOW, c), lambda i: (i, 0))] + [ispec] * TOP_K,
            out_specs=[],
            core_axis_name=("c", "s"), dimension_semantics=(pltpu.PARALLEL,),
        )(x_hbm, i0, i1, i2, i3)

    return scatter(x, *[idx[k].reshape(1, r) for k in range(TOP_K)])


def _sc_gather_rows(table, idx):
    m = idx.shape[0]
    c = table.shape[1]

    @functools.partial(pl.kernel, out_type=jax.ShapeDtypeStruct((m, c), table.dtype), mesh=_sc_mesh())
    def gather(t_hbm, i_hbm, o_hbm):
        def body(i_vmem, o_vmem):
            pltpu.sync_copy(t_hbm.at[i_vmem.at[0]], o_vmem)

        pltpu.emit_pipeline(
            body, grid=(m // SC_WINDOW,),
            in_specs=[pl.BlockSpec((1, SC_WINDOW), lambda i: (0, i))],
            out_specs=[pl.BlockSpec((SC_WINDOW, c), lambda i: (i, 0))],
            core_axis_name=("c", "s"), dimension_semantics=(pltpu.PARALLEL,),
        )(i_hbm, o_hbm)

    return gather(table, idx.reshape(1, m))


def _moe(h, idx_t, layer, w1, b1, w2, b2):
    n, d = h.shape
    e = w1.shape[1]
    m = n * TOP_K
    rank_t, cnt = _rank(idx_t, e)
    sizes = cnt[:, 0].astype(jnp.int32)
    padded = (sizes + MOE_BLOCK - 1) // MOE_BLOCK * MOE_BLOCK
    pad_ends = jnp.cumsum(padded)
    pad_starts = pad_ends - padded
    dest_t = pad_starts[idx_t] + rank_t
    n_blocks = (m + e * (MOE_BLOCK - 1)) // MOE_BLOCK + 1
    n_pad = n_blocks * MOE_BLOCK
    block_start = jnp.arange(n_blocks, dtype=jnp.int32) * MOE_BLOCK
    block_expert = jnp.minimum(jnp.sum((pad_ends[None, :] <= block_start[:, None]).astype(jnp.int32), axis=1), e - 1)
    n_used = (pad_ends[-1:] // MOE_BLOCK).astype(jnp.int32)
    pieces = d // SC_SUBROW
    dest_sub = (dest_t[:, :, None] * pieces + jnp.arange(pieces, dtype=jnp.int32)).reshape(TOP_K, n * pieces)
    x_pad = _sc_scatter_rows(h.reshape(n * pieces, SC_SUBROW), dest_sub, n_pad * pieces).reshape(n_pad, d)
    y_pad = _experts(block_expert, n_used, x_pad, layer, w1, b1, w2, b2)
    y_g = _sc_gather_rows(y_pad.reshape(n_pad * pieces, SC_SUBROW), dest_sub.reshape(-1))
    return y_g.reshape(TOP_K, n, d)


def _rope_tables(t_lat, n_ctx):
    t = jnp.arange(t_lat)
    row = (t // GRID_W).astype(F32)
    col = (t % GRID_W).astype(F32)
    nf = GLA_DK // 4
    freqs = ROPE_BASE ** (-jnp.arange(nf, dtype=F32) / nf)
    ang = jnp.concatenate([row[:, None] * freqs, col[:, None] * freqs], axis=-1)
    cos, sin = jnp.cos(ang), jnp.sin(ang)
    cos2 = jnp.concatenate([cos, cos], axis=-1)
    sin2 = jnp.concatenate([-sin, sin], axis=-1)
    return (jnp.concatenate([cos2, jnp.ones((n_ctx, GLA_DK), F32)], axis=0),
            jnp.concatenate([sin2, jnp.zeros((n_ctx, GLA_DK), F32)], axis=0))


def _chunk_tri(tg, rev):
    t = np.arange(tg)
    same = (t[:, None] // GLA_CHUNK) == (t[None, :] // GLA_CHUNK)
    side = (t[None, :] >= t[:, None]) if rev else (t[None, :] <= t[:, None])
    return jnp.asarray((same & side).astype(np.float32))


def _table(mods, rows, batch):
    lat = jnp.stack([mods[:batch, r] for r in rows], axis=1)
    ctx = jnp.broadcast_to(jnp.stack([mods[batch, r] for r in rows], axis=0)[None], lat.shape)
    tab = jnp.stack([lat, ctx], axis=1).reshape(2 * batch, len(rows), -1)
    return jnp.pad(tab, ((0, 0), (0, 8 - len(rows)), (0, 0)))


@jax.jit
def _forward(x, c, ctx, c_ctx, ada_w, ada_b, ln_g, ln_b, ab_w_in, ab_pool_w, ab_pool_scale, ab_rpb,
             ab_w_out, gla_w_in, gla_w_gate, gla_b_gate, gla_norm_g, gla_w_out, router_w, router_b,
             exp_w1, exp_b1, exp_w2, exp_b2):
    batch, t_lat, d = x.shape
    n_ctx = ctx.shape[1]
    depth = ada_w.shape[0]
    assert d == D_MODEL and n_ctx == TOK_TILE and t_lat % TOK_TILE == 0
    rows = t_lat // GRID_W
    assert rows % NA_QROWS == 0 and rows >= NA_KROWS + NA_QROWS
    n_lat = t_lat // TOK_TILE
    tpb = n_lat + 1
    l = t_lat + n_ctx
    n = batch * l
    alpha = (2.0 * depth) ** 0.25

    cc = jnp.concatenate([c, c_ctx[None], jnp.zeros((16 - batch - 1, d), F32)], axis=0)
    mods = _mods(cc, ada_w, ada_b).reshape(depth, 16, N_MOD, d)

    z = jnp.concatenate([x, ctx], axis=1).reshape(n, d)
    h = _modulate(z, _table(mods[0], (1, 0), batch), tpb)
    cos2, sin2 = _rope_tables(t_lat, n_ctx)

    for i in range(depth):
        j = i // 2
        last = i == depth - 1
        tab1 = _table(mods[i], (2, 4, 3), batch)
        ln1 = jnp.stack([ln_g[i, 0], ln_b[i, 0]])
        ln2 = jnp.stack([ln_g[i, 1], ln_b[i, 1]])
        rw_t = router_w[i].T
        rb = router_b[i][:, None]
        if i % 2 == 0:
            u, qkv = _proj(h, ab_w_in[j].astype(BF16), [(0, POOL_WIDTH), (POOL_WIDTH, POOL_WIDTH + 3 * NA_WIDTH)],
                           [F32, BF16])
            w_blk = jax.scipy.linalg.block_diag(*[ab_pool_w[j, g] for g in range(len(POOL_WINDOWS))])
            pooled = _pool(u.reshape(batch, l, POOL_WIDTH), w_blk.astype(BF16), ab_pool_scale[j][None, :],
                           n_lat, t_lat, n_ctx)
            bias = _na_bias_tables(ab_rpb[j], rows, n_ctx)
            attn = _na(qkv.reshape(batch, l, 3 * NA_WIDTH), bias, n_lat)
            w_out = ab_w_out[j].astype(BF16)
            acts = [pooled.reshape(n, POOL_WIDTH), attn.reshape(n, NA_WIDTH)]
            ws = [w_out[:POOL_WIDTH], w_out[POOL_WIDTH:]]
        else:
            qk, v, r, g = _proj(h, gla_w_in[j].astype(BF16),
                                [(0, 2 * GLA_QK), (2 * GLA_QK, 2 * GLA_QK + GLA_V),
                                 (2 * GLA_QK + GLA_V, 2 * GLA_QK + 2 * GLA_V),
                                 (2 * GLA_QK + 2 * GLA_V, 2 * GLA_QK + 2 * GLA_V + 2 * GATE_RANK)],
                                [F32, BF16, F32, F32])
            o_dirs = []
            for rev in (False, True):
                dr = int(rev)
                qe, kd, a, gd = _gla_prep(rev, qk, g, cos2, sin2, gla_w_gate[j, dr], gla_b_gate[j, dr][None, :],
                                          _chunk_tri(TOK_TILE, rev), tpb)
                o_dirs.append(_gla_scan(rev, qe, kd, a, gd, v, batch, tpb))
            acts = [_gla_out(o_dirs[0], o_dirs[1], r, gla_norm_g[j][None, :])]
            ws = [gla_w_out[j].astype(BF16)]
        z, h, idx_t, gates_t = _post(acts, ws, z, tab1, ln1, rw_t, rb, tpb, alpha)
        y_g = _moe(h, idx_t, i, exp_w1, exp_b1, exp_w2, exp_b2)
        nxt = mods[i + 1] if not last else mods[i]
        tab2 = _table(jnp.concatenate([mods[i][:, 5:6], nxt[:, 1:2], nxt[:, 0:1]], axis=1), (0, 1, 2), batch)
        z, h = _combine(y_g, gates_t.T, z, tab2, ln2, tpb, alpha)
    return z.reshape(batch, l, d)[:, :t_lat]


def kernel(x, c, ctx, c_ctx, ada_w, ada_b, ln_g, ln_b, ab_w_in, ab_pool_w, ab_pool_scale, ab_rpb, ab_w_out,
           gla_w_in, gla_w_gate, gla_b_gate, gla_norm_g, gla_w_out, router_w, router_b, exp_w1, exp_b1, exp_w2,
           exp_b2):
    return _forward(x, c, ctx, c_ctx, ada_w, ada_b, ln_g, ln_b, ab_w_in, ab_pool_w, ab_pool_scale, ab_rpb,
                    ab_w_out, gla_w_in, gla_w_gate, gla_b_gate, gla_norm_g, gla_w_out, router_w, router_b,
                    exp_w1, exp_b1, exp_w2, exp_b2)
```

```python
import functools
import math

import numpy as np
import jax
import jax.numpy as jnp
from jax import lax
from jax.experimental import pallas as pl
from jax.experimental.pallas import tpu as pltpu
from jax.experimental.pallas import tpu_sc as plsc

F32 = jnp.float32
BF16 = jnp.bfloat16
HIGHEST = lax.Precision.HIGHEST

D_MODEL = 1024
GRID_W = 64
N_MOD = 6
POOL_WINDOWS = (2, 4, 8, 16)
POOL_WIDTH = D_MODEL // 4
POOL_GROUP_DIM = POOL_WIDTH // len(POOL_WINDOWS)
POOL_HALO = max(POOL_WINDOWS) // 2
NA_HEAD_DIM = 64
NA_HEADS = (D_MODEL - POOL_WIDTH) // NA_HEAD_DIM
NA_WIDTH = NA_HEADS * NA_HEAD_DIM
WIN_H = 8
WIN_W = 16
GLA_HEADS = 4
GLA_DK = D_MODEL // 2 // GLA_HEADS
GLA_DV = D_MODEL // GLA_HEADS
GATE_RANK = 16
GATE_NORM = 16.0
GLA_CHUNK = 64
GLA_SUB = 16
GLA_QK = GLA_HEADS * GLA_DK
GLA_V = GLA_HEADS * GLA_DV
ROPE_BASE = 10000.0
TOP_K = 4
SWIGLU_LIMIT = 7.0
SWIGLU_ALPHA = 1.702
MOE_BLOCK = 256
LN_EPS = 1e-5
RMS_EPS = 1e-6
NEG_INF = -1e30

LANES = 128
TOK_TILE = 256
NA_QROWS = 4
NA_KROWS = 12
VMEM_LIMIT = 48 * 1024 * 1024
EXPERT_VMEM_LIMIT = 56 * 1024 * 1024
SC_CHUNK = 32


def _cparams(sem):
    return pltpu.CompilerParams(dimension_semantics=sem, vmem_limit_bytes=VMEM_LIMIT)


def _row_tile(n):
    return 2 * TOK_TILE if n % (2 * TOK_TILE) == 0 else TOK_TILE


def _mods_kernel(c_ref, w_ref, b_ref, o_ref):
    cv = c_ref[...]
    sc = cv * jax.nn.sigmoid(cv)
    o_ref[0] = jnp.dot(sc, w_ref[0], precision=HIGHEST, preferred_element_type=F32) + b_ref[0]


def _mods(cc, ada_w, ada_b):
    depth, d, n = ada_w.shape
    r = cc.shape[0]
    tn = n // 4
    return pl.pallas_call(
        _mods_kernel,
        grid=(depth, n // tn),
        in_specs=[pl.BlockSpec((r, d), lambda i, j: (0, 0)),
                  pl.BlockSpec((1, d, tn), lambda i, j: (i, 0, j)),
                  pl.BlockSpec((1, 1, tn), lambda i, j: (i, 0, j))],
        out_specs=pl.BlockSpec((1, r, tn), lambda i, j: (i, 0, j)),
        out_shape=jax.ShapeDtypeStruct((depth, r, n), F32),
        compiler_params=_cparams(("arbitrary", "arbitrary")),
        name="mods",
    )(cc, ada_w, ada_b.reshape(depth, 1, n))


def _tab_row(g, tpb):
    return (g // tpb) * 2 + (g % tpb == tpb - 1).astype(jnp.int32)


def _modulate_kernel(tpb, x_ref, tab_ref, h_ref):
    t = tab_ref[_tab_row(pl.program_id(0), tpb)]
    h_ref[...] = (x_ref[...] * (1.0 + t[0:1]) + t[1:2]).astype(h_ref.dtype)


def _modulate(z, tab, tpb):
    n, d = z.shape
    return pl.pallas_call(
        functools.partial(_modulate_kernel, tpb),
        grid=(n // TOK_TILE,),
        in_specs=[pl.BlockSpec((TOK_TILE, d), lambda i: (i, 0)),
                  pl.BlockSpec(tab.shape, lambda i: (0, 0, 0))],
        out_specs=pl.BlockSpec((TOK_TILE, d), lambda i: (i, 0)),
        out_shape=jax.ShapeDtypeStruct((n, d), BF16),
        compiler_params=_cparams(("arbitrary",)),
        name="modulate",
    )(z, tab)


def _proj_kernel(splits, x_ref, w_ref, *out_refs):
    x = x_ref[...]
    for (a, b), o_ref in zip(splits, out_refs):
        o_ref[...] = jnp.dot(x, w_ref[:, a:b], preferred_element_type=F32).astype(o_ref.dtype)


def _proj(h, w, splits, dtypes):
    n, k = h.shape
    m = w.shape[1]
    tm = _row_tile(n)
    return pl.pallas_call(
        functools.partial(_proj_kernel, tuple(splits)),
        grid=(n // tm,),
        in_specs=[pl.BlockSpec((tm, k), lambda i: (i, 0)),
                  pl.BlockSpec((k, m), lambda i: (0, 0))],
        out_specs=[pl.BlockSpec((tm, b - a), lambda i: (i, 0)) for a, b in splits],
        out_shape=[jax.ShapeDtypeStruct((n, b - a), dt) for (a, b), dt in zip(splits, dtypes)],
        compiler_params=_cparams(("arbitrary",)),
        name="proj",
    )(h, w)


def _pool_kernel(n_lat, t_lat, t_ctx, prev_ref, cur_ref, next_ref, w_ref, scale_ref, o_ref, halo_ref):
    j = pl.program_id(1)
    is_ctx = j == n_lat
    has_prev = jnp.logical_and(j > 0, jnp.logical_not(is_ctx))
    has_next = j < n_lat - 1
    cur = cur_ref[0]
    hl = POOL_HALO
    halo_ref[0:hl] = jnp.where(has_prev, prev_ref[0, TOK_TILE - hl:TOK_TILE], 0.0)
    halo_ref[hl:hl + TOK_TILE] = cur
    halo_ref[hl + TOK_TILE:2 * hl + TOK_TILE] = jnp.where(has_next, next_ref[0, 0:hl], 0.0)

    shape = cur.shape
    lane = lax.broadcasted_iota(jnp.int32, shape, 1)
    group = lane // POOL_GROUP_DIM
    half = jnp.ones(shape, jnp.int32)
    for gi, wdw in enumerate(POOL_WINDOWS):
        half = jnp.where(group == gi, wdw // 2, half)
    acc = jnp.zeros(shape, F32)
    for off in range(-hl, hl):
        v = halo_ref[hl + off:hl + off + TOK_TILE]
        inside = (half >= -off) if off < 0 else (half > off)
        acc = acc + jnp.where(inside, v, 0.0)
    pos0 = jnp.where(is_ctx, 0, j * TOK_TILE)
    seq = jnp.where(is_ctx, t_ctx, t_lat)
    t = pos0 + lax.broadcasted_iota(jnp.int32, shape, 0)
    cnt = jnp.minimum(t + half, seq) - jnp.maximum(t - half, 0)
    pooled = acc / cnt.astype(F32) - cur
    y = jnp.dot(pooled.astype(BF16), w_ref[...], preferred_element_type=F32) * scale_ref[...]
    o_ref[0] = y.astype(o_ref.dtype)


def _pool(u, w_blk, scale, n_lat, t_lat, t_ctx):
    b, l, pw = u.shape
    tpb = l // TOK_TILE
    blk = (1, TOK_TILE, pw)
    return pl.pallas_call(
        functools.partial(_pool_kernel, n_lat, t_lat, t_ctx),
        grid=(b, tpb),
        in_specs=[pl.BlockSpec(blk, lambda bi, j: (bi, jnp.maximum(j - 1, 0), 0)),
                  pl.BlockSpec(blk, lambda bi, j: (bi, j, 0)),
                  pl.BlockSpec(blk, lambda bi, j: (bi, jnp.minimum(j + 1, tpb - 1), 0)),
                  pl.BlockSpec((pw, pw), lambda bi, j: (0, 0)),
                  pl.BlockSpec((1, pw), lambda bi, j: (0, 0))],
        out_specs=pl.BlockSpec(blk, lambda bi, j: (bi, j, 0)),
        out_shape=jax.ShapeDtypeStruct((b, l, pw), BF16),
        scratch_shapes=[pltpu.VMEM((TOK_TILE + 2 * POOL_HALO, pw), F32)],
        compiler_params=_cparams(("arbitrary", "arbitrary")),
        name="pool",
    )(u, u, u, w_blk, scale)


def _na_bias_tables(rpb, rows, n_ctx):
    n_i = rows // NA_QROWS
    heads = rpb.shape[0]
    a = np.arange(NA_QROWS)
    kr = np.arange(NA_KROWS)
    cq = np.arange(GRID_W)
    ws = np.clip(cq - WIN_W // 2, 0, GRID_W - WIN_W)
    ok_col = (cq[None, :] >= ws[:, None]) & (cq[None, :] < ws[:, None] + WIN_W)
    dcol = np.clip(cq[None, :] - cq[:, None] + WIN_W - 1, 0, 2 * WIN_W - 2)
    oh_col = (dcol[..., None] == np.arange(2 * WIN_W - 1)).astype(np.float32)
    tabs = []
    for i in (0, 1, n_i - 1):
        start = int(np.clip(NA_QROWS * i - WIN_H // 2, 0, rows - NA_KROWS))
        r = NA_QROWS * i + a
        krow = start + kr
        rs = np.clip(r - WIN_H // 2, 0, rows - WIN_H)
        ok_row = (krow[None, :] >= rs[:, None]) & (krow[None, :] < rs[:, None] + WIN_H)
        drow = np.clip(krow[None, :] - r[:, None] + WIN_H - 1, 0, 2 * WIN_H - 2)
        oh_row = (drow[..., None] == np.arange(2 * WIN_H - 1)).astype(np.float32)
        by_row = jnp.einsum('hrc,akr->hakc', rpb, oh_row, precision=HIGHEST)
        bias = jnp.einsum('hakc,qjc->haqkj', by_row, oh_col, precision=HIGHEST)
        ok = ok_row[:, None, :, None] & ok_col[None, :, None, :]
        tabs.append(jnp.where(ok[None], bias, NEG_INF).reshape(heads, TOK_TILE, NA_KROWS * GRID_W))
    tabs.append(jnp.full_like(tabs[0], NEG_INF))
    band = jnp.stack(tabs).astype(F32)
    return jnp.concatenate([band, jnp.zeros(band.shape[:3] + (n_ctx,), F32)], axis=-1)


def _na_kernel(q_ref, k0_ref, k1_ref, k2_ref, kc_ref, v0_ref, v1_ref, v2_ref, vc_ref, bias_ref, o_ref):
    q = q_ref[0]
    kcat = jnp.concatenate([k0_ref[0], k1_ref[0], k2_ref[0], kc_ref[0]], axis=0)
    vcat = jnp.concatenate([v0_ref[0], v1_ref[0], v2_ref[0], vc_ref[0]], axis=0)
    lane = lax.broadcasted_iota(jnp.int32, q.shape, 1)
    first = lane < NA_HEAD_DIM
    outs = []
    for hh in range(2):
        qm = jnp.where(first if hh == 0 else jnp.logical_not(first), q, jnp.zeros_like(q))
        s = lax.dot_general(qm, kcat, (((1,), (1,)), ((), ())), preferred_element_type=F32)
        s = s * NA_HEAD_DIM ** -0.5 + bias_ref[0, hh]
        m = jnp.max(s, axis=-1, keepdims=True)
        p = jnp.exp(s - m)
        l = jnp.sum(p, axis=-1, keepdims=True)
        outs.append(jnp.dot(p.astype(BF16), vcat, preferred_element_type=F32) / l)
    o_ref[0] = jnp.where(first, outs[0], outs[1]).astype(o_ref.dtype)


def _na(qkv, bias, n_lat):
    b, l, _ = qkv.shape
    tpb = l // TOK_TILE
    n_pairs = NA_WIDTH // LANES
    blk = (1, TOK_TILE, LANES)

    def kstart(i):
        return jnp.clip(i - 1, 0, n_lat - NA_KROWS // NA_QROWS)

    def btype(i):
        return jnp.where(i == 0, 0, jnp.where(i == n_lat - 1, 2, jnp.where(i == n_lat, 3, 1)))

    def kv_spec(col0, j):
        return pl.BlockSpec(blk, lambda hp, i, bi: (bi, kstart(i) + j, col0 + hp))

    def ctx_spec(col0):
        return pl.BlockSpec(blk, lambda hp, i, bi: (bi, n_lat, col0 + hp))

    nk = bias.shape[-1]
    return pl.pallas_call(
        _na_kernel,
        grid=(n_pairs, tpb, b),
        in_specs=[pl.BlockSpec(blk, lambda hp, i, bi: (bi, i, hp)),
                  kv_spec(n_pairs, 0), kv_spec(n_pairs, 1), kv_spec(n_pairs, 2), ctx_spec(n_pairs),
                  kv_spec(2 * n_pairs, 0), kv_spec(2 * n_pairs, 1), kv_spec(2 * n_pairs, 2), ctx_spec(2 * n_pairs),
                  pl.BlockSpec((1, 2, TOK_TILE, nk), lambda hp, i, bi: (btype(i), hp, 0, 0))],
        out_specs=pl.BlockSpec(blk, lambda hp, i, bi: (bi, i, hp)),
        out_shape=jax.ShapeDtypeStruct((b, l, NA_WIDTH), BF16),
        compiler_params=_cparams(("arbitrary", "arbitrary", "arbitrary")),
        name="na",
    )(qkv, qkv, qkv, qkv, qkv, qkv, qkv, qkv, qkv, bias)


def _log_sigmoid(z):
    return jnp.minimum(z, 0.0) - jnp.log(1.0 + jnp.exp(-jnp.abs(z)))


def _gla_prep_kernel(rev, q_ref, k_ref, g_ref, cos_ref, sin_ref, wg_ref, bg_ref, tri_ref,
                     qe_ref, kd_ref, a_ref, gd_ref, b_scr, qr_scr, kr_scr):
    tg = q_ref.shape[0]
    gcol = GATE_RANK if rev else 0
    gg = g_ref[:, gcol:gcol + GATE_RANK]
    z = jnp.dot(gg, wg_ref[...], precision=HIGHEST, preferred_element_type=F32) + bg_ref[...]
    la = _log_sigmoid(z) * (1.0 / GATE_NORM)
    tri = tri_ref[...]
    b = jnp.zeros_like(la)
    rest = la
    for _ in range(3):
        piece = rest.astype(BF16)
        b = b + jnp.dot(tri, piece, preferred_element_type=F32)
        rest = rest - piece.astype(F32)
    cosv = cos_ref[...]
    sinv = sin_ref[...]
    q = q_ref[...]
    k = k_ref[...]
    qr = (q * cosv + pltpu.roll(q, GLA_DK // 2, 1) * sinv) * GLA_DK ** -0.5
    kr = k * cosv + pltpu.roll(k, GLA_DK // 2, 1) * sinv
    qe_ref[...] = (qr * jnp.exp(b)).astype(qe_ref.dtype)
    for c in range(tg // GLA_CHUNK):
        r0 = c * GLA_CHUNK
        last = r0 if rev else r0 + GLA_CHUNK - 1
        tot = b[last:last + 1]
        kd_ref[r0:r0 + GLA_CHUNK] = (kr[r0:r0 + GLA_CHUNK] * jnp.exp(tot - b[r0:r0 + GLA_CHUNK])).astype(kd_ref.dtype)
        gd_ref[0, 0, c:c + 1] = jnp.exp(tot)
    b_scr[...] = b
    qr_scr[...] = qr
    kr_scr[...] = kr

    n_sub = GLA_CHUNK // GLA_SUB
    colio = lax.broadcasted_iota(jnp.int32, (GLA_SUB, GLA_CHUNK), 1)
    rowio = lax.broadcasted_iota(jnp.int32, (GLA_SUB, GLA_CHUNK), 0)

    def sub_block(sb, carry):
        r0 = pl.multiple_of(sb * GLA_SUB, GLA_SUB)
        c0 = pl.multiple_of((sb // n_sub) * GLA_CHUNK, GLA_CHUNK)
        d0 = (sb % n_sub) * GLA_SUB
        bq = b_scr[pl.ds(r0, GLA_SUB), :]
        qq = qr_scr[pl.ds(r0, GLA_SUB), :]
        ref_row = jnp.minimum(r0 + GLA_SUB, tg - 1) if rev else jnp.maximum(r0 - 1, 0)
        rb = b_scr[pl.ds(ref_row, 1), :]
        bc = b_scr[pl.ds(c0, GLA_CHUNK), :]
        kc = kr_scr[pl.ds(c0, GLA_CHUNK), :]
        qt = qq * jnp.exp(jnp.minimum(bq - rb, 0.0))
        kt = kc * jnp.exp(jnp.minimum(rb - bc, 0.0))
        off = lax.dot_general(qt.astype(BF16), kt.astype(BF16), (((1,), (1,)), ((), ())),
                              preferred_element_type=F32)
        dacc = jnp.zeros((GLA_SUB, GLA_CHUNK), F32)
        for s in range(GLA_SUB):
            ks = kr_scr[pl.ds(r0 + s, 1), :]
            bs = b_scr[pl.ds(r0 + s, 1), :]
            col = jnp.sum(qq * ks * jnp.exp(jnp.minimum(bq - bs, 0.0)), axis=1, keepdims=True)
            dacc = jnp.where(colio == d0 + s, col, dacc)
        dcol = colio - d0
        in_diag = jnp.logical_and(dcol >= 0, dcol < GLA_SUB)
        if rev:
            ok_off = colio >= d0 + GLA_SUB
            ok_diag = jnp.logical_and(in_diag, dcol >= rowio)
        else:
            ok_off = colio < d0
            ok_diag = jnp.logical_and(in_diag, dcol <= rowio)
        a_blk = jnp.where(ok_off, off, jnp.where(ok_diag, dacc, 0.0))
        a_ref[0, pl.ds(r0, GLA_SUB), :] = a_blk.astype(a_ref.dtype)
        return carry

    lax.fori_loop(0, tg // GLA_SUB, sub_block, 0, unroll=GLA_CHUNK // GLA_SUB)


def _gla_prep(rev, qk, g, cos2, sin2, w_gate, b_gate, tri, tpb):
    n = qk.shape[0]
    tg = TOK_TILE
    nt = n // tg
    return pl.pallas_call(
        functools.partial(_gla_prep_kernel, rev),
        grid=(GLA_HEADS, nt),
        in_specs=[pl.BlockSpec((tg, GLA_DK), lambda hd, t: (t, hd)),
                  pl.BlockSpec((tg, GLA_DK), lambda hd, t: (t, GLA_HEADS + hd)),
                  pl.BlockSpec((tg, 2 * GATE_RANK), lambda hd, t: (t, 0)),
                  pl.BlockSpec((tg, GLA_DK), lambda hd, t: (t % tpb, 0)),
                  pl.BlockSpec((tg, GLA_DK), lambda hd, t: (t % tpb, 0)),
                  pl.BlockSpec((GATE_RANK, GLA_DK), lambda hd, t: (0, hd)),
                  pl.BlockSpec((1, GLA_DK), lambda hd, t: (0, hd)),
                  pl.BlockSpec((tg, tg), lambda hd, t: (0, 0))],
        out_specs=[pl.BlockSpec((tg, GLA_DK), lambda hd, t: (t, hd)),
                   pl.BlockSpec((tg, GLA_DK), lambda hd, t: (t, hd)),
                   pl.BlockSpec((1, tg, GLA_CHUNK), lambda hd, t: (hd, t, 0)),
                   pl.BlockSpec((1, 1, tg // GLA_CHUNK, GLA_DK), lambda hd, t: (hd, t, 0, 0))],
        out_shape=[jax.ShapeDtypeStruct((n, GLA_QK), BF16),
                   jax.ShapeDtypeStruct((n, GLA_QK), BF16),
                   jax.ShapeDtypeStruct((GLA_HEADS, n, GLA_CHUNK), BF16),
                   jax.ShapeDtypeStruct((GLA_HEADS, nt, tg // GLA_CHUNK, GLA_DK), F32)],
        scratch_shapes=[pltpu.VMEM((tg, GLA_DK), F32)] * 3,
        compiler_params=_cparams(("arbitrary", "arbitrary")),
        name="gla_prep_bwd" if rev else "gla_prep_fwd",
    )(qk, qk, g, cos2, sin2, w_gate, b_gate, tri)


def _gla_scan_kernel(rev, qe_ref, kd_ref, a_ref, gd_ref, v_ref, o_ref, st_ref):
    @pl.when(pl.program_id(1) == 0)
    def _():
        st_ref[...] = jnp.zeros_like(st_ref)

    n_chunks = qe_ref.shape[0] // GLA_CHUNK
    for hd in range(GLA_HEADS):
        st = st_ref[hd]
        kcols = slice(hd * GLA_DK, (hd + 1) * GLA_DK)
        vcols = slice(hd * GLA_DV, (hd + 1) * GLA_DV)
        for cc in range(n_chunks):
            c = n_chunks - 1 - cc if rev else cc
            rows = slice(c * GLA_CHUNK, (c + 1) * GLA_CHUNK)
            v_c = v_ref[rows, vcols]
            o = lax.dot_general(qe_ref[rows, kcols], st.astype(BF16), (((1,), (1,)), ((), ())),
                                preferred_element_type=F32)
            o = o + jnp.dot(a_ref[hd, rows, :], v_c, preferred_element_type=F32)
            o_ref[rows, vcols] = o
            upd = lax.dot_general(v_c, kd_ref[rows, kcols], (((0,), (0,)), ((), ())),
                                  preferred_element_type=F32)
            st = st * gd_ref[hd, 0, c:c + 1, :] + upd
        st_ref[hd] = st


def _gla_scan(rev, qe, kd, a, gd, v, batch, tpb):
    n = qe.shape[0]
    tg = TOK_TILE
    n_lat = tpb - 1

    def tile(bi, s):
        lat = n_lat - s if rev else s - 1
        return bi * tpb + jnp.where(s == 0, n_lat, lat)

    return pl.pallas_call(
        functools.partial(_gla_scan_kernel, rev),
        grid=(batch, tpb),
        in_specs=[pl.BlockSpec((tg, GLA_QK), lambda bi, s: (tile(bi, s), 0)),
                  pl.BlockSpec((tg, GLA_QK), lambda bi, s: (tile(bi, s), 0)),
                  pl.BlockSpec((GLA_HEADS, tg, GLA_CHUNK), lambda bi, s: (0, tile(bi, s), 0)),
                  pl.BlockSpec((GLA_HEADS, 1, tg // GLA_CHUNK, GLA_DK), lambda bi, s: (0, tile(bi, s), 0, 0)),
                  pl.BlockSpec((tg, GLA_V), lambda bi, s: (tile(bi, s), 0))],
        out_specs=pl.BlockSpec((tg, GLA_V), lambda bi, s: (tile(bi, s), 0)),
        out_shape=jax.ShapeDtypeStruct((n, GLA_V), F32),
        scratch_shapes=[pltpu.VMEM((GLA_HEADS, GLA_DV, GLA_DK), F32)],
        compiler_params=_cparams(("arbitrary", "arbitrary")),
        name="gla_scan_bwd" if rev else "gla_scan_fwd",
    )(qe, kd, a, gd, v)


def _gla_out_kernel(of_ref, ob_ref, r_ref, g_ref, o_ref):
    o = of_ref[...] + ob_ref[...]
    r = r_ref[...]
    gate = r * jax.nn.sigmoid(r)
    gn = g_ref[...]
    for hd in range(GLA_HEADS):
        cols = slice(hd * GLA_DV, (hd + 1) * GLA_DV)
        oh = o[:, cols]
        ms = jnp.mean(oh * oh, axis=-1, keepdims=True)
        o_ref[:, cols] = (oh * lax.rsqrt(ms + RMS_EPS) * gn * gate[:, cols]).astype(o_ref.dtype)


def _gla_out(o_f, o_b, r, norm_g):
    n, dv = o_f.shape
    tm = _row_tile(n)
    spec = pl.BlockSpec((tm, dv), lambda i: (i, 0))
    return pl.pallas_call(
        _gla_out_kernel,
        grid=(n // tm,),
        in_specs=[spec, spec, spec, pl.BlockSpec((1, GLA_DV), lambda i: (0, 0))],
        out_specs=spec,
        out_shape=jax.ShapeDtypeStruct((n, dv), BF16),
        compiler_params=_cparams(("arbitrary",)),
        name="gla_out",
    )(o_f, o_b, r, norm_g)


def _residual_ln(x, a, t, ln, alpha):
    y = alpha * x + t[0:1] * a
    mu = jnp.mean(y, axis=-1, keepdims=True)
    yc = y - mu
    var = jnp.mean(yc * yc, axis=-1, keepdims=True)
    xn = yc * lax.rsqrt(var + LN_EPS) * ln[0:1] + ln[1:2]
    return xn, xn * (1.0 + t[1:2]) + t[2:3]


def _top4_softmax(lt):
    e = lt.shape[0]
    io = lax.broadcasted_iota(jnp.int32, lt.shape, 0)
    work = lt
    idxs, vals = [], []
    for _ in range(TOP_K):
        m = jnp.max(work, axis=0, keepdims=True)
        ik = jnp.min(jnp.where(work == m, io, e), axis=0, keepdims=True)
        idxs.append(ik)
        vals.append(m)
        work = jnp.where(io == ik, -jnp.inf, work)
    ex = [jnp.exp(v - vals[0]) for v in vals]
    den = ex[0] + ex[1] + ex[2] + ex[3]
    return jnp.concatenate(idxs, axis=0), jnp.concatenate([x / den for x in ex], axis=0)


def _post_kernel(n_act, tpb, alpha, *refs):
    acts = refs[:n_act]
    ws = refs[n_act:2 * n_act]
    x_ref, tab_ref, ln_ref, rw_ref, rb_ref, xo_ref, h_ref, idx_ref, gate_ref = refs[2 * n_act:]
    tm = x_ref.shape[0]
    a = jnp.dot(acts[0][...], ws[0][...], preferred_element_type=F32)
    for k in range(1, n_act):
        a = a + jnp.dot(acts[k][...], ws[k][...], preferred_element_type=F32)
    ln = ln_ref[...]
    for s in range(tm // TOK_TILE):
        rows = slice(s * TOK_TILE, (s + 1) * TOK_TILE)
        t = tab_ref[_tab_row(pl.program_id(0) * (tm // TOK_TILE) + s, tpb)]
        xn, h = _residual_ln(x_ref[rows], a[rows], t, ln, alpha)
        xo_ref[rows] = xn
        h_ref[rows] = h.astype(h_ref.dtype)
        lt = lax.dot_general(rw_ref[...], h, (((1,), (1,)), ((), ())), precision=HIGHEST,
                             preferred_element_type=F32) + rb_ref[...]
        idx, gates = _top4_softmax(lt)
        idx_ref[:, rows] = idx
        gate_ref[:, rows] = gates


def _post(acts, ws, x, tab, ln, rw_t, rb, tpb, alpha):
    n, d = x.shape
    tm = _row_tile(n)
    e = rw_t.shape[0]
    row = lambda i: (i, 0)
    fixed = lambda i: (0, 0)
    return pl.pallas_call(
        functools.partial(_post_kernel, len(acts), tpb, alpha),
        grid=(n // tm,),
        in_specs=([pl.BlockSpec((tm, a.shape[1]), row) for a in acts]
                  + [pl.BlockSpec(w.shape, fixed) for w in ws]
                  + [pl.BlockSpec((tm, d), row),
                     pl.BlockSpec(tab.shape, lambda i: (0, 0, 0)),
                     pl.BlockSpec(ln.shape, fixed),
                     pl.BlockSpec((e, d), fixed),
                     pl.BlockSpec((e, 1), fixed)]),
        out_specs=[pl.BlockSpec((tm, d), row), pl.BlockSpec((tm, d), row),
                   pl.BlockSpec((TOP_K, tm), lambda i: (0, i)), pl.BlockSpec((TOP_K, tm), lambda i: (0, i))],
        out_shape=[jax.ShapeDtypeStruct((n, d), F32), jax.ShapeDtypeStruct((n, d), F32),
                   jax.ShapeDtypeStruct((TOP_K, n), jnp.int32), jax.ShapeDtypeStruct((TOP_K, n), F32)],
        compiler_params=_cparams(("arbitrary",)),
        name="post",
    )(*acts, *ws, x, tab, ln, rw_t, rb)


def _rank_kernel(idx_ref, tri_ref, rank_ref, cnt_ref, carry_ref):
    @pl.when(pl.program_id(0) == 0)
    def _():
        carry_ref[...] = jnp.zeros_like(carry_ref)

    idx = idx_ref[...]
    e = carry_ref.shape[0]
    tr = idx.shape[1]
    io = lax.broadcasted_iota(jnp.int32, (e, tr), 0)
    chosen = jnp.zeros((e, tr), F32)
    for k in range(TOP_K):
        chosen = chosen + (idx[k:k + 1] == io).astype(F32)
    cum = jnp.dot(chosen.astype(BF16), tri_ref[...], preferred_element_type=F32)
    base = carry_ref[:, 0:1]
    excl = base + cum - chosen
    ranks = [jnp.sum(jnp.where(idx[k:k + 1] == io, excl, 0.0), axis=0, keepdims=True) for k in range(TOP_K)]
    rank_ref[...] = jnp.concatenate(ranks, axis=0).astype(jnp.int32)
    carry_ref[...] = carry_ref[...] + jnp.sum(chosen, axis=1, keepdims=True)
    cnt_ref[...] = carry_ref[...]


def _rank(idx_t, n_experts):
    n = idx_t.shape[1]
    tr = _row_tile(n)
    tri = (np.arange(tr)[:, None] <= np.arange(tr)[None, :]).astype(np.float32)
    return pl.pallas_call(
        _rank_kernel,
        grid=(n // tr,),
        in_specs=[pl.BlockSpec((TOP_K, tr), lambda i: (0, i)),
                  pl.BlockSpec((tr, tr), lambda i: (0, 0))],
        out_specs=[pl.BlockSpec((TOP_K, tr), lambda i: (0, i)),
                   pl.BlockSpec((n_experts, LANES), lambda i: (0, 0))],
        out_shape=[jax.ShapeDtypeStruct((TOP_K, n), jnp.int32),
                   jax.ShapeDtypeStruct((n_experts, LANES), F32)],
        scratch_shapes=[pltpu.VMEM((n_experts, LANES), F32)],
        compiler_params=_cparams(("arbitrary",)),
        name="rank",
    )(idx_t, jnp.asarray(tri, BF16))


def _expert_kernel(be_ref, nb_ref, x_ref, w1_ref, b1_ref, w2_ref, b2_ref, o_ref, w1b_ref, w2b_ref):
    i = pl.program_id(0)
    used = i < nb_ref[0]
    new_expert = jnp.logical_or(i == 0, be_ref[i] != be_ref[jnp.maximum(i - 1, 0)])

    @pl.when(jnp.logical_and(used, new_expert))
    def _():
        w1b_ref[...] = w1_ref[0].astype(BF16)
        w2b_ref[...] = w2_ref[0].astype(BF16)

    @pl.when(used)
    def _():
        hid = jnp.dot(x_ref[...].astype(BF16), w1b_ref[...], preferred_element_type=F32) + b1_ref[0]
        half = hid.shape[1] // 2
        glu = jnp.minimum(hid[:, :half], SWIGLU_LIMIT)
        lin = jnp.clip(hid[:, half:], -SWIGLU_LIMIT, SWIGLU_LIMIT)
        act = glu * jax.nn.sigmoid(SWIGLU_ALPHA * glu) * (lin + 1.0)
        o_ref[...] = jnp.dot(act.astype(BF16), w2b_ref[...], preferred_element_type=F32) + b2_ref[0]

    @pl.when(i >= nb_ref[0])
    def _():
        o_ref[...] = jnp.zeros_like(o_ref)


def _experts(block_expert, n_used, x_pad, layer, w1, b1, w2, b2):
    n_pad, d = x_pad.shape
    depth, e, _, dh2 = w1.shape
    n_blocks = n_pad // MOE_BLOCK
    grid_spec = pltpu.PrefetchScalarGridSpec(
        num_scalar_prefetch=2,
        grid=(n_blocks,),
        in_specs=[pl.BlockSpec((MOE_BLOCK, d), lambda i, be, nb: (i, 0)),
                  pl.BlockSpec((None, 1, d, dh2), lambda i, be, nb: (layer, be[i], 0, 0)),
                  pl.BlockSpec((None, 1, 1, dh2), lambda i, be, nb: (layer, be[i], 0, 0)),
                  pl.BlockSpec((None, 1, dh2 // 2, d), lambda i, be, nb: (layer, be[i], 0, 0)),
                  pl.BlockSpec((None, 1, 1, d), lambda i, be, nb: (layer, be[i], 0, 0))],
        out_specs=pl.BlockSpec((MOE_BLOCK, d), lambda i, be, nb: (i, 0)),
        scratch_shapes=[pltpu.VMEM((d, dh2), BF16), pltpu.VMEM((dh2 // 2, d), BF16)],
    )
    return pl.pallas_call(
        _expert_kernel,
        grid_spec=grid_spec,
        out_shape=jax.ShapeDtypeStruct((n_pad, d), F32),
        compiler_params=pltpu.CompilerParams(dimension_semantics=("arbitrary",),
                                             vmem_limit_bytes=EXPERT_VMEM_LIMIT),
        name="experts",
    )(block_expert, n_used, x_pad, w1, b1.reshape(depth, e, 1, dh2), w2, b2.reshape(depth, e, 1, d))


def _combine_kernel(tpb, alpha, y_ref, gate_ref, x_ref, tab_ref, ln_ref, xo_ref, h_ref):
    g = gate_ref[...]
    y = y_ref[0] * g[:, 0:1]
    for k in range(1, TOP_K):
        y = y + y_ref[k] * g[:, k:k + 1]
    t = tab_ref[_tab_row(pl.program_id(0), tpb)]
    xn, h = _residual_ln(x_ref[...], y, t, ln_ref[...], alpha)
    xo_ref[...] = xn
    h_ref[...] = h.astype(h_ref.dtype)


def _combine(y_g, gates, x, tab, ln, tpb, alpha):
    n, d = x.shape
    tm = TOK_TILE
    row = lambda i: (i, 0)
    return pl.pallas_call(
        functools.partial(_combine_kernel, tpb, alpha),
        grid=(n // tm,),
        in_specs=[pl.BlockSpec((TOP_K, tm, d), lambda i: (0, i, 0)),
                  pl.BlockSpec((tm, TOP_K), row),
                  pl.BlockSpec((tm, d), row),
                  pl.BlockSpec(tab.shape, lambda i: (0, 0, 0)),
                  pl.BlockSpec(ln.shape, lambda i: (0, 0))],
        out_specs=[pl.BlockSpec((tm, d), row), pl.BlockSpec((tm, d), row)],
        out_shape=[jax.ShapeDtypeStruct((n, d), F32), jax.ShapeDtypeStruct((n, d), BF16)],
        compiler_params=_cparams(("arbitrary",)),
        name="combine",
    )(y_g, gates, x, tab, ln)


def _sc_mesh():
    return plsc.VectorSubcoreMesh(core_axis_name="c", subcore_axis_name="s")


def _sc_split(rows, mesh):
    workers = mesh.num_cores * mesh.num_subcores
    per = rows // workers
    assert per * workers == rows
    chunk = SC_CHUNK if per % SC_CHUNK == 0 else 8
    assert per % chunk == 0
    return per, chunk


def _sc_scatter_rows(x, idx, n_out):
    r, c = x.shape
    mesh = _sc_mesh()
    per, chunk = _sc_split(r, mesh)

    @functools.partial(pl.kernel, out_type=jax.ShapeDtypeStruct((n_out, c), x.dtype), mesh=mesh,
                       scratch_types=[pltpu.VMEM((chunk,), jnp.int32), pltpu.VMEM((chunk, c), x.dtype),
                                      pltpu.SemaphoreType.DMA])
    def scatter(x_hbm, i_hbm, o_hbm, idx_v, rows_v, sem):
        base = (lax.axis_index("s") * mesh.num_cores + lax.axis_index("c")) * per

        @pl.loop(0, per // chunk)
        def _(j):
            off = pl.multiple_of(base + j * chunk, chunk)
            pltpu.sync_copy(x_hbm.at[pl.ds(off, chunk)], rows_v)
            for k in range(TOP_K):
                pltpu.sync_copy(i_hbm.at[pl.ds(k * r + off, chunk)], idx_v)
                pltpu.async_copy(rows_v, o_hbm.at[idx_v], sem).wait()

    return scatter(x, idx)


def _sc_gather_rows(table, idx):
    m = idx.shape[0]
    c = table.shape[1]
    mesh = _sc_mesh()
    per, chunk = _sc_split(m, mesh)

    @functools.partial(pl.kernel, out_type=jax.ShapeDtypeStruct((m, c), table.dtype), mesh=mesh,
                       scratch_types=[pltpu.VMEM((chunk,), jnp.int32), pltpu.VMEM((chunk, c), table.dtype),
                                      pltpu.SemaphoreType.DMA])
    def gather(t_hbm, i_hbm, o_hbm, idx_v, rows_v, sem):
        base = (lax.axis_index("s") * mesh.num_cores + lax.axis_index("c")) * per

        @pl.loop(0, per // chunk)
        def _(j):
            off = pl.multiple_of(base + j * chunk, chunk)
            pltpu.sync_copy(i_hbm.at[pl.ds(off, chunk)], idx_v)
            pltpu.async_copy(t_hbm.at[idx_v], rows_v, sem).wait()
            pltpu.sync_copy(rows_v, o_hbm.at[pl.ds(off, chunk)])

    return gather(table, idx)


def _moe(h, idx_t, layer, w1, b1, w2, b2):
    n, d = h.shape
    e = w1.shape[1]
    m = n * TOP_K
    rank_t, cnt = _rank(idx_t, e)
    sizes = cnt[:, 0].astype(jnp.int32)
    padded = (sizes + MOE_BLOCK - 1) // MOE_BLOCK * MOE_BLOCK
    pad_ends = jnp.cumsum(padded)
    pad_starts = pad_ends - padded
    ids = jnp.arange(e, dtype=jnp.int32)[:, None, None]
    dest_t = jnp.sum(jnp.where(idx_t[None] == ids, pad_starts[:, None, None], 0), axis=0) + rank_t
    dest = dest_t.reshape(-1)
    n_blocks = (m + e * (MOE_BLOCK - 1)) // MOE_BLOCK + 1
    n_pad = n_blocks * MOE_BLOCK
    block_start = jnp.arange(n_blocks, dtype=jnp.int32) * MOE_BLOCK
    block_expert = jnp.minimum(jnp.sum((pad_ends[None, :] <= block_start[:, None]).astype(jnp.int32), axis=1), e - 1)
    n_used = (pad_ends[-1:] // MOE_BLOCK).astype(jnp.int32)
    x_pad = _sc_scatter_rows(h, dest, n_pad)
    y_pad = _experts(block_expert, n_used, x_pad, layer, w1, b1, w2, b2)
    return _sc_gather_rows(y_pad, dest).reshape(TOP_K, n, d)


def _rope_tables(t_lat, n_ctx):
    t = jnp.arange(t_lat)
    row = (t // GRID_W).astype(F32)
    col = (t % GRID_W).astype(F32)
    nf = GLA_DK // 4
    freqs = ROPE_BASE ** (-jnp.arange(nf, dtype=F32) / nf)
    ang = jnp.concatenate([row[:, None] * freqs, col[:, None] * freqs], axis=-1)
    cos, sin = jnp.cos(ang), jnp.sin(ang)
    cos2 = jnp.concatenate([cos, cos], axis=-1)
    sin2 = jnp.concatenate([-sin, sin], axis=-1)
    return (jnp.concatenate([cos2, jnp.ones((n_ctx, GLA_DK), F32)], axis=0),
            jnp.concatenate([sin2, jnp.zeros((n_ctx, GLA_DK), F32)], axis=0))


def _chunk_tri(tg, rev):
    t = np.arange(tg)
    same = (t[:, None] // GLA_CHUNK) == (t[None, :] // GLA_CHUNK)
    side = (t[None, :] >= t[:, None]) if rev else (t[None, :] <= t[:, None])
    return jnp.asarray((same & side).astype(np.float32), BF16)


def _table(mods, rows, batch):
    lat = jnp.stack([mods[:batch, r] for r in rows], axis=1)
    ctx = jnp.broadcast_to(jnp.stack([mods[batch, r] for r in rows], axis=0)[None], lat.shape)
    tab = jnp.stack([lat, ctx], axis=1).reshape(2 * batch, len(rows), -1)
    return jnp.pad(tab, ((0, 0), (0, 8 - len(rows)), (0, 0)))


@jax.jit
def _forward(x, c, ctx, c_ctx, ada_w, ada_b, ln_g, ln_b, ab_w_in, ab_pool_w, ab_pool_scale, ab_rpb,
             ab_w_out, gla_w_in, gla_w_gate, gla_b_gate, gla_norm_g, gla_w_out, router_w, router_b,
             exp_w1, exp_b1, exp_w2, exp_b2):
    batch, t_lat, d = x.shape
    n_ctx = ctx.shape[1]
    depth = ada_w.shape[0]
    assert d == D_MODEL and n_ctx == TOK_TILE and t_lat % TOK_TILE == 0
    rows = t_lat // GRID_W
    assert rows % NA_QROWS == 0 and rows >= NA_KROWS + NA_QROWS
    n_lat = t_lat // TOK_TILE
    tpb = n_lat + 1
    l = t_lat + n_ctx
    n = batch * l
    alpha = (2.0 * depth) ** 0.25

    cc = jnp.concatenate([c, c_ctx[None], jnp.zeros((16 - batch - 1, d), F32)], axis=0)
    mods = _mods(cc, ada_w, ada_b).reshape(depth, 16, N_MOD, d)

    z = jnp.concatenate([x, ctx], axis=1).reshape(n, d)
    h = _modulate(z, _table(mods[0], (1, 0), batch), tpb)
    cos2, sin2 = _rope_tables(t_lat, n_ctx)

    for i in range(depth):
        j = i // 2
        last = i == depth - 1
        tab1 = _table(mods[i], (2, 4, 3), batch)
        ln1 = jnp.stack([ln_g[i, 0], ln_b[i, 0]])
        ln2 = jnp.stack([ln_g[i, 1], ln_b[i, 1]])
        rw_t = router_w[i].T
        rb = router_b[i][:, None]
        if i % 2 == 0:
            u, qkv = _proj(h, ab_w_in[j].astype(BF16), [(0, POOL_WIDTH), (POOL_WIDTH, POOL_WIDTH + 3 * NA_WIDTH)],
                           [F32, BF16])
            w_blk = jax.scipy.linalg.block_diag(*[ab_pool_w[j, g] for g in range(len(POOL_WINDOWS))])
            pooled = _pool(u.reshape(batch, l, POOL_WIDTH), w_blk.astype(BF16), ab_pool_scale[j][None, :],
                           n_lat, t_lat, n_ctx)
            bias = _na_bias_tables(ab_rpb[j], rows, n_ctx)
            attn = _na(qkv.reshape(batch, l, 3 * NA_WIDTH), bias, n_lat)
            w_out = ab_w_out[j].astype(BF16)
            acts = [pooled.reshape(n, POOL_WIDTH), attn.reshape(n, NA_WIDTH)]
            ws = [w_out[:POOL_WIDTH], w_out[POOL_WIDTH:]]
        else:
            qk, v, r, g = _proj(h, gla_w_in[j].astype(BF16),
                                [(0, 2 * GLA_QK), (2 * GLA_QK, 2 * GLA_QK + GLA_V),
                                 (2 * GLA_QK + GLA_V, 2 * GLA_QK + 2 * GLA_V),
                                 (2 * GLA_QK + 2 * GLA_V, 2 * GLA_QK + 2 * GLA_V + 2 * GATE_RANK)],
                                [F32, BF16, F32, F32])
            o_dirs = []
            for rev in (False, True):
                dr = int(rev)
                qe, kd, a, gd = _gla_prep(rev, qk, g, cos2, sin2, gla_w_gate[j, dr], gla_b_gate[j, dr][None, :],
                                          _chunk_tri(TOK_TILE, rev), tpb)
                o_dirs.append(_gla_scan(rev, qe, kd, a, gd, v, batch, tpb))
            acts = [_gla_out(o_dirs[0], o_dirs[1], r, gla_norm_g[j][None, :])]
            ws = [gla_w_out[j].astype(BF16)]
        z, h, idx_t, gates_t = _post(acts, ws, z, tab1, ln1, rw_t, rb, tpb, alpha)
        y_g = _moe(h, idx_t, i, exp_w1, exp_b1, exp_w2, exp_b2)
        nxt = mods[i + 1] if not last else mods[i]
        tab2 = _table(jnp.concatenate([mods[i][:, 5:6], nxt[:, 1:2], nxt[:, 0:1]], axis=1), (0, 1, 2), batch)
        z, h = _combine(y_g, gates_t.T, z, tab2, ln2, tpb, alpha)
    return z.reshape(batch, l, d)[:, :t_lat]


def kernel(x, c, ctx, c_ctx, ada_w, ada_b, ln_g, ln_b, ab_w_in, ab_pool_w, ab_pool_scale, ab_rpb, ab_w_out,
           gla_w_in, gla_w_gate, gla_b_gate, gla_norm_g, gla_w_out, router_w, router_b, exp_w1, exp_b1, exp_w2,
           exp_b2):
    return _forward(x, c, ctx, c_ctx, ada_w, ada_b, ln_g, ln_b, ab_w_in, ab_pool_w, ab_pool_scale, ab_rpb,
                    ab_w_out, gla_w_in, gla_w_gate, gla_b_gate, gla_norm_g, gla_w_out, router_w, router_b,
                    exp_w1, exp_b1, exp_w2, exp_b2)
```

```python
import functools
import math

import numpy as np
import jax
import jax.numpy as jnp
from jax import lax
from jax.experimental import pallas as pl
from jax.experimental.pallas import tpu as pltpu
from jax.experimental.pallas import tpu_sc as plsc

F32 = jnp.float32
BF16 = jnp.bfloat16
HIGHEST = lax.Precision.HIGHEST

D_MODEL = 1024
GRID_W = 64
N_MOD = 6
POOL_WINDOWS = (2, 4, 8, 16)
POOL_WIDTH = D_MODEL // 4
POOL_GROUP_DIM = POOL_WIDTH // len(POOL_WINDOWS)
POOL_HALO = max(POOL_WINDOWS) // 2
NA_HEAD_DIM = 64
NA_HEADS = (D_MODEL - POOL_WIDTH) // NA_HEAD_DIM
NA_WIDTH = NA_HEADS * NA_HEAD_DIM
WIN_H = 8
WIN_W = 16
GLA_HEADS = 4
GLA_DK = D_MODEL // 2 // GLA_HEADS
GLA_DV = D_MODEL // GLA_HEADS
GATE_RANK = 16
GATE_NORM = 16.0
GLA_CHUNK = 64
GLA_SUB = 16
GLA_QK = GLA_HEADS * GLA_DK
GLA_V = GLA_HEADS * GLA_DV
ROPE_BASE = 10000.0
TOP_K = 4
SWIGLU_LIMIT = 7.0
SWIGLU_ALPHA = 1.702
LN_EPS = 1e-5
RMS_EPS = 1e-6
NEG_INF = -1e30

LANES = 128
TOK_TILE = 256
NA_QROWS = 4
NA_KROWS = 12
VMEM_LIMIT = 48 * 1024 * 1024
EXPERT_VMEM_LIMIT = 56 * 1024 * 1024
EXPERT_BLOCK = 512
SC_CHUNK = 64


def _cparams(sem):
    return pltpu.CompilerParams(dimension_semantics=sem, vmem_limit_bytes=VMEM_LIMIT)


def _row_tile(n):
    return 2 * TOK_TILE if n % (2 * TOK_TILE) == 0 else TOK_TILE


def _mods_kernel(c_ref, w_ref, b_ref, o_ref):
    cv = c_ref[...]
    sc = cv * jax.nn.sigmoid(cv)
    o_ref[0] = jnp.dot(sc, w_ref[0], precision=HIGHEST, preferred_element_type=F32) + b_ref[0]


def _mods(cc, ada_w, ada_b):
    depth, d, n = ada_w.shape
    r = cc.shape[0]
    tn = n // 4
    return pl.pallas_call(
        _mods_kernel,
        grid=(depth, n // tn),
        in_specs=[pl.BlockSpec((r, d), lambda i, j: (0, 0)),
                  pl.BlockSpec((1, d, tn), lambda i, j: (i, 0, j)),
                  pl.BlockSpec((1, 1, tn), lambda i, j: (i, 0, j))],
        out_specs=pl.BlockSpec((1, r, tn), lambda i, j: (i, 0, j)),
        out_shape=jax.ShapeDtypeStruct((depth, r, n), F32),
        compiler_params=_cparams(("arbitrary", "arbitrary")),
        name="mods",
    )(cc, ada_w, ada_b.reshape(depth, 1, n))


def _tab_row(g, tpb):
    return (g // tpb) * 2 + (g % tpb == tpb - 1).astype(jnp.int32)


def _modulate_kernel(tpb, x_ref, tab_ref, h_ref):
    t = tab_ref[_tab_row(pl.program_id(0), tpb)]
    h_ref[...] = (x_ref[...] * (1.0 + t[0:1]) + t[1:2]).astype(h_ref.dtype)


def _modulate(z, tab, tpb):
    n, d = z.shape
    return pl.pallas_call(
        functools.partial(_modulate_kernel, tpb),
        grid=(n // TOK_TILE,),
        in_specs=[pl.BlockSpec((TOK_TILE, d), lambda i: (i, 0)),
                  pl.BlockSpec(tab.shape, lambda i: (0, 0, 0))],
        out_specs=pl.BlockSpec((TOK_TILE, d), lambda i: (i, 0)),
        out_shape=jax.ShapeDtypeStruct((n, d), BF16),
        compiler_params=_cparams(("arbitrary",)),
        name="modulate",
    )(z, tab)


def _proj_kernel(splits, x_ref, w_ref, *out_refs):
    x = x_ref[...]
    for (a, b), o_ref in zip(splits, out_refs):
        o_ref[...] = jnp.dot(x, w_ref[:, a:b], preferred_element_type=F32).astype(o_ref.dtype)


def _proj(h, w, splits, dtypes):
    n, k = h.shape
    m = w.shape[1]
    tm = _row_tile(n)
    return pl.pallas_call(
        functools.partial(_proj_kernel, tuple(splits)),
        grid=(n // tm,),
        in_specs=[pl.BlockSpec((tm, k), lambda i: (i, 0)),
                  pl.BlockSpec((k, m), lambda i: (0, 0))],
        out_specs=[pl.BlockSpec((tm, b - a), lambda i: (i, 0)) for a, b in splits],
        out_shape=[jax.ShapeDtypeStruct((n, b - a), dt) for (a, b), dt in zip(splits, dtypes)],
        compiler_params=_cparams(("arbitrary",)),
        name="proj",
    )(h, w)


def _pool_kernel(n_lat, t_lat, t_ctx, prev_ref, cur_ref, next_ref, w_ref, scale_ref, o_ref, halo_ref):
    j = pl.program_id(1)
    is_ctx = j == n_lat
    has_prev = jnp.logical_and(j > 0, jnp.logical_not(is_ctx))
    has_next = j < n_lat - 1
    cur = cur_ref[0]
    hl = POOL_HALO
    halo_ref[0:hl] = jnp.where(has_prev, prev_ref[0, TOK_TILE - hl:TOK_TILE], 0.0)
    halo_ref[hl:hl + TOK_TILE] = cur
    halo_ref[hl + TOK_TILE:2 * hl + TOK_TILE] = jnp.where(has_next, next_ref[0, 0:hl], 0.0)

    shape = cur.shape
    lane = lax.broadcasted_iota(jnp.int32, shape, 1)
    group = lane // POOL_GROUP_DIM
    half = jnp.ones(shape, jnp.int32)
    for gi, wdw in enumerate(POOL_WINDOWS):
        half = jnp.where(group == gi, wdw // 2, half)
    acc = jnp.zeros(shape, F32)
    for off in range(-hl, hl):
        v = halo_ref[hl + off:hl + off + TOK_TILE]
        inside = (half >= -off) if off < 0 else (half > off)
        acc = acc + jnp.where(inside, v, 0.0)
    pos0 = jnp.where(is_ctx, 0, j * TOK_TILE)
    seq = jnp.where(is_ctx, t_ctx, t_lat)
    t = pos0 + lax.broadcasted_iota(jnp.int32, shape, 0)
    cnt = jnp.minimum(t + half, seq) - jnp.maximum(t - half, 0)
    pooled = acc / cnt.astype(F32) - cur
    y = jnp.dot(pooled.astype(BF16), w_ref[...], preferred_element_type=F32) * scale_ref[...]
    o_ref[0] = y.astype(o_ref.dtype)


def _pool(u, w_blk, scale, n_lat, t_lat, t_ctx):
    b, l, pw = u.shape
    tpb = l // TOK_TILE
    blk = (1, TOK_TILE, pw)
    return pl.pallas_call(
        functools.partial(_pool_kernel, n_lat, t_lat, t_ctx),
        grid=(b, tpb),
        in_specs=[pl.BlockSpec(blk, lambda bi, j: (bi, jnp.maximum(j - 1, 0), 0)),
                  pl.BlockSpec(blk, lambda bi, j: (bi, j, 0)),
                  pl.BlockSpec(blk, lambda bi, j: (bi, jnp.minimum(j + 1, tpb - 1), 0)),
                  pl.BlockSpec((pw, pw), lambda bi, j: (0, 0)),
                  pl.BlockSpec((1, pw), lambda bi, j: (0, 0))],
        out_specs=pl.BlockSpec(blk, lambda bi, j: (bi, j, 0)),
        out_shape=jax.ShapeDtypeStruct((b, l, pw), BF16),
        scratch_shapes=[pltpu.VMEM((TOK_TILE + 2 * POOL_HALO, pw), F32)],
        compiler_params=_cparams(("arbitrary", "arbitrary")),
        name="pool",
    )(u, u, u, w_blk, scale)


def _na_bias_tables(rpb, rows, n_ctx):
    n_i = rows // NA_QROWS
    heads = rpb.shape[0]
    a = np.arange(NA_QROWS)
    kr = np.arange(NA_KROWS)
    cq = np.arange(GRID_W)
    ws = np.clip(cq - WIN_W // 2, 0, GRID_W - WIN_W)
    ok_col = (cq[None, :] >= ws[:, None]) & (cq[None, :] < ws[:, None] + WIN_W)
    dcol = np.clip(cq[None, :] - cq[:, None] + WIN_W - 1, 0, 2 * WIN_W - 2)
    oh_col = (dcol[..., None] == np.arange(2 * WIN_W - 1)).astype(np.float32)
    tabs = []
    for i in (0, 1, n_i - 1):
        start = int(np.clip(NA_QROWS * i - WIN_H // 2, 0, rows - NA_KROWS))
        r = NA_QROWS * i + a
        krow = start + kr
        rs = np.clip(r - WIN_H // 2, 0, rows - WIN_H)
        ok_row = (krow[None, :] >= rs[:, None]) & (krow[None, :] < rs[:, None] + WIN_H)
        drow = np.clip(krow[None, :] - r[:, None] + WIN_H - 1, 0, 2 * WIN_H - 2)
        oh_row = (drow[..., None] == np.arange(2 * WIN_H - 1)).astype(np.float32)
        by_row = jnp.einsum('hrc,akr->hakc', rpb, oh_row, precision=HIGHEST)
        bias = jnp.einsum('hakc,qjc->haqkj', by_row, oh_col, precision=HIGHEST)
        ok = ok_row[:, None, :, None] & ok_col[None, :, None, :]
        tabs.append(jnp.where(ok[None], bias, NEG_INF).reshape(heads, TOK_TILE, NA_KROWS * GRID_W))
    tabs.append(jnp.full_like(tabs[0], NEG_INF))
    band = jnp.stack(tabs).astype(F32)
    return jnp.concatenate([band, jnp.zeros(band.shape[:3] + (n_ctx,), F32)], axis=-1)


def _na_kernel(q_ref, k0_ref, k1_ref, k2_ref, kc_ref, v0_ref, v1_ref, v2_ref, vc_ref, bias_ref, o_ref):
    q = q_ref[0]
    kcat = jnp.concatenate([k0_ref[0], k1_ref[0], k2_ref[0], kc_ref[0]], axis=0)
    vcat = jnp.concatenate([v0_ref[0], v1_ref[0], v2_ref[0], vc_ref[0]], axis=0)
    lane = lax.broadcasted_iota(jnp.int32, q.shape, 1)
    first = lane < NA_HEAD_DIM
    outs = []
    for hh in range(2):
        qm = jnp.where(first if hh == 0 else jnp.logical_not(first), q, jnp.zeros_like(q))
        s = lax.dot_general(qm, kcat, (((1,), (1,)), ((), ())), preferred_element_type=F32)
        s = s * NA_HEAD_DIM ** -0.5 + bias_ref[0, hh]
        m = jnp.max(s, axis=-1, keepdims=True)
        p = jnp.exp(s - m)
        l = jnp.sum(p, axis=-1, keepdims=True)
        outs.append(jnp.dot(p.astype(BF16), vcat, preferred_element_type=F32) / l)
    o_ref[0] = jnp.where(first, outs[0], outs[1]).astype(o_ref.dtype)


def _na(qkv, bias, n_lat):
    b, l, _ = qkv.shape
    tpb = l // TOK_TILE
    n_pairs = NA_WIDTH // LANES
    blk = (1, TOK_TILE, LANES)

    def kstart(i):
        return jnp.clip(i - 1, 0, n_lat - NA_KROWS // NA_QROWS)

    def btype(i):
        return jnp.where(i == 0, 0, jnp.where(i == n_lat - 1, 2, jnp.where(i == n_lat, 3, 1)))

    def kv_spec(col0, j):
        return pl.BlockSpec(blk, lambda hp, i, bi: (bi, kstart(i) + j, col0 + hp))

    def ctx_spec(col0):
        return pl.BlockSpec(blk, lambda hp, i, bi: (bi, n_lat, col0 + hp))

    nk = bias.shape[-1]
    return pl.pallas_call(
        _na_kernel,
        grid=(n_pairs, tpb, b),
        in_specs=[pl.BlockSpec(blk, lambda hp, i, bi: (bi, i, hp)),
                  kv_spec(n_pairs, 0), kv_spec(n_pairs, 1), kv_spec(n_pairs, 2), ctx_spec(n_pairs),
                  kv_spec(2 * n_pairs, 0), kv_spec(2 * n_pairs, 1), kv_spec(2 * n_pairs, 2), ctx_spec(2 * n_pairs),
                  pl.BlockSpec((1, 2, TOK_TILE, nk), lambda hp, i, bi: (btype(i), hp, 0, 0))],
        out_specs=pl.BlockSpec(blk, lambda hp, i, bi: (bi, i, hp)),
        out_shape=jax.ShapeDtypeStruct((b, l, NA_WIDTH), BF16),
        compiler_params=_cparams(("arbitrary", "arbitrary", "arbitrary")),
        name="na",
    )(qkv, qkv, qkv, qkv, qkv, qkv, qkv, qkv, qkv, bias)


def _log_sigmoid(z):
    return jnp.minimum(z, 0.0) - jnp.log(1.0 + jnp.exp(-jnp.abs(z)))


def _gla_prep_kernel(rev, q_ref, k_ref, g_ref, cos_ref, sin_ref, wg_ref, bg_ref, tri_ref,
                     qe_ref, kd_ref, a_ref, gd_ref, b_scr, qr_scr, kr_scr):
    tg = q_ref.shape[0]
    gcol = GATE_RANK if rev else 0
    gg = g_ref[:, gcol:gcol + GATE_RANK]
    z = jnp.dot(gg, wg_ref[...], precision=HIGHEST, preferred_element_type=F32) + bg_ref[...]
    la = _log_sigmoid(z) * (1.0 / GATE_NORM)
    tri = tri_ref[...]
    b = jnp.zeros_like(la)
    rest = la
    for _ in range(3):
        piece = rest.astype(BF16)
        b = b + jnp.dot(tri, piece, preferred_element_type=F32)
        rest = rest - piece.astype(F32)
    cosv = cos_ref[...]
    sinv = sin_ref[...]
    q = q_ref[...]
    k = k_ref[...]
    qr = (q * cosv + pltpu.roll(q, GLA_DK // 2, 1) * sinv) * GLA_DK ** -0.5
    kr = k * cosv + pltpu.roll(k, GLA_DK // 2, 1) * sinv
    qe_ref[...] = (qr * jnp.exp(b)).astype(qe_ref.dtype)
    for c in range(tg // GLA_CHUNK):
        r0 = c * GLA_CHUNK
        last = r0 if rev else r0 + GLA_CHUNK - 1
        tot = b[last:last + 1]
        kd_ref[r0:r0 + GLA_CHUNK] = (kr[r0:r0 + GLA_CHUNK] * jnp.exp(tot - b[r0:r0 + GLA_CHUNK])).astype(kd_ref.dtype)
        gd_ref[0, 0, c:c + 1] = jnp.exp(tot)
    b_scr[...] = b
    qr_scr[...] = qr
    kr_scr[...] = kr

    n_sub = GLA_CHUNK // GLA_SUB
    colio = lax.broadcasted_iota(jnp.int32, (GLA_SUB, GLA_CHUNK), 1)
    rowio = lax.broadcasted_iota(jnp.int32, (GLA_SUB, GLA_CHUNK), 0)

    def sub_block(sb, carry):
        r0 = pl.multiple_of(sb * GLA_SUB, GLA_SUB)
        c0 = pl.multiple_of((sb // n_sub) * GLA_CHUNK, GLA_CHUNK)
        d0 = (sb % n_sub) * GLA_SUB
        bq = b_scr[pl.ds(r0, GLA_SUB), :]
        qq = qr_scr[pl.ds(r0, GLA_SUB), :]
        ref_row = jnp.minimum(r0 + GLA_SUB, tg - 1) if rev else jnp.maximum(r0 - 1, 0)
        rb = b_scr[pl.ds(ref_row, 1), :]
        bc = b_scr[pl.ds(c0, GLA_CHUNK), :]
        kc = kr_scr[pl.ds(c0, GLA_CHUNK), :]
        qt = qq * jnp.exp(jnp.minimum(bq - rb, 0.0))
        kt = kc * jnp.exp(jnp.minimum(rb - bc, 0.0))
        off = lax.dot_general(qt.astype(BF16), kt.astype(BF16), (((1,), (1,)), ((), ())),
                              preferred_element_type=F32)
        hs = GLA_SUB // 2
        halves = [(bq[:hs], qq[:hs], jnp.zeros((hs, GLA_CHUNK), F32)),
                  (bq[hs:], qq[hs:], jnp.zeros((hs, GLA_CHUNK), F32))]
        for s in range(GLA_SUB):
            ks = kr_scr[pl.ds(r0 + s, 1), :]
            bs = b_scr[pl.ds(r0 + s, 1), :]
            for hf in range(2):
                if (hf == 1 and s < hs) if rev else (hf == 0 and s >= hs):
                    continue
                bh, qh, acc = halves[hf]
                col = jnp.sum(qh * ks * jnp.exp(jnp.minimum(bh - bs, 0.0)), axis=1, keepdims=True)
                halves[hf] = (bh, qh, jnp.where(colio[:hs] == d0 + s, col, acc))
        dacc = jnp.concatenate([halves[0][2], halves[1][2]], axis=0)
        dcol = colio - d0
        in_diag = jnp.logical_and(dcol >= 0, dcol < GLA_SUB)
        if rev:
            ok_off = colio >= d0 + GLA_SUB
            ok_diag = jnp.logical_and(in_diag, dcol >= rowio)
        else:
            ok_off = colio < d0
            ok_diag = jnp.logical_and(in_diag, dcol <= rowio)
        a_blk = jnp.where(ok_off, off, jnp.where(ok_diag, dacc, 0.0))
        a_ref[0, pl.ds(r0, GLA_SUB), :] = a_blk.astype(a_ref.dtype)
        return carry

    lax.fori_loop(0, tg // GLA_SUB, sub_block, 0, unroll=GLA_CHUNK // GLA_SUB)


def _gla_prep(rev, qk, g, cos2, sin2, w_gate, b_gate, tri, tpb):
    n = qk.shape[0]
    tg = TOK_TILE
    nt = n // tg
    return pl.pallas_call(
        functools.partial(_gla_prep_kernel, rev),
        grid=(GLA_HEADS, nt),
        in_specs=[pl.BlockSpec((tg, GLA_DK), lambda hd, t: (t, hd)),
                  pl.BlockSpec((tg, GLA_DK), lambda hd, t: (t, GLA_HEADS + hd)),
                  pl.BlockSpec((tg, 2 * GATE_RANK), lambda hd, t: (t, 0)),
                  pl.BlockSpec((tg, GLA_DK), lambda hd, t: (t % tpb, 0)),
                  pl.BlockSpec((tg, GLA_DK), lambda hd, t: (t % tpb, 0)),
                  pl.BlockSpec((GATE_RANK, GLA_DK), lambda hd, t: (0, hd)),
                  pl.BlockSpec((1, GLA_DK), lambda hd, t: (0, hd)),
                  pl.BlockSpec((tg, tg), lambda hd, t: (0, 0))],
        out_specs=[pl.BlockSpec((tg, GLA_DK), lambda hd, t: (t, hd)),
                   pl.BlockSpec((tg, GLA_DK), lambda hd, t: (t, hd)),
                   pl.BlockSpec((1, tg, GLA_CHUNK), lambda hd, t: (hd, t, 0)),
                   pl.BlockSpec((1, 1, tg // GLA_CHUNK, GLA_DK), lambda hd, t: (hd, t, 0, 0))],
        out_shape=[jax.ShapeDtypeStruct((n, GLA_QK), BF16),
                   jax.ShapeDtypeStruct((n, GLA_QK), BF16),
                   jax.ShapeDtypeStruct((GLA_HEADS, n, GLA_CHUNK), BF16),
                   jax.ShapeDtypeStruct((GLA_HEADS, nt, tg // GLA_CHUNK, GLA_DK), F32)],
        scratch_shapes=[pltpu.VMEM((tg, GLA_DK), F32)] * 3,
        compiler_params=_cparams(("arbitrary", "arbitrary")),
        name="gla_prep_bwd" if rev else "gla_prep_fwd",
    )(qk, qk, g, cos2, sin2, w_gate, b_gate, tri)


def _gla_scan_kernel(rev, qe_ref, kd_ref, a_ref, gd_ref, v_ref, o_ref, st_ref):
    @pl.when(pl.program_id(1) == 0)
    def _():
        st_ref[...] = jnp.zeros_like(st_ref)

    n_chunks = qe_ref.shape[0] // GLA_CHUNK
    for hd in range(GLA_HEADS):
        st = st_ref[hd]
        kcols = slice(hd * GLA_DK, (hd + 1) * GLA_DK)
        vcols = slice(hd * GLA_DV, (hd + 1) * GLA_DV)
        for cc in range(n_chunks):
            c = n_chunks - 1 - cc if rev else cc
            rows = slice(c * GLA_CHUNK, (c + 1) * GLA_CHUNK)
            v_c = v_ref[rows, vcols]
            o = lax.dot_general(qe_ref[rows, kcols], st.astype(BF16), (((1,), (1,)), ((), ())),
                                preferred_element_type=F32)
            o = o + jnp.dot(a_ref[hd, rows, :], v_c, preferred_element_type=F32)
            o_ref[rows, vcols] = o
            upd = lax.dot_general(v_c, kd_ref[rows, kcols], (((0,), (0,)), ((), ())),
                                  preferred_element_type=F32)
            st = st * gd_ref[hd, 0, c:c + 1, :] + upd
        st_ref[hd] = st


def _gla_scan(rev, qe, kd, a, gd, v, batch, tpb):
    n = qe.shape[0]
    tg = TOK_TILE
    n_lat = tpb - 1

    def tile(bi, s):
        lat = n_lat - s if rev else s - 1
        return bi * tpb + jnp.where(s == 0, n_lat, lat)

    return pl.pallas_call(
        functools.partial(_gla_scan_kernel, rev),
        grid=(batch, tpb),
        in_specs=[pl.BlockSpec((tg, GLA_QK), lambda bi, s: (tile(bi, s), 0)),
                  pl.BlockSpec((tg, GLA_QK), lambda bi, s: (tile(bi, s), 0)),
                  pl.BlockSpec((GLA_HEADS, tg, GLA_CHUNK), lambda bi, s: (0, tile(bi, s), 0)),
                  pl.BlockSpec((GLA_HEADS, 1, tg // GLA_CHUNK, GLA_DK), lambda bi, s: (0, tile(bi, s), 0, 0)),
                  pl.BlockSpec((tg, GLA_V), lambda bi, s: (tile(bi, s), 0))],
        out_specs=pl.BlockSpec((tg, GLA_V), lambda bi, s: (tile(bi, s), 0)),
        out_shape=jax.ShapeDtypeStruct((n, GLA_V), F32),
        scratch_shapes=[pltpu.VMEM((GLA_HEADS, GLA_DV, GLA_DK), F32)],
        compiler_params=_cparams(("arbitrary", "arbitrary")),
        name="gla_scan_bwd" if rev else "gla_scan_fwd",
    )(qe, kd, a, gd, v)


def _gla_out_kernel(of_ref, ob_ref, r_ref, g_ref, o_ref):
    o = of_ref[...] + ob_ref[...]
    r = r_ref[...]
    gate = r * jax.nn.sigmoid(r)
    gn = g_ref[...]
    for hd in range(GLA_HEADS):
        cols = slice(hd * GLA_DV, (hd + 1) * GLA_DV)
        oh = o[:, cols]
        ms = jnp.mean(oh * oh, axis=-1, keepdims=True)
        o_ref[:, cols] = (oh * lax.rsqrt(ms + RMS_EPS) * gn * gate[:, cols]).astype(o_ref.dtype)


def _gla_out(o_f, o_b, r, norm_g):
    n, dv = o_f.shape
    tm = _row_tile(n)
    spec = pl.BlockSpec((tm, dv), lambda i: (i, 0))
    return pl.pallas_call(
        _gla_out_kernel,
        grid=(n // tm,),
        in_specs=[spec, spec, spec, pl.BlockSpec((1, GLA_DV), lambda i: (0, 0))],
        out_specs=spec,
        out_shape=jax.ShapeDtypeStruct((n, dv), BF16),
        compiler_params=_cparams(("arbitrary",)),
        name="gla_out",
    )(o_f, o_b, r, norm_g)


def _pack_bf16(x):
    half = x.shape[1] // 2
    lo = lax.bitcast_convert_type(x[:, :half].astype(BF16).astype(F32), jnp.uint32)
    hi = lax.bitcast_convert_type(x[:, half:].astype(BF16).astype(F32), jnp.uint32)
    return (lo >> 16) | (hi & jnp.uint32(0xFFFF0000))


def _unpack_bf16(p):
    lo = lax.bitcast_convert_type(p << 16, F32)
    hi = lax.bitcast_convert_type(p & jnp.uint32(0xFFFF0000), F32)
    return lo, hi


def _residual_ln(x, a, t, ln, alpha):
    y = alpha * x + t[0:1] * a
    mu = jnp.mean(y, axis=-1, keepdims=True)
    yc = y - mu
    var = jnp.mean(yc * yc, axis=-1, keepdims=True)
    xn = yc * lax.rsqrt(var + LN_EPS) * ln[0:1] + ln[1:2]
    return xn, xn * (1.0 + t[1:2]) + t[2:3]


def _top4_softmax(lt):
    e = lt.shape[0]
    io = lax.broadcasted_iota(jnp.int32, lt.shape, 0)
    work = lt
    idxs, vals = [], []
    for _ in range(TOP_K):
        m = jnp.max(work, axis=0, keepdims=True)
        ik = jnp.min(jnp.where(work == m, io, e), axis=0, keepdims=True)
        idxs.append(ik)
        vals.append(m)
        work = jnp.where(io == ik, -jnp.inf, work)
    ex = [jnp.exp(v - vals[0]) for v in vals]
    den = ex[0] + ex[1] + ex[2] + ex[3]
    return jnp.concatenate(idxs, axis=0), jnp.concatenate([x / den for x in ex], axis=0)


def _post_kernel(n_act, tpb, alpha, *refs):
    acts = refs[:n_act]
    ws = refs[n_act:2 * n_act]
    x_ref, tab_ref, ln_ref, rw_ref, rb_ref, xo_ref, h_ref, idx_ref, gate_ref = refs[2 * n_act:]
    tm = x_ref.shape[0]
    a = jnp.dot(acts[0][...], ws[0][...], preferred_element_type=F32)
    for k in range(1, n_act):
        a = a + jnp.dot(acts[k][...], ws[k][...], preferred_element_type=F32)
    ln = ln_ref[...]
    for s in range(tm // TOK_TILE):
        rows = slice(s * TOK_TILE, (s + 1) * TOK_TILE)
        t = tab_ref[_tab_row(pl.program_id(0) * (tm // TOK_TILE) + s, tpb)]
        xn, h = _residual_ln(x_ref[rows], a[rows], t, ln, alpha)
        xo_ref[rows] = xn
        h_ref[rows] = _pack_bf16(h)
        lt = lax.dot_general(rw_ref[...], h, (((1,), (1,)), ((), ())), precision=HIGHEST,
                             preferred_element_type=F32) + rb_ref[...]
        idx, gates = _top4_softmax(lt)
        idx_ref[:, rows] = idx
        gate_ref[:, rows] = gates


def _post(acts, ws, x, tab, ln, rw_t, rb, tpb, alpha):
    n, d = x.shape
    tm = _row_tile(n)
    e = rw_t.shape[0]
    row = lambda i: (i, 0)
    fixed = lambda i: (0, 0)
    return pl.pallas_call(
        functools.partial(_post_kernel, len(acts), tpb, alpha),
        grid=(n // tm,),
        in_specs=([pl.BlockSpec((tm, a.shape[1]), row) for a in acts]
                  + [pl.BlockSpec(w.shape, fixed) for w in ws]
                  + [pl.BlockSpec((tm, d), row),
                     pl.BlockSpec(tab.shape, lambda i: (0, 0, 0)),
                     pl.BlockSpec(ln.shape, fixed),
                     pl.BlockSpec((e, d), fixed),
                     pl.BlockSpec((e, 1), fixed)]),
        out_specs=[pl.BlockSpec((tm, d), row), pl.BlockSpec((tm, d // 2), row),
                   pl.BlockSpec((TOP_K, tm), lambda i: (0, i)), pl.BlockSpec((TOP_K, tm), lambda i: (0, i))],
        out_shape=[jax.ShapeDtypeStruct((n, d), F32), jax.ShapeDtypeStruct((n, d // 2), jnp.uint32),
                   jax.ShapeDtypeStruct((TOP_K, n), jnp.int32), jax.ShapeDtypeStruct((TOP_K, n), F32)],
        compiler_params=_cparams(("arbitrary",)),
        name="post",
    )(*acts, *ws, x, tab, ln, rw_t, rb)


def _rank_kernel(idx_ref, tri_ref, rank_ref, cnt_ref, carry_ref):
    @pl.when(pl.program_id(0) == 0)
    def _():
        carry_ref[...] = jnp.zeros_like(carry_ref)

    idx = idx_ref[...]
    e = carry_ref.shape[0]
    tr = idx.shape[1]
    io = lax.broadcasted_iota(jnp.int32, (e, tr), 0)
    chosen = jnp.zeros((e, tr), F32)
    for k in range(TOP_K):
        chosen = chosen + (idx[k:k + 1] == io).astype(F32)
    cum = jnp.dot(chosen.astype(BF16), tri_ref[...], preferred_element_type=F32)
    base = carry_ref[:, 0:1]
    excl = base + cum - chosen
    ranks = [jnp.sum(jnp.where(idx[k:k + 1] == io, excl, 0.0), axis=0, keepdims=True) for k in range(TOP_K)]
    rank_ref[...] = jnp.concatenate(ranks, axis=0).astype(jnp.int32)
    carry_ref[...] = carry_ref[...] + jnp.sum(chosen, axis=1, keepdims=True)
    cnt_ref[...] = carry_ref[...]


def _rank(idx_t, n_experts):
    n = idx_t.shape[1]
    tr = _row_tile(n)
    tri = (np.arange(tr)[:, None] <= np.arange(tr)[None, :]).astype(np.float32)
    return pl.pallas_call(
        _rank_kernel,
        grid=(n // tr,),
        in_specs=[pl.BlockSpec((TOP_K, tr), lambda i: (0, i)),
                  pl.BlockSpec((tr, tr), lambda i: (0, 0))],
        out_specs=[pl.BlockSpec((TOP_K, tr), lambda i: (0, i)),
                   pl.BlockSpec((n_experts, LANES), lambda i: (0, 0))],
        out_shape=[jax.ShapeDtypeStruct((TOP_K, n), jnp.int32),
                   jax.ShapeDtypeStruct((n_experts, LANES), F32)],
        scratch_shapes=[pltpu.VMEM((n_experts, LANES), F32)],
        compiler_params=_cparams(("arbitrary",)),
        name="rank",
    )(idx_t, jnp.asarray(tri, BF16))


def _expert_kernel(be_ref, nb_ref, x_ref, w1_ref, b1_ref, w2_ref, b2_ref, o_ref, w1b_ref, w2b_ref):
    i = pl.program_id(0)
    used = i < nb_ref[0]
    new_expert = jnp.logical_or(i == 0, be_ref[i] != be_ref[jnp.maximum(i - 1, 0)])

    @pl.when(jnp.logical_and(used, new_expert))
    def _():
        w1b_ref[...] = w1_ref[0].astype(BF16)
        w2b_ref[...] = w2_ref[0].astype(BF16)

    @pl.when(used)
    def _():
        x_lo, x_hi = _unpack_bf16(x_ref[...])
        kh = x_lo.shape[1]
        hid = (jnp.dot(x_lo.astype(BF16), w1b_ref[:kh], preferred_element_type=F32)
               + jnp.dot(x_hi.astype(BF16), w1b_ref[kh:], preferred_element_type=F32) + b1_ref[0])
        half = hid.shape[1] // 2
        glu = jnp.minimum(hid[:, :half], SWIGLU_LIMIT)
        lin = jnp.clip(hid[:, half:], -SWIGLU_LIMIT, SWIGLU_LIMIT)
        act = glu * jax.nn.sigmoid(SWIGLU_ALPHA * glu) * (lin + 1.0)
        y = jnp.dot(act.astype(BF16), w2b_ref[...], preferred_element_type=F32) + b2_ref[0]
        o_ref[...] = _pack_bf16(y)

    @pl.when(i >= nb_ref[0])
    def _():
        o_ref[...] = jnp.zeros_like(o_ref)


def _experts(block_expert, n_used, x_pad, layer, w1, b1, w2, b2):
    n_pad, dp = x_pad.shape
    depth, e, d, dh2 = w1.shape
    n_blocks = n_pad // EXPERT_BLOCK
    grid_spec = pltpu.PrefetchScalarGridSpec(
        num_scalar_prefetch=2,
        grid=(n_blocks,),
        in_specs=[pl.BlockSpec((EXPERT_BLOCK, dp), lambda i, be, nb: (i, 0)),
                  pl.BlockSpec((None, 1, d, dh2), lambda i, be, nb: (layer, be[i], 0, 0)),
                  pl.BlockSpec((None, 1, 1, dh2), lambda i, be, nb: (layer, be[i], 0, 0)),
                  pl.BlockSpec((None, 1, dh2 // 2, d), lambda i, be, nb: (layer, be[i], 0, 0)),
                  pl.BlockSpec((None, 1, 1, d), lambda i, be, nb: (layer, be[i], 0, 0))],
        out_specs=pl.BlockSpec((EXPERT_BLOCK, dp), lambda i, be, nb: (i, 0)),
        scratch_shapes=[pltpu.VMEM((d, dh2), BF16), pltpu.VMEM((dh2 // 2, d), BF16)],
    )
    return pl.pallas_call(
        _expert_kernel,
        grid_spec=grid_spec,
        out_shape=jax.ShapeDtypeStruct((n_pad, dp), jnp.uint32),
        compiler_params=pltpu.CompilerParams(dimension_semantics=("arbitrary",),
                                             vmem_limit_bytes=EXPERT_VMEM_LIMIT),
        name="experts",
    )(block_expert, n_used, x_pad, w1, b1.reshape(depth, e, 1, dh2), w2, b2.reshape(depth, e, 1, d))


def _combine_kernel(tpb, alpha, y_ref, gate_ref, x_ref, tab_ref, ln_ref, xo_ref, h_ref):
    g = gate_ref[...]
    y_lo, y_hi = None, None
    for k in range(TOP_K):
        lo, hi = _unpack_bf16(y_ref[k])
        gk = g[:, k:k + 1]
        y_lo = lo * gk if y_lo is None else y_lo + lo * gk
        y_hi = hi * gk if y_hi is None else y_hi + hi * gk
    y = jnp.concatenate([y_lo, y_hi], axis=1)
    t = tab_ref[_tab_row(pl.program_id(0), tpb)]
    xn, h = _residual_ln(x_ref[...], y, t, ln_ref[...], alpha)
    xo_ref[...] = xn
    h_ref[...] = h.astype(h_ref.dtype)


def _combine(y_g, gates, x, tab, ln, tpb, alpha):
    n, d = x.shape
    tm = TOK_TILE
    row = lambda i: (i, 0)
    return pl.pallas_call(
        functools.partial(_combine_kernel, tpb, alpha),
        grid=(n // tm,),
        in_specs=[pl.BlockSpec((TOP_K, tm, d // 2), lambda i: (0, i, 0)),
                  pl.BlockSpec((tm, TOP_K), row),
                  pl.BlockSpec((tm, d), row),
                  pl.BlockSpec(tab.shape, lambda i: (0, 0, 0)),
                  pl.BlockSpec(ln.shape, lambda i: (0, 0))],
        out_specs=[pl.BlockSpec((tm, d), row), pl.BlockSpec((tm, d), row)],
        out_shape=[jax.ShapeDtypeStruct((n, d), F32), jax.ShapeDtypeStruct((n, d), BF16)],
        compiler_params=_cparams(("arbitrary",)),
        name="combine",
    )(y_g, gates, x, tab, ln)


def _sc_mesh():
    return plsc.VectorSubcoreMesh(core_axis_name="c", subcore_axis_name="s")


def _sc_split(rows, mesh):
    workers = mesh.num_cores * mesh.num_subcores
    per = rows // workers
    assert per * workers == rows
    chunk = SC_CHUNK if per % SC_CHUNK == 0 else 8
    assert per % chunk == 0
    return per, chunk


def _sc_scatter_rows(x, idx, n_out):
    r, c = x.shape
    mesh = _sc_mesh()
    per, chunk = _sc_split(r, mesh)

    @functools.partial(pl.kernel, out_type=jax.ShapeDtypeStruct((n_out, c), x.dtype), mesh=mesh,
                       scratch_types=[pltpu.VMEM((chunk,), jnp.int32), pltpu.VMEM((chunk, c), x.dtype),
                                      pltpu.SemaphoreType.DMA])
    def scatter(x_hbm, i_hbm, o_hbm, idx_v, rows_v, sem):
        base = (lax.axis_index("s") * mesh.num_cores + lax.axis_index("c")) * per

        @pl.loop(0, per // chunk)
        def _(j):
            off = pl.multiple_of(base + j * chunk, chunk)
            pltpu.sync_copy(x_hbm.at[pl.ds(off, chunk)], rows_v)
            for k in range(TOP_K):
                pltpu.sync_copy(i_hbm.at[pl.ds(k * r + off, chunk)], idx_v)
                pltpu.async_copy(rows_v, o_hbm.at[idx_v], sem).wait()

    return scatter(x, idx)


def _sc_gather_rows(table, idx):
    m = idx.shape[0]
    c = table.shape[1]
    mesh = _sc_mesh()
    per, chunk = _sc_split(m, mesh)

    @functools.partial(pl.kernel, out_type=jax.ShapeDtypeStruct((m, c), table.dtype), mesh=mesh,
                       scratch_types=[pltpu.VMEM((chunk,), jnp.int32), pltpu.VMEM((chunk, c), table.dtype),
                                      pltpu.SemaphoreType.DMA])
    def gather(t_hbm, i_hbm, o_hbm, idx_v, rows_v, sem):
        base = (lax.axis_index("s") * mesh.num_cores + lax.axis_index("c")) * per

        @pl.loop(0, per // chunk)
        def _(j):
            off = pl.multiple_of(base + j * chunk, chunk)
            pltpu.sync_copy(i_hbm.at[pl.ds(off, chunk)], idx_v)
            pltpu.async_copy(t_hbm.at[idx_v], rows_v, sem).wait()
            pltpu.sync_copy(rows_v, o_hbm.at[pl.ds(off, chunk)])

    return gather(table, idx)


def _moe(h, idx_t, layer, w1, b1, w2, b2):
    n, dp = h.shape
    e = w1.shape[1]
    m = n * TOP_K
    rank_t, cnt = _rank(idx_t, e)
    sizes = cnt[:, 0].astype(jnp.int32)
    padded = (sizes + EXPERT_BLOCK - 1) // EXPERT_BLOCK * EXPERT_BLOCK
    pad_ends = jnp.cumsum(padded)
    pad_starts = pad_ends - padded
    ids = jnp.arange(e, dtype=jnp.int32)[:, None, None]
    dest_t = jnp.sum(jnp.where(idx_t[None] == ids, pad_starts[:, None, None], 0), axis=0) + rank_t
    dest = dest_t.reshape(-1)
    n_blocks = (m + e * (EXPERT_BLOCK - 1)) // EXPERT_BLOCK + 1
    n_pad = n_blocks * EXPERT_BLOCK
    block_start = jnp.arange(n_blocks, dtype=jnp.int32) * EXPERT_BLOCK
    block_expert = jnp.minimum(jnp.sum((pad_ends[None, :] <= block_start[:, None]).astype(jnp.int32), axis=1), e - 1)
    n_used = (pad_ends[-1:] // EXPERT_BLOCK).astype(jnp.int32)
    x_pad = _sc_scatter_rows(h, dest, n_pad)
    y_pad = _experts(block_expert, n_used, x_pad, layer, w1, b1, w2, b2)
    return _sc_gather_rows(y_pad, dest).reshape(TOP_K, n, dp)


def _rope_tables(t_lat, n_ctx):
    t = jnp.arange(t_lat)
    row = (t // GRID_W).astype(F32)
    col = (t % GRID_W).astype(F32)
    nf = GLA_DK // 4
    freqs = ROPE_BASE ** (-jnp.arange(nf, dtype=F32) / nf)
    ang = jnp.concatenate([row[:, None] * freqs, col[:, None] * freqs], axis=-1)
    cos, sin = jnp.cos(ang), jnp.sin(ang)
    cos2 = jnp.concatenate([cos, cos], axis=-1)
    sin2 = jnp.concatenate([-sin, sin], axis=-1)
    return (jnp.concatenate([cos2, jnp.ones((n_ctx, GLA_DK), F32)], axis=0),
            jnp.concatenate([sin2, jnp.zeros((n_ctx, GLA_DK), F32)], axis=0))


def _chunk_tri(tg, rev):
    t = np.arange(tg)
    same = (t[:, None] // GLA_CHUNK) == (t[None, :] // GLA_CHUNK)
    side = (t[None, :] >= t[:, None]) if rev else (t[None, :] <= t[:, None])
    return jnp.asarray((same & side).astype(np.float32), BF16)


def _table(mods, rows, batch):
    lat = jnp.stack([mods[:batch, r] for r in rows], axis=1)
    ctx = jnp.broadcast_to(jnp.stack([mods[batch, r] for r in rows], axis=0)[None], lat.shape)
    tab = jnp.stack([lat, ctx], axis=1).reshape(2 * batch, len(rows), -1)
    return jnp.pad(tab, ((0, 0), (0, 8 - len(rows)), (0, 0)))


@jax.jit
def _forward(x, c, ctx, c_ctx, ada_w, ada_b, ln_g, ln_b, ab_w_in, ab_pool_w, ab_pool_scale, ab_rpb,
             ab_w_out, gla_w_in, gla_w_gate, gla_b_gate, gla_norm_g, gla_w_out, router_w, router_b,
             exp_w1, exp_b1, exp_w2, exp_b2):
    batch, t_lat, d = x.shape
    n_ctx = ctx.shape[1]
    depth = ada_w.shape[0]
    assert d == D_MODEL and n_ctx == TOK_TILE and t_lat % TOK_TILE == 0
    rows = t_lat // GRID_W
    assert rows % NA_QROWS == 0 and rows >= NA_KROWS + NA_QROWS
    n_lat = t_lat // TOK_TILE
    tpb = n_lat + 1
    l = t_lat + n_ctx
    n = batch * l
    alpha = (2.0 * depth) ** 0.25

    cc = jnp.concatenate([c, c_ctx[None], jnp.zeros((16 - batch - 1, d), F32)], axis=0)
    mods = _mods(cc, ada_w, ada_b).reshape(depth, 16, N_MOD, d)

    z = jnp.concatenate([x, ctx], axis=1).reshape(n, d)
    h = _modulate(z, _table(mods[0], (1, 0), batch), tpb)
    cos2, sin2 = _rope_tables(t_lat, n_ctx)

    for i in range(depth):
        j = i // 2
        last = i == depth - 1
        tab1 = _table(mods[i], (2, 4, 3), batch)
        ln1 = jnp.stack([ln_g[i, 0], ln_b[i, 0]])
        ln2 = jnp.stack([ln_g[i, 1], ln_b[i, 1]])
        rw_t = router_w[i].T
        rb = router_b[i][:, None]
        if i % 2 == 0:
            u, qkv = _proj(h, ab_w_in[j].astype(BF16), [(0, POOL_WIDTH), (POOL_WIDTH, POOL_WIDTH + 3 * NA_WIDTH)],
                           [F32, BF16])
            w_blk = jax.scipy.linalg.block_diag(*[ab_pool_w[j, g] for g in range(len(POOL_WINDOWS))])
            pooled = _pool(u.reshape(batch, l, POOL_WIDTH), w_blk.astype(BF16), ab_pool_scale[j][None, :],
                           n_lat, t_lat, n_ctx)
            bias = _na_bias_tables(ab_rpb[j], rows, n_ctx)
            attn = _na(qkv.reshape(batch, l, 3 * NA_WIDTH), bias, n_lat)
            w_out = ab_w_out[j].astype(BF16)
            acts = [pooled.reshape(n, POOL_WIDTH), attn.reshape(n, NA_WIDTH)]
            ws = [w_out[:POOL_WIDTH], w_out[POOL_WIDTH:]]
        else:
            qk, v, r, g = _proj(h, gla_w_in[j].astype(BF16),
                                [(0, 2 * GLA_QK), (2 * GLA_QK, 2 * GLA_QK + GLA_V),
                                 (2 * GLA_QK + GLA_V, 2 * GLA_QK + 2 * GLA_V),
                                 (2 * GLA_QK + 2 * GLA_V, 2 * GLA_QK + 2 * GLA_V + 2 * GATE_RANK)],
                                [F32, BF16, F32, F32])
            o_dirs = []
            for rev in (False, True):
                dr = int(rev)
                qe, kd, a, gd = _gla_prep(rev, qk, g, cos2, sin2, gla_w_gate[j, dr], gla_b_gate[j, dr][None, :],
                                          _chunk_tri(TOK_TILE, rev), tpb)
                o_dirs.append(_gla_scan(rev, qe, kd, a, gd, v, batch, tpb))
            acts = [_gla_out(o_dirs[0], o_dirs[1], r, gla_norm_g[j][None, :])]
            ws = [gla_w_out[j].astype(BF16)]
        z, h, idx_t, gates_t = _post(acts, ws, z, tab1, ln1, rw_t, rb, tpb, alpha)
        y_g = _moe(h, idx_t, i, exp_w1, exp_b1, exp_w2, exp_b2)
        nxt = mods[i + 1] if not last else mods[i]
        tab2 = _table(jnp.concatenate([mods[i][:, 5:6], nxt[:, 1:2], nxt[:, 0:1]], axis=1), (0, 1, 2), batch)
        z, h = _combine(y_g, gates_t.T, z, tab2, ln2, tpb, alpha)
    return z.reshape(batch, l, d)[:, :t_lat]


def kernel(x, c, ctx, c_ctx, ada_w, ada_b, ln_g, ln_b, ab_w_in, ab_pool_w, ab_pool_scale, ab_rpb, ab_w_out,
           gla_w_in, gla_w_gate, gla_b_gate, gla_norm_g, gla_w_out, router_w, router_b, exp_w1, exp_b1, exp_w2,
           exp_b2):
    return _forward(x, c, ctx, c_ctx, ada_w, ada_b, ln_g, ln_b, ab_w_in, ab_pool_w, ab_pool_scale, ab_rpb,
                    ab_w_out, gla_w_in, gla_w_gate, gla_b_gate, gla_norm_g, gla_w_out, router_w, router_b,
                    exp_w1, exp_b1, exp_w2, exp_b2)
```

```python
import functools
import math

import numpy as np
import jax
import jax.numpy as jnp
from jax import lax
from jax.experimental import pallas as pl
from jax.experimental.pallas import tpu as pltpu
from jax.experimental.pallas import tpu_sc as plsc

F32 = jnp.float32
BF16 = jnp.bfloat16
HIGHEST = lax.Precision.HIGHEST

D_MODEL = 1024
GRID_W = 64
N_MOD = 6
POOL_WINDOWS = (2, 4, 8, 16)
POOL_WIDTH = D_MODEL // 4
POOL_GROUP_DIM = POOL_WIDTH // len(POOL_WINDOWS)
POOL_HALO = max(POOL_WINDOWS) // 2
NA_HEAD_DIM = 64
NA_HEADS = (D_MODEL - POOL_WIDTH) // NA_HEAD_DIM
NA_WIDTH = NA_HEADS * NA_HEAD_DIM
WIN_H = 8
WIN_W = 16
GLA_HEADS = 4
GLA_DK = D_MODEL // 2 // GLA_HEADS
GLA_DV = D_MODEL // GLA_HEADS
GATE_RANK = 16
GATE_NORM = 16.0
GLA_CHUNK = 64
GLA_SUB = 16
GLA_QK = GLA_HEADS * GLA_DK
GLA_V = GLA_HEADS * GLA_DV
ROPE_BASE = 10000.0
TOP_K = 4
SWIGLU_LIMIT = 7.0
SWIGLU_ALPHA = 1.702
LN_EPS = 1e-5
RMS_EPS = 1e-6
NEG_INF = -1e30

LANES = 128
TOK_TILE = 256
NA_QROWS = 4
NA_KROWS = 12
VMEM_LIMIT = 48 * 1024 * 1024
EXPERT_VMEM_LIMIT = 56 * 1024 * 1024
EXPERT_BLOCK = 512
SC_CHUNK = 64


def _cparams(sem):
    return pltpu.CompilerParams(dimension_semantics=sem, vmem_limit_bytes=VMEM_LIMIT)


def _row_tile(n):
    return 2 * TOK_TILE if n % (2 * TOK_TILE) == 0 else TOK_TILE


def _mods_kernel(c_ref, w_ref, b_ref, o_ref):
    cv = c_ref[...]
    sc = cv * jax.nn.sigmoid(cv)
    o_ref[0] = jnp.dot(sc, w_ref[0], precision=HIGHEST, preferred_element_type=F32) + b_ref[0]


def _mods(cc, ada_w, ada_b):
    depth, d, n = ada_w.shape
    r = cc.shape[0]
    tn = n // 4
    return pl.pallas_call(
        _mods_kernel,
        grid=(depth, n // tn),
        in_specs=[pl.BlockSpec((r, d), lambda i, j: (0, 0)),
                  pl.BlockSpec((1, d, tn), lambda i, j: (i, 0, j)),
                  pl.BlockSpec((1, 1, tn), lambda i, j: (i, 0, j))],
        out_specs=pl.BlockSpec((1, r, tn), lambda i, j: (i, 0, j)),
        out_shape=jax.ShapeDtypeStruct((depth, r, n), F32),
        compiler_params=_cparams(("arbitrary", "arbitrary")),
        name="mods",
    )(cc, ada_w, ada_b.reshape(depth, 1, n))


def _tab_row(g, tpb):
    return (g // tpb) * 2 + (g % tpb == tpb - 1).astype(jnp.int32)


def _modulate_kernel(tpb, x_ref, tab_ref, h_ref):
    t = tab_ref[_tab_row(pl.program_id(0), tpb)]
    h_ref[...] = (x_ref[...] * (1.0 + t[0:1]) + t[1:2]).astype(h_ref.dtype)


def _modulate(z, tab, tpb):
    n, d = z.shape
    return pl.pallas_call(
        functools.partial(_modulate_kernel, tpb),
        grid=(n // TOK_TILE,),
        in_specs=[pl.BlockSpec((TOK_TILE, d), lambda i: (i, 0)),
                  pl.BlockSpec(tab.shape, lambda i: (0, 0, 0))],
        out_specs=pl.BlockSpec((TOK_TILE, d), lambda i: (i, 0)),
        out_shape=jax.ShapeDtypeStruct((n, d), BF16),
        compiler_params=_cparams(("arbitrary",)),
        name="modulate",
    )(z, tab)


def _proj_kernel(splits, x_ref, w_ref, *out_refs):
    x = x_ref[...]
    for (a, b), o_ref in zip(splits, out_refs):
        o_ref[...] = jnp.dot(x, w_ref[:, a:b], preferred_element_type=F32).astype(o_ref.dtype)


def _proj(h, w, splits, dtypes):
    n, k = h.shape
    m = w.shape[1]
    tm = _row_tile(n)
    return pl.pallas_call(
        functools.partial(_proj_kernel, tuple(splits)),
        grid=(n // tm,),
        in_specs=[pl.BlockSpec((tm, k), lambda i: (i, 0)),
                  pl.BlockSpec((k, m), lambda i: (0, 0))],
        out_specs=[pl.BlockSpec((tm, b - a), lambda i: (i, 0)) for a, b in splits],
        out_shape=[jax.ShapeDtypeStruct((n, b - a), dt) for (a, b), dt in zip(splits, dtypes)],
        compiler_params=_cparams(("arbitrary",)),
        name="proj",
    )(h, w)


def _pool_kernel(n_lat, t_lat, t_ctx, prev_ref, cur_ref, next_ref, w_ref, scale_ref, o_ref, halo_ref):
    j = pl.program_id(1)
    is_ctx = j == n_lat
    has_prev = jnp.logical_and(j > 0, jnp.logical_not(is_ctx))
    has_next = j < n_lat - 1
    cur = cur_ref[0]
    hl = POOL_HALO
    halo_ref[0:hl] = jnp.where(has_prev, prev_ref[0, TOK_TILE - hl:TOK_TILE], 0.0)
    halo_ref[hl:hl + TOK_TILE] = cur
    halo_ref[hl + TOK_TILE:2 * hl + TOK_TILE] = jnp.where(has_next, next_ref[0, 0:hl], 0.0)

    shape = cur.shape
    lane = lax.broadcasted_iota(jnp.int32, shape, 1)
    group = lane // POOL_GROUP_DIM
    half = jnp.ones(shape, jnp.int32)
    for gi, wdw in enumerate(POOL_WINDOWS):
        half = jnp.where(group == gi, wdw // 2, half)
    acc = jnp.zeros(shape, F32)
    for off in range(-hl, hl):
        v = halo_ref[hl + off:hl + off + TOK_TILE]
        inside = (half >= -off) if off < 0 else (half > off)
        acc = acc + jnp.where(inside, v, 0.0)
    pos0 = jnp.where(is_ctx, 0, j * TOK_TILE)
    seq = jnp.where(is_ctx, t_ctx, t_lat)
    t = pos0 + lax.broadcasted_iota(jnp.int32, shape, 0)
    cnt = jnp.minimum(t + half, seq) - jnp.maximum(t - half, 0)
    pooled = acc / cnt.astype(F32) - cur
    y = jnp.dot(pooled.astype(BF16), w_ref[...], preferred_element_type=F32) * scale_ref[...]
    o_ref[0] = y.astype(o_ref.dtype)


def _pool(u, w_blk, scale, n_lat, t_lat, t_ctx):
    b, l, pw = u.shape
    tpb = l // TOK_TILE
    blk = (1, TOK_TILE, pw)
    return pl.pallas_call(
        functools.partial(_pool_kernel, n_lat, t_lat, t_ctx),
        grid=(b, tpb),
        in_specs=[pl.BlockSpec(blk, lambda bi, j: (bi, jnp.maximum(j - 1, 0), 0)),
                  pl.BlockSpec(blk, lambda bi, j: (bi, j, 0)),
                  pl.BlockSpec(blk, lambda bi, j: (bi, jnp.minimum(j + 1, tpb - 1), 0)),
                  pl.BlockSpec((pw, pw), lambda bi, j: (0, 0)),
                  pl.BlockSpec((1, pw), lambda bi, j: (0, 0))],
        out_specs=pl.BlockSpec(blk, lambda bi, j: (bi, j, 0)),
        out_shape=jax.ShapeDtypeStruct((b, l, pw), BF16),
        scratch_shapes=[pltpu.VMEM((TOK_TILE + 2 * POOL_HALO, pw), F32)],
        compiler_params=_cparams(("arbitrary", "arbitrary")),
        name="pool",
    )(u, u, u, w_blk, scale)


def _na_bias_tables(rpb, rows):
    n_i = rows // NA_QROWS
    heads = rpb.shape[0]
    a = np.arange(NA_QROWS)
    kr = np.arange(NA_KROWS)
    cq = np.arange(GRID_W)
    ws = np.clip(cq - WIN_W // 2, 0, GRID_W - WIN_W)
    ok_col = (cq[None, :] >= ws[:, None]) & (cq[None, :] < ws[:, None] + WIN_W)
    dcol = np.clip(cq[None, :] - cq[:, None] + WIN_W - 1, 0, 2 * WIN_W - 2)
    oh_col = (dcol[..., None] == np.arange(2 * WIN_W - 1)).astype(np.float32)
    tabs = []
    for i in (0, 1, n_i - 1):
        start = int(np.clip(NA_QROWS * i - WIN_H // 2, 0, rows - NA_KROWS))
        r = NA_QROWS * i + a
        krow = start + kr
        rs = np.clip(r - WIN_H // 2, 0, rows - WIN_H)
        ok_row = (krow[None, :] >= rs[:, None]) & (krow[None, :] < rs[:, None] + WIN_H)
        drow = np.clip(krow[None, :] - r[:, None] + WIN_H - 1, 0, 2 * WIN_H - 2)
        oh_row = (drow[..., None] == np.arange(2 * WIN_H - 1)).astype(np.float32)
        by_row = jnp.einsum('hrc,akr->hakc', rpb, oh_row, precision=HIGHEST)
        bias = jnp.einsum('hakc,qjc->haqkj', by_row, oh_col, precision=HIGHEST)
        ok = ok_row[:, None, :, None] & ok_col[None, :, None, :]
        tabs.append(jnp.where(ok[None], bias, NEG_INF).reshape(heads, TOK_TILE, NA_KROWS * GRID_W))
    tabs.append(jnp.full_like(tabs[0], NEG_INF))
    return jnp.stack(tabs).astype(F32)


def _na_kernel(q_ref, k0_ref, k1_ref, k2_ref, kc_ref, v0_ref, v1_ref, v2_ref, vc_ref, bias_ref, o_ref):
    q = q_ref[0]
    k_refs = (k0_ref, k1_ref, k2_ref, kc_ref)
    v_refs = (v0_ref, v1_ref, v2_ref, vc_ref)
    n_band = len(k_refs) - 1
    lane = lax.broadcasted_iota(jnp.int32, q.shape, 1)
    first = lane < NA_HEAD_DIM
    outs = []
    for hh in range(2):
        qm = jnp.where(first if hh == 0 else jnp.logical_not(first), q, jnp.zeros_like(q)) * NA_HEAD_DIM ** -0.5
        scores = []
        for j, k_ref in enumerate(k_refs):
            s = lax.dot_general(qm, k_ref[0], (((1,), (1,)), ((), ())), preferred_element_type=F32)
            if j < n_band:
                s = s + bias_ref[0, hh, :, j * TOK_TILE:(j + 1) * TOK_TILE]
            scores.append(s)
        m = functools.reduce(jnp.maximum, [jnp.max(s, axis=-1, keepdims=True) for s in scores])
        l = 0.0
        o = 0.0
        for s, v_ref in zip(scores, v_refs):
            p = jnp.exp(s - m)
            l = l + jnp.sum(p, axis=-1, keepdims=True)
            o = o + jnp.dot(p.astype(BF16), v_ref[0], preferred_element_type=F32)
        outs.append(o / l)
    o_ref[0] = jnp.where(first, outs[0], outs[1]).astype(o_ref.dtype)


def _na(qkv, bias, n_lat):
    b, l, _ = qkv.shape
    tpb = l // TOK_TILE
    n_pairs = NA_WIDTH // LANES
    blk = (1, TOK_TILE, LANES)

    def kstart(i):
        return jnp.clip(i - 1, 0, n_lat - NA_KROWS // NA_QROWS)

    def btype(i):
        return jnp.where(i == 0, 0, jnp.where(i == n_lat - 1, 2, jnp.where(i == n_lat, 3, 1)))

    def kv_spec(col0, j):
        return pl.BlockSpec(blk, lambda hp, i, bi: (bi, kstart(i) + j, col0 + hp))

    def ctx_spec(col0):
        return pl.BlockSpec(blk, lambda hp, i, bi: (bi, n_lat, col0 + hp))

    nk = bias.shape[-1]
    return pl.pallas_call(
        _na_kernel,
        grid=(n_pairs, tpb, b),
        in_specs=[pl.BlockSpec(blk, lambda hp, i, bi: (bi, i, hp)),
                  kv_spec(n_pairs, 0), kv_spec(n_pairs, 1), kv_spec(n_pairs, 2), ctx_spec(n_pairs),
                  kv_spec(2 * n_pairs, 0), kv_spec(2 * n_pairs, 1), kv_spec(2 * n_pairs, 2), ctx_spec(2 * n_pairs),
                  pl.BlockSpec((1, 2, TOK_TILE, nk), lambda hp, i, bi: (btype(i), hp, 0, 0))],
        out_specs=pl.BlockSpec(blk, lambda hp, i, bi: (bi, i, hp)),
        out_shape=jax.ShapeDtypeStruct((b, l, NA_WIDTH), BF16),
        compiler_params=_cparams(("arbitrary", "arbitrary", "arbitrary")),
        name="na",
    )(qkv, qkv, qkv, qkv, qkv, qkv, qkv, qkv, qkv, bias)


def _log_sigmoid(z):
    return jnp.minimum(z, 0.0) - jnp.log(1.0 + jnp.exp(-jnp.abs(z)))


def _gla_prep_kernel(rev, q_ref, k_ref, g_ref, cos_ref, sin_ref, wg_ref, bg_ref, tri_ref,
                     qe_ref, kd_ref, a_ref, gd_ref, b_scr, qr_scr, kr_scr):
    tg = q_ref.shape[0]
    head_cols = [slice(hd * GLA_DK, (hd + 1) * GLA_DK) for hd in range(GLA_HEADS)]
    gcol = GATE_RANK if rev else 0
    gg = g_ref[:, gcol:gcol + GATE_RANK]
    z = jnp.dot(gg, wg_ref[...], precision=HIGHEST, preferred_element_type=F32) + bg_ref[...]
    la = _log_sigmoid(z) * (1.0 / GATE_NORM)
    tri = tri_ref[...]
    b = jnp.zeros_like(la)
    rest = la
    for _ in range(3):
        piece = rest.astype(BF16)
        b = b + jnp.dot(tri, piece, preferred_element_type=F32)
        rest = rest - piece.astype(F32)
    cosv = cos_ref[...]
    sinv = sin_ref[...]
    for hd, cols in enumerate(head_cols):
        q = q_ref[:, cols]
        k = k_ref[:, cols]
        bh = b[:, cols]
        b_scr[hd] = bh
        qr = (q * cosv + pltpu.roll(q, GLA_DK // 2, 1) * sinv) * GLA_DK ** -0.5
        kr = k * cosv + pltpu.roll(k, GLA_DK // 2, 1) * sinv
        qe_ref[:, cols] = (qr * jnp.exp(bh)).astype(qe_ref.dtype)
        for c in range(tg // GLA_CHUNK):
            r0 = c * GLA_CHUNK
            last = r0 if rev else r0 + GLA_CHUNK - 1
            tot = bh[last:last + 1]
            kd_ref[r0:r0 + GLA_CHUNK, cols] = (kr[r0:r0 + GLA_CHUNK]
                                               * jnp.exp(tot - bh[r0:r0 + GLA_CHUNK])).astype(kd_ref.dtype)
            gd_ref[hd, 0, c:c + 1] = jnp.exp(tot)
        qr_scr[hd] = qr
        kr_scr[hd] = kr

    n_sub = GLA_CHUNK // GLA_SUB
    colio = lax.broadcasted_iota(jnp.int32, (GLA_SUB, GLA_CHUNK), 1)
    rowio = lax.broadcasted_iota(jnp.int32, (GLA_SUB, GLA_CHUNK), 0)

    def sub_block(sb, carry):
        r0 = pl.multiple_of(sb * GLA_SUB, GLA_SUB)
        c0 = pl.multiple_of((sb // n_sub) * GLA_CHUNK, GLA_CHUNK)
        d0 = (sb % n_sub) * GLA_SUB
        ref_row = jnp.minimum(r0 + GLA_SUB, tg - 1) if rev else jnp.maximum(r0 - 1, 0)
        for hd, cols in enumerate(head_cols):
            sub_block_head(hd, cols, r0, c0, d0, ref_row)
        return carry

    def sub_block_head(hd, cols, r0, c0, d0, ref_row):
        bq = b_scr[hd, pl.ds(r0, GLA_SUB), :]
        qq = qr_scr[hd, pl.ds(r0, GLA_SUB), :]
        rb = b_scr[hd, pl.ds(ref_row, 1), :]
        bc = b_scr[hd, pl.ds(c0, GLA_CHUNK), :]
        kc = kr_scr[hd, pl.ds(c0, GLA_CHUNK), :]
        qt = qq * jnp.exp(jnp.minimum(bq - rb, 0.0))
        kt = kc * jnp.exp(jnp.minimum(rb - bc, 0.0))
        off = lax.dot_general(qt.astype(BF16), kt.astype(BF16), (((1,), (1,)), ((), ())),
                              preferred_element_type=F32)
        hs = GLA_SUB // 2
        halves = [(bq[:hs], qq[:hs], jnp.zeros((hs, GLA_CHUNK), F32)),
                  (bq[hs:], qq[hs:], jnp.zeros((hs, GLA_CHUNK), F32))]
        for s in range(GLA_SUB):
            ks = kr_scr[hd, pl.ds(r0 + s, 1), :]
            bs = b_scr[hd, pl.ds(r0 + s, 1), :]
            for hf in range(2):
                if (hf == 1 and s < hs) if rev else (hf == 0 and s >= hs):
                    continue
                bh, qh, acc = halves[hf]
                col = jnp.sum(qh * ks * jnp.exp(jnp.minimum(bh - bs, 0.0)), axis=1, keepdims=True)
                halves[hf] = (bh, qh, jnp.where(colio[:hs] == d0 + s, col, acc))
        dacc = jnp.concatenate([halves[0][2], halves[1][2]], axis=0)
        dcol = colio - d0
        in_diag = jnp.logical_and(dcol >= 0, dcol < GLA_SUB)
        if rev:
            ok_off = colio >= d0 + GLA_SUB
            ok_diag = jnp.logical_and(in_diag, dcol >= rowio)
        else:
            ok_off = colio < d0
            ok_diag = jnp.logical_and(in_diag, dcol <= rowio)
        a_blk = jnp.where(ok_off, off, jnp.where(ok_diag, dacc, 0.0))
        a_ref[hd, pl.ds(r0, GLA_SUB), :] = a_blk.astype(a_ref.dtype)

    lax.fori_loop(0, tg // GLA_SUB, sub_block, 0)


def _gla_prep(rev, qk, g, cos2, sin2, w_gate, b_gate, tri, tpb):
    n = qk.shape[0]
    tg = TOK_TILE
    nt = n // tg
    return pl.pallas_call(
        functools.partial(_gla_prep_kernel, rev),
        grid=(nt,),
        in_specs=[pl.BlockSpec((tg, GLA_QK), lambda t: (t, 0)),
                  pl.BlockSpec((tg, GLA_QK), lambda t: (t, 1)),
                  pl.BlockSpec((tg, 2 * GATE_RANK), lambda t: (t, 0)),
                  pl.BlockSpec((tg, GLA_DK), lambda t: (t % tpb, 0)),
                  pl.BlockSpec((tg, GLA_DK), lambda t: (t % tpb, 0)),
                  pl.BlockSpec((GATE_RANK, GLA_QK), lambda t: (0, 0)),
                  pl.BlockSpec((1, GLA_QK), lambda t: (0, 0)),
                  pl.BlockSpec((tg, tg), lambda t: (0, 0))],
        out_specs=[pl.BlockSpec((tg, GLA_QK), lambda t: (t, 0)),
                   pl.BlockSpec((tg, GLA_QK), lambda t: (t, 0)),
                   pl.BlockSpec((GLA_HEADS, tg, GLA_CHUNK), lambda t: (0, t, 0)),
                   pl.BlockSpec((GLA_HEADS, 1, tg // GLA_CHUNK, GLA_DK), lambda t: (0, t, 0, 0))],
        out_shape=[jax.ShapeDtypeStruct((n, GLA_QK), BF16),
                   jax.ShapeDtypeStruct((n, GLA_QK), BF16),
                   jax.ShapeDtypeStruct((GLA_HEADS, n, GLA_CHUNK), BF16),
                   jax.ShapeDtypeStruct((GLA_HEADS, nt, tg // GLA_CHUNK, GLA_DK), F32)],
        scratch_shapes=[pltpu.VMEM((GLA_HEADS, tg, GLA_DK), F32)] * 3,
        compiler_params=_cparams(("arbitrary",)),
        name="gla_prep_bwd" if rev else "gla_prep_fwd",
    )(qk, qk, g, cos2, sin2, w_gate, b_gate, tri)


def _gla_scan_kernel(rev, qe_ref, kd_ref, a_ref, gd_ref, v_ref, o_ref, st_ref):
    @pl.when(pl.program_id(1) == 0)
    def _():
        st_ref[...] = jnp.zeros_like(st_ref)

    n_chunks = qe_ref.shape[0] // GLA_CHUNK
    for hd in range(GLA_HEADS):
        st = st_ref[hd]
        kcols = slice(hd * GLA_DK, (hd + 1) * GLA_DK)
        vcols = slice(hd * GLA_DV, (hd + 1) * GLA_DV)
        for cc in range(n_chunks):
            c = n_chunks - 1 - cc if rev else cc
            rows = slice(c * GLA_CHUNK, (c + 1) * GLA_CHUNK)
            v_c = v_ref[rows, vcols]
            o = lax.dot_general(qe_ref[rows, kcols], st.astype(BF16), (((1,), (1,)), ((), ())),
                                preferred_element_type=F32)
            o = o + jnp.dot(a_ref[hd, rows, :], v_c, preferred_element_type=F32)
            o_ref[rows, vcols] = o
            upd = lax.dot_general(v_c, kd_ref[rows, kcols], (((0,), (0,)), ((), ())),
                                  preferred_element_type=F32)
            st = st * gd_ref[hd, 0, c:c + 1, :] + upd
        st_ref[hd] = st


def _gla_scan(rev, qe, kd, a, gd, v, batch, tpb):
    n = qe.shape[0]
    tg = TOK_TILE
    n_lat = tpb - 1

    def tile(bi, s):
        lat = n_lat - s if rev else s - 1
        return bi * tpb + jnp.where(s == 0, n_lat, lat)

    return pl.pallas_call(
        functools.partial(_gla_scan_kernel, rev),
        grid=(batch, tpb),
        in_specs=[pl.BlockSpec((tg, GLA_QK), lambda bi, s: (tile(bi, s), 0)),
                  pl.BlockSpec((tg, GLA_QK), lambda bi, s: (tile(bi, s), 0)),
                  pl.BlockSpec((GLA_HEADS, tg, GLA_CHUNK), lambda bi, s: (0, tile(bi, s), 0)),
                  pl.BlockSpec((GLA_HEADS, 1, tg // GLA_CHUNK, GLA_DK), lambda bi, s: (0, tile(bi, s), 0, 0)),
                  pl.BlockSpec((tg, GLA_V), lambda bi, s: (tile(bi, s), 0))],
        out_specs=pl.BlockSpec((tg, GLA_V), lambda bi, s: (tile(bi, s), 0)),
        out_shape=jax.ShapeDtypeStruct((n, GLA_V), F32),
        scratch_shapes=[pltpu.VMEM((GLA_HEADS, GLA_DV, GLA_DK), F32)],
        compiler_params=_cparams(("arbitrary", "arbitrary")),
        name="gla_scan_bwd" if rev else "gla_scan_fwd",
    )(qe, kd, a, gd, v)


def _gla_out_kernel(of_ref, ob_ref, r_ref, g_ref, o_ref):
    o = of_ref[...] + ob_ref[...]
    r = r_ref[...]
    gate = r * jax.nn.sigmoid(r)
    gn = g_ref[...]
    for hd in range(GLA_HEADS):
        cols = slice(hd * GLA_DV, (hd + 1) * GLA_DV)
        oh = o[:, cols]
        ms = jnp.mean(oh * oh, axis=-1, keepdims=True)
        o_ref[:, cols] = (oh * lax.rsqrt(ms + RMS_EPS) * gn * gate[:, cols]).astype(o_ref.dtype)


def _gla_out(o_f, o_b, r, norm_g):
    n, dv = o_f.shape
    tm = _row_tile(n)
    spec = pl.BlockSpec((tm, dv), lambda i: (i, 0))
    return pl.pallas_call(
        _gla_out_kernel,
        grid=(n // tm,),
        in_specs=[spec, spec, spec, pl.BlockSpec((1, GLA_DV), lambda i: (0, 0))],
        out_specs=spec,
        out_shape=jax.ShapeDtypeStruct((n, dv), BF16),
        compiler_params=_cparams(("arbitrary",)),
        name="gla_out",
    )(o_f, o_b, r, norm_g)


def _pack_bf16(x):
    half = x.shape[1] // 2
    lo = lax.bitcast_convert_type(x[:, :half].astype(BF16).astype(F32), jnp.uint32)
    hi = lax.bitcast_convert_type(x[:, half:].astype(BF16).astype(F32), jnp.uint32)
    return (lo >> 16) | (hi & jnp.uint32(0xFFFF0000))


def _unpack_bf16(p):
    lo = lax.bitcast_convert_type(p << 16, F32)
    hi = lax.bitcast_convert_type(p & jnp.uint32(0xFFFF0000), F32)
    return lo, hi


def _residual_ln(x, a, t, ln, alpha):
    y = alpha * x + t[0:1] * a
    mu = jnp.mean(y, axis=-1, keepdims=True)
    yc = y - mu
    var = jnp.mean(yc * yc, axis=-1, keepdims=True)
    xn = yc * lax.rsqrt(var + LN_EPS) * ln[0:1] + ln[1:2]
    return xn, xn * (1.0 + t[1:2]) + t[2:3]


def _top4_softmax(lt):
    e = lt.shape[0]
    io = lax.broadcasted_iota(jnp.int32, lt.shape, 0)
    work = lt
    idxs, vals = [], []
    for _ in range(TOP_K):
        m = jnp.max(work, axis=0, keepdims=True)
        ik = jnp.min(jnp.where(work == m, io, e), axis=0, keepdims=True)
        idxs.append(ik)
        vals.append(m)
        work = jnp.where(io == ik, -jnp.inf, work)
    ex = [jnp.exp(v - vals[0]) for v in vals]
    den = ex[0] + ex[1] + ex[2] + ex[3]
    return jnp.concatenate(idxs, axis=0), jnp.concatenate([x / den for x in ex], axis=0)


def _post_kernel(n_act, tpb, alpha, *refs):
    acts = refs[:n_act]
    ws = refs[n_act:2 * n_act]
    x_ref, tab_ref, ln_ref, rw_ref, rb_ref, xo_ref, h_ref, idx_ref, gate_ref = refs[2 * n_act:]
    tm = x_ref.shape[0]
    a = jnp.dot(acts[0][...], ws[0][...], preferred_element_type=F32)
    for k in range(1, n_act):
        a = a + jnp.dot(acts[k][...], ws[k][...], preferred_element_type=F32)
    ln = ln_ref[...]
    for s in range(tm // TOK_TILE):
        rows = slice(s * TOK_TILE, (s + 1) * TOK_TILE)
        t = tab_ref[_tab_row(pl.program_id(0) * (tm // TOK_TILE) + s, tpb)]
        xn, h = _residual_ln(x_ref[rows], a[rows], t, ln, alpha)
        xo_ref[rows] = xn
        h_ref[rows] = _pack_bf16(h)
        lt = lax.dot_general(rw_ref[...], h, (((1,), (1,)), ((), ())), precision=HIGHEST,
                             preferred_element_type=F32) + rb_ref[...]
        idx, gates = _top4_softmax(lt)
        idx_ref[:, rows] = idx
        gate_ref[:, rows] = gates


def _post(acts, ws, x, tab, ln, rw_t, rb, tpb, alpha):
    n, d = x.shape
    tm = _row_tile(n)
    e = rw_t.shape[0]
    row = lambda i: (i, 0)
    fixed = lambda i: (0, 0)
    return pl.pallas_call(
        functools.partial(_post_kernel, len(acts), tpb, alpha),
        grid=(n // tm,),
        in_specs=([pl.BlockSpec((tm, a.shape[1]), row) for a in acts]
                  + [pl.BlockSpec(w.shape, fixed) for w in ws]
                  + [pl.BlockSpec((tm, d), row),
                     pl.BlockSpec(tab.shape, lambda i: (0, 0, 0)),
                     pl.BlockSpec(ln.shape, fixed),
                     pl.BlockSpec((e, d), fixed),
                     pl.BlockSpec((e, 1), fixed)]),
        out_specs=[pl.BlockSpec((tm, d), row), pl.BlockSpec((tm, d // 2), row),
                   pl.BlockSpec((TOP_K, tm), lambda i: (0, i)), pl.BlockSpec((TOP_K, tm), lambda i: (0, i))],
        out_shape=[jax.ShapeDtypeStruct((n, d), F32), jax.ShapeDtypeStruct((n, d // 2), jnp.uint32),
                   jax.ShapeDtypeStruct((TOP_K, n), jnp.int32), jax.ShapeDtypeStruct((TOP_K, n), F32)],
        compiler_params=_cparams(("arbitrary",)),
        name="post",
    )(*acts, *ws, x, tab, ln, rw_t, rb)


def _rank_kernel(idx_ref, tri_ref, rank_ref, cnt_ref, carry_ref):
    @pl.when(pl.program_id(0) == 0)
    def _():
        carry_ref[...] = jnp.zeros_like(carry_ref)

    idx = idx_ref[...]
    e = carry_ref.shape[0]
    tr = idx.shape[1]
    io = lax.broadcasted_iota(jnp.int32, (e, tr), 0)
    chosen = jnp.zeros((e, tr), F32)
    for k in range(TOP_K):
        chosen = chosen + (idx[k:k + 1] == io).astype(F32)
    cum = jnp.dot(chosen.astype(BF16), tri_ref[...], preferred_element_type=F32)
    base = carry_ref[:, 0:1]
    excl = base + cum - chosen
    ranks = [jnp.sum(jnp.where(idx[k:k + 1] == io, excl, 0.0), axis=0, keepdims=True) for k in range(TOP_K)]
    rank_ref[...] = jnp.concatenate(ranks, axis=0).astype(jnp.int32)
    carry_ref[...] = carry_ref[...] + jnp.sum(chosen, axis=1, keepdims=True)
    cnt_ref[...] = carry_ref[...]


def _rank(idx_t, n_experts):
    n = idx_t.shape[1]
    tr = _row_tile(n)
    tri = (np.arange(tr)[:, None] <= np.arange(tr)[None, :]).astype(np.float32)
    return pl.pallas_call(
        _rank_kernel,
        grid=(n // tr,),
        in_specs=[pl.BlockSpec((TOP_K, tr), lambda i: (0, i)),
                  pl.BlockSpec((tr, tr), lambda i: (0, 0))],
        out_specs=[pl.BlockSpec((TOP_K, tr), lambda i: (0, i)),
                   pl.BlockSpec((n_experts, LANES), lambda i: (0, 0))],
        out_shape=[jax.ShapeDtypeStruct((TOP_K, n), jnp.int32),
                   jax.ShapeDtypeStruct((n_experts, LANES), F32)],
        scratch_shapes=[pltpu.VMEM((n_experts, LANES), F32)],
        compiler_params=_cparams(("arbitrary",)),
        name="rank",
    )(idx_t, jnp.asarray(tri, BF16))


def _expert_kernel(be_ref, nb_ref, x_ref, w1_ref, b1_ref, w2_ref, b2_ref, o_ref, w1b_ref, w2b_ref):
    i = pl.program_id(0)
    used = i < nb_ref[0]
    new_expert = jnp.logical_or(i == 0, be_ref[i] != be_ref[jnp.maximum(i - 1, 0)])

    @pl.when(jnp.logical_and(used, new_expert))
    def _():
        w1b_ref[...] = w1_ref[0].astype(BF16)
        w2b_ref[...] = w2_ref[0].astype(BF16)

    @pl.when(used)
    def _():
        x_lo, x_hi = _unpack_bf16(x_ref[...])
        kh = x_lo.shape[1]
        hid = (jnp.dot(x_lo.astype(BF16), w1b_ref[:kh], preferred_element_type=F32)
               + jnp.dot(x_hi.astype(BF16), w1b_ref[kh:], preferred_element_type=F32) + b1_ref[0])
        half = hid.shape[1] // 2
        glu = jnp.minimum(hid[:, :half], SWIGLU_LIMIT)
        lin = jnp.clip(hid[:, half:], -SWIGLU_LIMIT, SWIGLU_LIMIT)
        act = glu * jax.nn.sigmoid(SWIGLU_ALPHA * glu) * (lin + 1.0)
        y = jnp.dot(act.astype(BF16), w2b_ref[...], preferred_element_type=F32) + b2_ref[0]
        o_ref[...] = _pack_bf16(y)

    @pl.when(i >= nb_ref[0])
    def _():
        o_ref[...] = jnp.zeros_like(o_ref)


def _experts(block_expert, n_used, x_pad, layer, w1, b1, w2, b2):
    n_pad, dp = x_pad.shape
    depth, e, d, dh2 = w1.shape
    n_blocks = n_pad // EXPERT_BLOCK
    grid_spec = pltpu.PrefetchScalarGridSpec(
        num_scalar_prefetch=2,
        grid=(n_blocks,),
        in_specs=[pl.BlockSpec((EXPERT_BLOCK, dp), lambda i, be, nb: (i, 0)),
                  pl.BlockSpec((None, 1, d, dh2), lambda i, be, nb: (layer, be[i], 0, 0)),
                  pl.BlockSpec((None, 1, 1, dh2), lambda i, be, nb: (layer, be[i], 0, 0)),
                  pl.BlockSpec((None, 1, dh2 // 2, d), lambda i, be, nb: (layer, be[i], 0, 0)),
                  pl.BlockSpec((None, 1, 1, d), lambda i, be, nb: (layer, be[i], 0, 0))],
        out_specs=pl.BlockSpec((EXPERT_BLOCK, dp), lambda i, be, nb: (i, 0)),
        scratch_shapes=[pltpu.VMEM((d, dh2), BF16), pltpu.VMEM((dh2 // 2, d), BF16)],
    )
    return pl.pallas_call(
        _expert_kernel,
        grid_spec=grid_spec,
        out_shape=jax.ShapeDtypeStruct((n_pad, dp), jnp.uint32),
        compiler_params=pltpu.CompilerParams(dimension_semantics=("arbitrary",),
                                             vmem_limit_bytes=EXPERT_VMEM_LIMIT),
        name="experts",
    )(block_expert, n_used, x_pad, w1, b1.reshape(depth, e, 1, dh2), w2, b2.reshape(depth, e, 1, d))


def _stream_tile(i, tpb, latent_only):
    return (i // (tpb - 1)) * tpb + i % (tpb - 1) if latent_only else i


def _combine_kernel(tpb, alpha, latent_only, y_ref, gate_ref, x_ref, tab_ref, ln_ref, xo_ref, *h_ref):
    g = gate_ref[...]
    y_lo, y_hi = None, None
    for k in range(TOP_K):
        lo, hi = _unpack_bf16(y_ref[k])
        gk = g[:, k:k + 1]
        y_lo = lo * gk if y_lo is None else y_lo + lo * gk
        y_hi = hi * gk if y_hi is None else y_hi + hi * gk
    y = jnp.concatenate([y_lo, y_hi], axis=1)
    t = tab_ref[_tab_row(_stream_tile(pl.program_id(0), tpb, latent_only), tpb)]
    xn, h = _residual_ln(x_ref[...], y, t, ln_ref[...], alpha)
    xo_ref[...] = xn
    if not latent_only:
        h_ref[0][...] = h.astype(h_ref[0].dtype)


def _combine(y_g, gates, x, tab, ln, tpb, alpha, latent_only):
    n, d = x.shape
    tm = TOK_TILE
    tiles = n // tm
    steps = tiles // tpb * (tpb - 1) if latent_only else tiles
    src = lambda i: (_stream_tile(i, tpb, latent_only), 0)
    dst = lambda i: (i, 0)
    out_specs = [pl.BlockSpec((tm, d), dst)]
    out_shape = [jax.ShapeDtypeStruct((steps * tm, d), F32)]
    if not latent_only:
        out_specs.append(pl.BlockSpec((tm, d), dst))
        out_shape.append(jax.ShapeDtypeStruct((steps * tm, d), BF16))
    return pl.pallas_call(
        functools.partial(_combine_kernel, tpb, alpha, latent_only),
        grid=(steps,),
        in_specs=[pl.BlockSpec((TOP_K, tm, d // 2), lambda i: (0, _stream_tile(i, tpb, latent_only), 0)),
                  pl.BlockSpec((tm, TOP_K), src),
                  pl.BlockSpec((tm, d), src),
                  pl.BlockSpec(tab.shape, lambda i: (0, 0, 0)),
                  pl.BlockSpec(ln.shape, lambda i: (0, 0))],
        out_specs=out_specs,
        out_shape=out_shape,
        compiler_params=_cparams(("arbitrary",)),
        name="combine",
    )(y_g, gates, x, tab, ln)


def _sc_mesh():
    return plsc.VectorSubcoreMesh(core_axis_name="c", subcore_axis_name="s")


def _sc_split(rows, mesh):
    workers = mesh.num_cores * mesh.num_subcores
    per = rows // workers
    assert per * workers == rows
    chunk = SC_CHUNK if per % SC_CHUNK == 0 else 8
    assert per % chunk == 0
    return per, chunk


def _sc_scatter_rows(x, idx, n_out):
    r, c = x.shape
    mesh = _sc_mesh()
    per, chunk = _sc_split(r, mesh)

    n_chunks = per // chunk

    @functools.partial(pl.kernel, out_type=jax.ShapeDtypeStruct((n_out, c), x.dtype), mesh=mesh,
                       scratch_types=[pltpu.VMEM((chunk, c), x.dtype), pltpu.SemaphoreType.DMA] * 2
                       + [pltpu.VMEM((chunk,), jnp.int32)] * TOP_K + [pltpu.SemaphoreType.DMA])
    def scatter(x_hbm, i_hbm, o_hbm, rows_a, sem_a, rows_b, sem_b, *rest):
        idx_vs, sem_s = rest[:TOP_K], rest[TOP_K]
        base = (lax.axis_index("s") * mesh.num_cores + lax.axis_index("c")) * per
        slots = ((rows_a, sem_a), (rows_b, sem_b))

        def load(j, slot):
            rows_v, sem = slot
            off = pl.multiple_of(base + j * chunk, chunk)
            return pltpu.make_async_copy(x_hbm.at[pl.ds(off, chunk)], rows_v, sem)

        def scatter_chunk(j, slot, prefetch):
            rows_v, _ = slot
            off = pl.multiple_of(base + j * chunk, chunk)
            load(j, slot).wait()
            for k in range(TOP_K):
                pltpu.sync_copy(i_hbm.at[pl.ds(k * r + off, chunk)], idx_vs[k])
            for k in range(TOP_K):
                pltpu.async_copy(rows_v, o_hbm.at[idx_vs[k]], sem_s)
            prefetch()
            for k in range(TOP_K):
                pltpu.make_async_copy(rows_v, o_hbm.at[idx_vs[k]], sem_s).wait()

        load(0, slots[0]).start()

        @pl.loop(0, n_chunks // 2)
        def _(p):
            j = 2 * p
            scatter_chunk(j, slots[0], lambda: load(j + 1, slots[1]).start())

            def next_even():
                @pl.when(j + 2 < n_chunks)
                def _():
                    load(j + 2, slots[0]).start()

            scatter_chunk(j + 1, slots[1], next_even)

        if n_chunks % 2:
            scatter_chunk(n_chunks - 1, slots[0], lambda: None)

    return scatter(x, idx)


def _sc_gather_rows(table, idx):
    m = idx.shape[0]
    c = table.shape[1]
    mesh = _sc_mesh()
    per, chunk = _sc_split(m, mesh)

    n_chunks = per // chunk
    slot_types = [pltpu.VMEM((chunk,), jnp.int32), pltpu.VMEM((chunk, c), table.dtype), pltpu.SemaphoreType.DMA]

    @functools.partial(pl.kernel, out_type=jax.ShapeDtypeStruct((m, c), table.dtype), mesh=mesh,
                       scratch_types=slot_types * 2)
    def gather(t_hbm, i_hbm, o_hbm, idx_a, rows_a, sem_a, idx_b, rows_b, sem_b):
        base = (lax.axis_index("s") * mesh.num_cores + lax.axis_index("c")) * per
        slots = ((idx_a, rows_a, sem_a), (idx_b, rows_b, sem_b))

        def start(j, slot):
            idx_v, rows_v, sem = slot
            off = pl.multiple_of(base + j * chunk, chunk)
            pltpu.sync_copy(i_hbm.at[pl.ds(off, chunk)], idx_v)
            pltpu.async_copy(t_hbm.at[idx_v], rows_v, sem)

        def finish(j, slot):
            idx_v, rows_v, sem = slot
            off = pl.multiple_of(base + j * chunk, chunk)
            pltpu.make_async_copy(t_hbm.at[idx_v], rows_v, sem).wait()
            pltpu.sync_copy(rows_v, o_hbm.at[pl.ds(off, chunk)])

        start(0, slots[0])

        @pl.loop(0, n_chunks // 2)
        def _(p):
            j = 2 * p
            start(j + 1, slots[1])
            finish(j, slots[0])

            @pl.when(j + 2 < n_chunks)
            def _():
                start(j + 2, slots[0])

            finish(j + 1, slots[1])

        if n_chunks % 2:
            finish(n_chunks - 1, slots[0])

    return gather(table, idx)


def _moe(h, idx_t, layer, w1, b1, w2, b2):
    n, dp = h.shape
    e = w1.shape[1]
    m = n * TOP_K
    rank_t, cnt = _rank(idx_t, e)
    sizes = cnt[:, 0].astype(jnp.int32)
    padded = (sizes + EXPERT_BLOCK - 1) // EXPERT_BLOCK * EXPERT_BLOCK
    pad_ends = jnp.cumsum(padded)
    pad_starts = pad_ends - padded
    ids = jnp.arange(e, dtype=jnp.int32)[:, None, None]
    dest_t = jnp.sum(jnp.where(idx_t[None] == ids, pad_starts[:, None, None], 0), axis=0) + rank_t
    dest = dest_t.reshape(-1)
    n_blocks = (m + e * (EXPERT_BLOCK - 1)) // EXPERT_BLOCK + 1
    n_pad = n_blocks * EXPERT_BLOCK
    block_start = jnp.arange(n_blocks, dtype=jnp.int32) * EXPERT_BLOCK
    block_expert = jnp.minimum(jnp.sum((pad_ends[None, :] <= block_start[:, None]).astype(jnp.int32), axis=1), e - 1)
    n_used = (pad_ends[-1:] // EXPERT_BLOCK).astype(jnp.int32)
    x_pad = _sc_scatter_rows(h, dest, n_pad)
    y_pad = _experts(block_expert, n_used, x_pad, layer, w1, b1, w2, b2)
    return _sc_gather_rows(y_pad, dest).reshape(TOP_K, n, dp)


def _rope_tables(t_lat, n_ctx):
    t = jnp.arange(t_lat)
    row = (t // GRID_W).astype(F32)
    col = (t % GRID_W).astype(F32)
    nf = GLA_DK // 4
    freqs = ROPE_BASE ** (-jnp.arange(nf, dtype=F32) / nf)
    ang = jnp.concatenate([row[:, None] * freqs, col[:, None] * freqs], axis=-1)
    cos, sin = jnp.cos(ang), jnp.sin(ang)
    cos2 = jnp.concatenate([cos, cos], axis=-1)
    sin2 = jnp.concatenate([-sin, sin], axis=-1)
    return (jnp.concatenate([cos2, jnp.ones((n_ctx, GLA_DK), F32)], axis=0),
            jnp.concatenate([sin2, jnp.zeros((n_ctx, GLA_DK), F32)], axis=0))


def _chunk_tri(tg, rev):
    t = np.arange(tg)
    same = (t[:, None] // GLA_CHUNK) == (t[None, :] // GLA_CHUNK)
    side = (t[None, :] >= t[:, None]) if rev else (t[None, :] <= t[:, None])
    return jnp.asarray((same & side).astype(np.float32), BF16)


def _table(mods, rows, batch):
    lat = jnp.stack([mods[:batch, r] for r in rows], axis=1)
    ctx = jnp.broadcast_to(jnp.stack([mods[batch, r] for r in rows], axis=0)[None], lat.shape)
    tab = jnp.stack([lat, ctx], axis=1).reshape(2 * batch, len(rows), -1)
    return jnp.pad(tab, ((0, 0), (0, 8 - len(rows)), (0, 0)))


@jax.jit
def _forward(x, c, ctx, c_ctx, ada_w, ada_b, ln_g, ln_b, ab_w_in, ab_pool_w, ab_pool_scale, ab_rpb,
             ab_w_out, gla_w_in, gla_w_gate, gla_b_gate, gla_norm_g, gla_w_out, router_w, router_b,
             exp_w1, exp_b1, exp_w2, exp_b2):
    batch, t_lat, d = x.shape
    n_ctx = ctx.shape[1]
    depth = ada_w.shape[0]
    assert d == D_MODEL and n_ctx == TOK_TILE and t_lat % TOK_TILE == 0
    rows = t_lat // GRID_W
    assert rows % NA_QROWS == 0 and rows >= NA_KROWS + NA_QROWS
    n_lat = t_lat // TOK_TILE
    tpb = n_lat + 1
    l = t_lat + n_ctx
    n = batch * l
    alpha = (2.0 * depth) ** 0.25

    cc = jnp.concatenate([c, c_ctx[None], jnp.zeros((16 - batch - 1, d), F32)], axis=0)
    mods = _mods(cc, ada_w, ada_b).reshape(depth, 16, N_MOD, d)

    z = jnp.concatenate([x, ctx], axis=1).reshape(n, d)
    h = _modulate(z, _table(mods[0], (1, 0), batch), tpb)
    cos2, sin2 = _rope_tables(t_lat, n_ctx)

    for i in range(depth):
        j = i // 2
        last = i == depth - 1
        tab1 = _table(mods[i], (2, 4, 3), batch)
        ln1 = jnp.stack([ln_g[i, 0], ln_b[i, 0]])
        ln2 = jnp.stack([ln_g[i, 1], ln_b[i, 1]])
        rw_t = router_w[i].T
        rb = router_b[i][:, None]
        if i % 2 == 0:
            u, qkv = _proj(h, ab_w_in[j].astype(BF16), [(0, POOL_WIDTH), (POOL_WIDTH, POOL_WIDTH + 3 * NA_WIDTH)],
                           [F32, BF16])
            w_blk = jax.scipy.linalg.block_diag(*[ab_pool_w[j, g] for g in range(len(POOL_WINDOWS))])
            pooled = _pool(u.reshape(batch, l, POOL_WIDTH), w_blk.astype(BF16), ab_pool_scale[j][None, :],
                           n_lat, t_lat, n_ctx)
            bias = _na_bias_tables(ab_rpb[j], rows)
            attn = _na(qkv.reshape(batch, l, 3 * NA_WIDTH), bias, n_lat)
            w_out = ab_w_out[j].astype(BF16)
            acts = [pooled.reshape(n, POOL_WIDTH), attn.reshape(n, NA_WIDTH)]
            ws = [w_out[:POOL_WIDTH], w_out[POOL_WIDTH:]]
        else:
            qk, v, r, g = _proj(h, gla_w_in[j].astype(BF16),
                                [(0, 2 * GLA_QK), (2 * GLA_QK, 2 * GLA_QK + GLA_V),
                                 (2 * GLA_QK + GLA_V, 2 * GLA_QK + 2 * GLA_V),
                                 (2 * GLA_QK + 2 * GLA_V, 2 * GLA_QK + 2 * GLA_V + 2 * GATE_RANK)],
                                [F32, BF16, F32, F32])
            o_dirs = []
            for rev in (False, True):
                dr = int(rev)
                qe, kd, a, gd = _gla_prep(rev, qk, g, cos2, sin2, gla_w_gate[j, dr], gla_b_gate[j, dr][None, :],
                                          _chunk_tri(TOK_TILE, rev), tpb)
                o_dirs.append(_gla_scan(rev, qe, kd, a, gd, v, batch, tpb))
            acts = [_gla_out(o_dirs[0], o_dirs[1], r, gla_norm_g[j][None, :])]
            ws = [gla_w_out[j].astype(BF16)]
        z, h, idx_t, gates_t = _post(acts, ws, z, tab1, ln1, rw_t, rb, tpb, alpha)
        y_g = _moe(h, idx_t, i, exp_w1, exp_b1, exp_w2, exp_b2)
        nxt = mods[i + 1] if not last else mods[i]
        tab2 = _table(jnp.concatenate([mods[i][:, 5:6], nxt[:, 1:2], nxt[:, 0:1]], axis=1), (0, 1, 2), batch)
        if last:
            (out,) = _combine(y_g, gates_t.T, z, tab2, ln2, tpb, alpha, True)
            return out.reshape(batch, t_lat, d)
        z, h = _combine(y_g, gates_t.T, z, tab2, ln2, tpb, alpha, False)


def kernel(x, c, ctx, c_ctx, ada_w, ada_b, ln_g, ln_b, ab_w_in, ab_pool_w, ab_pool_scale, ab_rpb, ab_w_out,
           gla_w_in, gla_w_gate, gla_b_gate, gla_norm_g, gla_w_out, router_w, router_b, exp_w1, exp_b1, exp_w2,
           exp_b2):
    return _forward(x, c, ctx, c_ctx, ada_w, ada_b, ln_g, ln_b, ab_w_in, ab_pool_w, ab_pool_scale, ab_rpb,
                    ab_w_out, gla_w_in, gla_w_gate, gla_b_gate, gla_norm_g, gla_w_out, router_w, router_b,
                    exp_w1, exp_b1, exp_w2, exp_b2)
```

```python
import functools
import math

import numpy as np
import jax
import jax.numpy as jnp
from jax import lax
from jax.experimental import pallas as pl
from jax.experimental.pallas import tpu as pltpu
from jax.experimental.pallas import tpu_sc as plsc

F32 = jnp.float32
BF16 = jnp.bfloat16
HIGHEST = lax.Precision.HIGHEST

D_MODEL = 1024
GRID_W = 64
N_MOD = 6
POOL_WINDOWS = (2, 4, 8, 16)
POOL_WIDTH = D_MODEL // 4
POOL_GROUP_DIM = POOL_WIDTH // len(POOL_WINDOWS)
POOL_HALO = max(POOL_WINDOWS) // 2
NA_HEAD_DIM = 64
NA_HEADS = (D_MODEL - POOL_WIDTH) // NA_HEAD_DIM
NA_WIDTH = NA_HEADS * NA_HEAD_DIM
WIN_H = 8
WIN_W = 16
GLA_HEADS = 4
GLA_DK = D_MODEL // 2 // GLA_HEADS
GLA_DV = D_MODEL // GLA_HEADS
GATE_RANK = 16
GATE_NORM = 16.0
GLA_CHUNK = 64
GLA_STRIP = 8
GLA_QK = GLA_HEADS * GLA_DK
GLA_V = GLA_HEADS * GLA_DV
ROPE_BASE = 10000.0
TOP_K = 4
SWIGLU_LIMIT = 7.0
SWIGLU_ALPHA = 1.702
LN_EPS = 1e-5
RMS_EPS = 1e-6
NEG_INF = -1e30

LANES = 128
TOK_TILE = 256
NA_QROWS = 4
NA_KROWS = 12
VMEM_LIMIT = 48 * 1024 * 1024
EXPERT_VMEM_LIMIT = 56 * 1024 * 1024
EXPERT_BLOCK = 512
SC_CHUNK = 64


def _cparams(sem):
    return pltpu.CompilerParams(dimension_semantics=sem, vmem_limit_bytes=VMEM_LIMIT)


def _dot_bf16x3(a, b, dims):
    a_hi = a.astype(BF16)
    a_lo = (a - a_hi.astype(F32)).astype(BF16)
    b_hi = b.astype(BF16)
    b_lo = (b - b_hi.astype(F32)).astype(BF16)
    dg = functools.partial(lax.dot_general, dimension_numbers=(dims, ((), ())), preferred_element_type=F32)
    return dg(a_hi, b_hi) + dg(a_hi, b_lo) + dg(a_lo, b_hi)


def _row_tile(n):
    return 2 * TOK_TILE if n % (2 * TOK_TILE) == 0 else TOK_TILE


def _mods_kernel(c_ref, w_ref, b_ref, o_ref):
    cv = c_ref[...]
    sc = cv * jax.nn.sigmoid(cv)
    o_ref[0] = jnp.dot(sc, w_ref[0], precision=HIGHEST, preferred_element_type=F32) + b_ref[0]


def _mods(cc, ada_w, ada_b):
    depth, d, n = ada_w.shape
    r = cc.shape[0]
    tn = n // 4
    return pl.pallas_call(
        _mods_kernel,
        grid=(depth, n // tn),
        in_specs=[pl.BlockSpec((r, d), lambda i, j: (0, 0)),
                  pl.BlockSpec((1, d, tn), lambda i, j: (i, 0, j)),
                  pl.BlockSpec((1, 1, tn), lambda i, j: (i, 0, j))],
        out_specs=pl.BlockSpec((1, r, tn), lambda i, j: (i, 0, j)),
        out_shape=jax.ShapeDtypeStruct((depth, r, n), F32),
        compiler_params=_cparams(("arbitrary", "arbitrary")),
        name="mods",
    )(cc, ada_w, ada_b.reshape(depth, 1, n))


def _tab_row(g, tpb):
    return (g // tpb) * 2 + (g % tpb == tpb - 1).astype(jnp.int32)


def _modulate_kernel(tpb, x_ref, ctx_ref, tab_ref, z_ref, h_ref):
    i = pl.program_id(0)
    is_ctx = i % tpb == tpb - 1
    z = jnp.where(is_ctx, ctx_ref[...], x_ref[...])
    t = tab_ref[_tab_row(i, tpb)]
    z_ref[...] = z
    h_ref[...] = (z * (1.0 + t[0:1]) + t[1:2]).astype(h_ref.dtype)


def _modulate(x, ctx, tab, tpb):
    d = x.shape[1]
    n_lat = tpb - 1
    n = x.shape[0] + ctx.shape[0]
    row = lambda i: (i, 0)
    return pl.pallas_call(
        functools.partial(_modulate_kernel, tpb),
        grid=(n // TOK_TILE,),
        in_specs=[pl.BlockSpec((TOK_TILE, d), lambda i: ((i // tpb) * n_lat + jnp.minimum(i % tpb, n_lat - 1), 0)),
                  pl.BlockSpec((TOK_TILE, d), lambda i: (i // tpb, 0)),
                  pl.BlockSpec(tab.shape, lambda i: (0, 0, 0))],
        out_specs=[pl.BlockSpec((TOK_TILE, d), row), pl.BlockSpec((TOK_TILE, d), row)],
        out_shape=[jax.ShapeDtypeStruct((n, d), F32), jax.ShapeDtypeStruct((n, d), BF16)],
        compiler_params=_cparams(("arbitrary",)),
        name="modulate",
    )(x, ctx, tab)


def _proj_kernel(splits, x_ref, w_ref, *out_refs):
    x = x_ref[...]
    for (a, b), o_ref in zip(splits, out_refs):
        o_ref[...] = jnp.dot(x, w_ref[:, a:b], preferred_element_type=F32).astype(o_ref.dtype)


def _proj(h, w, splits, dtypes):
    n, k = h.shape
    m = w.shape[1]
    tm = _row_tile(n)
    return pl.pallas_call(
        functools.partial(_proj_kernel, tuple(splits)),
        grid=(n // tm,),
        in_specs=[pl.BlockSpec((tm, k), lambda i: (i, 0)),
                  pl.BlockSpec((k, m), lambda i: (0, 0))],
        out_specs=[pl.BlockSpec((tm, b - a), lambda i: (i, 0)) for a, b in splits],
        out_shape=[jax.ShapeDtypeStruct((n, b - a), dt) for (a, b), dt in zip(splits, dtypes)],
        compiler_params=_cparams(("arbitrary",)),
        name="proj",
    )(h, w)


def _pool_kernel(n_lat, t_lat, t_ctx, prev_ref, cur_ref, next_ref, w_ref, scale_ref, o_ref, halo_ref):
    j = pl.program_id(1)
    is_ctx = j == n_lat
    has_prev = jnp.logical_and(j > 0, jnp.logical_not(is_ctx))
    has_next = j < n_lat - 1
    cur = cur_ref[0]
    hl = POOL_HALO
    halo_ref[0:hl] = jnp.where(has_prev, prev_ref[0, TOK_TILE - hl:TOK_TILE], 0.0)
    halo_ref[hl:hl + TOK_TILE] = cur
    halo_ref[hl + TOK_TILE:2 * hl + TOK_TILE] = jnp.where(has_next, next_ref[0, 0:hl], 0.0)

    shape = cur.shape
    lane = lax.broadcasted_iota(jnp.int32, shape, 1)
    group = lane // POOL_GROUP_DIM
    half = jnp.ones(shape, jnp.int32)
    for gi, wdw in enumerate(POOL_WINDOWS):
        half = jnp.where(group == gi, wdw // 2, half)
    acc = jnp.zeros(shape, F32)
    for off in range(-hl, hl):
        v = halo_ref[hl + off:hl + off + TOK_TILE]
        inside = (half >= -off) if off < 0 else (half > off)
        acc = acc + jnp.where(inside, v, 0.0)
    pos0 = jnp.where(is_ctx, 0, j * TOK_TILE)
    seq = jnp.where(is_ctx, t_ctx, t_lat)
    t = pos0 + lax.broadcasted_iota(jnp.int32, shape, 0)
    cnt = jnp.minimum(t + half, seq) - jnp.maximum(t - half, 0)
    pooled = acc / cnt.astype(F32) - cur
    y = jnp.dot(pooled.astype(BF16), w_ref[...], preferred_element_type=F32) * scale_ref[...]
    o_ref[0] = y.astype(o_ref.dtype)


def _pool(u, w_blk, scale, n_lat, t_lat, t_ctx):
    b, l, pw = u.shape
    tpb = l // TOK_TILE
    blk = (1, TOK_TILE, pw)
    return pl.pallas_call(
        functools.partial(_pool_kernel, n_lat, t_lat, t_ctx),
        grid=(b, tpb),
        in_specs=[pl.BlockSpec(blk, lambda bi, j: (bi, jnp.maximum(j - 1, 0), 0)),
                  pl.BlockSpec(blk, lambda bi, j: (bi, j, 0)),
                  pl.BlockSpec(blk, lambda bi, j: (bi, jnp.minimum(j + 1, tpb - 1), 0)),
                  pl.BlockSpec((pw, pw), lambda bi, j: (0, 0)),
                  pl.BlockSpec((1, pw), lambda bi, j: (0, 0))],
        out_specs=pl.BlockSpec(blk, lambda bi, j: (bi, j, 0)),
        out_shape=jax.ShapeDtypeStruct((b, l, pw), BF16),
        scratch_shapes=[pltpu.VMEM((TOK_TILE + 2 * POOL_HALO, pw), F32)],
        compiler_params=_cparams(("arbitrary", "arbitrary")),
        name="pool",
    )(u, u, u, w_blk, scale)


def _na_bias_tables(rpb, rows):
    n_i = rows // NA_QROWS
    heads = rpb.shape[0]
    a = np.arange(NA_QROWS)
    kr = np.arange(NA_KROWS)
    cq = np.arange(GRID_W)
    ws = np.clip(cq - WIN_W // 2, 0, GRID_W - WIN_W)
    ok_col = (cq[None, :] >= ws[:, None]) & (cq[None, :] < ws[:, None] + WIN_W)
    dcol = np.clip(cq[None, :] - cq[:, None] + WIN_W - 1, 0, 2 * WIN_W - 2)
    oh_col = (dcol[..., None] == np.arange(2 * WIN_W - 1)).astype(np.float32)
    tabs = []
    for i in (0, 1, n_i - 1):
        start = int(np.clip(NA_QROWS * i - WIN_H // 2, 0, rows - NA_KROWS))
        r = NA_QROWS * i + a
        krow = start + kr
        rs = np.clip(r - WIN_H // 2, 0, rows - WIN_H)
        ok_row = (krow[None, :] >= rs[:, None]) & (krow[None, :] < rs[:, None] + WIN_H)
        drow = np.clip(krow[None, :] - r[:, None] + WIN_H - 1, 0, 2 * WIN_H - 2)
        oh_row = (drow[..., None] == np.arange(2 * WIN_H - 1)).astype(np.float32)
        by_row = jnp.einsum('hrc,akr->hakc', rpb, oh_row, precision=HIGHEST)
        bias = jnp.einsum('hakc,qjc->haqkj', by_row, oh_col, precision=HIGHEST)
        ok = ok_row[:, None, :, None] & ok_col[None, :, None, :]
        tabs.append(jnp.where(ok[None], bias, NEG_INF).reshape(heads, TOK_TILE, NA_KROWS * GRID_W))
    tabs.append(jnp.full_like(tabs[0], NEG_INF))
    return jnp.stack(tabs).astype(F32)


def _na_kernel(q_ref, k0_ref, k1_ref, k2_ref, kc_ref, v0_ref, v1_ref, v2_ref, vc_ref, bias_ref, o_ref):
    q = q_ref[0]
    k_refs = (k0_ref, k1_ref, k2_ref, kc_ref)
    v_refs = (v0_ref, v1_ref, v2_ref, vc_ref)
    n_band = len(k_refs) - 1
    lane = lax.broadcasted_iota(jnp.int32, q.shape, 1)
    first = lane < NA_HEAD_DIM
    outs = []
    for hh in range(2):
        qm = jnp.where(first if hh == 0 else jnp.logical_not(first), q, jnp.zeros_like(q)) * NA_HEAD_DIM ** -0.5
        scores = []
        for j, k_ref in enumerate(k_refs):
            s = lax.dot_general(qm, k_ref[0], (((1,), (1,)), ((), ())), preferred_element_type=F32)
            if j < n_band:
                s = s + bias_ref[0, hh, :, j * TOK_TILE:(j + 1) * TOK_TILE]
            scores.append(s)
        m = functools.reduce(jnp.maximum, [jnp.max(s, axis=-1, keepdims=True) for s in scores])
        l = 0.0
        o = 0.0
        for s, v_ref in zip(scores, v_refs):
            p = jnp.exp(s - m)
            l = l + jnp.sum(p, axis=-1, keepdims=True)
            o = o + jnp.dot(p.astype(BF16), v_ref[0], preferred_element_type=F32)
        outs.append(o / l)
    o_ref[0] = jnp.where(first, outs[0], outs[1]).astype(o_ref.dtype)


def _na(qkv, bias, n_lat):
    b, l, _ = qkv.shape
    tpb = l // TOK_TILE
    n_pairs = NA_WIDTH // LANES
    blk = (1, TOK_TILE, LANES)

    def kstart(i):
        return jnp.clip(i - 1, 0, n_lat - NA_KROWS // NA_QROWS)

    def btype(i):
        return jnp.where(i == 0, 0, jnp.where(i == n_lat - 1, 2, jnp.where(i == n_lat, 3, 1)))

    def kv_spec(col0, j):
        return pl.BlockSpec(blk, lambda hp, i, bi: (bi, kstart(i) + j, col0 + hp))

    def ctx_spec(col0):
        return pl.BlockSpec(blk, lambda hp, i, bi: (bi, n_lat, col0 + hp))

    nk = bias.shape[-1]
    return pl.pallas_call(
        _na_kernel,
        grid=(n_pairs, tpb, b),
        in_specs=[pl.BlockSpec(blk, lambda hp, i, bi: (bi, i, hp)),
                  kv_spec(n_pairs, 0), kv_spec(n_pairs, 1), kv_spec(n_pairs, 2), ctx_spec(n_pairs),
                  kv_spec(2 * n_pairs, 0), kv_spec(2 * n_pairs, 1), kv_spec(2 * n_pairs, 2), ctx_spec(2 * n_pairs),
                  pl.BlockSpec((1, 2, TOK_TILE, nk), lambda hp, i, bi: (btype(i), hp, 0, 0))],
        out_specs=pl.BlockSpec(blk, lambda hp, i, bi: (bi, i, hp)),
        out_shape=jax.ShapeDtypeStruct((b, l, NA_WIDTH), BF16),
        compiler_params=_cparams(("arbitrary", "arbitrary", "arbitrary")),
        name="na",
    )(qkv, qkv, qkv, qkv, qkv, qkv, qkv, qkv, qkv, bias)


def _log_sigmoid(z):
    return jnp.minimum(z, 0.0) - jnp.log(1.0 + jnp.exp(-jnp.abs(z)))


def _gla_prep_kernel(rev, q_ref, k_ref, g_ref, cos_ref, sin_ref, wg_ref, bg_ref, tri_ref,
                     qe_ref, kd_ref, a_ref, gd_ref, b_scr, qr_scr, kr_scr):
    tg = q_ref.shape[0]
    head_cols = [slice(hd * GLA_DK, (hd + 1) * GLA_DK) for hd in range(GLA_HEADS)]
    gcol = GATE_RANK if rev else 0
    gg = g_ref[:, gcol:gcol + GATE_RANK]
    z = _dot_bf16x3(gg, wg_ref[...], ((1,), (0,))) + bg_ref[...]
    la = _log_sigmoid(z) * (1.0 / GATE_NORM)
    tri = tri_ref[...]
    b = jnp.zeros_like(la)
    rest = la
    for _ in range(3):
        piece = rest.astype(BF16)
        b = b + jnp.dot(tri, piece, preferred_element_type=F32)
        rest = rest - piece.astype(F32)
    cosv = cos_ref[...]
    sinv = sin_ref[...]
    for hd, cols in enumerate(head_cols):
        q = q_ref[:, cols]
        k = k_ref[:, cols]
        bh = b[:, cols]
        b_scr[hd] = bh
        qr = (q * cosv + pltpu.roll(q, GLA_DK // 2, 1) * sinv) * GLA_DK ** -0.5
        kr = k * cosv + pltpu.roll(k, GLA_DK // 2, 1) * sinv
        qe_ref[:, cols] = (qr * jnp.exp(bh)).astype(qe_ref.dtype)
        for c in range(tg // GLA_CHUNK):
            r0 = c * GLA_CHUNK
            last = r0 if rev else r0 + GLA_CHUNK - 1
            tot = bh[last:last + 1]
            kd_ref[r0:r0 + GLA_CHUNK, cols] = (kr[r0:r0 + GLA_CHUNK]
                                               * jnp.exp(tot - bh[r0:r0 + GLA_CHUNK])).astype(kd_ref.dtype)
            gd_ref[hd, 0, c:c + 1] = jnp.exp(tot)
        qr_scr[hd] = qr
        kr_scr[hd] = kr

    n_strip = GLA_CHUNK // GLA_STRIP
    colio = lax.broadcasted_iota(jnp.int32, (GLA_STRIP, GLA_CHUNK), 1)
    rowio = lax.broadcasted_iota(jnp.int32, (GLA_STRIP, GLA_CHUNK), 0)

    def strip_scores(hd, c0, bch, qch, kch, u):
        lo, hi = u * GLA_STRIP, (u + 1) * GLA_STRIP
        bu, qu = bch[lo:hi], qch[lo:hi]
        krows, ref = (slice(hi, GLA_CHUNK), hi) if rev else (slice(0, lo), lo - 1)
        if krows.stop > krows.start:
            rb = b_scr[hd, pl.ds(c0 + ref, 1), :]
            qt = qu * jnp.exp(jnp.minimum(bu - rb, 0.0))
            kt = kch[krows] * jnp.exp(jnp.minimum(rb - bch[krows], 0.0))
            pad = jnp.zeros((GLA_CHUNK - kt.shape[0], GLA_DK), F32)
            kt = jnp.concatenate([pad, kt] if rev else [kt, pad], axis=0)
            acc = lax.dot_general(qt.astype(BF16), kt.astype(BF16), (((1,), (1,)), ((), ())),
                                  preferred_element_type=F32)
        else:
            acc = jnp.zeros((GLA_STRIP, GLA_CHUNK), F32)
        for s in range(GLA_STRIP):
            ks = kr_scr[hd, pl.ds(c0 + lo + s, 1), :]
            bs = b_scr[hd, pl.ds(c0 + lo + s, 1), :]
            col = jnp.sum(qu * ks * jnp.exp(jnp.minimum(bu - bs, 0.0)), axis=1, keepdims=True)
            causal = (rowio <= s) if rev else (rowio >= s)
            acc = jnp.where(jnp.logical_and(colio == lo + s, causal), col, acc)
        return acc

    def chunk_scores(c, carry):
        c0 = pl.multiple_of(c * GLA_CHUNK, GLA_CHUNK)
        for hd in range(GLA_HEADS):
            bch = b_scr[hd, pl.ds(c0, GLA_CHUNK), :]
            qch = qr_scr[hd, pl.ds(c0, GLA_CHUNK), :]
            kch = kr_scr[hd, pl.ds(c0, GLA_CHUNK), :]
            for u in range(0, n_strip, 2):
                pair = jnp.concatenate([strip_scores(hd, c0, bch, qch, kch, u),
                                        strip_scores(hd, c0, bch, qch, kch, u + 1)], axis=0)
                a_ref[hd, pl.ds(c0 + u * GLA_STRIP, 2 * GLA_STRIP), :] = pair.astype(a_ref.dtype)
        return carry

    lax.fori_loop(0, tg // GLA_CHUNK, chunk_scores, 0)


def _gla_prep(rev, qk, g, cos2, sin2, w_gate, b_gate, tri, tpb):
    n = qk.shape[0]
    tg = TOK_TILE
    nt = n // tg
    return pl.pallas_call(
        functools.partial(_gla_prep_kernel, rev),
        grid=(nt,),
        in_specs=[pl.BlockSpec((tg, GLA_QK), lambda t: (t, 0)),
                  pl.BlockSpec((tg, GLA_QK), lambda t: (t, 1)),
                  pl.BlockSpec((tg, 2 * GATE_RANK), lambda t: (t, 0)),
                  pl.BlockSpec((tg, GLA_DK), lambda t: (t % tpb, 0)),
                  pl.BlockSpec((tg, GLA_DK), lambda t: (t % tpb, 0)),
                  pl.BlockSpec((GATE_RANK, GLA_QK), lambda t: (0, 0)),
                  pl.BlockSpec((1, GLA_QK), lambda t: (0, 0)),
                  pl.BlockSpec((tg, tg), lambda t: (0, 0))],
        out_specs=[pl.BlockSpec((tg, GLA_QK), lambda t: (t, 0)),
                   pl.BlockSpec((tg, GLA_QK), lambda t: (t, 0)),
                   pl.BlockSpec((GLA_HEADS, tg, GLA_CHUNK), lambda t: (0, t, 0)),
                   pl.BlockSpec((GLA_HEADS, 1, tg // GLA_CHUNK, GLA_DK), lambda t: (0, t, 0, 0))],
        out_shape=[jax.ShapeDtypeStruct((n, GLA_QK), BF16),
                   jax.ShapeDtypeStruct((n, GLA_QK), BF16),
                   jax.ShapeDtypeStruct((GLA_HEADS, n, GLA_CHUNK), BF16),
                   jax.ShapeDtypeStruct((GLA_HEADS, nt, tg // GLA_CHUNK, GLA_DK), F32)],
        scratch_shapes=[pltpu.VMEM((GLA_HEADS, tg, GLA_DK), F32)] * 3,
        compiler_params=_cparams(("arbitrary",)),
        name="gla_prep_bwd" if rev else "gla_prep_fwd",
    )(qk, qk, g, cos2, sin2, w_gate, b_gate, tri)


def _gla_scan_kernel(*refs):
    dirs = (refs[0:5], refs[5:10])
    o_refs = refs[10:12]
    st_ref = refs[12]

    @pl.when(pl.program_id(1) == 0)
    def _():
        st_ref[...] = jnp.zeros_like(st_ref)

    n_chunks = refs[0].shape[0] // GLA_CHUNK
    for hd in range(GLA_HEADS):
        kcols = slice(hd * GLA_DK, (hd + 1) * GLA_DK)
        vcols = slice(hd * GLA_DV, (hd + 1) * GLA_DV)
        states = [st_ref[0, hd], st_ref[1, hd]]
        for cc in range(n_chunks):
            for rev in (0, 1):
                qe_ref, kd_ref, a_ref, gd_ref, v_ref = dirs[rev]
                c = n_chunks - 1 - cc if rev else cc
                rows = slice(c * GLA_CHUNK, (c + 1) * GLA_CHUNK)
                st = states[rev]
                v_c = v_ref[rows, vcols]
                o = lax.dot_general(qe_ref[rows, kcols], st.astype(BF16), (((1,), (1,)), ((), ())),
                                    preferred_element_type=F32)
                o = o + jnp.dot(a_ref[hd, rows, :], v_c, preferred_element_type=F32)
                o_refs[rev][rows, vcols] = o
                upd = lax.dot_general(v_c, kd_ref[rows, kcols], (((0,), (0,)), ((), ())),
                                      preferred_element_type=F32)
                states[rev] = st * gd_ref[hd, 0, c:c + 1, :] + upd
        st_ref[0, hd] = states[0]
        st_ref[1, hd] = states[1]


def _gla_scan(prep_f, prep_b, v, batch, tpb):
    n = v.shape[0]
    tg = TOK_TILE
    n_lat = tpb - 1

    def specs(rev):
        def tile(bi, s):
            lat = n_lat - s if rev else s - 1
            return bi * tpb + jnp.where(s == 0, n_lat, lat)

        ins = [pl.BlockSpec((tg, GLA_QK), lambda bi, s: (tile(bi, s), 0)),
               pl.BlockSpec((tg, GLA_QK), lambda bi, s: (tile(bi, s), 0)),
               pl.BlockSpec((GLA_HEADS, tg, GLA_CHUNK), lambda bi, s: (0, tile(bi, s), 0)),
               pl.BlockSpec((GLA_HEADS, 1, tg // GLA_CHUNK, GLA_DK), lambda bi, s: (0, tile(bi, s), 0, 0)),
               pl.BlockSpec((tg, GLA_V), lambda bi, s: (tile(bi, s), 0))]
        return ins, pl.BlockSpec((tg, GLA_V), lambda bi, s: (tile(bi, s), 0))

    in_f, out_f = specs(False)
    in_b, out_b = specs(True)
    return pl.pallas_call(
        _gla_scan_kernel,
        grid=(batch, tpb),
        in_specs=in_f + in_b,
        out_specs=[out_f, out_b],
        out_shape=[jax.ShapeDtypeStruct((n, GLA_V), F32)] * 2,
        scratch_shapes=[pltpu.VMEM((2, GLA_HEADS, GLA_DV, GLA_DK), F32)],
        compiler_params=_cparams(("arbitrary", "arbitrary")),
        name="gla_scan",
    )(*prep_f, v, *prep_b, v)


def _gla_out_kernel(of_ref, ob_ref, r_ref, g_ref, o_ref):
    o = of_ref[...] + ob_ref[...]
    r = r_ref[...]
    gate = r * jax.nn.sigmoid(r)
    gn = g_ref[...]
    for hd in range(GLA_HEADS):
        cols = slice(hd * GLA_DV, (hd + 1) * GLA_DV)
        oh = o[:, cols]
        ms = jnp.mean(oh * oh, axis=-1, keepdims=True)
        o_ref[:, cols] = (oh * lax.rsqrt(ms + RMS_EPS) * gn * gate[:, cols]).astype(o_ref.dtype)


def _gla_out(o_f, o_b, r, norm_g):
    n, dv = o_f.shape
    tm = _row_tile(n)
    spec = pl.BlockSpec((tm, dv), lambda i: (i, 0))
    return pl.pallas_call(
        _gla_out_kernel,
        grid=(n // tm,),
        in_specs=[spec, spec, spec, pl.BlockSpec((1, GLA_DV), lambda i: (0, 0))],
        out_specs=spec,
        out_shape=jax.ShapeDtypeStruct((n, dv), BF16),
        compiler_params=_cparams(("arbitrary",)),
        name="gla_out",
    )(o_f, o_b, r, norm_g)


def _pack_bf16(x):
    half = x.shape[1] // 2
    lo = lax.bitcast_convert_type(x[:, :half].astype(BF16).astype(F32), jnp.uint32)
    hi = lax.bitcast_convert_type(x[:, half:].astype(BF16).astype(F32), jnp.uint32)
    return (lo >> 16) | (hi & jnp.uint32(0xFFFF0000))


def _unpack_bf16(p):
    lo = lax.bitcast_convert_type(p << 16, F32)
    hi = lax.bitcast_convert_type(p & jnp.uint32(0xFFFF0000), F32)
    return lo, hi


def _residual_ln(x, a, t, ln, alpha):
    y = alpha * x + t[0:1] * a
    mu = jnp.mean(y, axis=-1, keepdims=True)
    yc = y - mu
    var = jnp.mean(yc * yc, axis=-1, keepdims=True)
    xn = yc * lax.rsqrt(var + LN_EPS) * ln[0:1] + ln[1:2]
    return xn, xn * (1.0 + t[1:2]) + t[2:3]


def _top4_softmax(lt):
    e = lt.shape[0]
    io = lax.broadcasted_iota(jnp.int32, lt.shape, 0)
    work = lt
    idxs, vals = [], []
    for _ in range(TOP_K):
        m = jnp.max(work, axis=0, keepdims=True)
        ik = jnp.min(jnp.where(work == m, io, e), axis=0, keepdims=True)
        idxs.append(ik)
        vals.append(m)
        work = jnp.where(io == ik, -jnp.inf, work)
    ex = [jnp.exp(v - vals[0]) for v in vals]
    den = ex[0] + ex[1] + ex[2] + ex[3]
    return jnp.concatenate(idxs, axis=0), jnp.concatenate([x / den for x in ex], axis=0)


def _post_kernel(n_act, tpb, alpha, *refs):
    acts = refs[:n_act]
    ws = refs[n_act:2 * n_act]
    x_ref, tab_ref, ln_ref, rw_ref, rb_ref, xo_ref, h_ref, idx_ref, gate_ref = refs[2 * n_act:]
    tm = x_ref.shape[0]
    a = jnp.dot(acts[0][...], ws[0][...], preferred_element_type=F32)
    for k in range(1, n_act):
        a = a + jnp.dot(acts[k][...], ws[k][...], preferred_element_type=F32)
    ln = ln_ref[...]
    for s in range(tm // TOK_TILE):
        rows = slice(s * TOK_TILE, (s + 1) * TOK_TILE)
        t = tab_ref[_tab_row(pl.program_id(0) * (tm // TOK_TILE) + s, tpb)]
        xn, h = _residual_ln(x_ref[rows], a[rows], t, ln, alpha)
        xo_ref[rows] = xn
        h_ref[rows] = _pack_bf16(h)
        lt = _dot_bf16x3(rw_ref[...], h, ((1,), (1,))) + rb_ref[...]
        idx, gates = _top4_softmax(lt)
        idx_ref[:, rows] = idx
        gate_ref[:, rows] = gates


def _post(acts, ws, x, tab, ln, rw_t, rb, tpb, alpha):
    n, d = x.shape
    tm = _row_tile(n)
    e = rw_t.shape[0]
    row = lambda i: (i, 0)
    fixed = lambda i: (0, 0)
    return pl.pallas_call(
        functools.partial(_post_kernel, len(acts), tpb, alpha),
        grid=(n // tm,),
        in_specs=([pl.BlockSpec((tm, a.shape[1]), row) for a in acts]
                  + [pl.BlockSpec(w.shape, fixed) for w in ws]
                  + [pl.BlockSpec((tm, d), row),
                     pl.BlockSpec(tab.shape, lambda i: (0, 0, 0)),
                     pl.BlockSpec(ln.shape, fixed),
                     pl.BlockSpec((e, d), fixed),
                     pl.BlockSpec((e, 1), fixed)]),
        out_specs=[pl.BlockSpec((tm, d), row), pl.BlockSpec((tm, d // 2), row),
                   pl.BlockSpec((TOP_K, tm), lambda i: (0, i)), pl.BlockSpec((TOP_K, tm), lambda i: (0, i))],
        out_shape=[jax.ShapeDtypeStruct((n, d), F32), jax.ShapeDtypeStruct((n, d // 2), jnp.uint32),
                   jax.ShapeDtypeStruct((TOP_K, n), jnp.int32), jax.ShapeDtypeStruct((TOP_K, n), F32)],
        compiler_params=_cparams(("arbitrary",)),
        name="post",
    )(*acts, *ws, x, tab, ln, rw_t, rb)


def _rank_kernel(idx_ref, tri_ref, rank_ref, cnt_ref, carry_ref):
    @pl.when(pl.program_id(0) == 0)
    def _():
        carry_ref[...] = jnp.zeros_like(carry_ref)

    idx = idx_ref[...]
    e = carry_ref.shape[0]
    tr = idx.shape[1]
    io = lax.broadcasted_iota(jnp.int32, (e, tr), 0)
    chosen = jnp.zeros((e, tr), F32)
    for k in range(TOP_K):
        chosen = chosen + (idx[k:k + 1] == io).astype(F32)
    cum = jnp.dot(chosen.astype(BF16), tri_ref[...], preferred_element_type=F32)
    base = carry_ref[:, 0:1]
    excl = base + cum - chosen
    ranks = [jnp.sum(jnp.where(idx[k:k + 1] == io, excl, 0.0), axis=0, keepdims=True) for k in range(TOP_K)]
    rank_ref[...] = jnp.concatenate(ranks, axis=0).astype(jnp.int32)
    carry_ref[...] = carry_ref[...] + jnp.sum(chosen, axis=1, keepdims=True)
    cnt_ref[...] = carry_ref[...]


def _rank(idx_t, n_experts):
    n = idx_t.shape[1]
    tr = _row_tile(n)
    tri = (np.arange(tr)[:, None] <= np.arange(tr)[None, :]).astype(np.float32)
    return pl.pallas_call(
        _rank_kernel,
        grid=(n // tr,),
        in_specs=[pl.BlockSpec((TOP_K, tr), lambda i: (0, i)),
                  pl.BlockSpec((tr, tr), lambda i: (0, 0))],
        out_specs=[pl.BlockSpec((TOP_K, tr), lambda i: (0, i)),
                   pl.BlockSpec((n_experts, LANES), lambda i: (0, 0))],
        out_shape=[jax.ShapeDtypeStruct((TOP_K, n), jnp.int32),
                   jax.ShapeDtypeStruct((n_experts, LANES), F32)],
        scratch_shapes=[pltpu.VMEM((n_experts, LANES), F32)],
        compiler_params=_cparams(("arbitrary",)),
        name="rank",
    )(idx_t, jnp.asarray(tri, BF16))


def _expert_kernel(be_ref, nb_ref, x_ref, w1_ref, b1_ref, w2_ref, b2_ref, o_ref, w1b_ref, w2b_ref):
    i = pl.program_id(0)
    used = i < nb_ref[0]
    new_expert = jnp.logical_or(i == 0, be_ref[i] != be_ref[jnp.maximum(i - 1, 0)])

    @pl.when(jnp.logical_and(used, new_expert))
    def _():
        w1b_ref[...] = w1_ref[0].astype(BF16)
        w2b_ref[...] = w2_ref[0].astype(BF16)

    @pl.when(used)
    def _():
        x_lo, x_hi = _unpack_bf16(x_ref[...])
        kh = x_lo.shape[1]
        hid = (jnp.dot(x_lo.astype(BF16), w1b_ref[:kh], preferred_element_type=F32)
               + jnp.dot(x_hi.astype(BF16), w1b_ref[kh:], preferred_element_type=F32) + b1_ref[0])
        half = hid.shape[1] // 2
        glu = jnp.minimum(hid[:, :half], SWIGLU_LIMIT)
        lin = jnp.clip(hid[:, half:], -SWIGLU_LIMIT, SWIGLU_LIMIT)
        act = glu * jax.nn.sigmoid(SWIGLU_ALPHA * glu) * (lin + 1.0)
        y = jnp.dot(act.astype(BF16), w2b_ref[...], preferred_element_type=F32) + b2_ref[0]
        o_ref[...] = _pack_bf16(y)

    @pl.when(i >= nb_ref[0])
    def _():
        o_ref[...] = jnp.zeros_like(o_ref)


def _experts(block_expert, n_used, x_pad, layer, w1, b1, w2, b2):
    n_pad, dp = x_pad.shape
    depth, e, d, dh2 = w1.shape
    n_blocks = n_pad // EXPERT_BLOCK
    grid_spec = pltpu.PrefetchScalarGridSpec(
        num_scalar_prefetch=2,
        grid=(n_blocks,),
        in_specs=[pl.BlockSpec((EXPERT_BLOCK, dp), lambda i, be, nb: (i, 0)),
                  pl.BlockSpec((None, 1, d, dh2), lambda i, be, nb: (layer, be[i], 0, 0)),
                  pl.BlockSpec((None, 1, 1, dh2), lambda i, be, nb: (layer, be[i], 0, 0)),
                  pl.BlockSpec((None, 1, dh2 // 2, d), lambda i, be, nb: (layer, be[i], 0, 0)),
                  pl.BlockSpec((None, 1, 1, d), lambda i, be, nb: (layer, be[i], 0, 0))],
        out_specs=pl.BlockSpec((EXPERT_BLOCK, dp), lambda i, be, nb: (i, 0)),
        scratch_shapes=[pltpu.VMEM((d, dh2), BF16), pltpu.VMEM((dh2 // 2, d), BF16)],
    )
    return pl.pallas_call(
        _expert_kernel,
        grid_spec=grid_spec,
        out_shape=jax.ShapeDtypeStruct((n_pad, dp), jnp.uint32),
        compiler_params=pltpu.CompilerParams(dimension_semantics=("arbitrary",),
                                             vmem_limit_bytes=EXPERT_VMEM_LIMIT),
        name="experts",
    )(block_expert, n_used, x_pad, w1, b1.reshape(depth, e, 1, dh2), w2, b2.reshape(depth, e, 1, d))


def _stream_tile(i, tpb, latent_only):
    return (i // (tpb - 1)) * tpb + i % (tpb - 1) if latent_only else i


def _combine_kernel(tpb, alpha, latent_only, y_ref, gate_ref, x_ref, tab_ref, ln_ref, xo_ref, *h_ref):
    g = gate_ref[...]
    y_lo, y_hi = None, None
    for k in range(TOP_K):
        lo, hi = _unpack_bf16(y_ref[k])
        gk = g[:, k:k + 1]
        y_lo = lo * gk if y_lo is None else y_lo + lo * gk
        y_hi = hi * gk if y_hi is None else y_hi + hi * gk
    y = jnp.concatenate([y_lo, y_hi], axis=1)
    t = tab_ref[_tab_row(_stream_tile(pl.program_id(0), tpb, latent_only), tpb)]
    xn, h = _residual_ln(x_ref[...], y, t, ln_ref[...], alpha)
    xo_ref[...] = xn
    if not latent_only:
        h_ref[0][...] = h.astype(h_ref[0].dtype)


def _combine(y_g, gates, x, tab, ln, tpb, alpha, latent_only):
    n, d = x.shape
    tm = TOK_TILE
    tiles = n // tm
    steps = tiles // tpb * (tpb - 1) if latent_only else tiles
    src = lambda i: (_stream_tile(i, tpb, latent_only), 0)
    dst = lambda i: (i, 0)
    out_specs = [pl.BlockSpec((tm, d), dst)]
    out_shape = [jax.ShapeDtypeStruct((steps * tm, d), F32)]
    if not latent_only:
        out_specs.append(pl.BlockSpec((tm, d), dst))
        out_shape.append(jax.ShapeDtypeStruct((steps * tm, d), BF16))
    return pl.pallas_call(
        functools.partial(_combine_kernel, tpb, alpha, latent_only),
        grid=(steps,),
        in_specs=[pl.BlockSpec((TOP_K, tm, d // 2), lambda i: (0, _stream_tile(i, tpb, latent_only), 0)),
                  pl.BlockSpec((tm, TOP_K), src),
                  pl.BlockSpec((tm, d), src),
                  pl.BlockSpec(tab.shape, lambda i: (0, 0, 0)),
                  pl.BlockSpec(ln.shape, lambda i: (0, 0))],
        out_specs=out_specs,
        out_shape=out_shape,
        compiler_params=_cparams(("arbitrary",)),
        name="combine",
    )(y_g, gates, x, tab, ln)


def _sc_mesh():
    return plsc.VectorSubcoreMesh(core_axis_name="c", subcore_axis_name="s")


def _sc_split(rows, mesh):
    workers = mesh.num_cores * mesh.num_subcores
    per = rows // workers
    assert per * workers == rows
    chunk = SC_CHUNK if per % SC_CHUNK == 0 else 8
    assert per % chunk == 0
    return per, chunk


def _sc_scatter_rows(x, idx, n_out):
    r, c = x.shape
    mesh = _sc_mesh()
    per, chunk = _sc_split(r, mesh)

    n_chunks = per // chunk

    @functools.partial(pl.kernel, out_type=jax.ShapeDtypeStruct((n_out, c), x.dtype), mesh=mesh,
                       scratch_types=[pltpu.VMEM((chunk, c), x.dtype), pltpu.SemaphoreType.DMA] * 2
                       + [pltpu.VMEM((chunk,), jnp.int32)] * TOP_K + [pltpu.SemaphoreType.DMA])
    def scatter(x_hbm, i_hbm, o_hbm, rows_a, sem_a, rows_b, sem_b, *rest):
        idx_vs, sem_s = rest[:TOP_K], rest[TOP_K]
        base = (lax.axis_index("s") * mesh.num_cores + lax.axis_index("c")) * per
        slots = ((rows_a, sem_a), (rows_b, sem_b))

        def load(j, slot):
            rows_v, sem = slot
            off = pl.multiple_of(base + j * chunk, chunk)
            return pltpu.make_async_copy(x_hbm.at[pl.ds(off, chunk)], rows_v, sem)

        def scatter_chunk(j, slot, prefetch):
            rows_v, _ = slot
            off = pl.multiple_of(base + j * chunk, chunk)
            load(j, slot).wait()
            for k in range(TOP_K):
                pltpu.sync_copy(i_hbm.at[pl.ds(k * r + off, chunk)], idx_vs[k])
            for k in range(TOP_K):
                pltpu.async_copy(rows_v, o_hbm.at[idx_vs[k]], sem_s)
            prefetch()
            for k in range(TOP_K):
                pltpu.make_async_copy(rows_v, o_hbm.at[idx_vs[k]], sem_s).wait()

        load(0, slots[0]).start()

        @pl.loop(0, n_chunks // 2)
        def _(p):
            j = 2 * p
            scatter_chunk(j, slots[0], lambda: load(j + 1, slots[1]).start())

            def next_even():
                @pl.when(j + 2 < n_chunks)
                def _():
                    load(j + 2, slots[0]).start()

            scatter_chunk(j + 1, slots[1], next_even)

        if n_chunks % 2:
            scatter_chunk(n_chunks - 1, slots[0], lambda: None)

    return scatter(x, idx)


def _sc_gather_rows(table, idx):
    m = idx.shape[0]
    c = table.shape[1]
    mesh = _sc_mesh()
    per, chunk = _sc_split(m, mesh)

    n_chunks = per // chunk
    slot_types = [pltpu.VMEM((chunk,), jnp.int32), pltpu.VMEM((chunk, c), table.dtype), pltpu.SemaphoreType.DMA]

    @functools.partial(pl.kernel, out_type=jax.ShapeDtypeStruct((m, c), table.dtype), mesh=mesh,
                       scratch_types=slot_types * 2)
    def gather(t_hbm, i_hbm, o_hbm, idx_a, rows_a, sem_a, idx_b, rows_b, sem_b):
        base = (lax.axis_index("s") * mesh.num_cores + lax.axis_index("c")) * per
        slots = ((idx_a, rows_a, sem_a), (idx_b, rows_b, sem_b))

        def start(j, slot):
            idx_v, rows_v, sem = slot
            off = pl.multiple_of(base + j * chunk, chunk)
            pltpu.sync_copy(i_hbm.at[pl.ds(off, chunk)], idx_v)
            pltpu.async_copy(t_hbm.at[idx_v], rows_v, sem)

        def finish(j, slot):
            idx_v, rows_v, sem = slot
            off = pl.multiple_of(base + j * chunk, chunk)
            pltpu.make_async_copy(t_hbm.at[idx_v], rows_v, sem).wait()
            pltpu.sync_copy(rows_v, o_hbm.at[pl.ds(off, chunk)])

        start(0, slots[0])

        @pl.loop(0, n_chunks // 2)
        def _(p):
            j = 2 * p
            start(j + 1, slots[1])
            finish(j, slots[0])

            @pl.when(j + 2 < n_chunks)
            def _():
                start(j + 2, slots[0])

            finish(j + 1, slots[1])

        if n_chunks % 2:
            finish(n_chunks - 1, slots[0])

    return gather(table, idx)


def _moe(h, idx_t, layer, w1, b1, w2, b2):
    n, dp = h.shape
    e = w1.shape[1]
    m = n * TOP_K
    rank_t, cnt = _rank(idx_t, e)
    sizes = cnt[:, 0].astype(jnp.int32)
    padded = (sizes + EXPERT_BLOCK - 1) // EXPERT_BLOCK * EXPERT_BLOCK
    pad_ends = jnp.cumsum(padded)
    pad_starts = pad_ends - padded
    ids = jnp.arange(e, dtype=jnp.int32)[:, None, None]
    dest_t = jnp.sum(jnp.where(idx_t[None] == ids, pad_starts[:, None, None], 0), axis=0) + rank_t
    dest = dest_t.reshape(-1)
    n_blocks = (m + e * (EXPERT_BLOCK - 1)) // EXPERT_BLOCK + 1
    n_pad = n_blocks * EXPERT_BLOCK
    block_start = jnp.arange(n_blocks, dtype=jnp.int32) * EXPERT_BLOCK
    block_expert = jnp.minimum(jnp.sum((pad_ends[None, :] <= block_start[:, None]).astype(jnp.int32), axis=1), e - 1)
    n_used = (pad_ends[-1:] // EXPERT_BLOCK).astype(jnp.int32)
    x_pad = _sc_scatter_rows(h, dest, n_pad)
    y_pad = _experts(block_expert, n_used, x_pad, layer, w1, b1, w2, b2)
    return _sc_gather_rows(y_pad, dest).reshape(TOP_K, n, dp)


def _rope_tables(t_lat, n_ctx):
    t = jnp.arange(t_lat)
    row = (t // GRID_W).astype(F32)
    col = (t % GRID_W).astype(F32)
    nf = GLA_DK // 4
    freqs = ROPE_BASE ** (-jnp.arange(nf, dtype=F32) / nf)
    ang = jnp.concatenate([row[:, None] * freqs, col[:, None] * freqs], axis=-1)
    cos, sin = jnp.cos(ang), jnp.sin(ang)
    cos2 = jnp.concatenate([cos, cos], axis=-1)
    sin2 = jnp.concatenate([-sin, sin], axis=-1)
    return (jnp.concatenate([cos2, jnp.ones((n_ctx, GLA_DK), F32)], axis=0),
            jnp.concatenate([sin2, jnp.zeros((n_ctx, GLA_DK), F32)], axis=0))


def _chunk_tri(tg, rev):
    t = np.arange(tg)
    same = (t[:, None] // GLA_CHUNK) == (t[None, :] // GLA_CHUNK)
    side = (t[None, :] >= t[:, None]) if rev else (t[None, :] <= t[:, None])
    return jnp.asarray((same & side).astype(np.float32), BF16)


def _table(mods, rows, batch):
    lat = jnp.stack([mods[:batch, r] for r in rows], axis=1)
    ctx = jnp.broadcast_to(jnp.stack([mods[batch, r] for r in rows], axis=0)[None], lat.shape)
    tab = jnp.stack([lat, ctx], axis=1).reshape(2 * batch, len(rows), -1)
    return jnp.pad(tab, ((0, 0), (0, 8 - len(rows)), (0, 0)))


@jax.jit
def _forward(x, c, ctx, c_ctx, ada_w, ada_b, ln_g, ln_b, ab_w_in, ab_pool_w, ab_pool_scale, ab_rpb,
             ab_w_out, gla_w_in, gla_w_gate, gla_b_gate, gla_norm_g, gla_w_out, router_w, router_b,
             exp_w1, exp_b1, exp_w2, exp_b2):
    batch, t_lat, d = x.shape
    n_ctx = ctx.shape[1]
    depth = ada_w.shape[0]
    assert d == D_MODEL and n_ctx == TOK_TILE and t_lat % TOK_TILE == 0
    rows = t_lat // GRID_W
    assert rows % NA_QROWS == 0 and rows >= NA_KROWS + NA_QROWS
    n_lat = t_lat // TOK_TILE
    tpb = n_lat + 1
    l = t_lat + n_ctx
    n = batch * l
    alpha = (2.0 * depth) ** 0.25

    cc = jnp.concatenate([c, c_ctx[None], jnp.zeros((16 - batch - 1, d), F32)], axis=0)
    mods = _mods(cc, ada_w, ada_b).reshape(depth, 16, N_MOD, d)

    z, h = _modulate(x.reshape(batch * t_lat, d), ctx.reshape(batch * n_ctx, d), _table(mods[0], (1, 0), batch), tpb)
    cos2, sin2 = _rope_tables(t_lat, n_ctx)

    for i in range(depth):
        j = i // 2
        last = i == depth - 1
        tab1 = _table(mods[i], (2, 4, 3), batch)
        ln1 = jnp.stack([ln_g[i, 0], ln_b[i, 0]])
        ln2 = jnp.stack([ln_g[i, 1], ln_b[i, 1]])
        rw_t = router_w[i].T
        rb = router_b[i][:, None]
        if i % 2 == 0:
            u, qkv = _proj(h, ab_w_in[j].astype(BF16), [(0, POOL_WIDTH), (POOL_WIDTH, POOL_WIDTH + 3 * NA_WIDTH)],
                           [F32, BF16])
            w_blk = jax.scipy.linalg.block_diag(*[ab_pool_w[j, g] for g in range(len(POOL_WINDOWS))])
            pooled = _pool(u.reshape(batch, l, POOL_WIDTH), w_blk.astype(BF16), ab_pool_scale[j][None, :],
                           n_lat, t_lat, n_ctx)
            bias = _na_bias_tables(ab_rpb[j], rows)
            attn = _na(qkv.reshape(batch, l, 3 * NA_WIDTH), bias, n_lat)
            w_out = ab_w_out[j].astype(BF16)
            acts = [pooled.reshape(n, POOL_WIDTH), attn.reshape(n, NA_WIDTH)]
            ws = [w_out[:POOL_WIDTH], w_out[POOL_WIDTH:]]
        else:
            qk, v, r, g = _proj(h, gla_w_in[j].astype(BF16),
                                [(0, 2 * GLA_QK), (2 * GLA_QK, 2 * GLA_QK + GLA_V),
                                 (2 * GLA_QK + GLA_V, 2 * GLA_QK + 2 * GLA_V),
                                 (2 * GLA_QK + 2 * GLA_V, 2 * GLA_QK + 2 * GLA_V + 2 * GATE_RANK)],
                                [F32, BF16, F32, F32])
            preps = [_gla_prep(rev, qk, g, cos2, sin2, gla_w_gate[j, int(rev)], gla_b_gate[j, int(rev)][None, :],
                               _chunk_tri(TOK_TILE, rev), tpb) for rev in (False, True)]
            o_f, o_b = _gla_scan(preps[0], preps[1], v, batch, tpb)
            acts = [_gla_out(o_f, o_b, r, gla_norm_g[j][None, :])]
            ws = [gla_w_out[j].astype(BF16)]
        z, h, idx_t, gates_t = _post(acts, ws, z, tab1, ln1, rw_t, rb, tpb, alpha)
        y_g = _moe(h, idx_t, i, exp_w1, exp_b1, exp_w2, exp_b2)
        nxt = mods[i + 1] if not last else mods[i]
        tab2 = _table(jnp.concatenate([mods[i][:, 5:6], nxt[:, 1:2], nxt[:, 0:1]], axis=1), (0, 1, 2), batch)
        if last:
            (out,) = _combine(y_g, gates_t.T, z, tab2, ln2, tpb, alpha, True)
            return out.reshape(batch, t_lat, d)
        z, h = _combine(y_g, gates_t.T, z, tab2, ln2, tpb, alpha, False)


def kernel(x, c, ctx, c_ctx, ada_w, ada_b, ln_g, ln_b, ab_w_in, ab_pool_w, ab_pool_scale, ab_rpb, ab_w_out,
           gla_w_in, gla_w_gate, gla_b_gate, gla_norm_g, gla_w_out, router_w, router_b, exp_w1, exp_b1, exp_w2,
           exp_b2):
    return _forward(x, c, ctx, c_ctx, ada_w, ada_b, ln_g, ln_b, ab_w_in, ab_pool_w, ab_pool_scale, ab_rpb,
                    ab_w_out, gla_w_in, gla_w_gate, gla_b_gate, gla_norm_g, gla_w_out, router_w, router_b,
                    exp_w1, exp_b1, exp_w2, exp_b2)
```

```python
import functools
import math

import numpy as np
import jax
import jax.numpy as jnp
from jax import lax
from jax.experimental import pallas as pl
from jax.experimental.pallas import tpu as pltpu
from jax.experimental.pallas import tpu_sc as plsc

F32 = jnp.float32
BF16 = jnp.bfloat16
HIGHEST = lax.Precision.HIGHEST

D_MODEL = 1024
GRID_W = 64
N_MOD = 6
POOL_WINDOWS = (2, 4, 8, 16)
POOL_WIDTH = D_MODEL // 4
POOL_GROUP_DIM = POOL_WIDTH // len(POOL_WINDOWS)
POOL_HALO = max(POOL_WINDOWS) // 2
NA_HEAD_DIM = 64
NA_HEADS = (D_MODEL - POOL_WIDTH) // NA_HEAD_DIM
NA_WIDTH = NA_HEADS * NA_HEAD_DIM
WIN_H = 8
WIN_W = 16
GLA_HEADS = 4
GLA_DK = D_MODEL // 2 // GLA_HEADS
GLA_DV = D_MODEL // GLA_HEADS
GATE_RANK = 16
GATE_NORM = 16.0
GLA_CHUNK = 64
GLA_STRIP = 8
GLA_QK = GLA_HEADS * GLA_DK
GLA_V = GLA_HEADS * GLA_DV
ROPE_BASE = 10000.0
TOP_K = 4
SWIGLU_LIMIT = 7.0
SWIGLU_ALPHA = 1.702
LN_EPS = 1e-5
RMS_EPS = 1e-6
NEG_INF = -1e30

LANES = 128
TOK_TILE = 256
NA_QROWS = 4
NA_KROWS = 12
VMEM_LIMIT = 48 * 1024 * 1024
EXPERT_VMEM_LIMIT = 56 * 1024 * 1024
EXPERT_BLOCK = 512
SC_CHUNK = 64


def _cparams(sem):
    return pltpu.CompilerParams(dimension_semantics=sem, vmem_limit_bytes=VMEM_LIMIT)


def _dot_bf16x3(a, b, dims):
    a_hi = a.astype(BF16)
    a_lo = (a - a_hi.astype(F32)).astype(BF16)
    b_hi = b.astype(BF16)
    b_lo = (b - b_hi.astype(F32)).astype(BF16)
    dg = functools.partial(lax.dot_general, dimension_numbers=(dims, ((), ())), preferred_element_type=F32)
    return dg(a_hi, b_hi) + dg(a_hi, b_lo) + dg(a_lo, b_hi)


def _row_tile(n):
    return 2 * TOK_TILE if n % (2 * TOK_TILE) == 0 else TOK_TILE


def _mods_kernel(c_ref, w_ref, b_ref, o_ref):
    cv = c_ref[...]
    sc = cv * jax.nn.sigmoid(cv)
    o_ref[0] = jnp.dot(sc, w_ref[0], precision=HIGHEST, preferred_element_type=F32) + b_ref[0]


def _mods(cc, ada_w, ada_b):
    depth, d, n = ada_w.shape
    r = cc.shape[0]
    tn = n // 4
    return pl.pallas_call(
        _mods_kernel,
        grid=(depth, n // tn),
        in_specs=[pl.BlockSpec((r, d), lambda i, j: (0, 0)),
                  pl.BlockSpec((1, d, tn), lambda i, j: (i, 0, j)),
                  pl.BlockSpec((1, 1, tn), lambda i, j: (i, 0, j))],
        out_specs=pl.BlockSpec((1, r, tn), lambda i, j: (i, 0, j)),
        out_shape=jax.ShapeDtypeStruct((depth, r, n), F32),
        compiler_params=_cparams(("arbitrary", "arbitrary")),
        name="mods",
    )(cc, ada_w, ada_b.reshape(depth, 1, n))


def _tab_row(g, tpb):
    return (g // tpb) * 2 + (g % tpb == tpb - 1).astype(jnp.int32)


def _modulate_kernel(tpb, splits, x_ref, ctx_ref, tab_ref, w_ref, z_ref, *out_refs):
    i = pl.program_id(0)
    is_ctx = i % tpb == tpb - 1
    z = jnp.where(is_ctx, ctx_ref[...], x_ref[...])
    t = tab_ref[_tab_row(i, tpb)]
    z_ref[...] = z
    _emit_proj(z * (1.0 + t[0:1]) + t[1:2], w_ref, splits, out_refs)


def _modulate(x, ctx, tab, tpb, plan):
    d = x.shape[1]
    n_lat = tpb - 1
    n = x.shape[0] + ctx.shape[0]
    w_spec, p_specs, p_shape = _proj_specs(n, TOK_TILE, plan)
    return pl.pallas_call(
        functools.partial(_modulate_kernel, tpb, tuple(plan[1])),
        grid=(n // TOK_TILE,),
        in_specs=[pl.BlockSpec((TOK_TILE, d), lambda i: ((i // tpb) * n_lat + jnp.minimum(i % tpb, n_lat - 1), 0)),
                  pl.BlockSpec((TOK_TILE, d), lambda i: (i // tpb, 0)),
                  pl.BlockSpec(tab.shape, lambda i: (0, 0, 0)),
                  w_spec],
        out_specs=[pl.BlockSpec((TOK_TILE, d), lambda i: (i, 0))] + p_specs,
        out_shape=[jax.ShapeDtypeStruct((n, d), F32)] + p_shape,
        compiler_params=_cparams(("arbitrary",)),
        name="modulate",
    )(x, ctx, tab, plan[0])


def _emit_proj(h, w_ref, splits, out_refs):
    hb = h.astype(BF16)
    for (a, b), o_ref in zip(splits, out_refs):
        o_ref[...] = jnp.dot(hb, w_ref[:, a:b], preferred_element_type=F32).astype(o_ref.dtype)


def _proj_specs(n, tm, plan):
    w, splits, dtypes = plan
    w_spec = pl.BlockSpec(w.shape, lambda i: (0, 0))
    out_specs = [pl.BlockSpec((tm, b - a), lambda i: (i, 0)) for a, b in splits]
    out_shape = [jax.ShapeDtypeStruct((n, b - a), dt) for (a, b), dt in zip(splits, dtypes)]
    return w_spec, out_specs, out_shape


def _pool_kernel(n_lat, t_lat, t_ctx, prev_ref, cur_ref, next_ref, w_ref, scale_ref, o_ref, halo_ref):
    j = pl.program_id(1)
    is_ctx = j == n_lat
    has_prev = jnp.logical_and(j > 0, jnp.logical_not(is_ctx))
    has_next = j < n_lat - 1
    cur = cur_ref[0]
    hl = POOL_HALO
    halo_ref[0:hl] = jnp.where(has_prev, prev_ref[0, TOK_TILE - hl:TOK_TILE], 0.0)
    halo_ref[hl:hl + TOK_TILE] = cur
    halo_ref[hl + TOK_TILE:2 * hl + TOK_TILE] = jnp.where(has_next, next_ref[0, 0:hl], 0.0)

    shape = cur.shape
    lane = lax.broadcasted_iota(jnp.int32, shape, 1)
    group = lane // POOL_GROUP_DIM
    half = jnp.ones(shape, jnp.int32)
    for gi, wdw in enumerate(POOL_WINDOWS):
        half = jnp.where(group == gi, wdw // 2, half)
    acc = jnp.zeros(shape, F32)
    for off in range(-hl, hl):
        v = halo_ref[hl + off:hl + off + TOK_TILE]
        inside = (half >= -off) if off < 0 else (half > off)
        acc = acc + jnp.where(inside, v, 0.0)
    pos0 = jnp.where(is_ctx, 0, j * TOK_TILE)
    seq = jnp.where(is_ctx, t_ctx, t_lat)
    t = pos0 + lax.broadcasted_iota(jnp.int32, shape, 0)
    cnt = jnp.minimum(t + half, seq) - jnp.maximum(t - half, 0)
    pooled = acc / cnt.astype(F32) - cur
    y = jnp.dot(pooled.astype(BF16), w_ref[...], preferred_element_type=F32) * scale_ref[...]
    o_ref[0] = y.astype(o_ref.dtype)


def _pool(u, w_blk, scale, n_lat, t_lat, t_ctx):
    b, l, pw = u.shape
    tpb = l // TOK_TILE
    blk = (1, TOK_TILE, pw)
    return pl.pallas_call(
        functools.partial(_pool_kernel, n_lat, t_lat, t_ctx),
        grid=(b, tpb),
        in_specs=[pl.BlockSpec(blk, lambda bi, j: (bi, jnp.maximum(j - 1, 0), 0)),
                  pl.BlockSpec(blk, lambda bi, j: (bi, j, 0)),
                  pl.BlockSpec(blk, lambda bi, j: (bi, jnp.minimum(j + 1, tpb - 1), 0)),
                  pl.BlockSpec((pw, pw), lambda bi, j: (0, 0)),
                  pl.BlockSpec((1, pw), lambda bi, j: (0, 0))],
        out_specs=pl.BlockSpec(blk, lambda bi, j: (bi, j, 0)),
        out_shape=jax.ShapeDtypeStruct((b, l, pw), BF16),
        scratch_shapes=[pltpu.VMEM((TOK_TILE + 2 * POOL_HALO, pw), F32)],
        compiler_params=_cparams(("arbitrary", "arbitrary")),
        name="pool",
    )(u, u, u, w_blk, scale)


def _na_bias_tables(rpb, rows):
    n_i = rows // NA_QROWS
    heads = rpb.shape[0]
    a = np.arange(NA_QROWS)
    kr = np.arange(NA_KROWS)
    cq = np.arange(GRID_W)
    ws = np.clip(cq - WIN_W // 2, 0, GRID_W - WIN_W)
    ok_col = (cq[None, :] >= ws[:, None]) & (cq[None, :] < ws[:, None] + WIN_W)
    dcol = np.clip(cq[None, :] - cq[:, None] + WIN_W - 1, 0, 2 * WIN_W - 2)
    oh_col = (dcol[..., None] == np.arange(2 * WIN_W - 1)).astype(np.float32)
    tabs = []
    for i in (0, 1, n_i - 1):
        start = int(np.clip(NA_QROWS * i - WIN_H // 2, 0, rows - NA_KROWS))
        r = NA_QROWS * i + a
        krow = start + kr
        rs = np.clip(r - WIN_H // 2, 0, rows - WIN_H)
        ok_row = (krow[None, :] >= rs[:, None]) & (krow[None, :] < rs[:, None] + WIN_H)
        drow = np.clip(krow[None, :] - r[:, None] + WIN_H - 1, 0, 2 * WIN_H - 2)
        oh_row = (drow[..., None] == np.arange(2 * WIN_H - 1)).astype(np.float32)
        by_row = jnp.einsum('hrc,akr->hakc', rpb, oh_row, precision=HIGHEST)
        bias = jnp.einsum('hakc,qjc->haqkj', by_row, oh_col, precision=HIGHEST)
        ok = ok_row[:, None, :, None] & ok_col[None, :, None, :]
        tabs.append(jnp.where(ok[None], bias, NEG_INF).reshape(heads, TOK_TILE, NA_KROWS * GRID_W))
    tabs.append(jnp.full_like(tabs[0], NEG_INF))
    return jnp.stack(tabs).astype(F32)


def _na_kernel(q_ref, k0_ref, k1_ref, k2_ref, kc_ref, v0_ref, v1_ref, v2_ref, vc_ref, bias_ref, o_ref):
    q = q_ref[0]
    k_refs = (k0_ref, k1_ref, k2_ref, kc_ref)
    v_refs = (v0_ref, v1_ref, v2_ref, vc_ref)
    n_band = len(k_refs) - 1
    lane = lax.broadcasted_iota(jnp.int32, q.shape, 1)
    first = lane < NA_HEAD_DIM
    outs = []
    for hh in range(2):
        qm = jnp.where(first if hh == 0 else jnp.logical_not(first), q, jnp.zeros_like(q)) * NA_HEAD_DIM ** -0.5
        scores = []
        for j, k_ref in enumerate(k_refs):
            s = lax.dot_general(qm, k_ref[0], (((1,), (1,)), ((), ())), preferred_element_type=F32)
            if j < n_band:
                s = s + bias_ref[0, hh, :, j * TOK_TILE:(j + 1) * TOK_TILE]
            scores.append(s)
        m = functools.reduce(jnp.maximum, [jnp.max(s, axis=-1, keepdims=True) for s in scores])
        l = 0.0
        o = 0.0
        for s, v_ref in zip(scores, v_refs):
            p = jnp.exp(s - m)
            l = l + jnp.sum(p, axis=-1, keepdims=True)
            o = o + jnp.dot(p.astype(BF16), v_ref[0], preferred_element_type=F32)
        outs.append(o / l)
    o_ref[0] = jnp.where(first, outs[0], outs[1]).astype(o_ref.dtype)


def _na(qkv, bias, n_lat):
    b, l, _ = qkv.shape
    tpb = l // TOK_TILE
    n_pairs = NA_WIDTH // LANES
    blk = (1, TOK_TILE, LANES)

    def kstart(i):
        return jnp.clip(i - 1, 0, n_lat - NA_KROWS // NA_QROWS)

    def btype(i):
        return jnp.where(i == 0, 0, jnp.where(i == n_lat - 1, 2, jnp.where(i == n_lat, 3, 1)))

    def kv_spec(col0, j):
        return pl.BlockSpec(blk, lambda hp, i, bi: (bi, kstart(i) + j, col0 + hp))

    def ctx_spec(col0):
        return pl.BlockSpec(blk, lambda hp, i, bi: (bi, n_lat, col0 + hp))

    nk = bias.shape[-1]
    return pl.pallas_call(
        _na_kernel,
        grid=(n_pairs, tpb, b),
        in_specs=[pl.BlockSpec(blk, lambda hp, i, bi: (bi, i, hp)),
                  kv_spec(n_pairs, 0), kv_spec(n_pairs, 1), kv_spec(n_pairs, 2), ctx_spec(n_pairs),
                  kv_spec(2 * n_pairs, 0), kv_spec(2 * n_pairs, 1), kv_spec(2 * n_pairs, 2), ctx_spec(2 * n_pairs),
                  pl.BlockSpec((1, 2, TOK_TILE, nk), lambda hp, i, bi: (btype(i), hp, 0, 0))],
        out_specs=pl.BlockSpec(blk, lambda hp, i, bi: (bi, i, hp)),
        out_shape=jax.ShapeDtypeStruct((b, l, NA_WIDTH), BF16),
        compiler_params=_cparams(("arbitrary", "arbitrary", "arbitrary")),
        name="na",
    )(qkv, qkv, qkv, qkv, qkv, qkv, qkv, qkv, qkv, bias)


def _log_sigmoid(z):
    return jnp.minimum(z, 0.0) - jnp.log(1.0 + jnp.exp(-jnp.abs(z)))


def _gla_prep_kernel(rev, q_ref, k_ref, g_ref, cos_ref, sin_ref, wg_ref, bg_ref, tri_ref,
                     qe_ref, kd_ref, a_ref, gd_ref, b_scr, qr_scr, kr_scr):
    tg = q_ref.shape[0]
    head_cols = [slice(hd * GLA_DK, (hd + 1) * GLA_DK) for hd in range(GLA_HEADS)]
    gcol = GATE_RANK if rev else 0
    gg = g_ref[:, gcol:gcol + GATE_RANK]
    z = _dot_bf16x3(gg, wg_ref[...], ((1,), (0,))) + bg_ref[...]
    la = _log_sigmoid(z) * (1.0 / GATE_NORM)
    tri = tri_ref[...]
    b = jnp.zeros_like(la)
    rest = la
    for _ in range(3):
        piece = rest.astype(BF16)
        b = b + jnp.dot(tri, piece, preferred_element_type=F32)
        rest = rest - piece.astype(F32)
    cosv = cos_ref[...]
    sinv = sin_ref[...]
    for hd, cols in enumerate(head_cols):
        q = q_ref[:, cols]
        k = k_ref[:, cols]
        bh = b[:, cols]
        b_scr[hd] = bh
        qr = (q * cosv + pltpu.roll(q, GLA_DK // 2, 1) * sinv) * GLA_DK ** -0.5
        kr = k * cosv + pltpu.roll(k, GLA_DK // 2, 1) * sinv
        qe_ref[:, cols] = (qr * jnp.exp(bh)).astype(qe_ref.dtype)
        for c in range(tg // GLA_CHUNK):
            r0 = c * GLA_CHUNK
            last = r0 if rev else r0 + GLA_CHUNK - 1
            tot = bh[last:last + 1]
            kd_ref[r0:r0 + GLA_CHUNK, cols] = (kr[r0:r0 + GLA_CHUNK]
                                               * jnp.exp(tot - bh[r0:r0 + GLA_CHUNK])).astype(kd_ref.dtype)
            gd_ref[hd, 0, c:c + 1] = jnp.exp(tot)
        qr_scr[hd] = qr
        kr_scr[hd] = kr

    n_strip = GLA_CHUNK // GLA_STRIP
    colio = lax.broadcasted_iota(jnp.int32, (GLA_STRIP, GLA_CHUNK), 1)
    rowio = lax.broadcasted_iota(jnp.int32, (GLA_STRIP, GLA_CHUNK), 0)

    def strip_scores(hd, c0, bch, qch, kch, u):
        lo, hi = u * GLA_STRIP, (u + 1) * GLA_STRIP
        bu, qu = bch[lo:hi], qch[lo:hi]
        krows, ref = (slice(hi, GLA_CHUNK), hi) if rev else (slice(0, lo), lo - 1)
        if krows.stop > krows.start:
            rb = b_scr[hd, pl.ds(c0 + ref, 1), :]
            qt = qu * jnp.exp(jnp.minimum(bu - rb, 0.0))
            kt = kch[krows] * jnp.exp(jnp.minimum(rb - bch[krows], 0.0))
            pad = jnp.zeros((GLA_CHUNK - kt.shape[0], GLA_DK), F32)
            kt = jnp.concatenate([pad, kt] if rev else [kt, pad], axis=0)
            acc = lax.dot_general(qt.astype(BF16), kt.astype(BF16), (((1,), (1,)), ((), ())),
                                  preferred_element_type=F32)
        else:
            acc = jnp.zeros((GLA_STRIP, GLA_CHUNK), F32)
        for s in range(GLA_STRIP):
            ks = kr_scr[hd, pl.ds(c0 + lo + s, 1), :]
            bs = b_scr[hd, pl.ds(c0 + lo + s, 1), :]
            col = jnp.sum(qu * ks * jnp.exp(bu - bs), axis=1, keepdims=True)
            causal = (rowio <= s) if rev else (rowio >= s)
            acc = jnp.where(jnp.logical_and(colio == lo + s, causal), col, acc)
        return acc

    def chunk_scores(c, carry):
        c0 = pl.multiple_of(c * GLA_CHUNK, GLA_CHUNK)
        for hd in range(GLA_HEADS):
            bch = b_scr[hd, pl.ds(c0, GLA_CHUNK), :]
            qch = qr_scr[hd, pl.ds(c0, GLA_CHUNK), :]
            kch = kr_scr[hd, pl.ds(c0, GLA_CHUNK), :]
            for u in range(0, n_strip, 2):
                pair = jnp.concatenate([strip_scores(hd, c0, bch, qch, kch, u),
                                        strip_scores(hd, c0, bch, qch, kch, u + 1)], axis=0)
                a_ref[hd, pl.ds(c0 + u * GLA_STRIP, 2 * GLA_STRIP), :] = pair.astype(a_ref.dtype)
        return carry

    lax.fori_loop(0, tg // GLA_CHUNK, chunk_scores, 0)


def _gla_prep(rev, qk, g, cos2, sin2, w_gate, b_gate, tri, tpb):
    n = qk.shape[0]
    tg = TOK_TILE
    nt = n // tg
    return pl.pallas_call(
        functools.partial(_gla_prep_kernel, rev),
        grid=(nt,),
        in_specs=[pl.BlockSpec((tg, GLA_QK), lambda t: (t, 0)),
                  pl.BlockSpec((tg, GLA_QK), lambda t: (t, 1)),
                  pl.BlockSpec((tg, 2 * GATE_RANK), lambda t: (t, 0)),
                  pl.BlockSpec((tg, GLA_DK), lambda t: (t % tpb, 0)),
                  pl.BlockSpec((tg, GLA_DK), lambda t: (t % tpb, 0)),
                  pl.BlockSpec((GATE_RANK, GLA_QK), lambda t: (0, 0)),
                  pl.BlockSpec((1, GLA_QK), lambda t: (0, 0)),
                  pl.BlockSpec((tg, tg), lambda t: (0, 0))],
        out_specs=[pl.BlockSpec((tg, GLA_QK), lambda t: (t, 0)),
                   pl.BlockSpec((tg, GLA_QK), lambda t: (t, 0)),
                   pl.BlockSpec((GLA_HEADS, tg, GLA_CHUNK), lambda t: (0, t, 0)),
                   pl.BlockSpec((GLA_HEADS, 1, tg // GLA_CHUNK, GLA_DK), lambda t: (0, t, 0, 0))],
        out_shape=[jax.ShapeDtypeStruct((n, GLA_QK), BF16),
                   jax.ShapeDtypeStruct((n, GLA_QK), BF16),
                   jax.ShapeDtypeStruct((GLA_HEADS, n, GLA_CHUNK), BF16),
                   jax.ShapeDtypeStruct((GLA_HEADS, nt, tg // GLA_CHUNK, GLA_DK), F32)],
        scratch_shapes=[pltpu.VMEM((GLA_HEADS, tg, GLA_DK), F32)] * 3,
        compiler_params=_cparams(("arbitrary",)),
        name="gla_prep_bwd" if rev else "gla_prep_fwd",
    )(qk, qk, g, cos2, sin2, w_gate, b_gate, tri)


def _gla_scan_kernel(*refs):
    dirs = (refs[0:5], refs[5:10])
    o_refs = refs[10:12]
    st_ref = refs[12]

    @pl.when(pl.program_id(1) == 0)
    def _():
        st_ref[...] = jnp.zeros_like(st_ref)

    n_chunks = refs[0].shape[0] // GLA_CHUNK
    for hd in range(GLA_HEADS):
        kcols = slice(hd * GLA_DK, (hd + 1) * GLA_DK)
        vcols = slice(hd * GLA_DV, (hd + 1) * GLA_DV)
        states = [st_ref[0, hd], st_ref[1, hd]]
        for cc in range(n_chunks):
            for rev in (0, 1):
                qe_ref, kd_ref, a_ref, gd_ref, v_ref = dirs[rev]
                c = n_chunks - 1 - cc if rev else cc
                rows = slice(c * GLA_CHUNK, (c + 1) * GLA_CHUNK)
                st = states[rev]
                v_c = v_ref[rows, vcols]
                o = lax.dot_general(qe_ref[rows, kcols], st.astype(BF16), (((1,), (1,)), ((), ())),
                                    preferred_element_type=F32)
                o = o + jnp.dot(a_ref[hd, rows, :], v_c, preferred_element_type=F32)
                o_refs[rev][rows, vcols] = o
                upd = lax.dot_general(v_c, kd_ref[rows, kcols], (((0,), (0,)), ((), ())),
                                      preferred_element_type=F32)
                states[rev] = st * gd_ref[hd, 0, c:c + 1, :] + upd
        st_ref[0, hd] = states[0]
        st_ref[1, hd] = states[1]


def _gla_scan(prep_f, prep_b, v, batch, tpb):
    n = v.shape[0]
    tg = TOK_TILE
    n_lat = tpb - 1

    def specs(rev):
        def tile(bi, s):
            lat = n_lat - s if rev else s - 1
            return bi * tpb + jnp.where(s == 0, n_lat, lat)

        ins = [pl.BlockSpec((tg, GLA_QK), lambda bi, s: (tile(bi, s), 0)),
               pl.BlockSpec((tg, GLA_QK), lambda bi, s: (tile(bi, s), 0)),
               pl.BlockSpec((GLA_HEADS, tg, GLA_CHUNK), lambda bi, s: (0, tile(bi, s), 0)),
               pl.BlockSpec((GLA_HEADS, 1, tg // GLA_CHUNK, GLA_DK), lambda bi, s: (0, tile(bi, s), 0, 0)),
               pl.BlockSpec((tg, GLA_V), lambda bi, s: (tile(bi, s), 0))]
        return ins, pl.BlockSpec((tg, GLA_V), lambda bi, s: (tile(bi, s), 0))

    in_f, out_f = specs(False)
    in_b, out_b = specs(True)
    return pl.pallas_call(
        _gla_scan_kernel,
        grid=(batch, tpb),
        in_specs=in_f + in_b,
        out_specs=[out_f, out_b],
        out_shape=[jax.ShapeDtypeStruct((n, GLA_V), F32)] * 2,
        scratch_shapes=[pltpu.VMEM((2, GLA_HEADS, GLA_DV, GLA_DK), F32)],
        compiler_params=_cparams(("arbitrary", "arbitrary")),
        name="gla_scan",
    )(*prep_f, v, *prep_b, v)


def _pack_bf16(x):
    half = x.shape[1] // 2
    lo = lax.bitcast_convert_type(x[:, :half].astype(BF16).astype(F32), jnp.uint32)
    hi = lax.bitcast_convert_type(x[:, half:].astype(BF16).astype(F32), jnp.uint32)
    return (lo >> 16) | (hi & jnp.uint32(0xFFFF0000))


def _unpack_bf16(p):
    lo = lax.bitcast_convert_type(p << 16, F32)
    hi = lax.bitcast_convert_type(p & jnp.uint32(0xFFFF0000), F32)
    return lo, hi


def _residual_ln(x, a, t, ln, alpha):
    y = alpha * x + t[0:1] * a
    mu = jnp.mean(y, axis=-1, keepdims=True)
    yc = y - mu
    var = jnp.mean(yc * yc, axis=-1, keepdims=True)
    xn = yc * lax.rsqrt(var + LN_EPS) * ln[0:1] + ln[1:2]
    return xn, xn * (1.0 + t[1:2]) + t[2:3]


def _top4_softmax(lt):
    e = lt.shape[0]
    io = lax.broadcasted_iota(jnp.int32, lt.shape, 0)
    work = lt
    idxs, vals = [], []
    for _ in range(TOP_K):
        m = jnp.max(work, axis=0, keepdims=True)
        ik = jnp.min(jnp.where(work == m, io, e), axis=0, keepdims=True)
        idxs.append(ik)
        vals.append(m)
        work = jnp.where(io == ik, -jnp.inf, work)
    ex = [jnp.exp(v - vals[0]) for v in vals]
    den = ex[0] + ex[1] + ex[2] + ex[3]
    return jnp.concatenate(idxs, axis=0), jnp.concatenate([x / den for x in ex], axis=0)


def _gla_gated_norm(o, r, gn):
    gate = r * jax.nn.sigmoid(r)
    heads = []
    for hd in range(GLA_HEADS):
        cols = slice(hd * GLA_DV, (hd + 1) * GLA_DV)
        oh = o[:, cols]
        ms = jnp.mean(oh * oh, axis=-1, keepdims=True)
        heads.append((oh * lax.rsqrt(ms + RMS_EPS) * gn * gate[:, cols]).astype(BF16))
    return jnp.concatenate(heads, axis=1)


def _post_kernel(gla, n_act, tpb, alpha, *refs):
    if gla:
        of_ref, ob_ref, r_ref, gn_ref, w_ref = refs[:5]
        rest = refs[5:]
        act = _gla_gated_norm(of_ref[...] + ob_ref[...], r_ref[...], gn_ref[...])
        a = jnp.dot(act, w_ref[...], preferred_element_type=F32)
    else:
        acts = refs[:n_act]
        ws = refs[n_act:2 * n_act]
        rest = refs[2 * n_act:]
        a = jnp.dot(acts[0][...], ws[0][...], preferred_element_type=F32)
        for k in range(1, n_act):
            a = a + jnp.dot(acts[k][...], ws[k][...], preferred_element_type=F32)
    x_ref, tab_ref, ln_ref, rw_ref, rb_ref, xo_ref, h_ref, idx_ref, gate_ref = rest
    tm = x_ref.shape[0]
    ln = ln_ref[...]
    for s in range(tm // TOK_TILE):
        rows = slice(s * TOK_TILE, (s + 1) * TOK_TILE)
        t = tab_ref[_tab_row(pl.program_id(0) * (tm // TOK_TILE) + s, tpb)]
        xn, h = _residual_ln(x_ref[rows], a[rows], t, ln, alpha)
        xo_ref[rows] = xn
        h_ref[rows] = _pack_bf16(h)
        lt = _dot_bf16x3(rw_ref[...], h, ((1,), (1,))) + rb_ref[...]
        idx, gates = _top4_softmax(lt)
        idx_ref[:, rows] = idx
        gate_ref[:, rows] = gates


def _post(acts, ws, x, tab, ln, rw_t, rb, tpb, alpha, gla=False):
    n, d = x.shape
    tm = _row_tile(n)
    e = rw_t.shape[0]
    row = lambda i: (i, 0)
    fixed = lambda i: (0, 0)
    return pl.pallas_call(
        functools.partial(_post_kernel, gla, len(acts), tpb, alpha),
        grid=(n // tm,),
        in_specs=([pl.BlockSpec((tm, a.shape[1]), row) for a in acts]
                  + [pl.BlockSpec(w.shape, fixed) for w in ws]
                  + [pl.BlockSpec((tm, d), row),
                     pl.BlockSpec(tab.shape, lambda i: (0, 0, 0)),
                     pl.BlockSpec(ln.shape, fixed),
                     pl.BlockSpec((e, d), fixed),
                     pl.BlockSpec((e, 1), fixed)]),
        out_specs=[pl.BlockSpec((tm, d), row), pl.BlockSpec((tm, d // 2), row),
                   pl.BlockSpec((TOP_K, tm), lambda i: (0, i)), pl.BlockSpec((TOP_K, tm), lambda i: (0, i))],
        out_shape=[jax.ShapeDtypeStruct((n, d), F32), jax.ShapeDtypeStruct((n, d // 2), jnp.uint32),
                   jax.ShapeDtypeStruct((TOP_K, n), jnp.int32), jax.ShapeDtypeStruct((TOP_K, n), F32)],
        compiler_params=_cparams(("arbitrary",)),
        name="post",
    )(*acts, *ws, x, tab, ln, rw_t, rb)


def _rank_kernel(idx_ref, tri_ref, rank_ref, cnt_ref, carry_ref):
    @pl.when(pl.program_id(0) == 0)
    def _():
        carry_ref[...] = jnp.zeros_like(carry_ref)

    idx = idx_ref[...]
    e = carry_ref.shape[0]
    tr = idx.shape[1]
    io = lax.broadcasted_iota(jnp.int32, (e, tr), 0)
    chosen = jnp.zeros((e, tr), F32)
    for k in range(TOP_K):
        chosen = chosen + (idx[k:k + 1] == io).astype(F32)
    cum = jnp.dot(chosen.astype(BF16), tri_ref[...], preferred_element_type=F32)
    base = carry_ref[:, 0:1]
    excl = base + cum - chosen
    ranks = [jnp.sum(jnp.where(idx[k:k + 1] == io, excl, 0.0), axis=0, keepdims=True) for k in range(TOP_K)]
    rank_ref[...] = jnp.concatenate(ranks, axis=0).astype(jnp.int32)
    carry_ref[...] = carry_ref[...] + jnp.sum(chosen, axis=1, keepdims=True)
    cnt_ref[...] = carry_ref[...]


def _rank(idx_t, n_experts):
    n = idx_t.shape[1]
    tr = _row_tile(n)
    tri = (np.arange(tr)[:, None] <= np.arange(tr)[None, :]).astype(np.float32)
    return pl.pallas_call(
        _rank_kernel,
        grid=(n // tr,),
        in_specs=[pl.BlockSpec((TOP_K, tr), lambda i: (0, i)),
                  pl.BlockSpec((tr, tr), lambda i: (0, 0))],
        out_specs=[pl.BlockSpec((TOP_K, tr), lambda i: (0, i)),
                   pl.BlockSpec((n_experts, LANES), lambda i: (0, 0))],
        out_shape=[jax.ShapeDtypeStruct((TOP_K, n), jnp.int32),
                   jax.ShapeDtypeStruct((n_experts, LANES), F32)],
        scratch_shapes=[pltpu.VMEM((n_experts, LANES), F32)],
        compiler_params=_cparams(("arbitrary",)),
        name="rank",
    )(idx_t, jnp.asarray(tri, BF16))


def _expert_kernel(be_ref, nb_ref, x_ref, w1_ref, b1_ref, w2_ref, b2_ref, o_ref, w1b_ref, w2b_ref):
    i = pl.program_id(0)
    used = i < nb_ref[0]
    new_expert = jnp.logical_or(i == 0, be_ref[i] != be_ref[jnp.maximum(i - 1, 0)])

    @pl.when(jnp.logical_and(used, new_expert))
    def _():
        w1b_ref[...] = w1_ref[0].astype(BF16)
        w2b_ref[...] = w2_ref[0].astype(BF16)

    @pl.when(used)
    def _():
        x_lo, x_hi = _unpack_bf16(x_ref[...])
        kh = x_lo.shape[1]
        hid = (jnp.dot(x_lo.astype(BF16), w1b_ref[:kh], preferred_element_type=F32)
               + jnp.dot(x_hi.astype(BF16), w1b_ref[kh:], preferred_element_type=F32) + b1_ref[0])
        half = hid.shape[1] // 2
        glu = jnp.minimum(hid[:, :half], SWIGLU_LIMIT)
        lin = jnp.clip(hid[:, half:], -SWIGLU_LIMIT, SWIGLU_LIMIT)
        act = glu * jax.nn.sigmoid(SWIGLU_ALPHA * glu) * (lin + 1.0)
        y = jnp.dot(act.astype(BF16), w2b_ref[...], preferred_element_type=F32) + b2_ref[0]
        o_ref[...] = _pack_bf16(y)

    @pl.when(i >= nb_ref[0])
    def _():
        o_ref[...] = jnp.zeros_like(o_ref)


def _experts(block_expert, n_used, x_pad, layer, w1, b1, w2, b2):
    n_pad, dp = x_pad.shape
    depth, e, d, dh2 = w1.shape
    n_blocks = n_pad // EXPERT_BLOCK
    grid_spec = pltpu.PrefetchScalarGridSpec(
        num_scalar_prefetch=2,
        grid=(n_blocks,),
        in_specs=[pl.BlockSpec((EXPERT_BLOCK, dp), lambda i, be, nb: (i, 0)),
                  pl.BlockSpec((None, 1, d, dh2), lambda i, be, nb: (layer, be[i], 0, 0)),
                  pl.BlockSpec((None, 1, 1, dh2), lambda i, be, nb: (layer, be[i], 0, 0)),
                  pl.BlockSpec((None, 1, dh2 // 2, d), lambda i, be, nb: (layer, be[i], 0, 0)),
                  pl.BlockSpec((None, 1, 1, d), lambda i, be, nb: (layer, be[i], 0, 0))],
        out_specs=pl.BlockSpec((EXPERT_BLOCK, dp), lambda i, be, nb: (i, 0)),
        scratch_shapes=[pltpu.VMEM((d, dh2), BF16), pltpu.VMEM((dh2 // 2, d), BF16)],
    )
    return pl.pallas_call(
        _expert_kernel,
        grid_spec=grid_spec,
        out_shape=jax.ShapeDtypeStruct((n_pad, dp), jnp.uint32),
        compiler_params=pltpu.CompilerParams(dimension_semantics=("arbitrary",),
                                             vmem_limit_bytes=EXPERT_VMEM_LIMIT),
        name="experts",
    )(block_expert, n_used, x_pad, w1, b1.reshape(depth, e, 1, dh2), w2, b2.reshape(depth, e, 1, d))


def _stream_tile(i, tpb, latent_only):
    return (i // (tpb - 1)) * tpb + i % (tpb - 1) if latent_only else i


def _combine_kernel(tpb, alpha, splits, y_ref, gate_ref, x_ref, tab_ref, ln_ref, *refs):
    latent_only = splits is None
    g = gate_ref[...]
    y_lo, y_hi = None, None
    for k in range(TOP_K):
        lo, hi = _unpack_bf16(y_ref[k])
        gk = g[:, k:k + 1]
        y_lo = lo * gk if y_lo is None else y_lo + lo * gk
        y_hi = hi * gk if y_hi is None else y_hi + hi * gk
    y = jnp.concatenate([y_lo, y_hi], axis=1)
    t = tab_ref[_tab_row(_stream_tile(pl.program_id(0), tpb, latent_only), tpb)]
    xn, h = _residual_ln(x_ref[...], y, t, ln_ref[...], alpha)
    if latent_only:
        refs[0][...] = xn
    else:
        w_ref, xo_ref = refs[:2]
        xo_ref[...] = xn
        _emit_proj(h, w_ref, splits, refs[2:])


def _combine(y_g, gates, x, tab, ln, tpb, alpha, plan):
    n, d = x.shape
    tm = TOK_TILE
    tiles = n // tm
    latent_only = plan is None
    steps = tiles // tpb * (tpb - 1) if latent_only else tiles
    src = lambda i: (_stream_tile(i, tpb, latent_only), 0)
    in_specs = [pl.BlockSpec((TOP_K, tm, d // 2), lambda i: (0, _stream_tile(i, tpb, latent_only), 0)),
                pl.BlockSpec((tm, TOP_K), src),
                pl.BlockSpec((tm, d), src),
                pl.BlockSpec(tab.shape, lambda i: (0, 0, 0)),
                pl.BlockSpec(ln.shape, lambda i: (0, 0))]
    out_specs = [pl.BlockSpec((tm, d), lambda i: (i, 0))]
    out_shape = [jax.ShapeDtypeStruct((steps * tm, d), F32)]
    operands = [y_g, gates, x, tab, ln]
    if not latent_only:
        w_spec, p_specs, p_shape = _proj_specs(n, tm, plan)
        in_specs.append(w_spec)
        out_specs += p_specs
        out_shape += p_shape
        operands.append(plan[0])
    return pl.pallas_call(
        functools.partial(_combine_kernel, tpb, alpha, None if latent_only else tuple(plan[1])),
        grid=(steps,),
        in_specs=in_specs,
        out_specs=out_specs,
        out_shape=out_shape,
        compiler_params=_cparams(("arbitrary",)),
        name="combine",
    )(*operands)


def _sc_mesh():
    return plsc.VectorSubcoreMesh(core_axis_name="c", subcore_axis_name="s")


def _sc_split(rows, mesh):
    workers = mesh.num_cores * mesh.num_subcores
    per = rows // workers
    assert per * workers == rows
    chunk = SC_CHUNK if per % SC_CHUNK == 0 else 8
    assert per % chunk == 0
    return per, chunk


def _sc_scatter_rows(x, idx, n_out):
    r, c = x.shape
    mesh = _sc_mesh()
    per, chunk = _sc_split(r, mesh)

    n_chunks = per // chunk

    @functools.partial(pl.kernel, out_type=jax.ShapeDtypeStruct((n_out, c), x.dtype), mesh=mesh,
                       scratch_types=[pltpu.VMEM((chunk, c), x.dtype), pltpu.SemaphoreType.DMA] * 2
                       + [pltpu.VMEM((chunk,), jnp.int32)] * TOP_K + [pltpu.SemaphoreType.DMA])
    def scatter(x_hbm, i_hbm, o_hbm, rows_a, sem_a, rows_b, sem_b, *rest):
        idx_vs, sem_s = rest[:TOP_K], rest[TOP_K]
        base = (lax.axis_index("s") * mesh.num_cores + lax.axis_index("c")) * per
        slots = ((rows_a, sem_a), (rows_b, sem_b))

        def load(j, slot):
            rows_v, sem = slot
            off = pl.multiple_of(base + j * chunk, chunk)
            return pltpu.make_async_copy(x_hbm.at[pl.ds(off, chunk)], rows_v, sem)

        def scatter_chunk(j, slot, prefetch):
            rows_v, _ = slot
            off = pl.multiple_of(base + j * chunk, chunk)
            load(j, slot).wait()
            for k in range(TOP_K):
                pltpu.sync_copy(i_hbm.at[pl.ds(k * r + off, chunk)], idx_vs[k])
            for k in range(TOP_K):
                pltpu.async_copy(rows_v, o_hbm.at[idx_vs[k]], sem_s)
            prefetch()
            for k in range(TOP_K):
                pltpu.make_async_copy(rows_v, o_hbm.at[idx_vs[k]], sem_s).wait()

        load(0, slots[0]).start()

        @pl.loop(0, n_chunks // 2)
        def _(p):
            j = 2 * p
            scatter_chunk(j, slots[0], lambda: load(j + 1, slots[1]).start())

            def next_even():
                @pl.when(j + 2 < n_chunks)
                def _():
                    load(j + 2, slots[0]).start()

            scatter_chunk(j + 1, slots[1], next_even)

        if n_chunks % 2:
            scatter_chunk(n_chunks - 1, slots[0], lambda: None)

    return scatter(x, idx)


def _sc_gather_rows(table, idx):
    m = idx.shape[0]
    c = table.shape[1]
    mesh = _sc_mesh()
    per, chunk = _sc_split(m, mesh)

    n_chunks = per // chunk
    slot_types = [pltpu.VMEM((chunk,), jnp.int32), pltpu.VMEM((chunk, c), table.dtype), pltpu.SemaphoreType.DMA]

    @functools.partial(pl.kernel, out_type=jax.ShapeDtypeStruct((m, c), table.dtype), mesh=mesh,
                       scratch_types=slot_types * 2)
    def gather(t_hbm, i_hbm, o_hbm, idx_a, rows_a, sem_a, idx_b, rows_b, sem_b):
        base = (lax.axis_index("s") * mesh.num_cores + lax.axis_index("c")) * per
        slots = ((idx_a, rows_a, sem_a), (idx_b, rows_b, sem_b))

        def start(j, slot):
            idx_v, rows_v, sem = slot
            off = pl.multiple_of(base + j * chunk, chunk)
            pltpu.sync_copy(i_hbm.at[pl.ds(off, chunk)], idx_v)
            pltpu.async_copy(t_hbm.at[idx_v], rows_v, sem)

        def finish(j, slot):
            idx_v, rows_v, sem = slot
            off = pl.multiple_of(base + j * chunk, chunk)
            pltpu.make_async_copy(t_hbm.at[idx_v], rows_v, sem).wait()
            pltpu.sync_copy(rows_v, o_hbm.at[pl.ds(off, chunk)])

        start(0, slots[0])

        @pl.loop(0, n_chunks // 2)
        def _(p):
            j = 2 * p
            start(j + 1, slots[1])
            finish(j, slots[0])

            @pl.when(j + 2 < n_chunks)
            def _():
                start(j + 2, slots[0])

            finish(j + 1, slots[1])

        if n_chunks % 2:
            finish(n_chunks - 1, slots[0])

    return gather(table, idx)


def _moe(h, idx_t, layer, w1, b1, w2, b2):
    n, dp = h.shape
    e = w1.shape[1]
    m = n * TOP_K
    rank_t, cnt = _rank(idx_t, e)
    sizes = cnt[:, 0].astype(jnp.int32)
    padded = (sizes + EXPERT_BLOCK - 1) // EXPERT_BLOCK * EXPERT_BLOCK
    pad_ends = jnp.cumsum(padded)
    pad_starts = pad_ends - padded
    ids = jnp.arange(e, dtype=jnp.int32)[:, None, None]
    dest_t = jnp.sum(jnp.where(idx_t[None] == ids, pad_starts[:, None, None], 0), axis=0) + rank_t
    dest = dest_t.reshape(-1)
    n_blocks = (m + e * (EXPERT_BLOCK - 1)) // EXPERT_BLOCK + 1
    n_pad = n_blocks * EXPERT_BLOCK
    block_start = jnp.arange(n_blocks, dtype=jnp.int32) * EXPERT_BLOCK
    block_expert = jnp.minimum(jnp.sum((pad_ends[None, :] <= block_start[:, None]).astype(jnp.int32), axis=1), e - 1)
    n_used = (pad_ends[-1:] // EXPERT_BLOCK).astype(jnp.int32)
    x_pad = _sc_scatter_rows(h, dest, n_pad)
    y_pad = _experts(block_expert, n_used, x_pad, layer, w1, b1, w2, b2)
    return _sc_gather_rows(y_pad, dest).reshape(TOP_K, n, dp)


def _rope_tables(t_lat, n_ctx):
    t = jnp.arange(t_lat)
    row = (t // GRID_W).astype(F32)
    col = (t % GRID_W).astype(F32)
    nf = GLA_DK // 4
    freqs = ROPE_BASE ** (-jnp.arange(nf, dtype=F32) / nf)
    ang = jnp.concatenate([row[:, None] * freqs, col[:, None] * freqs], axis=-1)
    cos, sin = jnp.cos(ang), jnp.sin(ang)
    cos2 = jnp.concatenate([cos, cos], axis=-1)
    sin2 = jnp.concatenate([-sin, sin], axis=-1)
    return (jnp.concatenate([cos2, jnp.ones((n_ctx, GLA_DK), F32)], axis=0),
            jnp.concatenate([sin2, jnp.zeros((n_ctx, GLA_DK), F32)], axis=0))


def _chunk_tri(tg, rev):
    t = np.arange(tg)
    same = (t[:, None] // GLA_CHUNK) == (t[None, :] // GLA_CHUNK)
    side = (t[None, :] >= t[:, None]) if rev else (t[None, :] <= t[:, None])
    return jnp.asarray((same & side).astype(np.float32), BF16)


def _table(mods, rows, batch):
    lat = jnp.stack([mods[:batch, r] for r in rows], axis=1)
    ctx = jnp.broadcast_to(jnp.stack([mods[batch, r] for r in rows], axis=0)[None], lat.shape)
    tab = jnp.stack([lat, ctx], axis=1).reshape(2 * batch, len(rows), -1)
    return jnp.pad(tab, ((0, 0), (0, 8 - len(rows)), (0, 0)))


@jax.jit
def _forward(x, c, ctx, c_ctx, ada_w, ada_b, ln_g, ln_b, ab_w_in, ab_pool_w, ab_pool_scale, ab_rpb,
             ab_w_out, gla_w_in, gla_w_gate, gla_b_gate, gla_norm_g, gla_w_out, router_w, router_b,
             exp_w1, exp_b1, exp_w2, exp_b2):
    batch, t_lat, d = x.shape
    n_ctx = ctx.shape[1]
    depth = ada_w.shape[0]
    assert d == D_MODEL and n_ctx == TOK_TILE and t_lat % TOK_TILE == 0
    rows = t_lat // GRID_W
    assert rows % NA_QROWS == 0 and rows >= NA_KROWS + NA_QROWS
    n_lat = t_lat // TOK_TILE
    tpb = n_lat + 1
    l = t_lat + n_ctx
    n = batch * l
    alpha = (2.0 * depth) ** 0.25

    cc = jnp.concatenate([c, c_ctx[None], jnp.zeros((16 - batch - 1, d), F32)], axis=0)
    mods = _mods(cc, ada_w, ada_b).reshape(depth, 16, N_MOD, d)

    def in_proj_plan(i):
        if i % 2 == 0:
            return (ab_w_in[i // 2].astype(BF16),
                    [(0, POOL_WIDTH), (POOL_WIDTH, POOL_WIDTH + 3 * NA_WIDTH)], [F32, BF16])
        edges = (0, 2 * GLA_QK, 2 * GLA_QK + GLA_V, 2 * GLA_QK + 2 * GLA_V, 2 * GLA_QK + 2 * GLA_V + 2 * GATE_RANK)
        return gla_w_in[i // 2].astype(BF16), list(zip(edges[:-1], edges[1:])), [F32, BF16, F32, F32]

    z, *projected = _modulate(x.reshape(batch * t_lat, d), ctx.reshape(batch * n_ctx, d),
                              _table(mods[0], (1, 0), batch), tpb, in_proj_plan(0))
    cos2, sin2 = _rope_tables(t_lat, n_ctx)

    for i in range(depth):
        j = i // 2
        last = i == depth - 1
        tab1 = _table(mods[i], (2, 4, 3), batch)
        ln1 = jnp.stack([ln_g[i, 0], ln_b[i, 0]])
        ln2 = jnp.stack([ln_g[i, 1], ln_b[i, 1]])
        rw_t = router_w[i].T
        rb = router_b[i][:, None]
        if i % 2 == 0:
            u, qkv = projected
            w_blk = jax.scipy.linalg.block_diag(*[ab_pool_w[j, g] for g in range(len(POOL_WINDOWS))])
            pooled = _pool(u.reshape(batch, l, POOL_WIDTH), w_blk.astype(BF16), ab_pool_scale[j][None, :],
                           n_lat, t_lat, n_ctx)
            bias = _na_bias_tables(ab_rpb[j], rows)
            attn = _na(qkv.reshape(batch, l, 3 * NA_WIDTH), bias, n_lat)
            w_out = ab_w_out[j].astype(BF16)
            acts = [pooled.reshape(n, POOL_WIDTH), attn.reshape(n, NA_WIDTH)]
            ws = [w_out[:POOL_WIDTH], w_out[POOL_WIDTH:]]
        else:
            qk, v, r, g = projected
            preps = [_gla_prep(rev, qk, g, cos2, sin2, gla_w_gate[j, int(rev)], gla_b_gate[j, int(rev)][None, :],
                               _chunk_tri(TOK_TILE, rev), tpb) for rev in (False, True)]
            o_f, o_b = _gla_scan(preps[0], preps[1], v, batch, tpb)
            acts = [o_f, o_b, r]
            ws = [gla_norm_g[j][None, :], gla_w_out[j].astype(BF16)]
        z, h, idx_t, gates_t = _post(acts, ws, z, tab1, ln1, rw_t, rb, tpb, alpha, gla=i % 2 == 1)
        y_g = _moe(h, idx_t, i, exp_w1, exp_b1, exp_w2, exp_b2)
        nxt = mods[i + 1] if not last else mods[i]
        tab2 = _table(jnp.concatenate([mods[i][:, 5:6], nxt[:, 1:2], nxt[:, 0:1]], axis=1), (0, 1, 2), batch)
        if last:
            (out,) = _combine(y_g, gates_t.T, z, tab2, ln2, tpb, alpha, None)
            return out.reshape(batch, t_lat, d)
        z, *projected = _combine(y_g, gates_t.T, z, tab2, ln2, tpb, alpha, in_proj_plan(i + 1))


def kernel(x, c, ctx, c_ctx, ada_w, ada_b, ln_g, ln_b, ab_w_in, ab_pool_w, ab_pool_scale, ab_rpb, ab_w_out,
           gla_w_in, gla_w_gate, gla_b_gate, gla_norm_g, gla_w_out, router_w, router_b, exp_w1, exp_b1, exp_w2,
           exp_b2):
    return _forward(x, c, ctx, c_ctx, ada_w, ada_b, ln_g, ln_b, ab_w_in, ab_pool_w, ab_pool_scale, ab_rpb,
                    ab_w_out, gla_w_in, gla_w_gate, gla_b_gate, gla_norm_g, gla_w_out, router_w, router_b,
                    exp_w1, exp_b1, exp_w2, exp_b2)
```

```python
import functools
import math

import numpy as np
import jax
import jax.numpy as jnp
from jax import lax
from jax.experimental import pallas as pl
from jax.experimental.pallas import tpu as pltpu
from jax.experimental.pallas import tpu_sc as plsc

F32 = jnp.float32
BF16 = jnp.bfloat16
HIGHEST = lax.Precision.HIGHEST

D_MODEL = 1024
GRID_W = 64
N_MOD = 6
POOL_WINDOWS = (2, 4, 8, 16)
POOL_WIDTH = D_MODEL // 4
POOL_GROUP_DIM = POOL_WIDTH // len(POOL_WINDOWS)
POOL_HALO = max(POOL_WINDOWS) // 2
NA_HEAD_DIM = 64
NA_HEADS = (D_MODEL - POOL_WIDTH) // NA_HEAD_DIM
NA_WIDTH = NA_HEADS * NA_HEAD_DIM
WIN_H = 8
WIN_W = 16
GLA_HEADS = 4
GLA_DK = D_MODEL // 2 // GLA_HEADS
GLA_DV = D_MODEL // GLA_HEADS
GATE_RANK = 16
GATE_NORM = 16.0
GLA_CHUNK = 64
GLA_STRIP = 8
GLA_QK = GLA_HEADS * GLA_DK
GLA_V = GLA_HEADS * GLA_DV
ROPE_BASE = 10000.0
TOP_K = 4
SWIGLU_LIMIT = 7.0
SWIGLU_ALPHA = 1.702
LN_EPS = 1e-5
RMS_EPS = 1e-6
NEG_INF = -1e30

LANES = 128
TOK_TILE = 256
NA_QROWS = 4
NA_KROWS = 12
NA_PAIRS_PER_STEP = 6
VMEM_LIMIT = 48 * 1024 * 1024
EXPERT_VMEM_LIMIT = 56 * 1024 * 1024
EXPERT_BLOCK = 512
SC_CHUNK = 64


def _cparams(sem):
    return pltpu.CompilerParams(dimension_semantics=sem, vmem_limit_bytes=VMEM_LIMIT)


def _dot_bf16x3(a, b, dims):
    a_hi = a.astype(BF16)
    a_lo = (a - a_hi.astype(F32)).astype(BF16)
    b_hi = b.astype(BF16)
    b_lo = (b - b_hi.astype(F32)).astype(BF16)
    dg = functools.partial(lax.dot_general, dimension_numbers=(dims, ((), ())), preferred_element_type=F32)
    return dg(a_hi, b_hi) + dg(a_hi, b_lo) + dg(a_lo, b_hi)


def _row_tile(n):
    return 2 * TOK_TILE if n % (2 * TOK_TILE) == 0 else TOK_TILE


def _mods_kernel(c_ref, w_ref, b_ref, o_ref):
    cv = c_ref[...]
    sc = cv * jax.nn.sigmoid(cv)
    o_ref[0] = jnp.dot(sc, w_ref[0], precision=HIGHEST, preferred_element_type=F32) + b_ref[0]


def _mods(cc, ada_w, ada_b):
    depth, d, n = ada_w.shape
    r = cc.shape[0]
    tn = n // 4
    return pl.pallas_call(
        _mods_kernel,
        grid=(depth, n // tn),
        in_specs=[pl.BlockSpec((r, d), lambda i, j: (0, 0)),
                  pl.BlockSpec((1, d, tn), lambda i, j: (i, 0, j)),
                  pl.BlockSpec((1, 1, tn), lambda i, j: (i, 0, j))],
        out_specs=pl.BlockSpec((1, r, tn), lambda i, j: (i, 0, j)),
        out_shape=jax.ShapeDtypeStruct((depth, r, n), F32),
        compiler_params=_cparams(("arbitrary", "arbitrary")),
        name="mods",
    )(cc, ada_w, ada_b.reshape(depth, 1, n))


def _tab_row(g, tpb):
    return (g // tpb) * 2 + (g % tpb == tpb - 1).astype(jnp.int32)


def _modulate_kernel(tpb, splits, x_ref, ctx_ref, tab_ref, w_ref, z_ref, *out_refs):
    i = pl.program_id(0)
    is_ctx = i % tpb == tpb - 1
    z = jnp.where(is_ctx, ctx_ref[...], x_ref[...])
    t = tab_ref[_tab_row(i, tpb)]
    z_ref[...] = z
    _emit_proj(z * (1.0 + t[0:1]) + t[1:2], w_ref, splits, out_refs)


def _modulate(x, ctx, tab, tpb, plan):
    d = x.shape[1]
    n_lat = tpb - 1
    n = x.shape[0] + ctx.shape[0]
    w_spec, p_specs, p_shape = _proj_specs(n, TOK_TILE, plan)
    return pl.pallas_call(
        functools.partial(_modulate_kernel, tpb, tuple(plan[1])),
        grid=(n // TOK_TILE,),
        in_specs=[pl.BlockSpec((TOK_TILE, d), lambda i: ((i // tpb) * n_lat + jnp.minimum(i % tpb, n_lat - 1), 0)),
                  pl.BlockSpec((TOK_TILE, d), lambda i: (i // tpb, 0)),
                  pl.BlockSpec(tab.shape, lambda i: (0, 0, 0)),
                  w_spec],
        out_specs=[pl.BlockSpec((TOK_TILE, d), lambda i: (i, 0))] + p_specs,
        out_shape=[jax.ShapeDtypeStruct((n, d), F32)] + p_shape,
        compiler_params=_cparams(("arbitrary",)),
        name="modulate",
    )(x, ctx, tab, plan[0])


def _emit_proj(h, w_ref, splits, out_refs):
    hb = h.astype(BF16)
    for (a, b), o_ref in zip(splits, out_refs):
        o_ref[...] = jnp.dot(hb, w_ref[:, a:b], preferred_element_type=F32).astype(o_ref.dtype)


def _proj_specs(n, tm, plan):
    w, splits, dtypes = plan
    w_spec = pl.BlockSpec(w.shape, lambda i: (0, 0), pipeline_mode=pl.Buffered(1))
    out_specs = [pl.BlockSpec((tm, b - a), lambda i: (i, 0)) for a, b in splits]
    out_shape = [jax.ShapeDtypeStruct((n, b - a), dt) for (a, b), dt in zip(splits, dtypes)]
    return w_spec, out_specs, out_shape


def _pool_kernel(n_lat, t_lat, t_ctx, prev_ref, cur_ref, next_ref, w_ref, scale_ref, o_ref, halo_ref):
    j = pl.program_id(1)
    is_ctx = j == n_lat
    has_prev = jnp.logical_and(j > 0, jnp.logical_not(is_ctx))
    has_next = j < n_lat - 1
    cur = cur_ref[0]
    hl = POOL_HALO
    halo_ref[0:hl] = jnp.where(has_prev, prev_ref[0, TOK_TILE - hl:TOK_TILE], 0.0)
    halo_ref[hl:hl + TOK_TILE] = cur
    halo_ref[hl + TOK_TILE:2 * hl + TOK_TILE] = jnp.where(has_next, next_ref[0, 0:hl], 0.0)

    shape = cur.shape
    lane = lax.broadcasted_iota(jnp.int32, shape, 1)
    group = lane // POOL_GROUP_DIM
    half = jnp.ones(shape, jnp.int32)
    for gi, wdw in enumerate(POOL_WINDOWS):
        half = jnp.where(group == gi, wdw // 2, half)
    acc = jnp.zeros(shape, F32)
    for off in range(-hl, hl):
        v = halo_ref[hl + off:hl + off + TOK_TILE]
        inside = (half >= -off) if off < 0 else (half > off)
        acc = acc + jnp.where(inside, v, 0.0)
    pos0 = jnp.where(is_ctx, 0, j * TOK_TILE)
    seq = jnp.where(is_ctx, t_ctx, t_lat)
    t = pos0 + lax.broadcasted_iota(jnp.int32, shape, 0)
    cnt = jnp.minimum(t + half, seq) - jnp.maximum(t - half, 0)
    pooled = acc / cnt.astype(F32) - cur
    y = jnp.dot(pooled.astype(BF16), w_ref[...], preferred_element_type=F32) * scale_ref[...]
    o_ref[0] = y.astype(o_ref.dtype)


def _pool(u, w_blk, scale, n_lat, t_lat, t_ctx):
    b, l, pw = u.shape
    tpb = l // TOK_TILE
    blk = (1, TOK_TILE, pw)
    return pl.pallas_call(
        functools.partial(_pool_kernel, n_lat, t_lat, t_ctx),
        grid=(b, tpb),
        in_specs=[pl.BlockSpec(blk, lambda bi, j: (bi, jnp.maximum(j - 1, 0), 0)),
                  pl.BlockSpec(blk, lambda bi, j: (bi, j, 0)),
                  pl.BlockSpec(blk, lambda bi, j: (bi, jnp.minimum(j + 1, tpb - 1), 0)),
                  pl.BlockSpec((pw, pw), lambda bi, j: (0, 0)),
                  pl.BlockSpec((1, pw), lambda bi, j: (0, 0))],
        out_specs=pl.BlockSpec(blk, lambda bi, j: (bi, j, 0)),
        out_shape=jax.ShapeDtypeStruct((b, l, pw), BF16),
        scratch_shapes=[pltpu.VMEM((TOK_TILE + 2 * POOL_HALO, pw), F32)],
        compiler_params=_cparams(("arbitrary", "arbitrary")),
        name="pool",
    )(u, u, u, w_blk, scale)


def _na_bias_tables(rpb, rows):
    n_i = rows // NA_QROWS
    heads = rpb.shape[0]
    a = np.arange(NA_QROWS)
    kr = np.arange(NA_KROWS)
    cq = np.arange(GRID_W)
    ws = np.clip(cq - WIN_W // 2, 0, GRID_W - WIN_W)
    ok_col = (cq[None, :] >= ws[:, None]) & (cq[None, :] < ws[:, None] + WIN_W)
    dcol = np.clip(cq[None, :] - cq[:, None] + WIN_W - 1, 0, 2 * WIN_W - 2)
    oh_col = (dcol[..., None] == np.arange(2 * WIN_W - 1)).astype(np.float32)
    tabs = []
    for i in (0, 1, n_i - 1):
        start = int(np.clip(NA_QROWS * i - WIN_H // 2, 0, rows - NA_KROWS))
        r = NA_QROWS * i + a
        krow = start + kr
        rs = np.clip(r - WIN_H // 2, 0, rows - WIN_H)
        ok_row = (krow[None, :] >= rs[:, None]) & (krow[None, :] < rs[:, None] + WIN_H)
        drow = np.clip(krow[None, :] - r[:, None] + WIN_H - 1, 0, 2 * WIN_H - 2)
        oh_row = (drow[..., None] == np.arange(2 * WIN_H - 1)).astype(np.float32)
        by_row = jnp.einsum('hrc,akr->hakc', rpb, oh_row, precision=HIGHEST)
        bias = jnp.einsum('hakc,qjc->haqkj', by_row, oh_col, precision=HIGHEST)
        ok = ok_row[:, None, :, None] & ok_col[None, :, None, :]
        tabs.append(jnp.where(ok[None], bias, NEG_INF).reshape(heads, TOK_TILE, NA_KROWS * GRID_W))
    tabs.append(jnp.full_like(tabs[0], NEG_INF))
    return jnp.stack(tabs).astype(F32)


def _na_kernel(q_ref, k0_ref, k1_ref, k2_ref, kc_ref, v0_ref, v1_ref, v2_ref, vc_ref, bias_ref, o_ref):
    k_refs = (k0_ref, k1_ref, k2_ref, kc_ref)
    v_refs = (v0_ref, v1_ref, v2_ref, vc_ref)
    n_band = len(k_refs) - 1
    lane = lax.broadcasted_iota(jnp.int32, (TOK_TILE, LANES), 1)
    first = lane < NA_HEAD_DIM
    for pair in range(NA_PAIRS_PER_STEP):
        cols = slice(pair * LANES, (pair + 1) * LANES)
        q = q_ref[0, :, cols]
        outs = []
        for hh in range(2):
            mine = first if hh == 0 else jnp.logical_not(first)
            qm = jnp.where(mine, q, jnp.zeros_like(q)) * NA_HEAD_DIM ** -0.5
            scores = []
            for j, k_ref in enumerate(k_refs):
                s = lax.dot_general(qm, k_ref[0, :, cols], (((1,), (1,)), ((), ())), preferred_element_type=F32)
                if j < n_band:
                    s = s + bias_ref[0, 2 * pair + hh, :, j * TOK_TILE:(j + 1) * TOK_TILE]
                scores.append(s)
            m = functools.reduce(jnp.maximum, [jnp.max(s, axis=-1, keepdims=True) for s in scores])
            l = 0.0
            o = 0.0
            for s, v_ref in zip(scores, v_refs):
                p = jnp.exp(s - m)
                l = l + jnp.sum(p, axis=-1, keepdims=True)
                o = o + jnp.dot(p.astype(BF16), v_ref[0, :, cols], preferred_element_type=F32)
            outs.append(o / l)
        o_ref[0, :, cols] = jnp.where(first, outs[0], outs[1]).astype(o_ref.dtype)


def _na(qkv, bias, n_lat):
    b, l, _ = qkv.shape
    tpb = l // TOK_TILE
    n_groups = NA_WIDTH // (LANES * NA_PAIRS_PER_STEP)
    blk = (1, TOK_TILE, LANES * NA_PAIRS_PER_STEP)

    def kstart(i):
        return jnp.clip(i - 1, 0, n_lat - NA_KROWS // NA_QROWS)

    def btype(i):
        return jnp.where(i == 0, 0, jnp.where(i == n_lat - 1, 2, jnp.where(i == n_lat, 3, 1)))

    def kv_spec(col0, j):
        return pl.BlockSpec(blk, lambda hp, i, bi: (bi, kstart(i) + j, col0 + hp))

    def ctx_spec(col0):
        return pl.BlockSpec(blk, lambda hp, i, bi: (bi, n_lat, col0 + hp))

    nk = bias.shape[-1]
    return pl.pallas_call(
        _na_kernel,
        grid=(n_groups, tpb, b),
        in_specs=[pl.BlockSpec(blk, lambda hp, i, bi: (bi, i, hp)),
                  kv_spec(n_groups, 0), kv_spec(n_groups, 1), kv_spec(n_groups, 2), ctx_spec(n_groups),
                  kv_spec(2 * n_groups, 0), kv_spec(2 * n_groups, 1), kv_spec(2 * n_groups, 2),
                  ctx_spec(2 * n_groups),
                  pl.BlockSpec((1, 2 * NA_PAIRS_PER_STEP, TOK_TILE, nk), lambda hp, i, bi: (btype(i), hp, 0, 0))],
        out_specs=pl.BlockSpec(blk, lambda hp, i, bi: (bi, i, hp)),
        out_shape=jax.ShapeDtypeStruct((b, l, NA_WIDTH), BF16),
        compiler_params=_cparams(("arbitrary", "arbitrary", "arbitrary")),
        name="na",
    )(qkv, qkv, qkv, qkv, qkv, qkv, qkv, qkv, qkv, bias)


def _log_sigmoid(z):
    return jnp.minimum(z, 0.0) - jnp.log(1.0 + jnp.exp(-jnp.abs(z)))


def _gla_prep_kernel(rev, q_ref, k_ref, g_ref, cos_ref, sin_ref, wg_ref, bg_ref, tri_ref,
                     qe_ref, kd_ref, a_ref, gd_ref, b_scr, qr_scr, kr_scr):
    tg = q_ref.shape[0]
    head_cols = [slice(hd * GLA_DK, (hd + 1) * GLA_DK) for hd in range(GLA_HEADS)]
    gcol = GATE_RANK if rev else 0
    gg = g_ref[:, gcol:gcol + GATE_RANK]
    z = _dot_bf16x3(gg, wg_ref[...], ((1,), (0,))) + bg_ref[...]
    la = _log_sigmoid(z) * (1.0 / GATE_NORM)
    tri = tri_ref[...]
    b = jnp.zeros_like(la)
    rest = la
    for _ in range(3):
        piece = rest.astype(BF16)
        b = b + jnp.dot(tri, piece, preferred_element_type=F32)
        rest = rest - piece.astype(F32)
    cosv = cos_ref[...]
    sinv = sin_ref[...]
    for hd, cols in enumerate(head_cols):
        q = q_ref[:, cols]
        k = k_ref[:, cols]
        bh = b[:, cols]
        b_scr[hd] = bh
        qr = (q * cosv + pltpu.roll(q, GLA_DK // 2, 1) * sinv) * GLA_DK ** -0.5
        kr = k * cosv + pltpu.roll(k, GLA_DK // 2, 1) * sinv
        qe_ref[:, cols] = (qr * jnp.exp(bh)).astype(qe_ref.dtype)
        for c in range(tg // GLA_CHUNK):
            r0 = c * GLA_CHUNK
            last = r0 if rev else r0 + GLA_CHUNK - 1
            tot = bh[last:last + 1]
            kd_ref[r0:r0 + GLA_CHUNK, cols] = (kr[r0:r0 + GLA_CHUNK]
                                               * jnp.exp(tot - bh[r0:r0 + GLA_CHUNK])).astype(kd_ref.dtype)
            gd_ref[hd, 0, c:c + 1] = jnp.exp(tot)
        qr_scr[hd] = qr
        kr_scr[hd] = kr

    n_strip = GLA_CHUNK // GLA_STRIP
    colio = lax.broadcasted_iota(jnp.int32, (GLA_STRIP, GLA_CHUNK), 1)
    rowio = lax.broadcasted_iota(jnp.int32, (GLA_STRIP, GLA_CHUNK), 0)

    def strip_scores(hd, c0, bch, qch, kch, u):
        lo, hi = u * GLA_STRIP, (u + 1) * GLA_STRIP
        bu, qu = bch[lo:hi], qch[lo:hi]
        krows, ref = (slice(hi, GLA_CHUNK), hi) if rev else (slice(0, lo), lo - 1)
        if krows.stop > krows.start:
            rb = b_scr[hd, pl.ds(c0 + ref, 1), :]
            qt = qu * jnp.exp(jnp.minimum(bu - rb, 0.0))
            kt = kch[krows] * jnp.exp(jnp.minimum(rb - bch[krows], 0.0))
            pad = jnp.zeros((GLA_CHUNK - kt.shape[0], GLA_DK), F32)
            kt = jnp.concatenate([pad, kt] if rev else [kt, pad], axis=0)
            acc = lax.dot_general(qt.astype(BF16), kt.astype(BF16), (((1,), (1,)), ((), ())),
                                  preferred_element_type=F32)
        else:
            acc = jnp.zeros((GLA_STRIP, GLA_CHUNK), F32)
        for s in range(GLA_STRIP):
            ks = kr_scr[hd, pl.ds(c0 + lo + s, 1), :]
            bs = b_scr[hd, pl.ds(c0 + lo + s, 1), :]
            col = jnp.sum(qu * ks * jnp.exp(bu - bs), axis=1, keepdims=True)
            causal = (rowio <= s) if rev else (rowio >= s)
            acc = jnp.where(jnp.logical_and(colio == lo + s, causal), col, acc)
        return acc

    def chunk_scores(c, carry):
        c0 = pl.multiple_of(c * GLA_CHUNK, GLA_CHUNK)
        for hd in range(GLA_HEADS):
            bch = b_scr[hd, pl.ds(c0, GLA_CHUNK), :]
            qch = qr_scr[hd, pl.ds(c0, GLA_CHUNK), :]
            kch = kr_scr[hd, pl.ds(c0, GLA_CHUNK), :]
            for u in range(0, n_strip, 2):
                pair = jnp.concatenate([strip_scores(hd, c0, bch, qch, kch, u),
                                        strip_scores(hd, c0, bch, qch, kch, u + 1)], axis=0)
                a_ref[hd, pl.ds(c0 + u * GLA_STRIP, 2 * GLA_STRIP), :] = pair.astype(a_ref.dtype)
        return carry

    lax.fori_loop(0, tg // GLA_CHUNK, chunk_scores, 0)


def _gla_prep(rev, qk, g, cos2, sin2, w_gate, b_gate, tri, tpb):
    n = qk.shape[0]
    tg = TOK_TILE
    nt = n // tg
    return pl.pallas_call(
        functools.partial(_gla_prep_kernel, rev),
        grid=(nt,),
        in_specs=[pl.BlockSpec((tg, GLA_QK), lambda t: (t, 0)),
                  pl.BlockSpec((tg, GLA_QK), lambda t: (t, 1)),
                  pl.BlockSpec((tg, 2 * GATE_RANK), lambda t: (t, 0)),
                  pl.BlockSpec((tg, GLA_DK), lambda t: (t % tpb, 0)),
                  pl.BlockSpec((tg, GLA_DK), lambda t: (t % tpb, 0)),
                  pl.BlockSpec((GATE_RANK, GLA_QK), lambda t: (0, 0)),
                  pl.BlockSpec((1, GLA_QK), lambda t: (0, 0)),
                  pl.BlockSpec((tg, tg), lambda t: (0, 0))],
        out_specs=[pl.BlockSpec((tg, GLA_QK), lambda t: (t, 0)),
                   pl.BlockSpec((tg, GLA_QK), lambda t: (t, 0)),
                   pl.BlockSpec((GLA_HEADS, tg, GLA_CHUNK), lambda t: (0, t, 0)),
                   pl.BlockSpec((GLA_HEADS, 1, tg // GLA_CHUNK, GLA_DK), lambda t: (0, t, 0, 0))],
        out_shape=[jax.ShapeDtypeStruct((n, GLA_QK), BF16),
                   jax.ShapeDtypeStruct((n, GLA_QK), BF16),
                   jax.ShapeDtypeStruct((GLA_HEADS, n, GLA_CHUNK), BF16),
                   jax.ShapeDtypeStruct((GLA_HEADS, nt, tg // GLA_CHUNK, GLA_DK), F32)],
        scratch_shapes=[pltpu.VMEM((GLA_HEADS, tg, GLA_DK), F32)] * 3,
        compiler_params=_cparams(("arbitrary",)),
        name="gla_prep_bwd" if rev else "gla_prep_fwd",
    )(qk, qk, g, cos2, sin2, w_gate, b_gate, tri)


def _gla_scan_kernel(*refs):
    dirs = (refs[0:5], refs[5:10])
    o_refs = refs[10:12]
    st_ref = refs[12]

    @pl.when(pl.program_id(1) == 0)
    def _():
        st_ref[...] = jnp.zeros_like(st_ref)

    n_chunks = refs[0].shape[0] // GLA_CHUNK
    for hd in range(GLA_HEADS):
        kcols = slice(hd * GLA_DK, (hd + 1) * GLA_DK)
        vcols = slice(hd * GLA_DV, (hd + 1) * GLA_DV)
        states = [st_ref[0, hd], st_ref[1, hd]]
        for cc in range(n_chunks):
            for rev in (0, 1):
                qe_ref, kd_ref, a_ref, gd_ref, v_ref = dirs[rev]
                c = n_chunks - 1 - cc if rev else cc
                rows = slice(c * GLA_CHUNK, (c + 1) * GLA_CHUNK)
                st = states[rev]
                v_c = v_ref[rows, vcols]
                o = lax.dot_general(qe_ref[rows, kcols], st.astype(BF16), (((1,), (1,)), ((), ())),
                                    preferred_element_type=F32)
                o = o + jnp.dot(a_ref[hd, rows, :], v_c, preferred_element_type=F32)
                o_refs[rev][rows, vcols] = o
                upd = lax.dot_general(v_c, kd_ref[rows, kcols], (((0,), (0,)), ((), ())),
                                      preferred_element_type=F32)
                states[rev] = st * gd_ref[hd, 0, c:c + 1, :] + upd
        st_ref[0, hd] = states[0]
        st_ref[1, hd] = states[1]


def _gla_scan(prep_f, prep_b, v, batch, tpb):
    n = v.shape[0]
    tg = TOK_TILE
    n_lat = tpb - 1

    def specs(rev):
        def tile(bi, s):
            lat = n_lat - s if rev else s - 1
            return bi * tpb + jnp.where(s == 0, n_lat, lat)

        ins = [pl.BlockSpec((tg, GLA_QK), lambda bi, s: (tile(bi, s), 0)),
               pl.BlockSpec((tg, GLA_QK), lambda bi, s: (tile(bi, s), 0)),
               pl.BlockSpec((GLA_HEADS, tg, GLA_CHUNK), lambda bi, s: (0, tile(bi, s), 0)),
               pl.BlockSpec((GLA_HEADS, 1, tg // GLA_CHUNK, GLA_DK), lambda bi, s: (0, tile(bi, s), 0, 0)),
               pl.BlockSpec((tg, GLA_V), lambda bi, s: (tile(bi, s), 0))]
        return ins, pl.BlockSpec((tg, GLA_V), lambda bi, s: (tile(bi, s), 0))

    in_f, out_f = specs(False)
    in_b, out_b = specs(True)
    return pl.pallas_call(
        _gla_scan_kernel,
        grid=(batch, tpb),
        in_specs=in_f + in_b,
        out_specs=[out_f, out_b],
        out_shape=[jax.ShapeDtypeStruct((n, GLA_V), F32)] * 2,
        scratch_shapes=[pltpu.VMEM((2, GLA_HEADS, GLA_DV, GLA_DK), F32)],
        compiler_params=_cparams(("arbitrary", "arbitrary")),
        name="gla_scan",
    )(*prep_f, v, *prep_b, v)


def _pack_bf16(x):
    half = x.shape[1] // 2
    lo = lax.bitcast_convert_type(x[:, :half].astype(BF16).astype(F32), jnp.uint32)
    hi = lax.bitcast_convert_type(x[:, half:].astype(BF16).astype(F32), jnp.uint32)
    return (lo >> 16) | (hi & jnp.uint32(0xFFFF0000))


def _unpack_bf16(p):
    lo = lax.bitcast_convert_type(p << 16, F32)
    hi = lax.bitcast_convert_type(p & jnp.uint32(0xFFFF0000), F32)
    return lo, hi


def _residual_ln(x, a, t, ln, alpha):
    y = alpha * x + t[0:1] * a
    mu = jnp.mean(y, axis=-1, keepdims=True)
    yc = y - mu
    var = jnp.mean(yc * yc, axis=-1, keepdims=True)
    xn = yc * lax.rsqrt(var + LN_EPS) * ln[0:1] + ln[1:2]
    return xn, xn * (1.0 + t[1:2]) + t[2:3]


def _top4_softmax(lt):
    e = lt.shape[0]
    io = lax.broadcasted_iota(jnp.int32, lt.shape, 0)
    work = lt
    idxs, vals = [], []
    for _ in range(TOP_K):
        m = jnp.max(work, axis=0, keepdims=True)
        ik = jnp.min(jnp.where(work == m, io, e), axis=0, keepdims=True)
        idxs.append(ik)
        vals.append(m)
        work = jnp.where(io == ik, -jnp.inf, work)
    ex = [jnp.exp(v - vals[0]) for v in vals]
    den = ex[0] + ex[1] + ex[2] + ex[3]
    return jnp.concatenate(idxs, axis=0), jnp.concatenate([x / den for x in ex], axis=0)


def _gla_gated_norm(o, r, gn):
    gate = r * jax.nn.sigmoid(r)
    heads = []
    for hd in range(GLA_HEADS):
        cols = slice(hd * GLA_DV, (hd + 1) * GLA_DV)
        oh = o[:, cols]
        ms = jnp.mean(oh * oh, axis=-1, keepdims=True)
        heads.append((oh * lax.rsqrt(ms + RMS_EPS) * gn * gate[:, cols]).astype(BF16))
    return jnp.concatenate(heads, axis=1)


def _post_kernel(gla, n_act, tpb, alpha, *refs):
    if gla:
        of_ref, ob_ref, r_ref, gn_ref, w_ref = refs[:5]
        rest = refs[5:]
        act = _gla_gated_norm(of_ref[...] + ob_ref[...], r_ref[...], gn_ref[...])
        a = jnp.dot(act, w_ref[...], preferred_element_type=F32)
    else:
        acts = refs[:n_act]
        ws = refs[n_act:2 * n_act]
        rest = refs[2 * n_act:]
        a = jnp.dot(acts[0][...], ws[0][...], preferred_element_type=F32)
        for k in range(1, n_act):
            a = a + jnp.dot(acts[k][...], ws[k][...], preferred_element_type=F32)
    x_ref, tab_ref, ln_ref, rw_ref, rb_ref, xo_ref, h_ref, idx_ref, gate_ref = rest
    tm = x_ref.shape[0]
    ln = ln_ref[...]
    for s in range(tm // TOK_TILE):
        rows = slice(s * TOK_TILE, (s + 1) * TOK_TILE)
        t = tab_ref[_tab_row(pl.program_id(0) * (tm // TOK_TILE) + s, tpb)]
        xn, h = _residual_ln(x_ref[rows], a[rows], t, ln, alpha)
        xo_ref[rows] = xn
        h_ref[rows] = _pack_bf16(h)
        lt = _dot_bf16x3(rw_ref[...], h, ((1,), (1,))) + rb_ref[...]
        idx, gates = _top4_softmax(lt)
        idx_ref[:, rows] = idx
        gate_ref[:, rows] = gates


def _post(acts, ws, x, tab, ln, rw_t, rb, tpb, alpha, gla=False):
    n, d = x.shape
    tm = _row_tile(n)
    e = rw_t.shape[0]
    row = lambda i: (i, 0)
    fixed = lambda i: (0, 0)
    return pl.pallas_call(
        functools.partial(_post_kernel, gla, len(acts), tpb, alpha),
        grid=(n // tm,),
        in_specs=([pl.BlockSpec((tm, a.shape[1]), row) for a in acts]
                  + [pl.BlockSpec(w.shape, fixed) for w in ws]
                  + [pl.BlockSpec((tm, d), row),
                     pl.BlockSpec(tab.shape, lambda i: (0, 0, 0)),
                     pl.BlockSpec(ln.shape, fixed),
                     pl.BlockSpec((e, d), fixed),
                     pl.BlockSpec((e, 1), fixed)]),
        out_specs=[pl.BlockSpec((tm, d), row), pl.BlockSpec((tm, d // 2), row),
                   pl.BlockSpec((TOP_K, tm), lambda i: (0, i)), pl.BlockSpec((TOP_K, tm), lambda i: (0, i))],
        out_shape=[jax.ShapeDtypeStruct((n, d), F32), jax.ShapeDtypeStruct((n, d // 2), jnp.uint32),
                   jax.ShapeDtypeStruct((TOP_K, n), jnp.int32), jax.ShapeDtypeStruct((TOP_K, n), F32)],
        compiler_params=_cparams(("arbitrary",)),
        name="post",
    )(*acts, *ws, x, tab, ln, rw_t, rb)


def _rank_kernel(idx_ref, tri_ref, rank_ref, cnt_ref, carry_ref):
    @pl.when(pl.program_id(0) == 0)
    def _():
        carry_ref[...] = jnp.zeros_like(carry_ref)

    idx = idx_ref[...]
    e = carry_ref.shape[0]
    tr = idx.shape[1]
    io = lax.broadcasted_iota(jnp.int32, (e, tr), 0)
    chosen = jnp.zeros((e, tr), F32)
    for k in range(TOP_K):
        chosen = chosen + (idx[k:k + 1] == io).astype(F32)
    cum = jnp.dot(chosen.astype(BF16), tri_ref[...], preferred_element_type=F32)
    base = carry_ref[:, 0:1]
    excl = base + cum - chosen
    ranks = [jnp.sum(jnp.where(idx[k:k + 1] == io, excl, 0.0), axis=0, keepdims=True) for k in range(TOP_K)]
    rank_ref[...] = jnp.concatenate(ranks, axis=0).astype(jnp.int32)
    carry_ref[...] = carry_ref[...] + jnp.sum(chosen, axis=1, keepdims=True)
    cnt_ref[...] = carry_ref[...]


def _rank(idx_t, n_experts):
    n = idx_t.shape[1]
    tr = _row_tile(n)
    tri = (np.arange(tr)[:, None] <= np.arange(tr)[None, :]).astype(np.float32)
    return pl.pallas_call(
        _rank_kernel,
        grid=(n // tr,),
        in_specs=[pl.BlockSpec((TOP_K, tr), lambda i: (0, i)),
                  pl.BlockSpec((tr, tr), lambda i: (0, 0))],
        out_specs=[pl.BlockSpec((TOP_K, tr), lambda i: (0, i)),
                   pl.BlockSpec((n_experts, LANES), lambda i: (0, 0))],
        out_shape=[jax.ShapeDtypeStruct((TOP_K, n), jnp.int32),
                   jax.ShapeDtypeStruct((n_experts, LANES), F32)],
        scratch_shapes=[pltpu.VMEM((n_experts, LANES), F32)],
        compiler_params=_cparams(("arbitrary",)),
        name="rank",
    )(idx_t, jnp.asarray(tri, BF16))


def _expert_kernel(be_ref, nb_ref, x_ref, w1_ref, b1_ref, w2_ref, b2_ref, o_ref, w1b_ref, w2b_ref):
    i = pl.program_id(0)
    used = i < nb_ref[0]
    new_expert = jnp.logical_or(i == 0, be_ref[i] != be_ref[jnp.maximum(i - 1, 0)])

    @pl.when(jnp.logical_and(used, new_expert))
    def _():
        w1b_ref[...] = w1_ref[0].astype(BF16)
        w2b_ref[...] = w2_ref[0].astype(BF16)

    @pl.when(used)
    def _():
        x_lo, x_hi = _unpack_bf16(x_ref[...])
        kh = x_lo.shape[1]
        hid = (jnp.dot(x_lo.astype(BF16), w1b_ref[:kh], preferred_element_type=F32)
               + jnp.dot(x_hi.astype(BF16), w1b_ref[kh:], preferred_element_type=F32) + b1_ref[0])
        half = hid.shape[1] // 2
        glu = jnp.minimum(hid[:, :half], SWIGLU_LIMIT)
        lin = jnp.clip(hid[:, half:], -SWIGLU_LIMIT, SWIGLU_LIMIT)
        act = glu * jax.nn.sigmoid(SWIGLU_ALPHA * glu) * (lin + 1.0)
        y = jnp.dot(act.astype(BF16), w2b_ref[...], preferred_element_type=F32) + b2_ref[0]
        o_ref[...] = _pack_bf16(y)

    @pl.when(i >= nb_ref[0])
    def _():
        o_ref[...] = jnp.zeros_like(o_ref)


def _experts(block_expert, n_used, x_pad, layer, w1, b1, w2, b2):
    n_pad, dp = x_pad.shape
    depth, e, d, dh2 = w1.shape
    n_blocks = n_pad // EXPERT_BLOCK
    grid_spec = pltpu.PrefetchScalarGridSpec(
        num_scalar_prefetch=2,
        grid=(n_blocks,),
        in_specs=[pl.BlockSpec((EXPERT_BLOCK, dp), lambda i, be, nb: (i, 0)),
                  pl.BlockSpec((None, 1, d, dh2), lambda i, be, nb: (layer, be[i], 0, 0)),
                  pl.BlockSpec((None, 1, 1, dh2), lambda i, be, nb: (layer, be[i], 0, 0)),
                  pl.BlockSpec((None, 1, dh2 // 2, d), lambda i, be, nb: (layer, be[i], 0, 0)),
                  pl.BlockSpec((None, 1, 1, d), lambda i, be, nb: (layer, be[i], 0, 0))],
        out_specs=pl.BlockSpec((EXPERT_BLOCK, dp), lambda i, be, nb: (i, 0)),
        scratch_shapes=[pltpu.VMEM((d, dh2), BF16), pltpu.VMEM((dh2 // 2, d), BF16)],
    )
    return pl.pallas_call(
        _expert_kernel,
        grid_spec=grid_spec,
        out_shape=jax.ShapeDtypeStruct((n_pad, dp), jnp.uint32),
        compiler_params=pltpu.CompilerParams(dimension_semantics=("arbitrary",),
                                             vmem_limit_bytes=EXPERT_VMEM_LIMIT),
        name="experts",
    )(block_expert, n_used, x_pad, w1, b1.reshape(depth, e, 1, dh2), w2, b2.reshape(depth, e, 1, d))


def _stream_tile(i, tpb, latent_only):
    return (i // (tpb - 1)) * tpb + i % (tpb - 1) if latent_only else i


def _combine_kernel(tpb, alpha, splits, y_ref, gate_ref, x_ref, tab_ref, ln_ref, *refs):
    latent_only = splits is None
    g = gate_ref[...]
    y_lo, y_hi = None, None
    for k in range(TOP_K):
        lo, hi = _unpack_bf16(y_ref[k])
        gk = g[:, k:k + 1]
        y_lo = lo * gk if y_lo is None else y_lo + lo * gk
        y_hi = hi * gk if y_hi is None else y_hi + hi * gk
    y = jnp.concatenate([y_lo, y_hi], axis=1)
    ln = ln_ref[...]
    n_sub = x_ref.shape[0] // TOK_TILE
    xns, hs = [], []
    for s in range(n_sub):
        rows = slice(s * TOK_TILE, (s + 1) * TOK_TILE)
        t = tab_ref[_tab_row(_stream_tile(pl.program_id(0) * n_sub + s, tpb, latent_only), tpb)]
        xn, h = _residual_ln(x_ref[rows], y[rows], t, ln, alpha)
        xns.append(xn)
        hs.append(h)
    xn = jnp.concatenate(xns, axis=0)
    if latent_only:
        refs[0][...] = xn
    else:
        w_ref, xo_ref = refs[:2]
        xo_ref[...] = xn
        _emit_proj(jnp.concatenate(hs, axis=0), w_ref, splits, refs[2:])


def _combine(y_g, gates, x, tab, ln, tpb, alpha, plan):
    n, d = x.shape
    latent_only = plan is None
    tm = TOK_TILE if latent_only else _row_tile(n)
    tiles = n // tm
    steps = tiles // tpb * (tpb - 1) if latent_only else tiles
    src = lambda i: (_stream_tile(i, tpb, latent_only), 0)
    in_specs = [pl.BlockSpec((TOP_K, tm, d // 2), lambda i: (0, _stream_tile(i, tpb, latent_only), 0)),
                pl.BlockSpec((tm, TOP_K), src),
                pl.BlockSpec((tm, d), src),
                pl.BlockSpec(tab.shape, lambda i: (0, 0, 0)),
                pl.BlockSpec(ln.shape, lambda i: (0, 0))]
    out_specs = [pl.BlockSpec((tm, d), lambda i: (i, 0))]
    out_shape = [jax.ShapeDtypeStruct((steps * tm, d), F32)]
    operands = [y_g, gates, x, tab, ln]
    if not latent_only:
        w_spec, p_specs, p_shape = _proj_specs(n, tm, plan)
        in_specs.append(w_spec)
        out_specs += p_specs
        out_shape += p_shape
        operands.append(plan[0])
    return pl.pallas_call(
        functools.partial(_combine_kernel, tpb, alpha, None if latent_only else tuple(plan[1])),
        grid=(steps,),
        in_specs=in_specs,
        out_specs=out_specs,
        out_shape=out_shape,
        compiler_params=_cparams(("arbitrary",)),
        name="combine",
    )(*operands)


def _sc_mesh():
    return plsc.VectorSubcoreMesh(core_axis_name="c", subcore_axis_name="s")


def _sc_split(rows, mesh):
    workers = mesh.num_cores * mesh.num_subcores
    per = rows // workers
    assert per * workers == rows
    chunk = SC_CHUNK if per % SC_CHUNK == 0 else 8
    assert per % chunk == 0
    return per, chunk


def _sc_scatter_rows(x, idx, n_out):
    r, c = x.shape
    mesh = _sc_mesh()
    per, chunk = _sc_split(r, mesh)

    n_chunks = per // chunk

    @functools.partial(pl.kernel, out_type=jax.ShapeDtypeStruct((n_out, c), x.dtype), mesh=mesh,
                       scratch_types=[pltpu.VMEM((chunk, c), x.dtype), pltpu.SemaphoreType.DMA] * 2
                       + [pltpu.VMEM((chunk,), jnp.int32)] * TOP_K + [pltpu.SemaphoreType.DMA])
    def scatter(x_hbm, i_hbm, o_hbm, rows_a, sem_a, rows_b, sem_b, *rest):
        idx_vs, sem_s = rest[:TOP_K], rest[TOP_K]
        base = (lax.axis_index("s") * mesh.num_cores + lax.axis_index("c")) * per
        slots = ((rows_a, sem_a), (rows_b, sem_b))

        def load(j, slot):
            rows_v, sem = slot
            off = pl.multiple_of(base + j * chunk, chunk)
            return pltpu.make_async_copy(x_hbm.at[pl.ds(off, chunk)], rows_v, sem)

        def scatter_chunk(j, slot, prefetch):
            rows_v, _ = slot
            off = pl.multiple_of(base + j * chunk, chunk)
            load(j, slot).wait()
            for k in range(TOP_K):
                pltpu.sync_copy(i_hbm.at[pl.ds(k * r + off, chunk)], idx_vs[k])
            for k in range(TOP_K):
                pltpu.async_copy(rows_v, o_hbm.at[idx_vs[k]], sem_s)
            prefetch()
            for k in range(TOP_K):
                pltpu.make_async_copy(rows_v, o_hbm.at[idx_vs[k]], sem_s).wait()

        load(0, slots[0]).start()

        @pl.loop(0, n_chunks // 2)
        def _(p):
            j = 2 * p
            scatter_chunk(j, slots[0], lambda: load(j + 1, slots[1]).start())

            def next_even():
                @pl.when(j + 2 < n_chunks)
                def _():
                    load(j + 2, slots[0]).start()

            scatter_chunk(j + 1, slots[1], next_even)

        if n_chunks % 2:
            scatter_chunk(n_chunks - 1, slots[0], lambda: None)

    return scatter(x, idx)


def _sc_gather_rows(table, idx):
    m = idx.shape[0]
    c = table.shape[1]
    mesh = _sc_mesh()
    per, chunk = _sc_split(m, mesh)

    n_chunks = per // chunk
    slot_types = [pltpu.VMEM((chunk,), jnp.int32), pltpu.VMEM((chunk, c), table.dtype), pltpu.SemaphoreType.DMA]

    @functools.partial(pl.kernel, out_type=jax.ShapeDtypeStruct((m, c), table.dtype), mesh=mesh,
                       scratch_types=slot_types * 2)
    def gather(t_hbm, i_hbm, o_hbm, idx_a, rows_a, sem_a, idx_b, rows_b, sem_b):
        base = (lax.axis_index("s") * mesh.num_cores + lax.axis_index("c")) * per
        slots = ((idx_a, rows_a, sem_a), (idx_b, rows_b, sem_b))

        def start(j, slot):
            idx_v, rows_v, sem = slot
            off = pl.multiple_of(base + j * chunk, chunk)
            pltpu.sync_copy(i_hbm.at[pl.ds(off, chunk)], idx_v)
            pltpu.async_copy(t_hbm.at[idx_v], rows_v, sem)

        def finish(j, slot):
            idx_v, rows_v, sem = slot
            off = pl.multiple_of(base + j * chunk, chunk)
            pltpu.make_async_copy(t_hbm.at[idx_v], rows_v, sem).wait()
            pltpu.sync_copy(rows_v, o_hbm.at[pl.ds(off, chunk)])

        start(0, slots[0])

        @pl.loop(0, n_chunks // 2)
        def _(p):
            j = 2 * p
            start(j + 1, slots[1])
            finish(j, slots[0])

            @pl.when(j + 2 < n_chunks)
            def _():
                start(j + 2, slots[0])

            finish(j + 1, slots[1])

        if n_chunks % 2:
            finish(n_chunks - 1, slots[0])

    return gather(table, idx)


def _moe(h, idx_t, layer, w1, b1, w2, b2):
    n, dp = h.shape
    e = w1.shape[1]
    m = n * TOP_K
    rank_t, cnt = _rank(idx_t, e)
    sizes = cnt[:, 0].astype(jnp.int32)
    padded = (sizes + EXPERT_BLOCK - 1) // EXPERT_BLOCK * EXPERT_BLOCK
    pad_ends = jnp.cumsum(padded)
    pad_starts = pad_ends - padded
    ids = jnp.arange(e, dtype=jnp.int32)[:, None, None]
    dest_t = jnp.sum(jnp.where(idx_t[None] == ids, pad_starts[:, None, None], 0), axis=0) + rank_t
    dest = dest_t.reshape(-1)
    n_blocks = (m + e * (EXPERT_BLOCK - 1)) // EXPERT_BLOCK + 1
    n_pad = n_blocks * EXPERT_BLOCK
    block_start = jnp.arange(n_blocks, dtype=jnp.int32) * EXPERT_BLOCK
    block_expert = jnp.minimum(jnp.sum((pad_ends[None, :] <= block_start[:, None]).astype(jnp.int32), axis=1), e - 1)
    n_used = (pad_ends[-1:] // EXPERT_BLOCK).astype(jnp.int32)
    x_pad = _sc_scatter_rows(h, dest, n_pad)
    y_pad = _experts(block_expert, n_used, x_pad, layer, w1, b1, w2, b2)
    return _sc_gather_rows(y_pad, dest).reshape(TOP_K, n, dp)


def _rope_tables(t_lat, n_ctx):
    t = jnp.arange(t_lat)
    row = (t // GRID_W).astype(F32)
    col = (t % GRID_W).astype(F32)
    nf = GLA_DK // 4
    freqs = ROPE_BASE ** (-jnp.arange(nf, dtype=F32) / nf)
    ang = jnp.concatenate([row[:, None] * freqs, col[:, None] * freqs], axis=-1)
    cos, sin = jnp.cos(ang), jnp.sin(ang)
    cos2 = jnp.concatenate([cos, cos], axis=-1)
    sin2 = jnp.concatenate([-sin, sin], axis=-1)
    return (jnp.concatenate([cos2, jnp.ones((n_ctx, GLA_DK), F32)], axis=0),
            jnp.concatenate([sin2, jnp.zeros((n_ctx, GLA_DK), F32)], axis=0))


def _chunk_tri(tg, rev):
    t = np.arange(tg)
    same = (t[:, None] // GLA_CHUNK) == (t[None, :] // GLA_CHUNK)
    side = (t[None, :] >= t[:, None]) if rev else (t[None, :] <= t[:, None])
    return jnp.asarray((same & side).astype(np.float32), BF16)


def _table(mods, rows, batch):
    lat = jnp.stack([mods[:batch, r] for r in rows], axis=1)
    ctx = jnp.broadcast_to(jnp.stack([mods[batch, r] for r in rows], axis=0)[None], lat.shape)
    tab = jnp.stack([lat, ctx], axis=1).reshape(2 * batch, len(rows), -1)
    return jnp.pad(tab, ((0, 0), (0, 8 - len(rows)), (0, 0)))


@jax.jit
def _forward(x, c, ctx, c_ctx, ada_w, ada_b, ln_g, ln_b, ab_w_in, ab_pool_w, ab_pool_scale, ab_rpb,
             ab_w_out, gla_w_in, gla_w_gate, gla_b_gate, gla_norm_g, gla_w_out, router_w, router_b,
             exp_w1, exp_b1, exp_w2, exp_b2):
    batch, t_lat, d = x.shape
    n_ctx = ctx.shape[1]
    depth = ada_w.shape[0]
    assert d == D_MODEL and n_ctx == TOK_TILE and t_lat % TOK_TILE == 0
    rows = t_lat // GRID_W
    assert rows % NA_QROWS == 0 and rows >= NA_KROWS + NA_QROWS
    n_lat = t_lat // TOK_TILE
    tpb = n_lat + 1
    l = t_lat + n_ctx
    n = batch * l
    alpha = (2.0 * depth) ** 0.25

    cc = jnp.concatenate([c, c_ctx[None], jnp.zeros((16 - batch - 1, d), F32)], axis=0)
    mods = _mods(cc, ada_w, ada_b).reshape(depth, 16, N_MOD, d)

    def in_proj_plan(i):
        if i % 2 == 0:
            return (ab_w_in[i // 2].astype(BF16),
                    [(0, POOL_WIDTH), (POOL_WIDTH, POOL_WIDTH + 3 * NA_WIDTH)], [F32, BF16])
        edges = (0, 2 * GLA_QK, 2 * GLA_QK + GLA_V, 2 * GLA_QK + 2 * GLA_V, 2 * GLA_QK + 2 * GLA_V + 2 * GATE_RANK)
        return gla_w_in[i // 2].astype(BF16), list(zip(edges[:-1], edges[1:])), [F32, BF16, F32, F32]

    z, *projected = _modulate(x.reshape(batch * t_lat, d), ctx.reshape(batch * n_ctx, d),
                              _table(mods[0], (1, 0), batch), tpb, in_proj_plan(0))
    cos2, sin2 = _rope_tables(t_lat, n_ctx)

    for i in range(depth):
        j = i // 2
        last = i == depth - 1
        tab1 = _table(mods[i], (2, 4, 3), batch)
        ln1 = jnp.stack([ln_g[i, 0], ln_b[i, 0]])
        ln2 = jnp.stack([ln_g[i, 1], ln_b[i, 1]])
        rw_t = router_w[i].T
        rb = router_b[i][:, None]
        if i % 2 == 0:
            u, qkv = projected
            w_blk = jax.scipy.linalg.block_diag(*[ab_pool_w[j, g] for g in range(len(POOL_WINDOWS))])
            pooled = _pool(u.reshape(batch, l, POOL_WIDTH), w_blk.astype(BF16), ab_pool_scale[j][None, :],
                           n_lat, t_lat, n_ctx)
            bias = _na_bias_tables(ab_rpb[j], rows)
            attn = _na(qkv.reshape(batch, l, 3 * NA_WIDTH), bias, n_lat)
            w_out = ab_w_out[j].astype(BF16)
            acts = [pooled.reshape(n, POOL_WIDTH), attn.reshape(n, NA_WIDTH)]
            ws = [w_out[:POOL_WIDTH], w_out[POOL_WIDTH:]]
        else:
            qk, v, r, g = projected
            preps = [_gla_prep(rev, qk, g, cos2, sin2, gla_w_gate[j, int(rev)], gla_b_gate[j, int(rev)][None, :],
                               _chunk_tri(TOK_TILE, rev), tpb) for rev in (False, True)]
            o_f, o_b = _gla_scan(preps[0], preps[1], v, batch, tpb)
            acts = [o_f, o_b, r]
            ws = [gla_norm_g[j][None, :], gla_w_out[j].astype(BF16)]
        z, h, idx_t, gates_t = _post(acts, ws, z, tab1, ln1, rw_t, rb, tpb, alpha, gla=i % 2 == 1)
        y_g = _moe(h, idx_t, i, exp_w1, exp_b1, exp_w2, exp_b2)
        nxt = mods[i + 1] if not last else mods[i]
        tab2 = _table(jnp.concatenate([mods[i][:, 5:6], nxt[:, 1:2], nxt[:, 0:1]], axis=1), (0, 1, 2), batch)
        if last:
            (out,) = _combine(y_g, gates_t.T, z, tab2, ln2, tpb, alpha, None)
            return out.reshape(batch, t_lat, d)
        z, *projected = _combine(y_g, gates_t.T, z, tab2, ln2, tpb, alpha, in_proj_plan(i + 1))


def kernel(x, c, ctx, c_ctx, ada_w, ada_b, ln_g, ln_b, ab_w_in, ab_pool_w, ab_pool_scale, ab_rpb, ab_w_out,
           gla_w_in, gla_w_gate, gla_b_gate, gla_norm_g, gla_w_out, router_w, router_b, exp_w1, exp_b1, exp_w2,
           exp_b2):
    return _forward(x, c, ctx, c_ctx, ada_w, ada_b, ln_g, ln_b, ab_w_in, ab_pool_w, ab_pool_scale, ab_rpb,
                    ab_w_out, gla_w_in, gla_w_gate, gla_b_gate, gla_norm_g, gla_w_out, router_w, router_b,
                    exp_w1, exp_b1, exp_w2, exp_b2)
```

```python
import functools
import math

import numpy as np
import jax
import jax.numpy as jnp
from jax import lax
from jax.experimental import pallas as pl
from jax.experimental.pallas import tpu as pltpu
from jax.experimental.pallas import tpu_sc as plsc

F32 = jnp.float32
BF16 = jnp.bfloat16
HIGHEST = lax.Precision.HIGHEST

D_MODEL = 1024
GRID_W = 64
N_MOD = 6
POOL_WINDOWS = (2, 4, 8, 16)
POOL_WIDTH = D_MODEL // 4
POOL_GROUP_DIM = POOL_WIDTH // len(POOL_WINDOWS)
POOL_HALO = max(POOL_WINDOWS) // 2
NA_HEAD_DIM = 64
NA_HEADS = (D_MODEL - POOL_WIDTH) // NA_HEAD_DIM
NA_WIDTH = NA_HEADS * NA_HEAD_DIM
WIN_H = 8
WIN_W = 16
GLA_HEADS = 4
GLA_DK = D_MODEL // 2 // GLA_HEADS
GLA_DV = D_MODEL // GLA_HEADS
GATE_RANK = 16
GATE_NORM = 16.0
GLA_CHUNK = 64
GLA_STRIP = 8
GLA_QK = GLA_HEADS * GLA_DK
GLA_V = GLA_HEADS * GLA_DV
ROPE_BASE = 10000.0
TOP_K = 4
SWIGLU_LIMIT = 7.0
SWIGLU_ALPHA = 1.702
LN_EPS = 1e-5
RMS_EPS = 1e-6
NEG_INF = -1e30

LANES = 128
TOK_TILE = 256
NA_QROWS = 4
NA_KROWS = 12
NA_PAIRS_PER_STEP = 6
VMEM_LIMIT = 48 * 1024 * 1024
EXPERT_VMEM_LIMIT = 56 * 1024 * 1024
EXPERT_BLOCK = 512
SC_CHUNK = 64


def _cparams(sem):
    return pltpu.CompilerParams(dimension_semantics=sem, vmem_limit_bytes=VMEM_LIMIT)


def _dot_bf16x3(a, b, dims):
    a_hi = a.astype(BF16)
    a_lo = (a - a_hi.astype(F32)).astype(BF16)
    b_hi = b.astype(BF16)
    b_lo = (b - b_hi.astype(F32)).astype(BF16)
    dg = functools.partial(lax.dot_general, dimension_numbers=(dims, ((), ())), preferred_element_type=F32)
    return dg(a_hi, b_hi) + dg(a_hi, b_lo) + dg(a_lo, b_hi)


def _row_tile(n):
    return 2 * TOK_TILE if n % (2 * TOK_TILE) == 0 else TOK_TILE


def _mods_kernel(c_ref, w_ref, b_ref, o_ref):
    cv = c_ref[...]
    sc = cv * jax.nn.sigmoid(cv)
    o_ref[0] = jnp.dot(sc, w_ref[0], precision=HIGHEST, preferred_element_type=F32) + b_ref[0]


def _mods(cc, ada_w, ada_b):
    depth, d, n = ada_w.shape
    r = cc.shape[0]
    tn = n // 4
    return pl.pallas_call(
        _mods_kernel,
        grid=(depth, n // tn),
        in_specs=[pl.BlockSpec((r, d), lambda i, j: (0, 0)),
                  pl.BlockSpec((1, d, tn), lambda i, j: (i, 0, j)),
                  pl.BlockSpec((1, 1, tn), lambda i, j: (i, 0, j))],
        out_specs=pl.BlockSpec((1, r, tn), lambda i, j: (i, 0, j)),
        out_shape=jax.ShapeDtypeStruct((depth, r, n), F32),
        compiler_params=_cparams(("arbitrary", "arbitrary")),
        name="mods",
    )(cc, ada_w, ada_b.reshape(depth, 1, n))


def _tab_row(g, tpb):
    return (g // tpb) * 2 + (g % tpb == tpb - 1).astype(jnp.int32)


def _modulate_kernel(tpb, splits, x_ref, ctx_ref, tab_ref, w_ref, z_ref, *out_refs):
    i = pl.program_id(0)
    is_ctx = i % tpb == tpb - 1
    z = jnp.where(is_ctx, ctx_ref[...], x_ref[...])
    t = tab_ref[_tab_row(i, tpb)]
    z_ref[...] = z
    _emit_proj(z * (1.0 + t[0:1]) + t[1:2], w_ref, splits, out_refs)


def _modulate(x, ctx, tab, tpb, plan):
    d = x.shape[1]
    n_lat = tpb - 1
    n = x.shape[0] + ctx.shape[0]
    w_spec, p_specs, p_shape = _proj_specs(n, TOK_TILE, plan)
    return pl.pallas_call(
        functools.partial(_modulate_kernel, tpb, tuple(plan[1])),
        grid=(n // TOK_TILE,),
        in_specs=[pl.BlockSpec((TOK_TILE, d), lambda i: ((i // tpb) * n_lat + jnp.minimum(i % tpb, n_lat - 1), 0)),
                  pl.BlockSpec((TOK_TILE, d), lambda i: (i // tpb, 0)),
                  pl.BlockSpec(tab.shape, lambda i: (0, 0, 0)),
                  w_spec],
        out_specs=[pl.BlockSpec((TOK_TILE, d), lambda i: (i, 0))] + p_specs,
        out_shape=[jax.ShapeDtypeStruct((n, d), F32)] + p_shape,
        compiler_params=_cparams(("arbitrary",)),
        name="modulate",
    )(x, ctx, tab, plan[0])


def _emit_proj(h, w_ref, splits, out_refs):
    hb = h.astype(BF16)
    for (a, b), o_ref in zip(splits, out_refs):
        o_ref[...] = jnp.dot(hb, w_ref[:, a:b], preferred_element_type=F32).astype(o_ref.dtype)


def _proj_specs(n, tm, plan):
    w, splits, dtypes = plan
    w_spec = pl.BlockSpec(w.shape, lambda i: (0, 0), pipeline_mode=pl.Buffered(1))
    out_specs = [pl.BlockSpec((tm, b - a), lambda i: (i, 0)) for a, b in splits]
    out_shape = [jax.ShapeDtypeStruct((n, b - a), dt) for (a, b), dt in zip(splits, dtypes)]
    return w_spec, out_specs, out_shape


def _pool_kernel(n_lat, t_lat, t_ctx, prev_ref, cur_ref, next_ref, w_ref, scale_ref, o_ref, halo_ref):
    j = pl.program_id(1)
    is_ctx = j == n_lat
    has_prev = jnp.logical_and(j > 0, jnp.logical_not(is_ctx))
    has_next = j < n_lat - 1
    cur = cur_ref[0]
    hl = POOL_HALO
    halo_ref[0:hl] = jnp.where(has_prev, prev_ref[0, TOK_TILE - hl:TOK_TILE], 0.0)
    halo_ref[hl:hl + TOK_TILE] = cur
    halo_ref[hl + TOK_TILE:2 * hl + TOK_TILE] = jnp.where(has_next, next_ref[0, 0:hl], 0.0)

    shape = cur.shape
    lane = lax.broadcasted_iota(jnp.int32, shape, 1)
    group = lane // POOL_GROUP_DIM
    half = jnp.ones(shape, jnp.int32)
    for gi, wdw in enumerate(POOL_WINDOWS):
        half = jnp.where(group == gi, wdw // 2, half)
    acc = jnp.zeros(shape, F32)
    for off in range(-hl, hl):
        v = halo_ref[hl + off:hl + off + TOK_TILE]
        inside = (half >= -off) if off < 0 else (half > off)
        acc = acc + jnp.where(inside, v, 0.0)
    pos0 = jnp.where(is_ctx, 0, j * TOK_TILE)
    seq = jnp.where(is_ctx, t_ctx, t_lat)
    t = pos0 + lax.broadcasted_iota(jnp.int32, shape, 0)
    cnt = jnp.minimum(t + half, seq) - jnp.maximum(t - half, 0)
    pooled = acc / cnt.astype(F32) - cur
    y = jnp.dot(pooled.astype(BF16), w_ref[...], preferred_element_type=F32) * scale_ref[...]
    o_ref[0] = y.astype(o_ref.dtype)


def _pool(u, w_blk, scale, n_lat, t_lat, t_ctx):
    b, l, pw = u.shape
    tpb = l // TOK_TILE
    blk = (1, TOK_TILE, pw)
    return pl.pallas_call(
        functools.partial(_pool_kernel, n_lat, t_lat, t_ctx),
        grid=(b, tpb),
        in_specs=[pl.BlockSpec(blk, lambda bi, j: (bi, jnp.maximum(j - 1, 0), 0)),
                  pl.BlockSpec(blk, lambda bi, j: (bi, j, 0)),
                  pl.BlockSpec(blk, lambda bi, j: (bi, jnp.minimum(j + 1, tpb - 1), 0)),
                  pl.BlockSpec((pw, pw), lambda bi, j: (0, 0)),
                  pl.BlockSpec((1, pw), lambda bi, j: (0, 0))],
        out_specs=pl.BlockSpec(blk, lambda bi, j: (bi, j, 0)),
        out_shape=jax.ShapeDtypeStruct((b, l, pw), BF16),
        scratch_shapes=[pltpu.VMEM((TOK_TILE + 2 * POOL_HALO, pw), F32)],
        compiler_params=_cparams(("arbitrary", "arbitrary")),
        name="pool",
    )(u, u, u, w_blk, scale)


def _na_bias_tables(rpb, rows):
    n_i = rows // NA_QROWS
    heads = rpb.shape[0]
    a = np.arange(NA_QROWS)
    kr = np.arange(NA_KROWS)
    cq = np.arange(GRID_W)
    ws = np.clip(cq - WIN_W // 2, 0, GRID_W - WIN_W)
    ok_col = (cq[None, :] >= ws[:, None]) & (cq[None, :] < ws[:, None] + WIN_W)
    dcol = np.clip(cq[None, :] - cq[:, None] + WIN_W - 1, 0, 2 * WIN_W - 2)
    oh_col = (dcol[..., None] == np.arange(2 * WIN_W - 1)).astype(np.float32)
    tabs = []
    for i in (0, 1, n_i - 1):
        start = int(np.clip(NA_QROWS * i - WIN_H // 2, 0, rows - NA_KROWS))
        r = NA_QROWS * i + a
        krow = start + kr
        rs = np.clip(r - WIN_H // 2, 0, rows - WIN_H)
        ok_row = (krow[None, :] >= rs[:, None]) & (krow[None, :] < rs[:, None] + WIN_H)
        drow = np.clip(krow[None, :] - r[:, None] + WIN_H - 1, 0, 2 * WIN_H - 2)
        oh_row = (drow[..., None] == np.arange(2 * WIN_H - 1)).astype(np.float32)
        by_row = jnp.einsum('hrc,akr->hakc', rpb, oh_row, precision=HIGHEST)
        bias = jnp.einsum('hakc,qjc->haqkj', by_row, oh_col, precision=HIGHEST)
        ok = ok_row[:, None, :, None] & ok_col[None, :, None, :]
        tabs.append(jnp.where(ok[None], bias, NEG_INF).reshape(heads, TOK_TILE, NA_KROWS * GRID_W))
    tabs.append(jnp.full_like(tabs[0], NEG_INF))
    return jnp.stack(tabs).astype(F32)


def _na_kernel(q_ref, k0_ref, k1_ref, k2_ref, kc_ref, v0_ref, v1_ref, v2_ref, vc_ref, bias_ref, o_ref):
    k_refs = (k0_ref, k1_ref, k2_ref, kc_ref)
    v_refs = (v0_ref, v1_ref, v2_ref, vc_ref)
    n_band = len(k_refs) - 1
    lane = lax.broadcasted_iota(jnp.int32, (TOK_TILE, LANES), 1)
    first = lane < NA_HEAD_DIM
    for pair in range(NA_PAIRS_PER_STEP):
        cols = slice(pair * LANES, (pair + 1) * LANES)
        q = q_ref[0, :, cols]
        outs = []
        for hh in range(2):
            mine = first if hh == 0 else jnp.logical_not(first)
            qm = jnp.where(mine, q, jnp.zeros_like(q)) * NA_HEAD_DIM ** -0.5
            scores = []
            for j, k_ref in enumerate(k_refs):
                s = lax.dot_general(qm, k_ref[0, :, cols], (((1,), (1,)), ((), ())), preferred_element_type=F32)
                if j < n_band:
                    s = s + bias_ref[0, 2 * pair + hh, :, j * TOK_TILE:(j + 1) * TOK_TILE]
                scores.append(s)
            m = functools.reduce(jnp.maximum, [jnp.max(s, axis=-1, keepdims=True) for s in scores])
            l = 0.0
            o = 0.0
            for s, v_ref in zip(scores, v_refs):
                p = jnp.exp(s - m)
                l = l + jnp.sum(p, axis=-1, keepdims=True)
                o = o + jnp.dot(p.astype(BF16), v_ref[0, :, cols], preferred_element_type=F32)
            outs.append(o / l)
        o_ref[0, :, cols] = jnp.where(first, outs[0], outs[1]).astype(o_ref.dtype)


def _na(qkv, bias, n_lat):
    b, l, _ = qkv.shape
    tpb = l // TOK_TILE
    n_groups = NA_WIDTH // (LANES * NA_PAIRS_PER_STEP)
    blk = (1, TOK_TILE, LANES * NA_PAIRS_PER_STEP)

    def kstart(i):
        return jnp.clip(i - 1, 0, n_lat - NA_KROWS // NA_QROWS)

    def btype(i):
        return jnp.where(i == 0, 0, jnp.where(i == n_lat - 1, 2, jnp.where(i == n_lat, 3, 1)))

    def kv_spec(col0, j):
        return pl.BlockSpec(blk, lambda hp, i, bi: (bi, kstart(i) + j, col0 + hp))

    def ctx_spec(col0):
        return pl.BlockSpec(blk, lambda hp, i, bi: (bi, n_lat, col0 + hp))

    nk = bias.shape[-1]
    return pl.pallas_call(
        _na_kernel,
        grid=(n_groups, tpb, b),
        in_specs=[pl.BlockSpec(blk, lambda hp, i, bi: (bi, i, hp)),
                  kv_spec(n_groups, 0), kv_spec(n_groups, 1), kv_spec(n_groups, 2), ctx_spec(n_groups),
                  kv_spec(2 * n_groups, 0), kv_spec(2 * n_groups, 1), kv_spec(2 * n_groups, 2),
                  ctx_spec(2 * n_groups),
                  pl.BlockSpec((1, 2 * NA_PAIRS_PER_STEP, TOK_TILE, nk), lambda hp, i, bi: (btype(i), hp, 0, 0))],
        out_specs=pl.BlockSpec(blk, lambda hp, i, bi: (bi, i, hp)),
        out_shape=jax.ShapeDtypeStruct((b, l, NA_WIDTH), BF16),
        compiler_params=_cparams(("arbitrary", "arbitrary", "arbitrary")),
        name="na",
    )(qkv, qkv, qkv, qkv, qkv, qkv, qkv, qkv, qkv, bias)


def _log_sigmoid(z):
    return jnp.minimum(z, 0.0) - jnp.log(1.0 + jnp.exp(-jnp.abs(z)))


def _gla_prep_kernel(q_ref, k_ref, g_ref, cos_ref, sin_ref, wg_ref, bg_ref, tri_ref,
                     qe_ref, kd_ref, a_ref, gd_ref, b_scr, qr_scr, kr_scr):
    tg = q_ref.shape[0]
    head_cols = [slice(hd * GLA_DK, (hd + 1) * GLA_DK) for hd in range(GLA_HEADS)]
    cosv = cos_ref[...]
    sinv = sin_ref[...]
    rotated = []
    for hd, cols in enumerate(head_cols):
        q = q_ref[:, cols]
        k = k_ref[:, cols]
        qr = (q * cosv + pltpu.roll(q, GLA_DK // 2, 1) * sinv) * GLA_DK ** -0.5
        kr = k * cosv + pltpu.roll(k, GLA_DK // 2, 1) * sinv
        qr_scr[hd] = qr
        kr_scr[hd] = kr
        rotated.append((qr, kr))
    for rev in (0, 1):
        gg = g_ref[:, rev * GATE_RANK:(rev + 1) * GATE_RANK]
        z = _dot_bf16x3(gg, wg_ref[rev], ((1,), (0,))) + bg_ref[rev]
        la = _log_sigmoid(z) * (1.0 / GATE_NORM)
        tri = tri_ref[rev]
        b = jnp.zeros_like(la)
        rest = la
        for _ in range(3):
            piece = rest.astype(BF16)
            b = b + jnp.dot(tri, piece, preferred_element_type=F32)
            rest = rest - piece.astype(F32)
        for hd, cols in enumerate(head_cols):
            qr, kr = rotated[hd]
            bh = b[:, cols]
            b_scr[rev, hd] = bh
            qe_ref[rev, :, cols] = (qr * jnp.exp(bh)).astype(qe_ref.dtype)
            for c in range(tg // GLA_CHUNK):
                r0 = c * GLA_CHUNK
                last = r0 if rev else r0 + GLA_CHUNK - 1
                tot = bh[last:last + 1]
                kd_ref[rev, r0:r0 + GLA_CHUNK, cols] = (
                    kr[r0:r0 + GLA_CHUNK] * jnp.exp(tot - bh[r0:r0 + GLA_CHUNK])).astype(kd_ref.dtype)
                gd_ref[rev, hd, 0, c:c + 1] = jnp.exp(tot)

    n_strip = GLA_CHUNK // GLA_STRIP
    colio = lax.broadcasted_iota(jnp.int32, (GLA_STRIP, GLA_CHUNK), 1)
    rowio = lax.broadcasted_iota(jnp.int32, (GLA_STRIP, GLA_CHUNK), 0)

    def strip_scores(rev, hd, c0, bch, qch, kch, u):
        lo, hi = u * GLA_STRIP, (u + 1) * GLA_STRIP
        bu, qu = bch[lo:hi], qch[lo:hi]
        krows, ref = (slice(hi, GLA_CHUNK), hi) if rev else (slice(0, lo), lo - 1)
        if krows.stop > krows.start:
            rb = b_scr[rev, hd, pl.ds(c0 + ref, 1), :]
            qt = qu * jnp.exp(jnp.minimum(bu - rb, 0.0))
            kt = kch[krows] * jnp.exp(jnp.minimum(rb - bch[krows], 0.0))
            pad = jnp.zeros((GLA_CHUNK - kt.shape[0], GLA_DK), F32)
            kt = jnp.concatenate([pad, kt] if rev else [kt, pad], axis=0)
            acc = lax.dot_general(qt.astype(BF16), kt.astype(BF16), (((1,), (1,)), ((), ())),
                                  preferred_element_type=F32)
        else:
            acc = jnp.zeros((GLA_STRIP, GLA_CHUNK), F32)
        for s in range(GLA_STRIP):
            ks = kr_scr[hd, pl.ds(c0 + lo + s, 1), :]
            bs = b_scr[rev, hd, pl.ds(c0 + lo + s, 1), :]
            col = jnp.sum(qu * ks * jnp.exp(bu - bs), axis=1, keepdims=True)
            causal = (rowio <= s) if rev else (rowio >= s)
            acc = jnp.where(jnp.logical_and(colio == lo + s, causal), col, acc)
        return acc

    def chunk_scores(c, carry):
        c0 = pl.multiple_of(c * GLA_CHUNK, GLA_CHUNK)
        for hd in range(GLA_HEADS):
            qch = qr_scr[hd, pl.ds(c0, GLA_CHUNK), :]
            kch = kr_scr[hd, pl.ds(c0, GLA_CHUNK), :]
            for rev in (0, 1):
                bch = b_scr[rev, hd, pl.ds(c0, GLA_CHUNK), :]
                for u in range(0, n_strip, 2):
                    pair = jnp.concatenate([strip_scores(rev, hd, c0, bch, qch, kch, u),
                                            strip_scores(rev, hd, c0, bch, qch, kch, u + 1)], axis=0)
                    a_ref[rev, hd, pl.ds(c0 + u * GLA_STRIP, 2 * GLA_STRIP), :] = pair.astype(a_ref.dtype)
        return carry

    lax.fori_loop(0, tg // GLA_CHUNK, chunk_scores, 0)


def _gla_prep(qk, g, cos2, sin2, w_gate, b_gate, tri, tpb):
    n = qk.shape[0]
    tg = TOK_TILE
    nt = n // tg
    fixed3 = lambda t: (0, 0, 0)
    return pl.pallas_call(
        _gla_prep_kernel,
        grid=(nt,),
        in_specs=[pl.BlockSpec((tg, GLA_QK), lambda t: (t, 0)),
                  pl.BlockSpec((tg, GLA_QK), lambda t: (t, 1)),
                  pl.BlockSpec((tg, 2 * GATE_RANK), lambda t: (t, 0)),
                  pl.BlockSpec((tg, GLA_DK), lambda t: (t % tpb, 0)),
                  pl.BlockSpec((tg, GLA_DK), lambda t: (t % tpb, 0)),
                  pl.BlockSpec((2, GATE_RANK, GLA_QK), fixed3),
                  pl.BlockSpec((2, 1, GLA_QK), fixed3),
                  pl.BlockSpec((2, tg, tg), fixed3)],
        out_specs=[pl.BlockSpec((2, tg, GLA_QK), lambda t: (0, t, 0)),
                   pl.BlockSpec((2, tg, GLA_QK), lambda t: (0, t, 0)),
                   pl.BlockSpec((2, GLA_HEADS, tg, GLA_CHUNK), lambda t: (0, 0, t, 0)),
                   pl.BlockSpec((2, GLA_HEADS, 1, tg // GLA_CHUNK, GLA_DK), lambda t: (0, 0, t, 0, 0))],
        out_shape=[jax.ShapeDtypeStruct((2, n, GLA_QK), BF16),
                   jax.ShapeDtypeStruct((2, n, GLA_QK), BF16),
                   jax.ShapeDtypeStruct((2, GLA_HEADS, n, GLA_CHUNK), BF16),
                   jax.ShapeDtypeStruct((2, GLA_HEADS, nt, tg // GLA_CHUNK, GLA_DK), F32)],
        scratch_shapes=[pltpu.VMEM((2, GLA_HEADS, tg, GLA_DK), F32),
                        pltpu.VMEM((GLA_HEADS, tg, GLA_DK), F32), pltpu.VMEM((GLA_HEADS, tg, GLA_DK), F32)],
        compiler_params=_cparams(("arbitrary",)),
        name="gla_prep",
    )(qk, qk, g, cos2, sin2, w_gate, b_gate, tri)


def _gla_scan_kernel(*refs):
    dirs = (refs[0:5], refs[5:10])
    o_refs = refs[10:12]
    st_ref = refs[12]

    @pl.when(pl.program_id(1) == 0)
    def _():
        st_ref[...] = jnp.zeros_like(st_ref)

    n_chunks = refs[0].shape[0] // GLA_CHUNK
    for hd in range(GLA_HEADS):
        kcols = slice(hd * GLA_DK, (hd + 1) * GLA_DK)
        vcols = slice(hd * GLA_DV, (hd + 1) * GLA_DV)
        states = [st_ref[0, hd], st_ref[1, hd]]
        for cc in range(n_chunks):
            for rev in (0, 1):
                qe_ref, kd_ref, a_ref, gd_ref, v_ref = dirs[rev]
                c = n_chunks - 1 - cc if rev else cc
                rows = slice(c * GLA_CHUNK, (c + 1) * GLA_CHUNK)
                st = states[rev]
                v_c = v_ref[rows, vcols]
                o = lax.dot_general(qe_ref[rows, kcols], st.astype(BF16), (((1,), (1,)), ((), ())),
                                    preferred_element_type=F32)
                o = o + jnp.dot(a_ref[hd, rows, :], v_c, preferred_element_type=F32)
                o_refs[rev][rows, vcols] = o
                upd = lax.dot_general(v_c, kd_ref[rows, kcols], (((0,), (0,)), ((), ())),
                                      preferred_element_type=F32)
                states[rev] = st * gd_ref[hd, 0, c:c + 1, :] + upd
        st_ref[0, hd] = states[0]
        st_ref[1, hd] = states[1]


def _gla_scan(prep, v, batch, tpb):
    n = v.shape[0]
    tg = TOK_TILE
    n_lat = tpb - 1

    def specs(rev):
        def tile(bi, s):
            lat = n_lat - s if rev else s - 1
            return bi * tpb + jnp.where(s == 0, n_lat, lat)

        ins = [pl.BlockSpec((None, tg, GLA_QK), lambda bi, s: (rev, tile(bi, s), 0)),
               pl.BlockSpec((None, tg, GLA_QK), lambda bi, s: (rev, tile(bi, s), 0)),
               pl.BlockSpec((None, GLA_HEADS, tg, GLA_CHUNK), lambda bi, s: (rev, 0, tile(bi, s), 0)),
               pl.BlockSpec((None, GLA_HEADS, 1, tg // GLA_CHUNK, GLA_DK),
                            lambda bi, s: (rev, 0, tile(bi, s), 0, 0)),
               pl.BlockSpec((tg, GLA_V), lambda bi, s: (tile(bi, s), 0))]
        return ins, pl.BlockSpec((tg, GLA_V), lambda bi, s: (tile(bi, s), 0))

    in_f, out_f = specs(0)
    in_b, out_b = specs(1)
    return pl.pallas_call(
        _gla_scan_kernel,
        grid=(batch, tpb),
        in_specs=in_f + in_b,
        out_specs=[out_f, out_b],
        out_shape=[jax.ShapeDtypeStruct((n, GLA_V), F32)] * 2,
        scratch_shapes=[pltpu.VMEM((2, GLA_HEADS, GLA_DV, GLA_DK), F32)],
        compiler_params=_cparams(("arbitrary", "arbitrary")),
        name="gla_scan",
    )(*prep, v, *prep, v)


def _pack_bf16(x):
    half = x.shape[1] // 2
    lo = lax.bitcast_convert_type(x[:, :half].astype(BF16).astype(F32), jnp.uint32)
    hi = lax.bitcast_convert_type(x[:, half:].astype(BF16).astype(F32), jnp.uint32)
    return (lo >> 16) | (hi & jnp.uint32(0xFFFF0000))


def _unpack_bf16(p):
    lo = lax.bitcast_convert_type(p << 16, F32)
    hi = lax.bitcast_convert_type(p & jnp.uint32(0xFFFF0000), F32)
    return lo, hi


def _residual_ln(x, a, t, ln, alpha):
    y = alpha * x + t[0:1] * a
    mu = jnp.mean(y, axis=-1, keepdims=True)
    yc = y - mu
    var = jnp.mean(yc * yc, axis=-1, keepdims=True)
    xn = yc * lax.rsqrt(var + LN_EPS) * ln[0:1] + ln[1:2]
    return xn, xn * (1.0 + t[1:2]) + t[2:3]


def _top4_softmax(lt):
    e = lt.shape[0]
    io = lax.broadcasted_iota(jnp.int32, lt.shape, 0)
    work = lt
    idxs, vals = [], []
    for _ in range(TOP_K):
        m = jnp.max(work, axis=0, keepdims=True)
        ik = jnp.min(jnp.where(work == m, io, e), axis=0, keepdims=True)
        idxs.append(ik)
        vals.append(m)
        work = jnp.where(io == ik, -jnp.inf, work)
    ex = [jnp.exp(v - vals[0]) for v in vals]
    den = ex[0] + ex[1] + ex[2] + ex[3]
    return jnp.concatenate(idxs, axis=0), jnp.concatenate([x / den for x in ex], axis=0)


def _gla_gated_norm(o, r, gn):
    gate = r * jax.nn.sigmoid(r)
    heads = []
    for hd in range(GLA_HEADS):
        cols = slice(hd * GLA_DV, (hd + 1) * GLA_DV)
        oh = o[:, cols]
        ms = jnp.mean(oh * oh, axis=-1, keepdims=True)
        heads.append((oh * lax.rsqrt(ms + RMS_EPS) * gn * gate[:, cols]).astype(BF16))
    return jnp.concatenate(heads, axis=1)


def _post_kernel(gla, n_act, tpb, alpha, *refs):
    if gla:
        of_ref, ob_ref, r_ref, gn_ref, w_ref = refs[:5]
        rest = refs[5:]
        act = _gla_gated_norm(of_ref[...] + ob_ref[...], r_ref[...], gn_ref[...])
        a = jnp.dot(act, w_ref[...], preferred_element_type=F32)
    else:
        acts = refs[:n_act]
        ws = refs[n_act:2 * n_act]
        rest = refs[2 * n_act:]
        a = jnp.dot(acts[0][...], ws[0][...], preferred_element_type=F32)
        for k in range(1, n_act):
            a = a + jnp.dot(acts[k][...], ws[k][...], preferred_element_type=F32)
    x_ref, tab_ref, ln_ref, rw_ref, rb_ref, xo_ref, h_ref, idx_ref, gate_ref = rest
    tm = x_ref.shape[0]
    ln = ln_ref[...]
    for s in range(tm // TOK_TILE):
        rows = slice(s * TOK_TILE, (s + 1) * TOK_TILE)
        t = tab_ref[_tab_row(pl.program_id(0) * (tm // TOK_TILE) + s, tpb)]
        xn, h = _residual_ln(x_ref[rows], a[rows], t, ln, alpha)
        xo_ref[rows] = xn
        h_ref[rows] = _pack_bf16(h)
        lt = _dot_bf16x3(rw_ref[...], h, ((1,), (1,))) + rb_ref[...]
        idx, gates = _top4_softmax(lt)
        idx_ref[:, rows] = idx
        gate_ref[:, rows] = gates


def _post(acts, ws, x, tab, ln, rw_t, rb, tpb, alpha, gla=False):
    n, d = x.shape
    tm = _row_tile(n)
    e = rw_t.shape[0]
    row = lambda i: (i, 0)
    fixed = lambda i: (0, 0)
    return pl.pallas_call(
        functools.partial(_post_kernel, gla, len(acts), tpb, alpha),
        grid=(n // tm,),
        in_specs=([pl.BlockSpec((tm, a.shape[1]), row) for a in acts]
                  + [pl.BlockSpec(w.shape, fixed) for w in ws]
                  + [pl.BlockSpec((tm, d), row),
                     pl.BlockSpec(tab.shape, lambda i: (0, 0, 0)),
                     pl.BlockSpec(ln.shape, fixed),
                     pl.BlockSpec((e, d), fixed),
                     pl.BlockSpec((e, 1), fixed)]),
        out_specs=[pl.BlockSpec((tm, d), row), pl.BlockSpec((tm, d // 2), row),
                   pl.BlockSpec((TOP_K, tm), lambda i: (0, i)), pl.BlockSpec((TOP_K, tm), lambda i: (0, i))],
        out_shape=[jax.ShapeDtypeStruct((n, d), F32), jax.ShapeDtypeStruct((n, d // 2), jnp.uint32),
                   jax.ShapeDtypeStruct((TOP_K, n), jnp.int32), jax.ShapeDtypeStruct((TOP_K, n), F32)],
        compiler_params=_cparams(("arbitrary",)),
        name="post",
    )(*acts, *ws, x, tab, ln, rw_t, rb)


def _rank_kernel(idx_ref, tri_ref, rank_ref, cnt_ref, carry_ref):
    @pl.when(pl.program_id(0) == 0)
    def _():
        carry_ref[...] = jnp.zeros_like(carry_ref)

    idx = idx_ref[...]
    e = carry_ref.shape[0]
    tr = idx.shape[1]
    io = lax.broadcasted_iota(jnp.int32, (e, tr), 0)
    chosen = jnp.zeros((e, tr), F32)
    for k in range(TOP_K):
        chosen = chosen + (idx[k:k + 1] == io).astype(F32)
    cum = jnp.dot(chosen.astype(BF16), tri_ref[...], preferred_element_type=F32)
    base = carry_ref[:, 0:1]
    excl = base + cum - chosen
    ranks = [jnp.sum(jnp.where(idx[k:k + 1] == io, excl, 0.0), axis=0, keepdims=True) for k in range(TOP_K)]
    rank_ref[...] = jnp.concatenate(ranks, axis=0).astype(jnp.int32)
    carry_ref[...] = carry_ref[...] + jnp.sum(chosen, axis=1, keepdims=True)
    cnt_ref[...] = carry_ref[...]


def _rank(idx_t, n_experts):
    n = idx_t.shape[1]
    tr = _row_tile(n)
    tri = (np.arange(tr)[:, None] <= np.arange(tr)[None, :]).astype(np.float32)
    return pl.pallas_call(
        _rank_kernel,
        grid=(n // tr,),
        in_specs=[pl.BlockSpec((TOP_K, tr), lambda i: (0, i)),
                  pl.BlockSpec((tr, tr), lambda i: (0, 0))],
        out_specs=[pl.BlockSpec((TOP_K, tr), lambda i: (0, i)),
                   pl.BlockSpec((n_experts, LANES), lambda i: (0, 0))],
        out_shape=[jax.ShapeDtypeStruct((TOP_K, n), jnp.int32),
                   jax.ShapeDtypeStruct((n_experts, LANES), F32)],
        scratch_shapes=[pltpu.VMEM((n_experts, LANES), F32)],
        compiler_params=_cparams(("arbitrary",)),
        name="rank",
    )(idx_t, jnp.asarray(tri, BF16))


def _expert_kernel(be_ref, rows_ref, x_ref, w1_ref, b1_ref, w2_ref, b2_ref, o_ref, w1b_ref, w2b_ref):
    i = pl.program_id(0)
    n_rows = rows_ref[i]
    half_block = EXPERT_BLOCK // 2
    new_expert = jnp.logical_or(i == 0, be_ref[i] != be_ref[jnp.maximum(i - 1, 0)])

    @pl.when(jnp.logical_and(n_rows > 0, new_expert))
    def _():
        w1b_ref[...] = w1_ref[0].astype(BF16)
        w2b_ref[...] = w2_ref[0].astype(BF16)

    def ffn(rows):
        x_lo, x_hi = _unpack_bf16(x_ref[rows])
        kh = x_lo.shape[1]
        hid = (jnp.dot(x_lo.astype(BF16), w1b_ref[:kh], preferred_element_type=F32)
               + jnp.dot(x_hi.astype(BF16), w1b_ref[kh:], preferred_element_type=F32) + b1_ref[0])
        half = hid.shape[1] // 2
        glu = jnp.minimum(hid[:, :half], SWIGLU_LIMIT)
        lin = jnp.clip(hid[:, half:], -SWIGLU_LIMIT, SWIGLU_LIMIT)
        act = glu * jax.nn.sigmoid(SWIGLU_ALPHA * glu) * (lin + 1.0)
        y = jnp.dot(act.astype(BF16), w2b_ref[...], preferred_element_type=F32) + b2_ref[0]
        o_ref[rows] = _pack_bf16(y)

    @pl.when(n_rows > half_block)
    def _():
        ffn(slice(0, EXPERT_BLOCK))

    @pl.when(jnp.logical_and(n_rows > 0, n_rows <= half_block))
    def _():
        ffn(slice(0, half_block))
        o_ref[half_block:] = jnp.zeros((EXPERT_BLOCK - half_block, o_ref.shape[1]), o_ref.dtype)

    @pl.when(n_rows == 0)
    def _():
        o_ref[...] = jnp.zeros_like(o_ref)


def _experts(block_expert, block_rows, x_pad, layer, w1, b1, w2, b2):
    n_pad, dp = x_pad.shape
    depth, e, d, dh2 = w1.shape
    n_blocks = n_pad // EXPERT_BLOCK
    grid_spec = pltpu.PrefetchScalarGridSpec(
        num_scalar_prefetch=2,
        grid=(n_blocks,),
        in_specs=[pl.BlockSpec((EXPERT_BLOCK, dp), lambda i, be, nb: (i, 0)),
                  pl.BlockSpec((None, 1, d, dh2), lambda i, be, nb: (layer, be[i], 0, 0)),
                  pl.BlockSpec((None, 1, 1, dh2), lambda i, be, nb: (layer, be[i], 0, 0)),
                  pl.BlockSpec((None, 1, dh2 // 2, d), lambda i, be, nb: (layer, be[i], 0, 0)),
                  pl.BlockSpec((None, 1, 1, d), lambda i, be, nb: (layer, be[i], 0, 0))],
        out_specs=pl.BlockSpec((EXPERT_BLOCK, dp), lambda i, be, nb: (i, 0)),
        scratch_shapes=[pltpu.VMEM((d, dh2), BF16), pltpu.VMEM((dh2 // 2, d), BF16)],
    )
    return pl.pallas_call(
        _expert_kernel,
        grid_spec=grid_spec,
        out_shape=jax.ShapeDtypeStruct((n_pad, dp), jnp.uint32),
        compiler_params=pltpu.CompilerParams(dimension_semantics=("arbitrary",),
                                             vmem_limit_bytes=EXPERT_VMEM_LIMIT),
        name="experts",
    )(block_expert, block_rows, x_pad, w1, b1.reshape(depth, e, 1, dh2), w2, b2.reshape(depth, e, 1, d))


def _stream_tile(i, tpb, latent_only):
    return (i // (tpb - 1)) * tpb + i % (tpb - 1) if latent_only else i


def _combine_kernel(tpb, alpha, splits, y_ref, gate_ref, x_ref, tab_ref, ln_ref, *refs):
    latent_only = splits is None
    g = gate_ref[...]
    y_lo, y_hi = None, None
    for k in range(TOP_K):
        lo, hi = _unpack_bf16(y_ref[k])
        gk = g[:, k:k + 1]
        y_lo = lo * gk if y_lo is None else y_lo + lo * gk
        y_hi = hi * gk if y_hi is None else y_hi + hi * gk
    y = jnp.concatenate([y_lo, y_hi], axis=1)
    ln = ln_ref[...]
    n_sub = x_ref.shape[0] // TOK_TILE
    xns, hs = [], []
    for s in range(n_sub):
        rows = slice(s * TOK_TILE, (s + 1) * TOK_TILE)
        t = tab_ref[_tab_row(_stream_tile(pl.program_id(0) * n_sub + s, tpb, latent_only), tpb)]
        xn, h = _residual_ln(x_ref[rows], y[rows], t, ln, alpha)
        xns.append(xn)
        hs.append(h)
    xn = jnp.concatenate(xns, axis=0)
    if latent_only:
        refs[0][...] = xn
    else:
        w_ref, xo_ref = refs[:2]
        xo_ref[...] = xn
        _emit_proj(jnp.concatenate(hs, axis=0), w_ref, splits, refs[2:])


def _combine(y_g, gates, x, tab, ln, tpb, alpha, plan):
    n, d = x.shape
    latent_only = plan is None
    tm = TOK_TILE if latent_only else _row_tile(n)
    tiles = n // tm
    steps = tiles // tpb * (tpb - 1) if latent_only else tiles
    src = lambda i: (_stream_tile(i, tpb, latent_only), 0)
    in_specs = [pl.BlockSpec((TOP_K, tm, d // 2), lambda i: (0, _stream_tile(i, tpb, latent_only), 0)),
                pl.BlockSpec((tm, TOP_K), src),
                pl.BlockSpec((tm, d), src),
                pl.BlockSpec(tab.shape, lambda i: (0, 0, 0)),
                pl.BlockSpec(ln.shape, lambda i: (0, 0))]
    out_specs = [pl.BlockSpec((tm, d), lambda i: (i, 0))]
    out_shape = [jax.ShapeDtypeStruct((steps * tm, d), F32)]
    operands = [y_g, gates, x, tab, ln]
    if not latent_only:
        w_spec, p_specs, p_shape = _proj_specs(n, tm, plan)
        in_specs.append(w_spec)
        out_specs += p_specs
        out_shape += p_shape
        operands.append(plan[0])
    return pl.pallas_call(
        functools.partial(_combine_kernel, tpb, alpha, None if latent_only else tuple(plan[1])),
        grid=(steps,),
        in_specs=in_specs,
        out_specs=out_specs,
        out_shape=out_shape,
        compiler_params=_cparams(("arbitrary",)),
        name="combine",
    )(*operands)


def _sc_mesh():
    return plsc.VectorSubcoreMesh(core_axis_name="c", subcore_axis_name="s")


def _sc_split(rows, mesh):
    workers = mesh.num_cores * mesh.num_subcores
    per = rows // workers
    assert per * workers == rows
    chunk = SC_CHUNK if per % SC_CHUNK == 0 else 8
    assert per % chunk == 0
    return per, chunk


def _sc_scatter_rows(x, idx, n_out):
    r, c = x.shape
    mesh = _sc_mesh()
    per, chunk = _sc_split(r, mesh)

    n_chunks = per // chunk

    @functools.partial(pl.kernel, out_type=jax.ShapeDtypeStruct((n_out, c), x.dtype), mesh=mesh,
                       scratch_types=[pltpu.VMEM((chunk, c), x.dtype), pltpu.SemaphoreType.DMA] * 2
                       + [pltpu.VMEM((chunk,), jnp.int32)] * TOP_K + [pltpu.SemaphoreType.DMA])
    def scatter(x_hbm, i_hbm, o_hbm, rows_a, sem_a, rows_b, sem_b, *rest):
        idx_vs, sem_s = rest[:TOP_K], rest[TOP_K]
        base = (lax.axis_index("s") * mesh.num_cores + lax.axis_index("c")) * per
        slots = ((rows_a, sem_a), (rows_b, sem_b))

        def load(j, slot):
            rows_v, sem = slot
            off = pl.multiple_of(base + j * chunk, chunk)
            return pltpu.make_async_copy(x_hbm.at[pl.ds(off, chunk)], rows_v, sem)

        def scatter_chunk(j, slot, prefetch):
            rows_v, _ = slot
            off = pl.multiple_of(base + j * chunk, chunk)
            load(j, slot).wait()
            for k in range(TOP_K):
                pltpu.sync_copy(i_hbm.at[pl.ds(k * r + off, chunk)], idx_vs[k])
            for k in range(TOP_K):
                pltpu.async_copy(rows_v, o_hbm.at[idx_vs[k]], sem_s)
            prefetch()
            for k in range(TOP_K):
                pltpu.make_async_copy(rows_v, o_hbm.at[idx_vs[k]], sem_s).wait()

        load(0, slots[0]).start()

        @pl.loop(0, n_chunks // 2)
        def _(p):
            j = 2 * p
            scatter_chunk(j, slots[0], lambda: load(j + 1, slots[1]).start())

            def next_even():
                @pl.when(j + 2 < n_chunks)
                def _():
                    load(j + 2, slots[0]).start()

            scatter_chunk(j + 1, slots[1], next_even)

        if n_chunks % 2:
            scatter_chunk(n_chunks - 1, slots[0], lambda: None)

    return scatter(x, idx)


def _sc_gather_rows(table, idx):
    m = idx.shape[0]
    c = table.shape[1]
    mesh = _sc_mesh()
    per, chunk = _sc_split(m, mesh)

    n_chunks = per // chunk
    slot_types = [pltpu.VMEM((chunk,), jnp.int32), pltpu.VMEM((chunk, c), table.dtype), pltpu.SemaphoreType.DMA]

    @functools.partial(pl.kernel, out_type=jax.ShapeDtypeStruct((m, c), table.dtype), mesh=mesh,
                       scratch_types=slot_types * 2)
    def gather(t_hbm, i_hbm, o_hbm, idx_a, rows_a, sem_a, idx_b, rows_b, sem_b):
        base = (lax.axis_index("s") * mesh.num_cores + lax.axis_index("c")) * per
        slots = ((idx_a, rows_a, sem_a), (idx_b, rows_b, sem_b))

        def start(j, slot):
            idx_v, rows_v, sem = slot
            off = pl.multiple_of(base + j * chunk, chunk)
            pltpu.sync_copy(i_hbm.at[pl.ds(off, chunk)], idx_v)
            pltpu.async_copy(t_hbm.at[idx_v], rows_v, sem)

        def finish(j, slot):
            idx_v, rows_v, sem = slot
            off = pl.multiple_of(base + j * chunk, chunk)
            pltpu.make_async_copy(t_hbm.at[idx_v], rows_v, sem).wait()
            pltpu.sync_copy(rows_v, o_hbm.at[pl.ds(off, chunk)])

        start(0, slots[0])

        @pl.loop(0, n_chunks // 2)
        def _(p):
            j = 2 * p
            start(j + 1, slots[1])
            finish(j, slots[0])

            @pl.when(j + 2 < n_chunks)
            def _():
                start(j + 2, slots[0])

            finish(j + 1, slots[1])

        if n_chunks % 2:
            finish(n_chunks - 1, slots[0])

    return gather(table, idx)


def _moe(h, idx_t, layer, w1, b1, w2, b2):
    n, dp = h.shape
    e = w1.shape[1]
    m = n * TOP_K
    rank_t, cnt = _rank(idx_t, e)
    sizes = cnt[:, 0].astype(jnp.int32)
    padded = (sizes + EXPERT_BLOCK - 1) // EXPERT_BLOCK * EXPERT_BLOCK
    pad_ends = jnp.cumsum(padded)
    pad_starts = pad_ends - padded
    ids = jnp.arange(e, dtype=jnp.int32)[:, None, None]
    dest_t = jnp.sum(jnp.where(idx_t[None] == ids, pad_starts[:, None, None], 0), axis=0) + rank_t
    dest = dest_t.reshape(-1)
    n_blocks = (m + e * (EXPERT_BLOCK - 1)) // EXPERT_BLOCK + 1
    n_pad = n_blocks * EXPERT_BLOCK
    block_start = jnp.arange(n_blocks, dtype=jnp.int32) * EXPERT_BLOCK
    block_expert = jnp.minimum(jnp.sum((pad_ends[None, :] <= block_start[:, None]).astype(jnp.int32), axis=1), e - 1)
    group_end = jnp.sum(jnp.where(block_expert[:, None] == jnp.arange(e, dtype=jnp.int32)[None, :],
                                  (pad_starts + sizes)[None, :], 0), axis=1)
    block_rows = jnp.clip(group_end - block_start, 0, EXPERT_BLOCK).astype(jnp.int32)
    x_pad = _sc_scatter_rows(h, dest, n_pad)
    y_pad = _experts(block_expert, block_rows, x_pad, layer, w1, b1, w2, b2)
    return _sc_gather_rows(y_pad, dest).reshape(TOP_K, n, dp)


def _rope_tables(t_lat, n_ctx):
    t = jnp.arange(t_lat)
    row = (t // GRID_W).astype(F32)
    col = (t % GRID_W).astype(F32)
    nf = GLA_DK // 4
    freqs = ROPE_BASE ** (-jnp.arange(nf, dtype=F32) / nf)
    ang = jnp.concatenate([row[:, None] * freqs, col[:, None] * freqs], axis=-1)
    cos, sin = jnp.cos(ang), jnp.sin(ang)
    cos2 = jnp.concatenate([cos, cos], axis=-1)
    sin2 = jnp.concatenate([-sin, sin], axis=-1)
    return (jnp.concatenate([cos2, jnp.ones((n_ctx, GLA_DK), F32)], axis=0),
            jnp.concatenate([sin2, jnp.zeros((n_ctx, GLA_DK), F32)], axis=0))


def _chunk_tri(tg, rev):
    t = np.arange(tg)
    same = (t[:, None] // GLA_CHUNK) == (t[None, :] // GLA_CHUNK)
    side = (t[None, :] >= t[:, None]) if rev else (t[None, :] <= t[:, None])
    return jnp.asarray((same & side).astype(np.float32), BF16)


def _table(mods, rows, batch):
    lat = jnp.stack([mods[:batch, r] for r in rows], axis=1)
    ctx = jnp.broadcast_to(jnp.stack([mods[batch, r] for r in rows], axis=0)[None], lat.shape)
    tab = jnp.stack([lat, ctx], axis=1).reshape(2 * batch, len(rows), -1)
    return jnp.pad(tab, ((0, 0), (0, 8 - len(rows)), (0, 0)))


@jax.jit
def _forward(x, c, ctx, c_ctx, ada_w, ada_b, ln_g, ln_b, ab_w_in, ab_pool_w, ab_pool_scale, ab_rpb,
             ab_w_out, gla_w_in, gla_w_gate, gla_b_gate, gla_norm_g, gla_w_out, router_w, router_b,
             exp_w1, exp_b1, exp_w2, exp_b2):
    batch, t_lat, d = x.shape
    n_ctx = ctx.shape[1]
    depth = ada_w.shape[0]
    assert d == D_MODEL and n_ctx == TOK_TILE and t_lat % TOK_TILE == 0
    rows = t_lat // GRID_W
    assert rows % NA_QROWS == 0 and rows >= NA_KROWS + NA_QROWS
    n_lat = t_lat // TOK_TILE
    tpb = n_lat + 1
    l = t_lat + n_ctx
    n = batch * l
    alpha = (2.0 * depth) ** 0.25

    cc = jnp.concatenate([c, c_ctx[None], jnp.zeros((16 - batch - 1, d), F32)], axis=0)
    mods = _mods(cc, ada_w, ada_b).reshape(depth, 16, N_MOD, d)

    def in_proj_plan(i):
        if i % 2 == 0:
            return (ab_w_in[i // 2].astype(BF16),
                    [(0, POOL_WIDTH), (POOL_WIDTH, POOL_WIDTH + 3 * NA_WIDTH)], [F32, BF16])
        edges = (0, 2 * GLA_QK, 2 * GLA_QK + GLA_V, 2 * GLA_QK + 2 * GLA_V, 2 * GLA_QK + 2 * GLA_V + 2 * GATE_RANK)
        return gla_w_in[i // 2].astype(BF16), list(zip(edges[:-1], edges[1:])), [F32, BF16, F32, F32]

    z, *projected = _modulate(x.reshape(batch * t_lat, d), ctx.reshape(batch * n_ctx, d),
                              _table(mods[0], (1, 0), batch), tpb, in_proj_plan(0))
    cos2, sin2 = _rope_tables(t_lat, n_ctx)

    for i in range(depth):
        j = i // 2
        last = i == depth - 1
        tab1 = _table(mods[i], (2, 4, 3), batch)
        ln1 = jnp.stack([ln_g[i, 0], ln_b[i, 0]])
        ln2 = jnp.stack([ln_g[i, 1], ln_b[i, 1]])
        rw_t = router_w[i].T
        rb = router_b[i][:, None]
        if i % 2 == 0:
            u, qkv = projected
            w_blk = jax.scipy.linalg.block_diag(*[ab_pool_w[j, g] for g in range(len(POOL_WINDOWS))])
            pooled = _pool(u.reshape(batch, l, POOL_WIDTH), w_blk.astype(BF16), ab_pool_scale[j][None, :],
                           n_lat, t_lat, n_ctx)
            bias = _na_bias_tables(ab_rpb[j], rows)
            attn = _na(qkv.reshape(batch, l, 3 * NA_WIDTH), bias, n_lat)
            w_out = ab_w_out[j].astype(BF16)
            acts = [pooled.reshape(n, POOL_WIDTH), attn.reshape(n, NA_WIDTH)]
            ws = [w_out[:POOL_WIDTH], w_out[POOL_WIDTH:]]
        else:
            qk, v, r, g = projected
            tri = jnp.stack([_chunk_tri(TOK_TILE, False), _chunk_tri(TOK_TILE, True)])
            prep = _gla_prep(qk, g, cos2, sin2, gla_w_gate[j], gla_b_gate[j][:, None, :], tri, tpb)
            o_f, o_b = _gla_scan(prep, v, batch, tpb)
            acts = [o_f, o_b, r]
            ws = [gla_norm_g[j][None, :], gla_w_out[j].astype(BF16)]
        z, h, idx_t, gates_t = _post(acts, ws, z, tab1, ln1, rw_t, rb, tpb, alpha, gla=i % 2 == 1)
        y_g = _moe(h, idx_t, i, exp_w1, exp_b1, exp_w2, exp_b2)
        nxt = mods[i + 1] if not last else mods[i]
        tab2 = _table(jnp.concatenate([mods[i][:, 5:6], nxt[:, 1:2], nxt[:, 0:1]], axis=1), (0, 1, 2), batch)
        if last:
            (out,) = _combine(y_g, gates_t.T, z, tab2, ln2, tpb, alpha, None)
            return out.reshape(batch, t_lat, d)
        z, *projected = _combine(y_g, gates_t.T, z, tab2, ln2, tpb, alpha, in_proj_plan(i + 1))


def kernel(x, c, ctx, c_ctx, ada_w, ada_b, ln_g, ln_b, ab_w_in, ab_pool_w, ab_pool_scale, ab_rpb, ab_w_out,
           gla_w_in, gla_w_gate, gla_b_gate, gla_norm_g, gla_w_out, router_w, router_b, exp_w1, exp_b1, exp_w2,
           exp_b2):
    return _forward(x, c, ctx, c_ctx, ada_w, ada_b, ln_g, ln_b, ab_w_in, ab_pool_w, ab_pool_scale, ab_rpb,
                    ab_w_out, gla_w_in, gla_w_gate, gla_b_gate, gla_norm_g, gla_w_out, router_w, router_b,
                    exp_w1, exp_b1, exp_w2, exp_b2)
```

```python
import functools
import math

import numpy as np
import jax
import jax.numpy as jnp
from jax import lax
from jax.experimental import pallas as pl
from jax.experimental.pallas import tpu as pltpu
from jax.experimental.pallas import tpu_sc as plsc

F32 = jnp.float32
BF16 = jnp.bfloat16
HIGHEST = lax.Precision.HIGHEST

D_MODEL = 1024
GRID_W = 64
N_MOD = 6
POOL_WINDOWS = (2, 4, 8, 16)
POOL_WIDTH = D_MODEL // 4
POOL_GROUP_DIM = POOL_WIDTH // len(POOL_WINDOWS)
POOL_HALO = max(POOL_WINDOWS) // 2
NA_HEAD_DIM = 64
NA_HEADS = (D_MODEL - POOL_WIDTH) // NA_HEAD_DIM
NA_WIDTH = NA_HEADS * NA_HEAD_DIM
WIN_H = 8
WIN_W = 16
GLA_HEADS = 4
GLA_DK = D_MODEL // 2 // GLA_HEADS
GLA_DV = D_MODEL // GLA_HEADS
GATE_RANK = 16
GATE_NORM = 16.0
GLA_CHUNK = 64
GLA_STRIP = 8
GLA_QK = GLA_HEADS * GLA_DK
GLA_V = GLA_HEADS * GLA_DV
ROPE_BASE = 10000.0
TOP_K = 4
SWIGLU_LIMIT = 7.0
SWIGLU_ALPHA = 1.702
LN_EPS = 1e-5
RMS_EPS = 1e-6
NEG_INF = -1e30

LANES = 128
TOK_TILE = 256
NA_QROWS = 4
NA_KROWS = 12
NA_PAIRS_PER_STEP = 6
SCAN_GROUP = 2
VMEM_LIMIT = 48 * 1024 * 1024
EXPERT_VMEM_LIMIT = 56 * 1024 * 1024
EXPERT_BLOCK = 512
SC_CHUNK = 64


def _cparams(sem):
    return pltpu.CompilerParams(dimension_semantics=sem, vmem_limit_bytes=VMEM_LIMIT)


def _dot_bf16x3(a, b, dims):
    a_hi = a.astype(BF16)
    a_lo = (a - a_hi.astype(F32)).astype(BF16)
    b_hi = b.astype(BF16)
    b_lo = (b - b_hi.astype(F32)).astype(BF16)
    dg = functools.partial(lax.dot_general, dimension_numbers=(dims, ((), ())), preferred_element_type=F32)
    return dg(a_hi, b_hi) + dg(a_hi, b_lo) + dg(a_lo, b_hi)


def _row_tile(n):
    return 2 * TOK_TILE if n % (2 * TOK_TILE) == 0 else TOK_TILE


def _mods_kernel(c_ref, w_ref, b_ref, o_ref):
    cv = c_ref[...]
    sc = cv * jax.nn.sigmoid(cv)
    o_ref[0] = jnp.dot(sc, w_ref[0], precision=HIGHEST, preferred_element_type=F32) + b_ref[0]


def _mods(cc, ada_w, ada_b):
    depth, d, n = ada_w.shape
    r = cc.shape[0]
    tn = n // 4
    return pl.pallas_call(
        _mods_kernel,
        grid=(depth, n // tn),
        in_specs=[pl.BlockSpec((r, d), lambda i, j: (0, 0)),
                  pl.BlockSpec((1, d, tn), lambda i, j: (i, 0, j)),
                  pl.BlockSpec((1, 1, tn), lambda i, j: (i, 0, j))],
        out_specs=pl.BlockSpec((1, r, tn), lambda i, j: (i, 0, j)),
        out_shape=jax.ShapeDtypeStruct((depth, r, n), F32),
        compiler_params=_cparams(("arbitrary", "arbitrary")),
        name="mods",
    )(cc, ada_w, ada_b.reshape(depth, 1, n))


def _tab_row(g, tpb):
    return (g // tpb) * 2 + (g % tpb == tpb - 1).astype(jnp.int32)


def _modulate_kernel(tpb, splits, x_ref, ctx_ref, tab_ref, w_ref, z_ref, *out_refs):
    i = pl.program_id(0)
    is_ctx = i % tpb == tpb - 1
    z = jnp.where(is_ctx, ctx_ref[...], x_ref[...])
    t = tab_ref[_tab_row(i, tpb)]
    z_ref[...] = z
    _emit_proj(z * (1.0 + t[0:1]) + t[1:2], w_ref, splits, out_refs)


def _modulate(x, ctx, tab, tpb, plan):
    d = x.shape[1]
    n_lat = tpb - 1
    n = x.shape[0] + ctx.shape[0]
    w_spec, p_specs, p_shape = _proj_specs(n, TOK_TILE, plan)
    return pl.pallas_call(
        functools.partial(_modulate_kernel, tpb, tuple(plan[1])),
        grid=(n // TOK_TILE,),
        in_specs=[pl.BlockSpec((TOK_TILE, d), lambda i: ((i // tpb) * n_lat + jnp.minimum(i % tpb, n_lat - 1), 0)),
                  pl.BlockSpec((TOK_TILE, d), lambda i: (i // tpb, 0)),
                  pl.BlockSpec(tab.shape, lambda i: (0, 0, 0)),
                  w_spec],
        out_specs=[pl.BlockSpec((TOK_TILE, d), lambda i: (i, 0))] + p_specs,
        out_shape=[jax.ShapeDtypeStruct((n, d), F32)] + p_shape,
        compiler_params=_cparams(("arbitrary",)),
        name="modulate",
    )(x, ctx, tab, plan[0])


def _emit_proj(h, w_ref, splits, out_refs):
    hb = h.astype(BF16)
    for (a, b), o_ref in zip(splits, out_refs):
        o_ref[...] = jnp.dot(hb, w_ref[:, a:b], preferred_element_type=F32).astype(o_ref.dtype)


def _proj_specs(n, tm, plan):
    w, splits, dtypes = plan
    w_spec = pl.BlockSpec(w.shape, lambda i: (0, 0), pipeline_mode=pl.Buffered(1))
    out_specs = [pl.BlockSpec((tm, b - a), lambda i: (i, 0)) for a, b in splits]
    out_shape = [jax.ShapeDtypeStruct((n, b - a), dt) for (a, b), dt in zip(splits, dtypes)]
    return w_spec, out_specs, out_shape


def _pool_tile(n_lat, t_lat, t_ctx, prev_ref, cur_ref, next_ref, w_ref, scale_ref, o_ref, halo_ref):
    j = pl.program_id(1)
    is_ctx = j == n_lat
    has_prev = jnp.logical_and(j > 0, jnp.logical_not(is_ctx))
    has_next = j < n_lat - 1
    cur = cur_ref[0]
    hl = POOL_HALO
    halo_ref[0:hl] = jnp.where(has_prev, prev_ref[0, TOK_TILE - hl:TOK_TILE], 0.0)
    halo_ref[hl:hl + TOK_TILE] = cur
    halo_ref[hl + TOK_TILE:2 * hl + TOK_TILE] = jnp.where(has_next, next_ref[0, 0:hl], 0.0)

    shape = cur.shape
    lane = lax.broadcasted_iota(jnp.int32, shape, 1)
    group = lane // POOL_GROUP_DIM
    half = jnp.ones(shape, jnp.int32)
    for gi, wdw in enumerate(POOL_WINDOWS):
        half = jnp.where(group == gi, wdw // 2, half)
    acc = jnp.zeros(shape, F32)
    for off in range(-hl, hl):
        v = halo_ref[hl + off:hl + off + TOK_TILE]
        inside = (half >= -off) if off < 0 else (half > off)
        acc = acc + jnp.where(inside, v, 0.0)
    pos0 = jnp.where(is_ctx, 0, j * TOK_TILE)
    seq = jnp.where(is_ctx, t_ctx, t_lat)
    t = pos0 + lax.broadcasted_iota(jnp.int32, shape, 0)
    cnt = jnp.minimum(t + half, seq) - jnp.maximum(t - half, 0)
    pooled = acc / cnt.astype(F32) - cur
    y = jnp.dot(pooled.astype(BF16), w_ref[...], preferred_element_type=F32) * scale_ref[...]
    o_ref[0] = y.astype(o_ref.dtype)


def _na_bias_tables(rpb, rows):
    n_i = rows // NA_QROWS
    heads = rpb.shape[0]
    a = np.arange(NA_QROWS)
    kr = np.arange(NA_KROWS)
    cq = np.arange(GRID_W)
    ws = np.clip(cq - WIN_W // 2, 0, GRID_W - WIN_W)
    ok_col = (cq[None, :] >= ws[:, None]) & (cq[None, :] < ws[:, None] + WIN_W)
    dcol = np.clip(cq[None, :] - cq[:, None] + WIN_W - 1, 0, 2 * WIN_W - 2)
    oh_col = (dcol[..., None] == np.arange(2 * WIN_W - 1)).astype(np.float32)
    tabs = []
    for i in (0, 1, n_i - 1):
        start = int(np.clip(NA_QROWS * i - WIN_H // 2, 0, rows - NA_KROWS))
        r = NA_QROWS * i + a
        krow = start + kr
        rs = np.clip(r - WIN_H // 2, 0, rows - WIN_H)
        ok_row = (krow[None, :] >= rs[:, None]) & (krow[None, :] < rs[:, None] + WIN_H)
        drow = np.clip(krow[None, :] - r[:, None] + WIN_H - 1, 0, 2 * WIN_H - 2)
        oh_row = (drow[..., None] == np.arange(2 * WIN_H - 1)).astype(np.float32)
        by_row = jnp.einsum('hrc,akr->hakc', rpb, oh_row, precision=HIGHEST)
        bias = jnp.einsum('hakc,qjc->haqkj', by_row, oh_col, precision=HIGHEST)
        ok = ok_row[:, None, :, None] & ok_col[None, :, None, :]
        tabs.append(jnp.where(ok[None], bias, NEG_INF).reshape(heads, TOK_TILE, NA_KROWS * GRID_W))
    tabs.append(jnp.full_like(tabs[0], NEG_INF))
    return jnp.stack(tabs).astype(F32)


def _na_kernel(pool_args, q_ref, k0_ref, k1_ref, k2_ref, kc_ref, v0_ref, v1_ref, v2_ref, vc_ref, bias_ref,
               uprev_ref, ucur_ref, unext_ref, pw_ref, ps_ref, o_ref, po_ref, halo_ref):
    _pool_tile(*pool_args, uprev_ref, ucur_ref, unext_ref, pw_ref, ps_ref, po_ref, halo_ref)
    k_refs = (k0_ref, k1_ref, k2_ref, kc_ref)
    v_refs = (v0_ref, v1_ref, v2_ref, vc_ref)
    n_band = len(k_refs) - 1
    lane = lax.broadcasted_iota(jnp.int32, (TOK_TILE, LANES), 1)
    first = lane < NA_HEAD_DIM
    for pair in range(NA_PAIRS_PER_STEP):
        cols = slice(pair * LANES, (pair + 1) * LANES)
        q = q_ref[0, :, cols]
        outs = []
        for hh in range(2):
            mine = first if hh == 0 else jnp.logical_not(first)
            qm = jnp.where(mine, q, jnp.zeros_like(q)) * NA_HEAD_DIM ** -0.5
            scores = []
            for j, k_ref in enumerate(k_refs):
                s = lax.dot_general(qm, k_ref[0, :, cols], (((1,), (1,)), ((), ())), preferred_element_type=F32)
                if j < n_band:
                    s = s + bias_ref[0, 2 * pair + hh, :, j * TOK_TILE:(j + 1) * TOK_TILE]
                scores.append(s)
            m = functools.reduce(jnp.maximum, [jnp.max(s, axis=-1, keepdims=True) for s in scores])
            l = 0.0
            o = 0.0
            for s, v_ref in zip(scores, v_refs):
                p = jnp.exp(s - m)
                l = l + jnp.sum(p, axis=-1, keepdims=True)
                o = o + jnp.dot(p.astype(BF16), v_ref[0, :, cols], preferred_element_type=F32)
            outs.append(o / l)
        o_ref[0, :, cols] = jnp.where(first, outs[0], outs[1]).astype(o_ref.dtype)


def _na_pool(qkv, bias, u, pool_w, pool_scale, n_lat, t_lat, t_ctx):
    b, l, _ = qkv.shape
    pw = u.shape[-1]
    tpb = l // TOK_TILE
    n_groups = NA_WIDTH // (LANES * NA_PAIRS_PER_STEP)
    assert n_groups == 1
    blk = (1, TOK_TILE, LANES * NA_PAIRS_PER_STEP)
    ublk = (1, TOK_TILE, pw)

    def kstart(i):
        return jnp.clip(i - 1, 0, n_lat - NA_KROWS // NA_QROWS)

    def btype(i):
        return jnp.where(i == 0, 0, jnp.where(i == n_lat - 1, 2, jnp.where(i == n_lat, 3, 1)))

    def kv_spec(col0, j):
        return pl.BlockSpec(blk, lambda hp, i, bi: (bi, kstart(i) + j, col0 + hp))

    def ctx_spec(col0):
        return pl.BlockSpec(blk, lambda hp, i, bi: (bi, n_lat, col0 + hp))

    nk = bias.shape[-1]
    return pl.pallas_call(
        functools.partial(_na_kernel, (n_lat, t_lat, t_ctx)),
        grid=(n_groups, tpb, b),
        in_specs=[pl.BlockSpec(blk, lambda hp, i, bi: (bi, i, hp)),
                  kv_spec(n_groups, 0), kv_spec(n_groups, 1), kv_spec(n_groups, 2), ctx_spec(n_groups),
                  kv_spec(2 * n_groups, 0), kv_spec(2 * n_groups, 1), kv_spec(2 * n_groups, 2),
                  ctx_spec(2 * n_groups),
                  pl.BlockSpec((1, 2 * NA_PAIRS_PER_STEP, TOK_TILE, nk), lambda hp, i, bi: (btype(i), hp, 0, 0)),
                  pl.BlockSpec(ublk, lambda hp, i, bi: (bi, jnp.maximum(i - 1, 0), 0)),
                  pl.BlockSpec(ublk, lambda hp, i, bi: (bi, i, 0)),
                  pl.BlockSpec(ublk, lambda hp, i, bi: (bi, jnp.minimum(i + 1, tpb - 1), 0)),
                  pl.BlockSpec((pw, pw), lambda hp, i, bi: (0, 0)),
                  pl.BlockSpec((1, pw), lambda hp, i, bi: (0, 0))],
        out_specs=[pl.BlockSpec(blk, lambda hp, i, bi: (bi, i, hp)),
                   pl.BlockSpec(ublk, lambda hp, i, bi: (bi, i, 0))],
        out_shape=[jax.ShapeDtypeStruct((b, l, NA_WIDTH), BF16), jax.ShapeDtypeStruct((b, l, pw), BF16)],
        scratch_shapes=[pltpu.VMEM((TOK_TILE + 2 * POOL_HALO, pw), F32)],
        compiler_params=_cparams(("arbitrary", "arbitrary", "arbitrary")),
        name="na_pool",
    )(qkv, qkv, qkv, qkv, qkv, qkv, qkv, qkv, qkv, bias, u, u, u, pool_w, pool_scale)


def _log_sigmoid(z):
    return jnp.minimum(z, 0.0) - jnp.log(1.0 + jnp.exp(-jnp.abs(z)))


def _gla_prep_kernel(q_ref, k_ref, g_ref, cos_ref, sin_ref, wg_ref, bg_ref, tri_ref,
                     qe_ref, kd_ref, a_ref, gd_ref, b_scr, qr_scr, kr_scr):
    tg = q_ref.shape[0]
    head_cols = [slice(hd * GLA_DK, (hd + 1) * GLA_DK) for hd in range(GLA_HEADS)]
    cosv = cos_ref[...]
    sinv = sin_ref[...]
    rotated = []
    for hd, cols in enumerate(head_cols):
        q = q_ref[:, cols]
        k = k_ref[:, cols]
        qr = (q * cosv + pltpu.roll(q, GLA_DK // 2, 1) * sinv) * GLA_DK ** -0.5
        kr = k * cosv + pltpu.roll(k, GLA_DK // 2, 1) * sinv
        qr_scr[hd] = qr
        kr_scr[hd] = kr
        rotated.append((qr, kr))
    for rev in (0, 1):
        gg = g_ref[:, rev * GATE_RANK:(rev + 1) * GATE_RANK]
        z = _dot_bf16x3(gg, wg_ref[rev], ((1,), (0,))) + bg_ref[rev]
        la = _log_sigmoid(z) * (1.0 / GATE_NORM)
        tri = tri_ref[rev]
        b = jnp.zeros_like(la)
        rest = la
        for _ in range(3):
            piece = rest.astype(BF16)
            b = b + jnp.dot(tri, piece, preferred_element_type=F32)
            rest = rest - piece.astype(F32)
        for hd, cols in enumerate(head_cols):
            qr, kr = rotated[hd]
            bh = b[:, cols]
            b_scr[rev, hd] = bh
            qe_ref[rev, :, cols] = (qr * jnp.exp(bh)).astype(qe_ref.dtype)
            for c in range(tg // GLA_CHUNK):
                r0 = c * GLA_CHUNK
                last = r0 if rev else r0 + GLA_CHUNK - 1
                tot = bh[last:last + 1]
                kd_ref[rev, r0:r0 + GLA_CHUNK, cols] = (
                    kr[r0:r0 + GLA_CHUNK] * jnp.exp(tot - bh[r0:r0 + GLA_CHUNK])).astype(kd_ref.dtype)
                gd_ref[rev, hd, 0, c:c + 1] = jnp.exp(tot)

    n_strip = GLA_CHUNK // GLA_STRIP
    colio = lax.broadcasted_iota(jnp.int32, (GLA_STRIP, GLA_CHUNK), 1)
    rowio = lax.broadcasted_iota(jnp.int32, (GLA_STRIP, GLA_CHUNK), 0)

    def strip_scores(rev, hd, c0, bch, qch, kch, u):
        lo, hi = u * GLA_STRIP, (u + 1) * GLA_STRIP
        bu, qu = bch[lo:hi], qch[lo:hi]
        krows, ref = (slice(hi, GLA_CHUNK), hi) if rev else (slice(0, lo), lo - 1)
        if krows.stop > krows.start:
            rb = b_scr[rev, hd, pl.ds(c0 + ref, 1), :]
            qt = qu * jnp.exp(jnp.minimum(bu - rb, 0.0))
            kt = kch[krows] * jnp.exp(jnp.minimum(rb - bch[krows], 0.0))
            pad = jnp.zeros((GLA_CHUNK - kt.shape[0], GLA_DK), F32)
            kt = jnp.concatenate([pad, kt] if rev else [kt, pad], axis=0)
            acc = lax.dot_general(qt.astype(BF16), kt.astype(BF16), (((1,), (1,)), ((), ())),
                                  preferred_element_type=F32)
        else:
            acc = jnp.zeros((GLA_STRIP, GLA_CHUNK), F32)
        for s in range(GLA_STRIP):
            ks = kr_scr[hd, pl.ds(c0 + lo + s, 1), :]
            bs = b_scr[rev, hd, pl.ds(c0 + lo + s, 1), :]
            col = jnp.sum(qu * ks * jnp.exp(bu - bs), axis=1, keepdims=True)
            causal = (rowio <= s) if rev else (rowio >= s)
            acc = jnp.where(jnp.logical_and(colio == lo + s, causal), col, acc)
        return acc

    def chunk_scores(c, carry):
        c0 = pl.multiple_of(c * GLA_CHUNK, GLA_CHUNK)
        for hd in range(GLA_HEADS):
            qch = qr_scr[hd, pl.ds(c0, GLA_CHUNK), :]
            kch = kr_scr[hd, pl.ds(c0, GLA_CHUNK), :]
            for rev in (0, 1):
                bch = b_scr[rev, hd, pl.ds(c0, GLA_CHUNK), :]
                for u in range(0, n_strip, 2):
                    pair = jnp.concatenate([strip_scores(rev, hd, c0, bch, qch, kch, u),
                                            strip_scores(rev, hd, c0, bch, qch, kch, u + 1)], axis=0)
                    a_ref[rev, hd, pl.ds(c0 + u * GLA_STRIP, 2 * GLA_STRIP), :] = pair.astype(a_ref.dtype)
        return carry

    lax.fori_loop(0, tg // GLA_CHUNK, chunk_scores, 0)


def _gla_prep(qk, g, cos2, sin2, w_gate, b_gate, tri, tpb):
    n = qk.shape[0]
    tg = TOK_TILE
    nt = n // tg
    fixed3 = lambda t: (0, 0, 0)
    return pl.pallas_call(
        _gla_prep_kernel,
        grid=(nt,),
        in_specs=[pl.BlockSpec((tg, GLA_QK), lambda t: (t, 0)),
                  pl.BlockSpec((tg, GLA_QK), lambda t: (t, 1)),
                  pl.BlockSpec((tg, 2 * GATE_RANK), lambda t: (t, 0)),
                  pl.BlockSpec((tg, GLA_DK), lambda t: (t % tpb, 0)),
                  pl.BlockSpec((tg, GLA_DK), lambda t: (t % tpb, 0)),
                  pl.BlockSpec((2, GATE_RANK, GLA_QK), fixed3),
                  pl.BlockSpec((2, 1, GLA_QK), fixed3),
                  pl.BlockSpec((2, tg, tg), fixed3)],
        out_specs=[pl.BlockSpec((2, tg, GLA_QK), lambda t: (0, t, 0)),
                   pl.BlockSpec((2, tg, GLA_QK), lambda t: (0, t, 0)),
                   pl.BlockSpec((2, GLA_HEADS, tg, GLA_CHUNK), lambda t: (0, 0, t, 0)),
                   pl.BlockSpec((2, GLA_HEADS, 1, tg // GLA_CHUNK, GLA_DK), lambda t: (0, 0, t, 0, 0))],
        out_shape=[jax.ShapeDtypeStruct((2, n, GLA_QK), BF16),
                   jax.ShapeDtypeStruct((2, n, GLA_QK), BF16),
                   jax.ShapeDtypeStruct((2, GLA_HEADS, n, GLA_CHUNK), BF16),
                   jax.ShapeDtypeStruct((2, GLA_HEADS, nt, tg // GLA_CHUNK, GLA_DK), F32)],
        scratch_shapes=[pltpu.VMEM((2, GLA_HEADS, tg, GLA_DK), F32),
                        pltpu.VMEM((GLA_HEADS, tg, GLA_DK), F32), pltpu.VMEM((GLA_HEADS, tg, GLA_DK), F32)],
        compiler_params=_cparams(("arbitrary",)),
        name="gla_prep",
    )(qk, qk, g, cos2, sin2, w_gate, b_gate, tri)


def _gla_scan_kernel(*refs):
    dirs = (refs[0:5], refs[5:10])
    o_refs = refs[10:12]
    st_ref = refs[12]

    @pl.when(pl.program_id(1) == 0)
    def _():
        st_ref[...] = jnp.zeros_like(st_ref)

    group = refs[0].shape[0]
    n_chunks = refs[0].shape[1] // GLA_CHUNK
    lanes = [(rev, p) for rev in (0, 1) for p in range(group)]
    for hd in range(GLA_HEADS):
        kcols = slice(hd * GLA_DK, (hd + 1) * GLA_DK)
        vcols = slice(hd * GLA_DV, (hd + 1) * GLA_DV)
        states = {lane: st_ref[lane[0], lane[1], hd] for lane in lanes}
        for cc in range(n_chunks):
            for rev, p in lanes:
                qe_ref, kd_ref, a_ref, gd_ref, v_ref = dirs[rev]
                c = n_chunks - 1 - cc if rev else cc
                rows = slice(c * GLA_CHUNK, (c + 1) * GLA_CHUNK)
                st = states[rev, p]
                v_c = v_ref[p, rows, vcols]
                o = lax.dot_general(qe_ref[p, rows, kcols], st.astype(BF16), (((1,), (1,)), ((), ())),
                                    preferred_element_type=F32)
                o = o + jnp.dot(a_ref[hd, p, rows, :], v_c, preferred_element_type=F32)
                o_refs[rev][p, rows, vcols] = o
                upd = lax.dot_general(v_c, kd_ref[p, rows, kcols], (((0,), (0,)), ((), ())),
                                      preferred_element_type=F32)
                states[rev, p] = st * gd_ref[hd, p, 0, c:c + 1, :] + upd
        for rev, p in lanes:
            st_ref[rev, p, hd] = states[rev, p]


def _gla_scan(prep, v, batch, tpb):
    n = v.shape[0]
    tg = TOK_TILE
    n_lat = tpb - 1
    l = tpb * tg
    group = SCAN_GROUP if batch % SCAN_GROUP == 0 else 1
    nb = batch // group
    qe, kd, a, gd = prep
    qe = qe.reshape(2, nb, group, l, GLA_QK)
    kd = kd.reshape(2, nb, group, l, GLA_QK)
    a = a.reshape(2, GLA_HEADS, nb, group, l, GLA_CHUNK)
    gd = gd.reshape(2, GLA_HEADS, nb, group, tpb, tg // GLA_CHUNK, GLA_DK)
    v = v.reshape(nb, group, l, GLA_V)

    def specs(rev):
        def tile(s):
            return jnp.where(s == 0, n_lat, n_lat - s if rev else s - 1)

        ins = [pl.BlockSpec((None, None, group, tg, GLA_QK), lambda bi, s: (rev, bi, 0, tile(s), 0)),
               pl.BlockSpec((None, None, group, tg, GLA_QK), lambda bi, s: (rev, bi, 0, tile(s), 0)),
               pl.BlockSpec((None, GLA_HEADS, None, group, tg, GLA_CHUNK),
                            lambda bi, s: (rev, 0, bi, 0, tile(s), 0)),
               pl.BlockSpec((None, GLA_HEADS, None, group, 1, tg // GLA_CHUNK, GLA_DK),
                            lambda bi, s: (rev, 0, bi, 0, tile(s), 0, 0)),
               pl.BlockSpec((None, group, tg, GLA_V), lambda bi, s: (bi, 0, tile(s), 0))]
        return ins, pl.BlockSpec((None, group, tg, GLA_V), lambda bi, s: (bi, 0, tile(s), 0))

    in_f, out_f = specs(0)
    in_b, out_b = specs(1)
    o_f, o_b = pl.pallas_call(
        _gla_scan_kernel,
        grid=(nb, tpb),
        in_specs=in_f + in_b,
        out_specs=[out_f, out_b],
        out_shape=[jax.ShapeDtypeStruct((nb, group, l, GLA_V), F32)] * 2,
        scratch_shapes=[pltpu.VMEM((2, group, GLA_HEADS, GLA_DV, GLA_DK), F32)],
        compiler_params=_cparams(("arbitrary", "arbitrary")),
        name="gla_scan",
    )(qe, kd, a, gd, v, qe, kd, a, gd, v)
    return o_f.reshape(n, GLA_V), o_b.reshape(n, GLA_V)


def _pack_bf16(x):
    half = x.shape[1] // 2
    lo = lax.bitcast_convert_type(x[:, :half].astype(BF16).astype(F32), jnp.uint32)
    hi = lax.bitcast_convert_type(x[:, half:].astype(BF16).astype(F32), jnp.uint32)
    return (lo >> 16) | (hi & jnp.uint32(0xFFFF0000))


def _unpack_bf16(p):
    lo = lax.bitcast_convert_type(p << 16, F32)
    hi = lax.bitcast_convert_type(p & jnp.uint32(0xFFFF0000), F32)
    return lo, hi


def _residual_ln(x, a, t, ln, alpha):
    y = alpha * x + t[0:1] * a
    mu = jnp.mean(y, axis=-1, keepdims=True)
    yc = y - mu
    var = jnp.mean(yc * yc, axis=-1, keepdims=True)
    xn = yc * lax.rsqrt(var + LN_EPS) * ln[0:1] + ln[1:2]
    return xn, xn * (1.0 + t[1:2]) + t[2:3]


def _top4_softmax(lt):
    e = lt.shape[0]
    io = lax.broadcasted_iota(jnp.int32, lt.shape, 0)
    work = lt
    idxs, vals = [], []
    for _ in range(TOP_K):
        m = jnp.max(work, axis=0, keepdims=True)
        ik = jnp.min(jnp.where(work == m, io, e), axis=0, keepdims=True)
        idxs.append(ik)
        vals.append(m)
        work = jnp.where(io == ik, -jnp.inf, work)
    ex = [jnp.exp(v - vals[0]) for v in vals]
    den = ex[0] + ex[1] + ex[2] + ex[3]
    return jnp.concatenate(idxs, axis=0), jnp.concatenate([x / den for x in ex], axis=0)


def _gla_gated_norm(o, r, gn):
    gate = r * jax.nn.sigmoid(r)
    heads = []
    for hd in range(GLA_HEADS):
        cols = slice(hd * GLA_DV, (hd + 1) * GLA_DV)
        oh = o[:, cols]
        ms = jnp.mean(oh * oh, axis=-1, keepdims=True)
        heads.append((oh * lax.rsqrt(ms + RMS_EPS) * gn * gate[:, cols]).astype(BF16))
    return jnp.concatenate(heads, axis=1)


def _post_kernel(gla, n_act, tpb, alpha, *refs):
    if gla:
        of_ref, ob_ref, r_ref, gn_ref, w_ref = refs[:5]
        rest = refs[5:]
        act = _gla_gated_norm(of_ref[...] + ob_ref[...], r_ref[...], gn_ref[...])
        a = jnp.dot(act, w_ref[...], preferred_element_type=F32)
    else:
        acts = refs[:n_act]
        ws = refs[n_act:2 * n_act]
        rest = refs[2 * n_act:]
        a = jnp.dot(acts[0][...], ws[0][...], preferred_element_type=F32)
        for k in range(1, n_act):
            a = a + jnp.dot(acts[k][...], ws[k][...], preferred_element_type=F32)
    x_ref, tab_ref, ln_ref, rw_ref, rb_ref, xo_ref, h_ref, idx_ref, gate_ref = rest
    tm = x_ref.shape[0]
    ln = ln_ref[...]
    for s in range(tm // TOK_TILE):
        rows = slice(s * TOK_TILE, (s + 1) * TOK_TILE)
        t = tab_ref[_tab_row(pl.program_id(0) * (tm // TOK_TILE) + s, tpb)]
        xn, h = _residual_ln(x_ref[rows], a[rows], t, ln, alpha)
        xo_ref[rows] = xn
        h_ref[rows] = _pack_bf16(h)
        lt = _dot_bf16x3(rw_ref[...], h, ((1,), (1,))) + rb_ref[...]
        idx, gates = _top4_softmax(lt)
        idx_ref[:, rows] = idx
        gate_ref[:, rows] = gates


def _post(acts, ws, x, tab, ln, rw_t, rb, tpb, alpha, gla=False):
    n, d = x.shape
    tm = _row_tile(n)
    e = rw_t.shape[0]
    row = lambda i: (i, 0)
    fixed = lambda i: (0, 0)
    return pl.pallas_call(
        functools.partial(_post_kernel, gla, len(acts), tpb, alpha),
        grid=(n // tm,),
        in_specs=([pl.BlockSpec((tm, a.shape[1]), row) for a in acts]
                  + [pl.BlockSpec(w.shape, fixed) for w in ws]
                  + [pl.BlockSpec((tm, d), row),
                     pl.BlockSpec(tab.shape, lambda i: (0, 0, 0)),
                     pl.BlockSpec(ln.shape, fixed),
                     pl.BlockSpec((e, d), fixed),
                     pl.BlockSpec((e, 1), fixed)]),
        out_specs=[pl.BlockSpec((tm, d), row), pl.BlockSpec((tm, d // 2), row),
                   pl.BlockSpec((TOP_K, tm), lambda i: (0, i)), pl.BlockSpec((TOP_K, tm), lambda i: (0, i))],
        out_shape=[jax.ShapeDtypeStruct((n, d), F32), jax.ShapeDtypeStruct((n, d // 2), jnp.uint32),
                   jax.ShapeDtypeStruct((TOP_K, n), jnp.int32), jax.ShapeDtypeStruct((TOP_K, n), F32)],
        compiler_params=_cparams(("arbitrary",)),
        name="post",
    )(*acts, *ws, x, tab, ln, rw_t, rb)


def _rank_kernel(idx_ref, tri_ref, rank_ref, cnt_ref, carry_ref):
    @pl.when(pl.program_id(0) == 0)
    def _():
        carry_ref[...] = jnp.zeros_like(carry_ref)

    idx = idx_ref[...]
    e = carry_ref.shape[0]
    tr = idx.shape[1]
    io = lax.broadcasted_iota(jnp.int32, (e, tr), 0)
    chosen = jnp.zeros((e, tr), F32)
    for k in range(TOP_K):
        chosen = chosen + (idx[k:k + 1] == io).astype(F32)
    cum = jnp.dot(chosen.astype(BF16), tri_ref[...], preferred_element_type=F32)
    base = carry_ref[:, 0:1]
    excl = base + cum - chosen
    ranks = [jnp.sum(jnp.where(idx[k:k + 1] == io, excl, 0.0), axis=0, keepdims=True) for k in range(TOP_K)]
    rank_ref[...] = jnp.concatenate(ranks, axis=0).astype(jnp.int32)
    carry_ref[...] = carry_ref[...] + jnp.sum(chosen, axis=1, keepdims=True)
    cnt_ref[...] = carry_ref[...]


def _rank(idx_t, n_experts):
    n = idx_t.shape[1]
    tr = _row_tile(n)
    tri = (np.arange(tr)[:, None] <= np.arange(tr)[None, :]).astype(np.float32)
    return pl.pallas_call(
        _rank_kernel,
        grid=(n // tr,),
        in_specs=[pl.BlockSpec((TOP_K, tr), lambda i: (0, i)),
                  pl.BlockSpec((tr, tr), lambda i: (0, 0))],
        out_specs=[pl.BlockSpec((TOP_K, tr), lambda i: (0, i)),
                   pl.BlockSpec((n_experts, LANES), lambda i: (0, 0))],
        out_shape=[jax.ShapeDtypeStruct((TOP_K, n), jnp.int32),
                   jax.ShapeDtypeStruct((n_experts, LANES), F32)],
        scratch_shapes=[pltpu.VMEM((n_experts, LANES), F32)],
        compiler_params=_cparams(("arbitrary",)),
        name="rank",
    )(idx_t, jnp.asarray(tri, BF16))


def _expert_kernel(be_ref, rows_ref, x_ref, w1_ref, b1_ref, w2_ref, b2_ref, o_ref, w1b_ref, w2b_ref):
    i = pl.program_id(0)
    used = rows_ref[i] > 0
    new_expert = jnp.logical_or(i == 0, be_ref[i] != be_ref[jnp.maximum(i - 1, 0)])

    @pl.when(jnp.logical_and(used, new_expert))
    def _():
        w1b_ref[...] = w1_ref[0].astype(BF16)
        w2b_ref[...] = w2_ref[0].astype(BF16)

    @pl.when(used)
    def _():
        x_lo, x_hi = _unpack_bf16(x_ref[...])
        kh = x_lo.shape[1]
        hid = (jnp.dot(x_lo.astype(BF16), w1b_ref[:kh], preferred_element_type=F32)
               + jnp.dot(x_hi.astype(BF16), w1b_ref[kh:], preferred_element_type=F32) + b1_ref[0])
        half = hid.shape[1] // 2
        glu = jnp.minimum(hid[:, :half], SWIGLU_LIMIT)
        lin = jnp.clip(hid[:, half:], -SWIGLU_LIMIT, SWIGLU_LIMIT)
        act = glu * jax.nn.sigmoid(SWIGLU_ALPHA * glu) * (lin + 1.0)
        y = jnp.dot(act.astype(BF16), w2b_ref[...], preferred_element_type=F32) + b2_ref[0]
        o_ref[...] = _pack_bf16(y)

    @pl.when(jnp.logical_not(used))
    def _():
        o_ref[...] = jnp.zeros_like(o_ref)


def _experts(block_expert, block_rows, x_pad, layer, w1, b1, w2, b2):
    n_pad, dp = x_pad.shape
    depth, e, d, dh2 = w1.shape
    n_blocks = n_pad // EXPERT_BLOCK
    grid_spec = pltpu.PrefetchScalarGridSpec(
        num_scalar_prefetch=2,
        grid=(n_blocks,),
        in_specs=[pl.BlockSpec((EXPERT_BLOCK, dp), lambda i, be, nb: (i, 0)),
                  pl.BlockSpec((None, 1, d, dh2), lambda i, be, nb: (layer, be[i], 0, 0)),
                  pl.BlockSpec((None, 1, 1, dh2), lambda i, be, nb: (layer, be[i], 0, 0)),
                  pl.BlockSpec((None, 1, dh2 // 2, d), lambda i, be, nb: (layer, be[i], 0, 0)),
                  pl.BlockSpec((None, 1, 1, d), lambda i, be, nb: (layer, be[i], 0, 0))],
        out_specs=pl.BlockSpec((EXPERT_BLOCK, dp), lambda i, be, nb: (i, 0)),
        scratch_shapes=[pltpu.VMEM((d, dh2), BF16), pltpu.VMEM((dh2 // 2, d), BF16)],
    )
    return pl.pallas_call(
        _expert_kernel,
        grid_spec=grid_spec,
        out_shape=jax.ShapeDtypeStruct((n_pad, dp), jnp.uint32),
        compiler_params=pltpu.CompilerParams(dimension_semantics=("arbitrary",),
                                             vmem_limit_bytes=EXPERT_VMEM_LIMIT),
        name="experts",
    )(block_expert, block_rows, x_pad, w1, b1.reshape(depth, e, 1, dh2), w2, b2.reshape(depth, e, 1, d))


def _stream_tile(i, tpb, latent_only):
    return (i // (tpb - 1)) * tpb + i % (tpb - 1) if latent_only else i


def _combine_kernel(tpb, alpha, splits, y_ref, gate_ref, x_ref, tab_ref, ln_ref, *refs):
    latent_only = splits is None
    g = gate_ref[...]
    y_lo, y_hi = None, None
    for k in range(TOP_K):
        lo, hi = _unpack_bf16(y_ref[k])
        gk = g[:, k:k + 1]
        y_lo = lo * gk if y_lo is None else y_lo + lo * gk
        y_hi = hi * gk if y_hi is None else y_hi + hi * gk
    y = jnp.concatenate([y_lo, y_hi], axis=1)
    ln = ln_ref[...]
    n_sub = x_ref.shape[0] // TOK_TILE
    xns, hs = [], []
    for s in range(n_sub):
        rows = slice(s * TOK_TILE, (s + 1) * TOK_TILE)
        t = tab_ref[_tab_row(_stream_tile(pl.program_id(0) * n_sub + s, tpb, latent_only), tpb)]
        xn, h = _residual_ln(x_ref[rows], y[rows], t, ln, alpha)
        xns.append(xn)
        hs.append(h)
    xn = jnp.concatenate(xns, axis=0)
    if latent_only:
        refs[0][...] = xn
    else:
        w_ref, xo_ref = refs[:2]
        xo_ref[...] = xn
        _emit_proj(jnp.concatenate(hs, axis=0), w_ref, splits, refs[2:])


def _combine(y_g, gates, x, tab, ln, tpb, alpha, plan):
    n, d = x.shape
    latent_only = plan is None
    tm = TOK_TILE if latent_only else _row_tile(n)
    tiles = n // tm
    steps = tiles // tpb * (tpb - 1) if latent_only else tiles
    src = lambda i: (_stream_tile(i, tpb, latent_only), 0)
    in_specs = [pl.BlockSpec((TOP_K, tm, d // 2), lambda i: (0, _stream_tile(i, tpb, latent_only), 0)),
                pl.BlockSpec((tm, TOP_K), src),
                pl.BlockSpec((tm, d), src),
                pl.BlockSpec(tab.shape, lambda i: (0, 0, 0)),
                pl.BlockSpec(ln.shape, lambda i: (0, 0))]
    out_specs = [pl.BlockSpec((tm, d), lambda i: (i, 0))]
    out_shape = [jax.ShapeDtypeStruct((steps * tm, d), F32)]
    operands = [y_g, gates, x, tab, ln]
    if not latent_only:
        w_spec, p_specs, p_shape = _proj_specs(n, tm, plan)
        in_specs.append(w_spec)
        out_specs += p_specs
        out_shape += p_shape
        operands.append(plan[0])
    return pl.pallas_call(
        functools.partial(_combine_kernel, tpb, alpha, None if latent_only else tuple(plan[1])),
        grid=(steps,),
        in_specs=in_specs,
        out_specs=out_specs,
        out_shape=out_shape,
        compiler_params=_cparams(("arbitrary",)),
        name="combine",
    )(*operands)


def _sc_mesh():
    return plsc.VectorSubcoreMesh(core_axis_name="c", subcore_axis_name="s")


def _sc_split(rows, mesh):
    workers = mesh.num_cores * mesh.num_subcores
    per = rows // workers
    assert per * workers == rows
    chunk = SC_CHUNK if per % SC_CHUNK == 0 else 8
    assert per % chunk == 0
    return per, chunk


def _sc_scatter_rows(x, idx, n_out):
    r, c = x.shape
    mesh = _sc_mesh()
    per, chunk = _sc_split(r, mesh)

    n_chunks = per // chunk

    @functools.partial(pl.kernel, out_type=jax.ShapeDtypeStruct((n_out, c), x.dtype), mesh=mesh,
                       scratch_types=[pltpu.VMEM((chunk, c), x.dtype), pltpu.SemaphoreType.DMA] * 2
                       + [pltpu.VMEM((chunk,), jnp.int32)] * TOP_K + [pltpu.SemaphoreType.DMA])
    def scatter(x_hbm, i_hbm, o_hbm, rows_a, sem_a, rows_b, sem_b, *rest):
        idx_vs, sem_s = rest[:TOP_K], rest[TOP_K]
        base = (lax.axis_index("s") * mesh.num_cores + lax.axis_index("c")) * per
        slots = ((rows_a, sem_a), (rows_b, sem_b))

        def load(j, slot):
            rows_v, sem = slot
            off = pl.multiple_of(base + j * chunk, chunk)
            return pltpu.make_async_copy(x_hbm.at[pl.ds(off, chunk)], rows_v, sem)

        def scatter_chunk(j, slot, prefetch):
            rows_v, _ = slot
            off = pl.multiple_of(base + j * chunk, chunk)
            load(j, slot).wait()
            for k in range(TOP_K):
                pltpu.sync_copy(i_hbm.at[pl.ds(k * r + off, chunk)], idx_vs[k])
            for k in range(TOP_K):
                pltpu.async_copy(rows_v, o_hbm.at[idx_vs[k]], sem_s)
            prefetch()
            for k in range(TOP_K):
                pltpu.make_async_copy(rows_v, o_hbm.at[idx_vs[k]], sem_s).wait()

        load(0, slots[0]).start()

        @pl.loop(0, n_chunks // 2)
        def _(p):
            j = 2 * p
            scatter_chunk(j, slots[0], lambda: load(j + 1, slots[1]).start())

            def next_even():
                @pl.when(j + 2 < n_chunks)
                def _():
                    load(j + 2, slots[0]).start()

            scatter_chunk(j + 1, slots[1], next_even)

        if n_chunks % 2:
            scatter_chunk(n_chunks - 1, slots[0], lambda: None)

    return scatter(x, idx)


def _sc_gather_rows(table, idx):
    m = idx.shape[0]
    c = table.shape[1]
    mesh = _sc_mesh()
    per, chunk = _sc_split(m, mesh)

    n_chunks = per // chunk
    slot_types = [pltpu.VMEM((chunk,), jnp.int32), pltpu.VMEM((chunk, c), table.dtype), pltpu.SemaphoreType.DMA]

    @functools.partial(pl.kernel, out_type=jax.ShapeDtypeStruct((m, c), table.dtype), mesh=mesh,
                       scratch_types=slot_types * 2)
    def gather(t_hbm, i_hbm, o_hbm, idx_a, rows_a, sem_a, idx_b, rows_b, sem_b):
        base = (lax.axis_index("s") * mesh.num_cores + lax.axis_index("c")) * per
        slots = ((idx_a, rows_a, sem_a), (idx_b, rows_b, sem_b))

        def start(j, slot):
            idx_v, rows_v, sem = slot
            off = pl.multiple_of(base + j * chunk, chunk)
            pltpu.sync_copy(i_hbm.at[pl.ds(off, chunk)], idx_v)
            pltpu.async_copy(t_hbm.at[idx_v], rows_v, sem)

        def finish(j, slot):
            idx_v, rows_v, sem = slot
            off = pl.multiple_of(base + j * chunk, chunk)
            pltpu.make_async_copy(t_hbm.at[idx_v], rows_v, sem).wait()
            pltpu.sync_copy(rows_v, o_hbm.at[pl.ds(off, chunk)])

        start(0, slots[0])

        @pl.loop(0, n_chunks // 2)
        def _(p):
            j = 2 * p
            start(j + 1, slots[1])
            finish(j, slots[0])

            @pl.when(j + 2 < n_chunks)
            def _():
                start(j + 2, slots[0])

            finish(j + 1, slots[1])

        if n_chunks % 2:
            finish(n_chunks - 1, slots[0])

    return gather(table, idx)


def _moe(h, idx_t, layer, w1, b1, w2, b2):
    n, dp = h.shape
    e = w1.shape[1]
    m = n * TOP_K
    rank_t, cnt = _rank(idx_t, e)
    sizes = cnt[:, 0].astype(jnp.int32)
    padded = (sizes + EXPERT_BLOCK - 1) // EXPERT_BLOCK * EXPERT_BLOCK
    pad_ends = jnp.cumsum(padded)
    pad_starts = pad_ends - padded
    ids = jnp.arange(e, dtype=jnp.int32)[:, None, None]
    dest_t = jnp.sum(jnp.where(idx_t[None] == ids, pad_starts[:, None, None], 0), axis=0) + rank_t
    dest = dest_t.reshape(-1)
    n_blocks = (m + e * (EXPERT_BLOCK - 1)) // EXPERT_BLOCK + 1
    n_pad = n_blocks * EXPERT_BLOCK
    block_start = jnp.arange(n_blocks, dtype=jnp.int32) * EXPERT_BLOCK
    block_expert = jnp.minimum(jnp.sum((pad_ends[None, :] <= block_start[:, None]).astype(jnp.int32), axis=1), e - 1)
    group_end = jnp.sum(jnp.where(block_expert[:, None] == jnp.arange(e, dtype=jnp.int32)[None, :],
                                  (pad_starts + sizes)[None, :], 0), axis=1)
    block_rows = jnp.clip(group_end - block_start, 0, EXPERT_BLOCK).astype(jnp.int32)
    x_pad = _sc_scatter_rows(h, dest, n_pad)
    y_pad = _experts(block_expert, block_rows, x_pad, layer, w1, b1, w2, b2)
    return _sc_gather_rows(y_pad, dest).reshape(TOP_K, n, dp)


def _rope_tables(t_lat, n_ctx):
    t = jnp.arange(t_lat)
    row = (t // GRID_W).astype(F32)
    col = (t % GRID_W).astype(F32)
    nf = GLA_DK // 4
    freqs = ROPE_BASE ** (-jnp.arange(nf, dtype=F32) / nf)
    ang = jnp.concatenate([row[:, None] * freqs, col[:, None] * freqs], axis=-1)
    cos, sin = jnp.cos(ang), jnp.sin(ang)
    cos2 = jnp.concatenate([cos, cos], axis=-1)
    sin2 = jnp.concatenate([-sin, sin], axis=-1)
    return (jnp.concatenate([cos2, jnp.ones((n_ctx, GLA_DK), F32)], axis=0),
            jnp.concatenate([sin2, jnp.zeros((n_ctx, GLA_DK), F32)], axis=0))


def _chunk_tri(tg, rev):
    t = np.arange(tg)
    same = (t[:, None] // GLA_CHUNK) == (t[None, :] // GLA_CHUNK)
    side = (t[None, :] >= t[:, None]) if rev else (t[None, :] <= t[:, None])
    return jnp.asarray((same & side).astype(np.float32), BF16)


def _table(mods, rows, batch):
    lat = jnp.stack([mods[:batch, r] for r in rows], axis=1)
    ctx = jnp.broadcast_to(jnp.stack([mods[batch, r] for r in rows], axis=0)[None], lat.shape)
    tab = jnp.stack([lat, ctx], axis=1).reshape(2 * batch, len(rows), -1)
    return jnp.pad(tab, ((0, 0), (0, 8 - len(rows)), (0, 0)))


@jax.jit
def _forward(x, c, ctx, c_ctx, ada_w, ada_b, ln_g, ln_b, ab_w_in, ab_pool_w, ab_pool_scale, ab_rpb,
             ab_w_out, gla_w_in, gla_w_gate, gla_b_gate, gla_norm_g, gla_w_out, router_w, router_b,
             exp_w1, exp_b1, exp_w2, exp_b2):
    batch, t_lat, d = x.shape
    n_ctx = ctx.shape[1]
    depth = ada_w.shape[0]
    assert d == D_MODEL and n_ctx == TOK_TILE and t_lat % TOK_TILE == 0
    rows = t_lat // GRID_W
    assert rows % NA_QROWS == 0 and rows >= NA_KROWS + NA_QROWS
    n_lat = t_lat // TOK_TILE
    tpb = n_lat + 1
    l = t_lat + n_ctx
    n = batch * l
    alpha = (2.0 * depth) ** 0.25

    cc = jnp.concatenate([c, c_ctx[None], jnp.zeros((16 - batch - 1, d), F32)], axis=0)
    mods = _mods(cc, ada_w, ada_b).reshape(depth, 16, N_MOD, d)

    def in_proj_plan(i):
        if i % 2 == 0:
            return (ab_w_in[i // 2].astype(BF16),
                    [(0, POOL_WIDTH), (POOL_WIDTH, POOL_WIDTH + 3 * NA_WIDTH)], [F32, BF16])
        edges = (0, 2 * GLA_QK, 2 * GLA_QK + GLA_V, 2 * GLA_QK + 2 * GLA_V, 2 * GLA_QK + 2 * GLA_V + 2 * GATE_RANK)
        return gla_w_in[i // 2].astype(BF16), list(zip(edges[:-1], edges[1:])), [F32, BF16, F32, F32]

    z, *projected = _modulate(x.reshape(batch * t_lat, d), ctx.reshape(batch * n_ctx, d),
                              _table(mods[0], (1, 0), batch), tpb, in_proj_plan(0))
    cos2, sin2 = _rope_tables(t_lat, n_ctx)

    for i in range(depth):
        j = i // 2
        last = i == depth - 1
        tab1 = _table(mods[i], (2, 4, 3), batch)
        ln1 = jnp.stack([ln_g[i, 0], ln_b[i, 0]])
        ln2 = jnp.stack([ln_g[i, 1], ln_b[i, 1]])
        rw_t = router_w[i].T
        rb = router_b[i][:, None]
        if i % 2 == 0:
            u, qkv = projected
            w_blk = jax.scipy.linalg.block_diag(*[ab_pool_w[j, g] for g in range(len(POOL_WINDOWS))])
            bias = _na_bias_tables(ab_rpb[j], rows)
            attn, pooled = _na_pool(qkv.reshape(batch, l, 3 * NA_WIDTH), bias, u.reshape(batch, l, POOL_WIDTH),
                                    w_blk.astype(BF16), ab_pool_scale[j][None, :], n_lat, t_lat, n_ctx)
            w_out = ab_w_out[j].astype(BF16)
            acts = [pooled.reshape(n, POOL_WIDTH), attn.reshape(n, NA_WIDTH)]
            ws = [w_out[:POOL_WIDTH], w_out[POOL_WIDTH:]]
        else:
            qk, v, r, g = projected
            tri = jnp.stack([_chunk_tri(TOK_TILE, False), _chunk_tri(TOK_TILE, True)])
            prep = _gla_prep(qk, g, cos2, sin2, gla_w_gate[j], gla_b_gate[j][:, None, :], tri, tpb)
            o_f, o_b = _gla_scan(prep, v, batch, tpb)
            acts = [o_f, o_b, r]
            ws = [gla_norm_g[j][None, :], gla_w_out[j].astype(BF16)]
        z, h, idx_t, gates_t = _post(acts, ws, z, tab1, ln1, rw_t, rb, tpb, alpha, gla=i % 2 == 1)
        y_g = _moe(h, idx_t, i, exp_w1, exp_b1, exp_w2, exp_b2)
        nxt = mods[i + 1] if not last else mods[i]
        tab2 = _table(jnp.concatenate([mods[i][:, 5:6], nxt[:, 1:2], nxt[:, 0:1]], axis=1), (0, 1, 2), batch)
        if last:
            (out,) = _combine(y_g, gates_t.T, z, tab2, ln2, tpb, alpha, None)
            return out.reshape(batch, t_lat, d)
        z, *projected = _combine(y_g, gates_t.T, z, tab2, ln2, tpb, alpha, in_proj_plan(i + 1))


def kernel(x, c, ctx, c_ctx, ada_w, ada_b, ln_g, ln_b, ab_w_in, ab_pool_w, ab_pool_scale, ab_rpb, ab_w_out,
           gla_w_in, gla_w_gate, gla_b_gate, gla_norm_g, gla_w_out, router_w, router_b, exp_w1, exp_b1, exp_w2,
           exp_b2):
    return _forward(x, c, ctx, c_ctx, ada_w, ada_b, ln_g, ln_b, ab_w_in, ab_pool_w, ab_pool_scale, ab_rpb,
                    ab_w_out, gla_w_in, gla_w_gate, gla_b_gate, gla_norm_g, gla_w_out, router_w, router_b,
                    exp_w1, exp_b1, exp_w2, exp_b2)
```

```python
import functools
import math

import numpy as np
import jax
import jax.numpy as jnp
from jax import lax
from jax.experimental import pallas as pl
from jax.experimental.pallas import tpu as pltpu
from jax.experimental.pallas import tpu_sc as plsc

F32 = jnp.float32
BF16 = jnp.bfloat16
HIGHEST = lax.Precision.HIGHEST

D_MODEL = 1024
GRID_W = 64
N_MOD = 6
POOL_WINDOWS = (2, 4, 8, 16)
POOL_WIDTH = D_MODEL // 4
POOL_GROUP_DIM = POOL_WIDTH // len(POOL_WINDOWS)
POOL_HALO = max(POOL_WINDOWS) // 2
NA_HEAD_DIM = 64
NA_HEADS = (D_MODEL - POOL_WIDTH) // NA_HEAD_DIM
NA_WIDTH = NA_HEADS * NA_HEAD_DIM
WIN_H = 8
WIN_W = 16
GLA_HEADS = 4
GLA_DK = D_MODEL // 2 // GLA_HEADS
GLA_DV = D_MODEL // GLA_HEADS
GATE_RANK = 16
GATE_NORM = 16.0
GLA_CHUNK = 64
GLA_STRIP = 8
GLA_QK = GLA_HEADS * GLA_DK
GLA_V = GLA_HEADS * GLA_DV
ROPE_BASE = 10000.0
TOP_K = 4
SWIGLU_LIMIT = 7.0
SWIGLU_ALPHA = 1.702
LN_EPS = 1e-5
RMS_EPS = 1e-6
NEG_INF = -1e30
LOG2_E = math.log2(math.e)

LANES = 128
TOK_TILE = 256
NA_QROWS = 4
NA_KROWS = 12
NA_PAIRS_PER_STEP = 6
SCAN_GROUP = 2
VMEM_LIMIT = 48 * 1024 * 1024
EXPERT_VMEM_LIMIT = 56 * 1024 * 1024
EXPERT_BLOCK = 512
SC_CHUNK = 64


def _cparams(sem):
    return pltpu.CompilerParams(dimension_semantics=sem, vmem_limit_bytes=VMEM_LIMIT)


def _dot_bf16x3(a, b, dims):
    a_hi = a.astype(BF16)
    a_lo = (a - a_hi.astype(F32)).astype(BF16)
    b_hi = b.astype(BF16)
    b_lo = (b - b_hi.astype(F32)).astype(BF16)
    dg = functools.partial(lax.dot_general, dimension_numbers=(dims, ((), ())), preferred_element_type=F32)
    return dg(a_hi, b_hi) + dg(a_hi, b_lo) + dg(a_lo, b_hi)


def _row_tile(n):
    return 2 * TOK_TILE if n % (2 * TOK_TILE) == 0 else TOK_TILE


def _mods_kernel(c_ref, w_ref, b_ref, o_ref):
    cv = c_ref[...]
    sc = cv * jax.nn.sigmoid(cv)
    o_ref[0] = jnp.dot(sc, w_ref[0], precision=HIGHEST, preferred_element_type=F32) + b_ref[0]


def _mods(cc, ada_w, ada_b):
    depth, d, n = ada_w.shape
    r = cc.shape[0]
    tn = n // 4
    return pl.pallas_call(
        _mods_kernel,
        grid=(depth, n // tn),
        in_specs=[pl.BlockSpec((r, d), lambda i, j: (0, 0)),
                  pl.BlockSpec((1, d, tn), lambda i, j: (i, 0, j)),
                  pl.BlockSpec((1, 1, tn), lambda i, j: (i, 0, j))],
        out_specs=pl.BlockSpec((1, r, tn), lambda i, j: (i, 0, j)),
        out_shape=jax.ShapeDtypeStruct((depth, r, n), F32),
        compiler_params=_cparams(("arbitrary", "arbitrary")),
        name="mods",
    )(cc, ada_w, ada_b.reshape(depth, 1, n))


def _tab_row(g, tpb):
    return (g // tpb) * 2 + (g % tpb == tpb - 1).astype(jnp.int32)


def _modulate_kernel(tpb, splits, x_ref, ctx_ref, tab_ref, w_ref, z_ref, *out_refs):
    i = pl.program_id(0)
    is_ctx = i % tpb == tpb - 1
    z = jnp.where(is_ctx, ctx_ref[...], x_ref[...])
    t = tab_ref[_tab_row(i, tpb)]
    z_ref[...] = z
    _emit_proj(z * (1.0 + t[0:1]) + t[1:2], w_ref, splits, out_refs)


def _modulate(x, ctx, tab, tpb, plan):
    d = x.shape[1]
    n_lat = tpb - 1
    n = x.shape[0] + ctx.shape[0]
    w_spec, p_specs, p_shape = _proj_specs(n, TOK_TILE, plan)
    return pl.pallas_call(
        functools.partial(_modulate_kernel, tpb, tuple(plan[1])),
        grid=(n // TOK_TILE,),
        in_specs=[pl.BlockSpec((TOK_TILE, d), lambda i: ((i // tpb) * n_lat + jnp.minimum(i % tpb, n_lat - 1), 0)),
                  pl.BlockSpec((TOK_TILE, d), lambda i: (i // tpb, 0)),
                  pl.BlockSpec(tab.shape, lambda i: (0, 0, 0)),
                  w_spec],
        out_specs=[pl.BlockSpec((TOK_TILE, d), lambda i: (i, 0))] + p_specs,
        out_shape=[jax.ShapeDtypeStruct((n, d), F32)] + p_shape,
        compiler_params=_cparams(("arbitrary",)),
        name="modulate",
    )(x, ctx, tab, plan[0])


def _emit_proj(h, w_ref, splits, out_refs):
    hb = h.astype(BF16)
    for (a, b), o_ref in zip(splits, out_refs):
        o_ref[...] = jnp.dot(hb, w_ref[:, a:b], preferred_element_type=F32).astype(o_ref.dtype)


def _proj_specs(n, tm, plan):
    w, splits, dtypes = plan
    w_spec = pl.BlockSpec(w.shape, lambda i: (0, 0), pipeline_mode=pl.Buffered(1))
    out_specs = [pl.BlockSpec((tm, b - a), lambda i: (i, 0)) for a, b in splits]
    out_shape = [jax.ShapeDtypeStruct((n, b - a), dt) for (a, b), dt in zip(splits, dtypes)]
    return w_spec, out_specs, out_shape


def _pool_tile(n_lat, t_lat, t_ctx, prev_ref, cur_ref, next_ref, w_ref, scale_ref, o_ref, halo_ref):
    j = pl.program_id(1)
    is_ctx = j == n_lat
    has_prev = jnp.logical_and(j > 0, jnp.logical_not(is_ctx))
    has_next = j < n_lat - 1
    cur = cur_ref[0]
    hl = POOL_HALO
    halo_ref[0:hl] = jnp.where(has_prev, prev_ref[0, TOK_TILE - hl:TOK_TILE], 0.0)
    halo_ref[hl:hl + TOK_TILE] = cur
    halo_ref[hl + TOK_TILE:2 * hl + TOK_TILE] = jnp.where(has_next, next_ref[0, 0:hl], 0.0)

    shape = cur.shape
    lane = lax.broadcasted_iota(jnp.int32, shape, 1)
    group = lane // POOL_GROUP_DIM
    half = jnp.ones(shape, jnp.int32)
    for gi, wdw in enumerate(POOL_WINDOWS):
        half = jnp.where(group == gi, wdw // 2, half)
    acc = jnp.zeros(shape, F32)
    for off in range(-hl, hl):
        v = halo_ref[hl + off:hl + off + TOK_TILE]
        inside = (half >= -off) if off < 0 else (half > off)
        acc = acc + jnp.where(inside, v, 0.0)
    pos0 = jnp.where(is_ctx, 0, j * TOK_TILE)
    seq = jnp.where(is_ctx, t_ctx, t_lat)
    t = pos0 + lax.broadcasted_iota(jnp.int32, shape, 0)
    cnt = jnp.minimum(t + half, seq) - jnp.maximum(t - half, 0)
    pooled = acc / cnt.astype(F32) - cur
    y = jnp.dot(pooled.astype(BF16), w_ref[...], preferred_element_type=F32) * scale_ref[...]
    o_ref[0] = y.astype(o_ref.dtype)


def _na_bias_tables(rpb, rows):
    n_i = rows // NA_QROWS
    heads = rpb.shape[0]
    a = np.arange(NA_QROWS)
    kr = np.arange(NA_KROWS)
    cq = np.arange(GRID_W)
    ws = np.clip(cq - WIN_W // 2, 0, GRID_W - WIN_W)
    ok_col = (cq[None, :] >= ws[:, None]) & (cq[None, :] < ws[:, None] + WIN_W)
    dcol = np.clip(cq[None, :] - cq[:, None] + WIN_W - 1, 0, 2 * WIN_W - 2)
    oh_col = (dcol[..., None] == np.arange(2 * WIN_W - 1)).astype(np.float32)
    tabs = []
    for i in (0, 1, n_i - 1):
        start = int(np.clip(NA_QROWS * i - WIN_H // 2, 0, rows - NA_KROWS))
        r = NA_QROWS * i + a
        krow = start + kr
        rs = np.clip(r - WIN_H // 2, 0, rows - WIN_H)
        ok_row = (krow[None, :] >= rs[:, None]) & (krow[None, :] < rs[:, None] + WIN_H)
        drow = np.clip(krow[None, :] - r[:, None] + WIN_H - 1, 0, 2 * WIN_H - 2)
        oh_row = (drow[..., None] == np.arange(2 * WIN_H - 1)).astype(np.float32)
        by_row = jnp.einsum('hrc,akr->hakc', rpb, oh_row, precision=HIGHEST)
        bias = jnp.einsum('hakc,qjc->haqkj', by_row, oh_col, precision=HIGHEST)
        ok = ok_row[:, None, :, None] & ok_col[None, :, None, :]
        tabs.append(jnp.where(ok[None], bias, NEG_INF).reshape(heads, TOK_TILE, NA_KROWS * GRID_W))
    tabs.append(jnp.full_like(tabs[0], NEG_INF))
    return jnp.stack(tabs).astype(F32)


def _na_kernel(pool_args, q_ref, k0_ref, k1_ref, k2_ref, kc_ref, v0_ref, v1_ref, v2_ref, vc_ref, bias_ref,
               uprev_ref, ucur_ref, unext_ref, pw_ref, ps_ref, o_ref, po_ref, halo_ref):
    _pool_tile(*pool_args, uprev_ref, ucur_ref, unext_ref, pw_ref, ps_ref, po_ref, halo_ref)
    k_refs = (k0_ref, k1_ref, k2_ref, kc_ref)
    v_refs = (v0_ref, v1_ref, v2_ref, vc_ref)
    n_band = len(k_refs) - 1
    lane = lax.broadcasted_iota(jnp.int32, (TOK_TILE, LANES), 1)
    first = lane < NA_HEAD_DIM
    for pair in range(NA_PAIRS_PER_STEP):
        cols = slice(pair * LANES, (pair + 1) * LANES)
        q = q_ref[0, :, cols]
        outs = []
        for hh in range(2):
            mine = first if hh == 0 else jnp.logical_not(first)
            qm = jnp.where(mine, q, jnp.zeros_like(q)) * NA_HEAD_DIM ** -0.5
            scores = []
            for j, k_ref in enumerate(k_refs):
                s = lax.dot_general(qm, k_ref[0, :, cols], (((1,), (1,)), ((), ())), preferred_element_type=F32)
                if j < n_band:
                    s = s + bias_ref[0, 2 * pair + hh, :, j * TOK_TILE:(j + 1) * TOK_TILE]
                scores.append(s)
            m = functools.reduce(jnp.maximum, [jnp.max(s, axis=-1, keepdims=True) for s in scores])
            l = 0.0
            o = 0.0
            for s, v_ref in zip(scores, v_refs):
                p = jnp.exp(s - m)
                l = l + jnp.sum(p, axis=-1, keepdims=True)
                o = o + jnp.dot(p.astype(BF16), v_ref[0, :, cols], preferred_element_type=F32)
            outs.append(o / l)
        o_ref[0, :, cols] = jnp.where(first, outs[0], outs[1]).astype(o_ref.dtype)


def _na_pool(qkv, bias, u, pool_w, pool_scale, n_lat, t_lat, t_ctx):
    b, l, _ = qkv.shape
    pw = u.shape[-1]
    tpb = l // TOK_TILE
    n_groups = NA_WIDTH // (LANES * NA_PAIRS_PER_STEP)
    assert n_groups == 1
    blk = (1, TOK_TILE, LANES * NA_PAIRS_PER_STEP)
    ublk = (1, TOK_TILE, pw)

    def kstart(i):
        return jnp.clip(i - 1, 0, n_lat - NA_KROWS // NA_QROWS)

    def btype(i):
        return jnp.where(i == 0, 0, jnp.where(i == n_lat - 1, 2, jnp.where(i == n_lat, 3, 1)))

    def kv_spec(col0, j):
        return pl.BlockSpec(blk, lambda hp, i, bi: (bi, kstart(i) + j, col0 + hp))

    def ctx_spec(col0):
        return pl.BlockSpec(blk, lambda hp, i, bi: (bi, n_lat, col0 + hp))

    nk = bias.shape[-1]
    return pl.pallas_call(
        functools.partial(_na_kernel, (n_lat, t_lat, t_ctx)),
        grid=(n_groups, tpb, b),
        in_specs=[pl.BlockSpec(blk, lambda hp, i, bi: (bi, i, hp)),
                  kv_spec(n_groups, 0), kv_spec(n_groups, 1), kv_spec(n_groups, 2), ctx_spec(n_groups),
                  kv_spec(2 * n_groups, 0), kv_spec(2 * n_groups, 1), kv_spec(2 * n_groups, 2),
                  ctx_spec(2 * n_groups),
                  pl.BlockSpec((1, 2 * NA_PAIRS_PER_STEP, TOK_TILE, nk), lambda hp, i, bi: (btype(i), hp, 0, 0)),
                  pl.BlockSpec(ublk, lambda hp, i, bi: (bi, jnp.maximum(i - 1, 0), 0)),
                  pl.BlockSpec(ublk, lambda hp, i, bi: (bi, i, 0)),
                  pl.BlockSpec(ublk, lambda hp, i, bi: (bi, jnp.minimum(i + 1, tpb - 1), 0)),
                  pl.BlockSpec((pw, pw), lambda hp, i, bi: (0, 0)),
                  pl.BlockSpec((1, pw), lambda hp, i, bi: (0, 0))],
        out_specs=[pl.BlockSpec(blk, lambda hp, i, bi: (bi, i, hp)),
                   pl.BlockSpec(ublk, lambda hp, i, bi: (bi, i, 0))],
        out_shape=[jax.ShapeDtypeStruct((b, l, NA_WIDTH), BF16), jax.ShapeDtypeStruct((b, l, pw), BF16)],
        scratch_shapes=[pltpu.VMEM((TOK_TILE + 2 * POOL_HALO, pw), F32)],
        compiler_params=_cparams(("arbitrary", "arbitrary", "arbitrary")),
        name="na_pool",
    )(qkv, qkv, qkv, qkv, qkv, qkv, qkv, qkv, qkv, bias, u, u, u, pool_w, pool_scale)


def _log_sigmoid(z):
    return jnp.minimum(z, 0.0) - jnp.log(1.0 + jnp.exp(-jnp.abs(z)))


def _gla_prep_kernel(q_ref, k_ref, g_ref, cos_ref, sin_ref, wg_ref, bg_ref, tri_ref,
                     qe_ref, kd_ref, a_ref, gd_ref, b_scr, qr_scr, kr_scr):
    tg = q_ref.shape[0]
    head_cols = [slice(hd * GLA_DK, (hd + 1) * GLA_DK) for hd in range(GLA_HEADS)]
    cosv = cos_ref[...]
    sinv = sin_ref[...]
    rotated = []
    for hd, cols in enumerate(head_cols):
        q = q_ref[:, cols]
        k = k_ref[:, cols]
        qr = (q * cosv + pltpu.roll(q, GLA_DK // 2, 1) * sinv) * GLA_DK ** -0.5
        kr = k * cosv + pltpu.roll(k, GLA_DK // 2, 1) * sinv
        qr_scr[hd] = qr
        kr_scr[hd] = kr
        rotated.append((qr, kr))
    for rev in (0, 1):
        gg = g_ref[:, rev * GATE_RANK:(rev + 1) * GATE_RANK]
        z = _dot_bf16x3(gg, wg_ref[rev], ((1,), (0,))) + bg_ref[rev]
        la = _log_sigmoid(z) * (1.0 / GATE_NORM)
        tri = tri_ref[rev]
        b = jnp.zeros_like(la)
        rest = la
        for _ in range(3):
            piece = rest.astype(BF16)
            b = b + jnp.dot(tri, piece, preferred_element_type=F32)
            rest = rest - piece.astype(F32)
        for hd, cols in enumerate(head_cols):
            qr, kr = rotated[hd]
            bh = b[:, cols]
            b_scr[rev, hd] = bh * LOG2_E
            qe_ref[rev, :, cols] = (qr * jnp.exp(bh)).astype(qe_ref.dtype)
            for c in range(tg // GLA_CHUNK):
                r0 = c * GLA_CHUNK
                last = r0 if rev else r0 + GLA_CHUNK - 1
                tot = bh[last:last + 1]
                kd_ref[rev, r0:r0 + GLA_CHUNK, cols] = (
                    kr[r0:r0 + GLA_CHUNK] * jnp.exp(tot - bh[r0:r0 + GLA_CHUNK])).astype(kd_ref.dtype)
                gd_ref[rev, hd, 0, c:c + 1] = jnp.exp(tot)

    n_strip = GLA_CHUNK // GLA_STRIP
    colio = lax.broadcasted_iota(jnp.int32, (GLA_STRIP, GLA_CHUNK), 1)
    rowio = lax.broadcasted_iota(jnp.int32, (GLA_STRIP, GLA_CHUNK), 0)

    def strip_scores(rev, hd, c0, bch, qch, kch, u):
        lo, hi = u * GLA_STRIP, (u + 1) * GLA_STRIP
        bu, qu = bch[lo:hi], qch[lo:hi]
        krows, ref = (slice(hi, GLA_CHUNK), hi) if rev else (slice(0, lo), lo - 1)
        if krows.stop > krows.start:
            rb = b_scr[rev, hd, pl.ds(c0 + ref, 1), :]
            qt = qu * jnp.exp2(jnp.minimum(bu - rb, 0.0))
            kt = kch[krows] * jnp.exp2(jnp.minimum(rb - bch[krows], 0.0))
            pad = jnp.zeros((GLA_CHUNK - kt.shape[0], GLA_DK), F32)
            kt = jnp.concatenate([pad, kt] if rev else [kt, pad], axis=0)
            acc = lax.dot_general(qt.astype(BF16), kt.astype(BF16), (((1,), (1,)), ((), ())),
                                  preferred_element_type=F32)
        else:
            acc = jnp.zeros((GLA_STRIP, GLA_CHUNK), F32)
        for s in range(GLA_STRIP):
            ks = kr_scr[hd, pl.ds(c0 + lo + s, 1), :]
            bs = b_scr[rev, hd, pl.ds(c0 + lo + s, 1), :]
            col = jnp.sum(qu * ks * jnp.exp2(bu - bs), axis=1, keepdims=True)
            causal = (rowio <= s) if rev else (rowio >= s)
            acc = jnp.where(jnp.logical_and(colio == lo + s, causal), col, acc)
        return acc

    def chunk_scores(c, carry):
        c0 = pl.multiple_of(c * GLA_CHUNK, GLA_CHUNK)
        for hd in range(GLA_HEADS):
            qch = qr_scr[hd, pl.ds(c0, GLA_CHUNK), :]
            kch = kr_scr[hd, pl.ds(c0, GLA_CHUNK), :]
            for rev in (0, 1):
                bch = b_scr[rev, hd, pl.ds(c0, GLA_CHUNK), :]
                for u in range(0, n_strip, 2):
                    pair = jnp.concatenate([strip_scores(rev, hd, c0, bch, qch, kch, u),
                                            strip_scores(rev, hd, c0, bch, qch, kch, u + 1)], axis=0)
                    a_ref[rev, hd, pl.ds(c0 + u * GLA_STRIP, 2 * GLA_STRIP), :] = pair.astype(a_ref.dtype)
        return carry

    lax.fori_loop(0, tg // GLA_CHUNK, chunk_scores, 0)


def _gla_prep(qk, g, cos2, sin2, w_gate, b_gate, tri, tpb):
    n = qk.shape[0]
    tg = TOK_TILE
    nt = n // tg
    fixed3 = lambda t: (0, 0, 0)
    return pl.pallas_call(
        _gla_prep_kernel,
        grid=(nt,),
        in_specs=[pl.BlockSpec((tg, GLA_QK), lambda t: (t, 0)),
                  pl.BlockSpec((tg, GLA_QK), lambda t: (t, 1)),
                  pl.BlockSpec((tg, 2 * GATE_RANK), lambda t: (t, 0)),
                  pl.BlockSpec((tg, GLA_DK), lambda t: (t % tpb, 0)),
                  pl.BlockSpec((tg, GLA_DK), lambda t: (t % tpb, 0)),
                  pl.BlockSpec((2, GATE_RANK, GLA_QK), fixed3),
                  pl.BlockSpec((2, 1, GLA_QK), fixed3),
                  pl.BlockSpec((2, tg, tg), fixed3)],
        out_specs=[pl.BlockSpec((2, tg, GLA_QK), lambda t: (0, t, 0)),
                   pl.BlockSpec((2, tg, GLA_QK), lambda t: (0, t, 0)),
                   pl.BlockSpec((2, GLA_HEADS, tg, GLA_CHUNK), lambda t: (0, 0, t, 0)),
                   pl.BlockSpec((2, GLA_HEADS, 1, tg // GLA_CHUNK, GLA_DK), lambda t: (0, 0, t, 0, 0))],
        out_shape=[jax.ShapeDtypeStruct((2, n, GLA_QK), BF16),
                   jax.ShapeDtypeStruct((2, n, GLA_QK), BF16),
                   jax.ShapeDtypeStruct((2, GLA_HEADS, n, GLA_CHUNK), BF16),
                   jax.ShapeDtypeStruct((2, GLA_HEADS, nt, tg // GLA_CHUNK, GLA_DK), F32)],
        scratch_shapes=[pltpu.VMEM((2, GLA_HEADS, tg, GLA_DK), F32),
                        pltpu.VMEM((GLA_HEADS, tg, GLA_DK), F32), pltpu.VMEM((GLA_HEADS, tg, GLA_DK), F32)],
        compiler_params=_cparams(("arbitrary",)),
        name="gla_prep",
    )(qk, qk, g, cos2, sin2, w_gate, b_gate, tri)


def _gla_scan_kernel(*refs):
    dirs = (refs[0:5], refs[5:10])
    o_refs = refs[10:12]
    st_ref = refs[12]

    @pl.when(pl.program_id(1) == 0)
    def _():
        st_ref[...] = jnp.zeros_like(st_ref)

    group = refs[0].shape[0]
    n_chunks = refs[0].shape[1] // GLA_CHUNK
    lanes = [(rev, p) for rev in (0, 1) for p in range(group)]
    for hd in range(GLA_HEADS):
        kcols = slice(hd * GLA_DK, (hd + 1) * GLA_DK)
        vcols = slice(hd * GLA_DV, (hd + 1) * GLA_DV)
        states = {lane: st_ref[lane[0], lane[1], hd] for lane in lanes}
        for cc in range(n_chunks):
            for rev, p in lanes:
                qe_ref, kd_ref, a_ref, gd_ref, v_ref = dirs[rev]
                c = n_chunks - 1 - cc if rev else cc
                rows = slice(c * GLA_CHUNK, (c + 1) * GLA_CHUNK)
                st = states[rev, p]
                v_c = v_ref[p, rows, vcols]
                o = lax.dot_general(qe_ref[p, rows, kcols], st.astype(BF16), (((1,), (1,)), ((), ())),
                                    preferred_element_type=F32)
                o = o + jnp.dot(a_ref[hd, p, rows, :], v_c, preferred_element_type=F32)
                o_refs[rev][p, rows, vcols] = o
                upd = lax.dot_general(v_c, kd_ref[p, rows, kcols], (((0,), (0,)), ((), ())),
                                      preferred_element_type=F32)
                states[rev, p] = st * gd_ref[hd, p, 0, c:c + 1, :] + upd
        for rev, p in lanes:
            st_ref[rev, p, hd] = states[rev, p]


def _gla_scan(prep, v, batch, tpb):
    n = v.shape[0]
    tg = TOK_TILE
    n_lat = tpb - 1
    l = tpb * tg
    group = SCAN_GROUP if batch % SCAN_GROUP == 0 else 1
    nb = batch // group
    qe, kd, a, gd = prep
    qe = qe.reshape(2, nb, group, l, GLA_QK)
    kd = kd.reshape(2, nb, group, l, GLA_QK)
    a = a.reshape(2, GLA_HEADS, nb, group, l, GLA_CHUNK)
    gd = gd.reshape(2, GLA_HEADS, nb, group, tpb, tg // GLA_CHUNK, GLA_DK)
    v = v.reshape(nb, group, l, GLA_V)

    def specs(rev):
        def tile(s):
            return jnp.where(s == 0, n_lat, n_lat - s if rev else s - 1)

        ins = [pl.BlockSpec((None, None, group, tg, GLA_QK), lambda bi, s: (rev, bi, 0, tile(s), 0)),
               pl.BlockSpec((None, None, group, tg, GLA_QK), lambda bi, s: (rev, bi, 0, tile(s), 0)),
               pl.BlockSpec((None, GLA_HEADS, None, group, tg, GLA_CHUNK),
                            lambda bi, s: (rev, 0, bi, 0, tile(s), 0)),
               pl.BlockSpec((None, GLA_HEADS, None, group, 1, tg // GLA_CHUNK, GLA_DK),
                            lambda bi, s: (rev, 0, bi, 0, tile(s), 0, 0)),
               pl.BlockSpec((None, group, tg, GLA_V), lambda bi, s: (bi, 0, tile(s), 0))]
        return ins, pl.BlockSpec((None, group, tg, GLA_V), lambda bi, s: (bi, 0, tile(s), 0))

    in_f, out_f = specs(0)
    in_b, out_b = specs(1)
    o_f, o_b = pl.pallas_call(
        _gla_scan_kernel,
        grid=(nb, tpb),
        in_specs=in_f + in_b,
        out_specs=[out_f, out_b],
        out_shape=[jax.ShapeDtypeStruct((nb, group, l, GLA_V), F32)] * 2,
        scratch_shapes=[pltpu.VMEM((2, group, GLA_HEADS, GLA_DV, GLA_DK), F32)],
        compiler_params=_cparams(("arbitrary", "arbitrary")),
        name="gla_scan",
    )(qe, kd, a, gd, v, qe, kd, a, gd, v)
    return o_f.reshape(n, GLA_V), o_b.reshape(n, GLA_V)


def _pack_bf16(x):
    half = x.shape[1] // 2
    lo = lax.bitcast_convert_type(x[:, :half].astype(BF16).astype(F32), jnp.uint32)
    hi = lax.bitcast_convert_type(x[:, half:].astype(BF16).astype(F32), jnp.uint32)
    return (lo >> 16) | (hi & jnp.uint32(0xFFFF0000))


def _unpack_bf16(p):
    lo = lax.bitcast_convert_type(p << 16, F32)
    hi = lax.bitcast_convert_type(p & jnp.uint32(0xFFFF0000), F32)
    return lo, hi


def _residual_ln(x, a, t, ln, alpha):
    y = alpha * x + t[0:1] * a
    mu = jnp.mean(y, axis=-1, keepdims=True)
    yc = y - mu
    var = jnp.mean(yc * yc, axis=-1, keepdims=True)
    xn = yc * lax.rsqrt(var + LN_EPS) * ln[0:1] + ln[1:2]
    return xn, xn * (1.0 + t[1:2]) + t[2:3]


def _top4_softmax(lt):
    e = lt.shape[0]
    io = lax.broadcasted_iota(jnp.int32, lt.shape, 0)
    work = lt
    idxs, vals = [], []
    for _ in range(TOP_K):
        m = jnp.max(work, axis=0, keepdims=True)
        ik = jnp.min(jnp.where(work == m, io, e), axis=0, keepdims=True)
        idxs.append(ik)
        vals.append(m)
        work = jnp.where(io == ik, -jnp.inf, work)
    ex = [jnp.exp(v - vals[0]) for v in vals]
    den = ex[0] + ex[1] + ex[2] + ex[3]
    return jnp.concatenate(idxs, axis=0), jnp.concatenate([x / den for x in ex], axis=0)


def _gla_gated_norm(o, r, gn):
    gate = r * jax.nn.sigmoid(r)
    heads = []
    for hd in range(GLA_HEADS):
        cols = slice(hd * GLA_DV, (hd + 1) * GLA_DV)
        oh = o[:, cols]
        ms = jnp.mean(oh * oh, axis=-1, keepdims=True)
        heads.append((oh * lax.rsqrt(ms + RMS_EPS) * gn * gate[:, cols]).astype(BF16))
    return jnp.concatenate(heads, axis=1)


def _post_kernel(gla, n_act, tpb, alpha, *refs):
    if gla:
        of_ref, ob_ref, r_ref, gn_ref, w_ref = refs[:5]
        rest = refs[5:]
        act = _gla_gated_norm(of_ref[...] + ob_ref[...], r_ref[...], gn_ref[...])
        a = jnp.dot(act, w_ref[...], preferred_element_type=F32)
    else:
        acts = refs[:n_act]
        ws = refs[n_act:2 * n_act]
        rest = refs[2 * n_act:]
        a = jnp.dot(acts[0][...], ws[0][...], preferred_element_type=F32)
        for k in range(1, n_act):
            a = a + jnp.dot(acts[k][...], ws[k][...], preferred_element_type=F32)
    x_ref, tab_ref, ln_ref, rw_ref, rb_ref, xo_ref, h_ref, idx_ref, gate_ref = rest
    tm = x_ref.shape[0]
    ln = ln_ref[...]
    for s in range(tm // TOK_TILE):
        rows = slice(s * TOK_TILE, (s + 1) * TOK_TILE)
        t = tab_ref[_tab_row(pl.program_id(0) * (tm // TOK_TILE) + s, tpb)]
        xn, h = _residual_ln(x_ref[rows], a[rows], t, ln, alpha)
        xo_ref[rows] = xn
        h_ref[rows] = _pack_bf16(h)
        lt = _dot_bf16x3(rw_ref[...], h, ((1,), (1,))) + rb_ref[...]
        idx, gates = _top4_softmax(lt)
        idx_ref[:, rows] = idx
        gate_ref[:, rows] = gates


def _post(acts, ws, x, tab, ln, rw_t, rb, tpb, alpha, gla=False):
    n, d = x.shape
    tm = _row_tile(n)
    e = rw_t.shape[0]
    row = lambda i: (i, 0)
    fixed = lambda i: (0, 0)
    return pl.pallas_call(
        functools.partial(_post_kernel, gla, len(acts), tpb, alpha),
        grid=(n // tm,),
        in_specs=([pl.BlockSpec((tm, a.shape[1]), row) for a in acts]
                  + [pl.BlockSpec(w.shape, fixed) for w in ws]
                  + [pl.BlockSpec((tm, d), row),
                     pl.BlockSpec(tab.shape, lambda i: (0, 0, 0)),
                     pl.BlockSpec(ln.shape, fixed),
                     pl.BlockSpec((e, d), fixed),
                     pl.BlockSpec((e, 1), fixed)]),
        out_specs=[pl.BlockSpec((tm, d), row), pl.BlockSpec((tm, d // 2), row),
                   pl.BlockSpec((TOP_K, tm), lambda i: (0, i)), pl.BlockSpec((TOP_K, tm), lambda i: (0, i))],
        out_shape=[jax.ShapeDtypeStruct((n, d), F32), jax.ShapeDtypeStruct((n, d // 2), jnp.uint32),
                   jax.ShapeDtypeStruct((TOP_K, n), jnp.int32), jax.ShapeDtypeStruct((TOP_K, n), F32)],
        compiler_params=_cparams(("arbitrary",)),
        name="post",
    )(*acts, *ws, x, tab, ln, rw_t, rb)


def _rank_kernel(idx_ref, tri_ref, rank_ref, cnt_ref, carry_ref):
    @pl.when(pl.program_id(0) == 0)
    def _():
        carry_ref[...] = jnp.zeros_like(carry_ref)

    idx = idx_ref[...]
    e = carry_ref.shape[0]
    tr = idx.shape[1]
    io = lax.broadcasted_iota(jnp.int32, (e, tr), 0)
    chosen = jnp.zeros((e, tr), F32)
    for k in range(TOP_K):
        chosen = chosen + (idx[k:k + 1] == io).astype(F32)
    cum = jnp.dot(chosen.astype(BF16), tri_ref[...], preferred_element_type=F32)
    base = carry_ref[:, 0:1]
    excl = base + cum - chosen
    ranks = [jnp.sum(jnp.where(idx[k:k + 1] == io, excl, 0.0), axis=0, keepdims=True) for k in range(TOP_K)]
    rank_ref[...] = jnp.concatenate(ranks, axis=0).astype(jnp.int32)
    carry_ref[...] = carry_ref[...] + jnp.sum(chosen, axis=1, keepdims=True)
    cnt_ref[...] = carry_ref[...]


def _rank(idx_t, n_experts):
    n = idx_t.shape[1]
    tr = _row_tile(n)
    tri = (np.arange(tr)[:, None] <= np.arange(tr)[None, :]).astype(np.float32)
    return pl.pallas_call(
        _rank_kernel,
        grid=(n // tr,),
        in_specs=[pl.BlockSpec((TOP_K, tr), lambda i: (0, i)),
                  pl.BlockSpec((tr, tr), lambda i: (0, 0))],
        out_specs=[pl.BlockSpec((TOP_K, tr), lambda i: (0, i)),
                   pl.BlockSpec((n_experts, LANES), lambda i: (0, 0))],
        out_shape=[jax.ShapeDtypeStruct((TOP_K, n), jnp.int32),
                   jax.ShapeDtypeStruct((n_experts, LANES), F32)],
        scratch_shapes=[pltpu.VMEM((n_experts, LANES), F32)],
        compiler_params=_cparams(("arbitrary",)),
        name="rank",
    )(idx_t, jnp.asarray(tri, BF16))


def _expert_kernel(be_ref, rows_ref, x_ref, w1_ref, b1_ref, w2_ref, b2_ref, o_ref, w1b_ref, w2b_ref):
    i = pl.program_id(0)
    used = rows_ref[i] > 0
    new_expert = jnp.logical_or(i == 0, be_ref[i] != be_ref[jnp.maximum(i - 1, 0)])

    @pl.when(jnp.logical_and(used, new_expert))
    def _():
        w1b_ref[...] = w1_ref[0].astype(BF16)
        w2b_ref[...] = w2_ref[0].astype(BF16)

    @pl.when(used)
    def _():
        x_lo, x_hi = _unpack_bf16(x_ref[...])
        kh = x_lo.shape[1]
        hid = (jnp.dot(x_lo.astype(BF16), w1b_ref[:kh], preferred_element_type=F32)
               + jnp.dot(x_hi.astype(BF16), w1b_ref[kh:], preferred_element_type=F32) + b1_ref[0])
        half = hid.shape[1] // 2
        glu = jnp.minimum(hid[:, :half], SWIGLU_LIMIT)
        lin = jnp.clip(hid[:, half:], -SWIGLU_LIMIT, SWIGLU_LIMIT)
        act = glu * jax.nn.sigmoid(SWIGLU_ALPHA * glu) * (lin + 1.0)
        y = jnp.dot(act.astype(BF16), w2b_ref[...], preferred_element_type=F32) + b2_ref[0]
        o_ref[...] = _pack_bf16(y)

    @pl.when(jnp.logical_not(used))
    def _():
        o_ref[...] = jnp.zeros_like(o_ref)


def _experts(block_expert, block_rows, x_pad, layer, w1, b1, w2, b2):
    n_pad, dp = x_pad.shape
    depth, e, d, dh2 = w1.shape
    n_blocks = n_pad // EXPERT_BLOCK
    grid_spec = pltpu.PrefetchScalarGridSpec(
        num_scalar_prefetch=2,
        grid=(n_blocks,),
        in_specs=[pl.BlockSpec((EXPERT_BLOCK, dp), lambda i, be, nb: (i, 0)),
                  pl.BlockSpec((None, 1, d, dh2), lambda i, be, nb: (layer, be[i], 0, 0)),
                  pl.BlockSpec((None, 1, 1, dh2), lambda i, be, nb: (layer, be[i], 0, 0)),
                  pl.BlockSpec((None, 1, dh2 // 2, d), lambda i, be, nb: (layer, be[i], 0, 0)),
                  pl.BlockSpec((None, 1, 1, d), lambda i, be, nb: (layer, be[i], 0, 0))],
        out_specs=pl.BlockSpec((EXPERT_BLOCK, dp), lambda i, be, nb: (i, 0)),
        scratch_shapes=[pltpu.VMEM((d, dh2), BF16), pltpu.VMEM((dh2 // 2, d), BF16)],
    )
    return pl.pallas_call(
        _expert_kernel,
        grid_spec=grid_spec,
        out_shape=jax.ShapeDtypeStruct((n_pad, dp), jnp.uint32),
        compiler_params=pltpu.CompilerParams(dimension_semantics=("arbitrary",),
                                             vmem_limit_bytes=EXPERT_VMEM_LIMIT),
        name="experts",
    )(block_expert, block_rows, x_pad, w1, b1.reshape(depth, e, 1, dh2), w2, b2.reshape(depth, e, 1, d))


def _stream_tile(i, tpb, latent_only):
    return (i // (tpb - 1)) * tpb + i % (tpb - 1) if latent_only else i


def _combine_kernel(tpb, alpha, splits, y_ref, gate_ref, x_ref, tab_ref, ln_ref, *refs):
    latent_only = splits is None
    g = gate_ref[...]
    y_lo, y_hi = None, None
    for k in range(TOP_K):
        lo, hi = _unpack_bf16(y_ref[k])
        gk = g[:, k:k + 1]
        y_lo = lo * gk if y_lo is None else y_lo + lo * gk
        y_hi = hi * gk if y_hi is None else y_hi + hi * gk
    y = jnp.concatenate([y_lo, y_hi], axis=1)
    ln = ln_ref[...]
    n_sub = x_ref.shape[0] // TOK_TILE
    xns, hs = [], []
    for s in range(n_sub):
        rows = slice(s * TOK_TILE, (s + 1) * TOK_TILE)
        t = tab_ref[_tab_row(_stream_tile(pl.program_id(0) * n_sub + s, tpb, latent_only), tpb)]
        xn, h = _residual_ln(x_ref[rows], y[rows], t, ln, alpha)
        xns.append(xn)
        hs.append(h)
    xn = jnp.concatenate(xns, axis=0)
    if latent_only:
        refs[0][...] = xn
    else:
        w_ref, xo_ref = refs[:2]
        xo_ref[...] = xn
        _emit_proj(jnp.concatenate(hs, axis=0), w_ref, splits, refs[2:])


def _combine(y_g, gates, x, tab, ln, tpb, alpha, plan):
    n, d = x.shape
    latent_only = plan is None
    tm = TOK_TILE if latent_only else _row_tile(n)
    tiles = n // tm
    steps = tiles // tpb * (tpb - 1) if latent_only else tiles
    src = lambda i: (_stream_tile(i, tpb, latent_only), 0)
    in_specs = [pl.BlockSpec((TOP_K, tm, d // 2), lambda i: (0, _stream_tile(i, tpb, latent_only), 0)),
                pl.BlockSpec((tm, TOP_K), src),
                pl.BlockSpec((tm, d), src),
                pl.BlockSpec(tab.shape, lambda i: (0, 0, 0)),
                pl.BlockSpec(ln.shape, lambda i: (0, 0))]
    out_specs = [pl.BlockSpec((tm, d), lambda i: (i, 0))]
    out_shape = [jax.ShapeDtypeStruct((steps * tm, d), F32)]
    operands = [y_g, gates, x, tab, ln]
    if not latent_only:
        w_spec, p_specs, p_shape = _proj_specs(n, tm, plan)
        in_specs.append(w_spec)
        out_specs += p_specs
        out_shape += p_shape
        operands.append(plan[0])
    return pl.pallas_call(
        functools.partial(_combine_kernel, tpb, alpha, None if latent_only else tuple(plan[1])),
        grid=(steps,),
        in_specs=in_specs,
        out_specs=out_specs,
        out_shape=out_shape,
        compiler_params=_cparams(("arbitrary",)),
        name="combine",
    )(*operands)


def _sc_mesh():
    return plsc.VectorSubcoreMesh(core_axis_name="c", subcore_axis_name="s")


def _sc_split(rows, mesh):
    workers = mesh.num_cores * mesh.num_subcores
    per = rows // workers
    assert per * workers == rows
    chunk = SC_CHUNK if per % SC_CHUNK == 0 else 8
    assert per % chunk == 0
    return per, chunk


def _sc_scatter_rows(x, idx, n_out):
    r, c = x.shape
    mesh = _sc_mesh()
    per, chunk = _sc_split(r, mesh)

    n_chunks = per // chunk

    @functools.partial(pl.kernel, out_type=jax.ShapeDtypeStruct((n_out, c), x.dtype), mesh=mesh,
                       scratch_types=[pltpu.VMEM((chunk, c), x.dtype), pltpu.SemaphoreType.DMA] * 2
                       + [pltpu.VMEM((chunk,), jnp.int32)] * TOP_K + [pltpu.SemaphoreType.DMA])
    def scatter(x_hbm, i_hbm, o_hbm, rows_a, sem_a, rows_b, sem_b, *rest):
        idx_vs, sem_s = rest[:TOP_K], rest[TOP_K]
        base = (lax.axis_index("s") * mesh.num_cores + lax.axis_index("c")) * per
        slots = ((rows_a, sem_a), (rows_b, sem_b))

        def load(j, slot):
            rows_v, sem = slot
            off = pl.multiple_of(base + j * chunk, chunk)
            return pltpu.make_async_copy(x_hbm.at[pl.ds(off, chunk)], rows_v, sem)

        def scatter_chunk(j, slot, prefetch):
            rows_v, _ = slot
            off = pl.multiple_of(base + j * chunk, chunk)
            load(j, slot).wait()
            for k in range(TOP_K):
                pltpu.sync_copy(i_hbm.at[pl.ds(k * r + off, chunk)], idx_vs[k])
            for k in range(TOP_K):
                pltpu.async_copy(rows_v, o_hbm.at[idx_vs[k]], sem_s)
            prefetch()
            for k in range(TOP_K):
                pltpu.make_async_copy(rows_v, o_hbm.at[idx_vs[k]], sem_s).wait()

        load(0, slots[0]).start()

        @pl.loop(0, n_chunks // 2)
        def _(p):
            j = 2 * p
            scatter_chunk(j, slots[0], lambda: load(j + 1, slots[1]).start())

            def next_even():
                @pl.when(j + 2 < n_chunks)
                def _():
                    load(j + 2, slots[0]).start()

            scatter_chunk(j + 1, slots[1], next_even)

        if n_chunks % 2:
            scatter_chunk(n_chunks - 1, slots[0], lambda: None)

    return scatter(x, idx)


def _sc_gather_rows(table, idx):
    m = idx.shape[0]
    c = table.shape[1]
    mesh = _sc_mesh()
    per, chunk = _sc_split(m, mesh)

    n_chunks = per // chunk
    slot_types = [pltpu.VMEM((chunk,), jnp.int32), pltpu.VMEM((chunk, c), table.dtype), pltpu.SemaphoreType.DMA]

    @functools.partial(pl.kernel, out_type=jax.ShapeDtypeStruct((m, c), table.dtype), mesh=mesh,
                       scratch_types=slot_types * 2)
    def gather(t_hbm, i_hbm, o_hbm, idx_a, rows_a, sem_a, idx_b, rows_b, sem_b):
        base = (lax.axis_index("s") * mesh.num_cores + lax.axis_index("c")) * per
        slots = ((idx_a, rows_a, sem_a), (idx_b, rows_b, sem_b))

        def start(j, slot):
            idx_v, rows_v, sem = slot
            off = pl.multiple_of(base + j * chunk, chunk)
            pltpu.sync_copy(i_hbm.at[pl.ds(off, chunk)], idx_v)
            pltpu.async_copy(t_hbm.at[idx_v], rows_v, sem)

        def finish(j, slot):
            idx_v, rows_v, sem = slot
            off = pl.multiple_of(base + j * chunk, chunk)
            pltpu.make_async_copy(t_hbm.at[idx_v], rows_v, sem).wait()
            pltpu.sync_copy(rows_v, o_hbm.at[pl.ds(off, chunk)])

        start(0, slots[0])

        @pl.loop(0, n_chunks // 2)
        def _(p):
            j = 2 * p
            start(j + 1, slots[1])
            finish(j, slots[0])

            @pl.when(j + 2 < n_chunks)
            def _():
                start(j + 2, slots[0])

            finish(j + 1, slots[1])

        if n_chunks % 2:
            finish(n_chunks - 1, slots[0])

    return gather(table, idx)


def _moe(h, idx_t, layer, w1, b1, w2, b2):
    n, dp = h.shape
    e = w1.shape[1]
    m = n * TOP_K
    rank_t, cnt = _rank(idx_t, e)
    sizes = cnt[:, 0].astype(jnp.int32)
    padded = (sizes + EXPERT_BLOCK - 1) // EXPERT_BLOCK * EXPERT_BLOCK
    pad_ends = jnp.cumsum(padded)
    pad_starts = pad_ends - padded
    ids = jnp.arange(e, dtype=jnp.int32)[:, None, None]
    dest_t = jnp.sum(jnp.where(idx_t[None] == ids, pad_starts[:, None, None], 0), axis=0) + rank_t
    dest = dest_t.reshape(-1)
    n_blocks = (m + e * (EXPERT_BLOCK - 1)) // EXPERT_BLOCK + 1
    n_pad = n_blocks * EXPERT_BLOCK
    block_start = jnp.arange(n_blocks, dtype=jnp.int32) * EXPERT_BLOCK
    block_expert = jnp.minimum(jnp.sum((pad_ends[None, :] <= block_start[:, None]).astype(jnp.int32), axis=1), e - 1)
    group_end = jnp.sum(jnp.where(block_expert[:, None] == jnp.arange(e, dtype=jnp.int32)[None, :],
                                  (pad_starts + sizes)[None, :], 0), axis=1)
    block_rows = jnp.clip(group_end - block_start, 0, EXPERT_BLOCK).astype(jnp.int32)
    x_pad = _sc_scatter_rows(h, dest, n_pad)
    y_pad = _experts(block_expert, block_rows, x_pad, layer, w1, b1, w2, b2)
    return _sc_gather_rows(y_pad, dest).reshape(TOP_K, n, dp)


def _rope_tables(t_lat, n_ctx):
    t = jnp.arange(t_lat)
    row = (t // GRID_W).astype(F32)
    col = (t % GRID_W).astype(F32)
    nf = GLA_DK // 4
    freqs = ROPE_BASE ** (-jnp.arange(nf, dtype=F32) / nf)
    ang = jnp.concatenate([row[:, None] * freqs, col[:, None] * freqs], axis=-1)
    cos, sin = jnp.cos(ang), jnp.sin(ang)
    cos2 = jnp.concatenate([cos, cos], axis=-1)
    sin2 = jnp.concatenate([-sin, sin], axis=-1)
    return (jnp.concatenate([cos2, jnp.ones((n_ctx, GLA_DK), F32)], axis=0),
            jnp.concatenate([sin2, jnp.zeros((n_ctx, GLA_DK), F32)], axis=0))


def _chunk_tri(tg, rev):
    t = np.arange(tg)
    same = (t[:, None] // GLA_CHUNK) == (t[None, :] // GLA_CHUNK)
    side = (t[None, :] >= t[:, None]) if rev else (t[None, :] <= t[:, None])
    return jnp.asarray((same & side).astype(np.float32), BF16)


def _table(mods, rows, batch):
    lat = jnp.stack([mods[:batch, r] for r in rows], axis=1)
    ctx = jnp.broadcast_to(jnp.stack([mods[batch, r] for r in rows], axis=0)[None], lat.shape)
    tab = jnp.stack([lat, ctx], axis=1).reshape(2 * batch, len(rows), -1)
    return jnp.pad(tab, ((0, 0), (0, 8 - len(rows)), (0, 0)))


@jax.jit
def _forward(x, c, ctx, c_ctx, ada_w, ada_b, ln_g, ln_b, ab_w_in, ab_pool_w, ab_pool_scale, ab_rpb,
             ab_w_out, gla_w_in, gla_w_gate, gla_b_gate, gla_norm_g, gla_w_out, router_w, router_b,
             exp_w1, exp_b1, exp_w2, exp_b2):
    batch, t_lat, d = x.shape
    n_ctx = ctx.shape[1]
    depth = ada_w.shape[0]
    assert d == D_MODEL and n_ctx == TOK_TILE and t_lat % TOK_TILE == 0
    rows = t_lat // GRID_W
    assert rows % NA_QROWS == 0 and rows >= NA_KROWS + NA_QROWS
    n_lat = t_lat // TOK_TILE
    tpb = n_lat + 1
    l = t_lat + n_ctx
    n = batch * l
    alpha = (2.0 * depth) ** 0.25

    cc = jnp.concatenate([c, c_ctx[None], jnp.zeros((16 - batch - 1, d), F32)], axis=0)
    mods = _mods(cc, ada_w, ada_b).reshape(depth, 16, N_MOD, d)

    def in_proj_plan(i):
        if i % 2 == 0:
            return (ab_w_in[i // 2].astype(BF16),
                    [(0, POOL_WIDTH), (POOL_WIDTH, POOL_WIDTH + 3 * NA_WIDTH)], [F32, BF16])
        edges = (0, 2 * GLA_QK, 2 * GLA_QK + GLA_V, 2 * GLA_QK + 2 * GLA_V, 2 * GLA_QK + 2 * GLA_V + 2 * GATE_RANK)
        return gla_w_in[i // 2].astype(BF16), list(zip(edges[:-1], edges[1:])), [F32, BF16, F32, F32]

    z, *projected = _modulate(x.reshape(batch * t_lat, d), ctx.reshape(batch * n_ctx, d),
                              _table(mods[0], (1, 0), batch), tpb, in_proj_plan(0))
    cos2, sin2 = _rope_tables(t_lat, n_ctx)

    for i in range(depth):
        j = i // 2
        last = i == depth - 1
        tab1 = _table(mods[i], (2, 4, 3), batch)
        ln1 = jnp.stack([ln_g[i, 0], ln_b[i, 0]])
        ln2 = jnp.stack([ln_g[i, 1], ln_b[i, 1]])
        rw_t = router_w[i].T
        rb = router_b[i][:, None]
        if i % 2 == 0:
            u, qkv = projected
            w_blk = jax.scipy.linalg.block_diag(*[ab_pool_w[j, g] for g in range(len(POOL_WINDOWS))])
            bias = _na_bias_tables(ab_rpb[j], rows)
            attn, pooled = _na_pool(qkv.reshape(batch, l, 3 * NA_WIDTH), bias, u.reshape(batch, l, POOL_WIDTH),
                                    w_blk.astype(BF16), ab_pool_scale[j][None, :], n_lat, t_lat, n_ctx)
            w_out = ab_w_out[j].astype(BF16)
            acts = [pooled.reshape(n, POOL_WIDTH), attn.reshape(n, NA_WIDTH)]
            ws = [w_out[:POOL_WIDTH], w_out[POOL_WIDTH:]]
        else:
            qk, v, r, g = projected
            tri = jnp.stack([_chunk_tri(TOK_TILE, False), _chunk_tri(TOK_TILE, True)])
            prep = _gla_prep(qk, g, cos2, sin2, gla_w_gate[j], gla_b_gate[j][:, None, :], tri, tpb)
            o_f, o_b = _gla_scan(prep, v, batch, tpb)
            acts = [o_f, o_b, r]
            ws = [gla_norm_g[j][None, :], gla_w_out[j].astype(BF16)]
        z, h, idx_t, gates_t = _post(acts, ws, z, tab1, ln1, rw_t, rb, tpb, alpha, gla=i % 2 == 1)
        y_g = _moe(h, idx_t, i, exp_w1, exp_b1, exp_w2, exp_b2)
        nxt = mods[i + 1] if not last else mods[i]
        tab2 = _table(jnp.concatenate([mods[i][:, 5:6], nxt[:, 1:2], nxt[:, 0:1]], axis=1), (0, 1, 2), batch)
        if last:
            (out,) = _combine(y_g, gates_t.T, z, tab2, ln2, tpb, alpha, None)
            return out.reshape(batch, t_lat, d)
        z, *projected = _combine(y_g, gates_t.T, z, tab2, ln2, tpb, alpha, in_proj_plan(i + 1))


def kernel(x, c, ctx, c_ctx, ada_w, ada_b, ln_g, ln_b, ab_w_in, ab_pool_w, ab_pool_scale, ab_rpb, ab_w_out,
           gla_w_in, gla_w_gate, gla_b_gate, gla_norm_g, gla_w_out, router_w, router_b, exp_w1, exp_b1, exp_w2,
           exp_b2):
    return _forward(x, c, ctx, c_ctx, ada_w, ada_b, ln_g, ln_b, ab_w_in, ab_pool_w, ab_pool_scale, ab_rpb,
                    ab_w_out, gla_w_in, gla_w_gate, gla_b_gate, gla_norm_g, gla_w_out, router_w, router_b,
                    exp_w1, exp_b1, exp_w2, exp_b2)
```

```python
import functools
import math

import numpy as np
import jax
import jax.numpy as jnp
from jax import lax
from jax.experimental import pallas as pl
from jax.experimental.pallas import tpu as pltpu
from jax.experimental.pallas import tpu_sc as plsc

F32 = jnp.float32
BF16 = jnp.bfloat16
HIGHEST = lax.Precision.HIGHEST

D_MODEL = 1024
GRID_W = 64
N_MOD = 6
POOL_WINDOWS = (2, 4, 8, 16)
POOL_WIDTH = D_MODEL // 4
POOL_GROUP_DIM = POOL_WIDTH // len(POOL_WINDOWS)
POOL_HALO = max(POOL_WINDOWS) // 2
NA_HEAD_DIM = 64
NA_HEADS = (D_MODEL - POOL_WIDTH) // NA_HEAD_DIM
NA_WIDTH = NA_HEADS * NA_HEAD_DIM
WIN_H = 8
WIN_W = 16
GLA_HEADS = 4
GLA_DK = D_MODEL // 2 // GLA_HEADS
GLA_DV = D_MODEL // GLA_HEADS
GATE_RANK = 16
GATE_NORM = 16.0
GLA_CHUNK = 64
GLA_STRIP = 8
GLA_QK = GLA_HEADS * GLA_DK
GLA_V = GLA_HEADS * GLA_DV
ROPE_BASE = 10000.0
TOP_K = 4
SWIGLU_LIMIT = 7.0
SWIGLU_ALPHA = 1.702
LN_EPS = 1e-5
RMS_EPS = 1e-6
NEG_INF = -1e30
LOG2_E = math.log2(math.e)

LANES = 128
TOK_TILE = 256
NA_QROWS = 4
NA_KROWS = 12
NA_PAIRS_PER_STEP = 6
SCAN_GROUP = 2
VMEM_LIMIT = 48 * 1024 * 1024
EXPERT_VMEM_LIMIT = 56 * 1024 * 1024
EXPERT_BLOCK = 512
SC_CHUNK = 64


def _cparams(sem):
    return pltpu.CompilerParams(dimension_semantics=sem, vmem_limit_bytes=VMEM_LIMIT)


def _dot_bf16x3(a, b, dims):
    a_hi = a.astype(BF16)
    a_lo = (a - a_hi.astype(F32)).astype(BF16)
    b_hi = b.astype(BF16)
    b_lo = (b - b_hi.astype(F32)).astype(BF16)
    dg = functools.partial(lax.dot_general, dimension_numbers=(dims, ((), ())), preferred_element_type=F32)
    return dg(a_hi, b_hi) + dg(a_hi, b_lo) + dg(a_lo, b_hi)


def _row_tile(n):
    return 2 * TOK_TILE if n % (2 * TOK_TILE) == 0 else TOK_TILE


def _mods_kernel(c_ref, w_ref, b_ref, o_ref):
    cv = c_ref[...]
    sc = cv * jax.nn.sigmoid(cv)
    o_ref[0] = jnp.dot(sc, w_ref[0], precision=HIGHEST, preferred_element_type=F32) + b_ref[0]


def _mods(cc, ada_w, ada_b):
    depth, d, n = ada_w.shape
    r = cc.shape[0]
    tn = n // 4
    return pl.pallas_call(
        _mods_kernel,
        grid=(depth, n // tn),
        in_specs=[pl.BlockSpec((r, d), lambda i, j: (0, 0)),
                  pl.BlockSpec((1, d, tn), lambda i, j: (i, 0, j)),
                  pl.BlockSpec((1, 1, tn), lambda i, j: (i, 0, j))],
        out_specs=pl.BlockSpec((1, r, tn), lambda i, j: (i, 0, j)),
        out_shape=jax.ShapeDtypeStruct((depth, r, n), F32),
        compiler_params=_cparams(("arbitrary", "arbitrary")),
        name="mods",
    )(cc, ada_w, ada_b.reshape(depth, 1, n))


def _tab_row(g, tpb):
    return (g // tpb) * 2 + (g % tpb == tpb - 1).astype(jnp.int32)


def _modulate_kernel(tpb, splits, x_ref, ctx_ref, tab_ref, w_ref, z_ref, *out_refs):
    i = pl.program_id(0)
    is_ctx = i % tpb == tpb - 1
    z = jnp.where(is_ctx, ctx_ref[...], x_ref[...])
    t = tab_ref[_tab_row(i, tpb)]
    z_ref[...] = z
    _emit_proj(z * (1.0 + t[0:1]) + t[1:2], w_ref, splits, out_refs)


def _modulate(x, ctx, tab, tpb, plan):
    d = x.shape[1]
    n_lat = tpb - 1
    n = x.shape[0] + ctx.shape[0]
    w_spec, p_specs, p_shape = _proj_specs(n, TOK_TILE, plan)
    return pl.pallas_call(
        functools.partial(_modulate_kernel, tpb, tuple(plan[1])),
        grid=(n // TOK_TILE,),
        in_specs=[pl.BlockSpec((TOK_TILE, d), lambda i: ((i // tpb) * n_lat + jnp.minimum(i % tpb, n_lat - 1), 0)),
                  pl.BlockSpec((TOK_TILE, d), lambda i: (i // tpb, 0)),
                  pl.BlockSpec(tab.shape, lambda i: (0, 0, 0)),
                  w_spec],
        out_specs=[pl.BlockSpec((TOK_TILE, d), lambda i: (i, 0))] + p_specs,
        out_shape=[jax.ShapeDtypeStruct((n, d), F32)] + p_shape,
        compiler_params=_cparams(("arbitrary",)),
        name="modulate",
    )(x, ctx, tab, plan[0])


def _emit_proj(h, w_ref, splits, out_refs):
    hb = h.astype(BF16)
    for (a, b), o_ref in zip(splits, out_refs):
        o_ref[...] = jnp.dot(hb, w_ref[:, a:b], preferred_element_type=F32).astype(o_ref.dtype)


def _proj_specs(n, tm, plan):
    w, splits, dtypes = plan
    w_spec = pl.BlockSpec(w.shape, lambda i: (0, 0), pipeline_mode=pl.Buffered(1))
    out_specs = [pl.BlockSpec((tm, b - a), lambda i: (i, 0)) for a, b in splits]
    out_shape = [jax.ShapeDtypeStruct((n, b - a), dt) for (a, b), dt in zip(splits, dtypes)]
    return w_spec, out_specs, out_shape


def _pool_tile(n_lat, t_lat, t_ctx, prev_ref, cur_ref, next_ref, w_ref, scale_ref, o_ref, halo_ref):
    j = pl.program_id(1)
    is_ctx = j == n_lat
    has_prev = jnp.logical_and(j > 0, jnp.logical_not(is_ctx))
    has_next = j < n_lat - 1
    cur = cur_ref[0]
    hl = POOL_HALO
    halo_ref[0:hl] = jnp.where(has_prev, prev_ref[0, TOK_TILE - hl:TOK_TILE], 0.0)
    halo_ref[hl:hl + TOK_TILE] = cur
    halo_ref[hl + TOK_TILE:2 * hl + TOK_TILE] = jnp.where(has_next, next_ref[0, 0:hl], 0.0)

    shape = cur.shape
    lane = lax.broadcasted_iota(jnp.int32, shape, 1)
    group = lane // POOL_GROUP_DIM
    half = jnp.ones(shape, jnp.int32)
    for gi, wdw in enumerate(POOL_WINDOWS):
        half = jnp.where(group == gi, wdw // 2, half)
    acc = jnp.zeros(shape, F32)
    for off in range(-hl, hl):
        v = halo_ref[hl + off:hl + off + TOK_TILE]
        inside = (half >= -off) if off < 0 else (half > off)
        acc = acc + jnp.where(inside, v, 0.0)
    pos0 = jnp.where(is_ctx, 0, j * TOK_TILE)
    seq = jnp.where(is_ctx, t_ctx, t_lat)
    t = pos0 + lax.broadcasted_iota(jnp.int32, shape, 0)
    cnt = jnp.minimum(t + half, seq) - jnp.maximum(t - half, 0)
    pooled = acc / cnt.astype(F32) - cur
    y = jnp.dot(pooled.astype(BF16), w_ref[...], preferred_element_type=F32) * scale_ref[...]
    o_ref[0] = y.astype(o_ref.dtype)


def _na_bias_tables(rpb, rows):
    n_i = rows // NA_QROWS
    heads = rpb.shape[0]
    a = np.arange(NA_QROWS)
    kr = np.arange(NA_KROWS)
    cq = np.arange(GRID_W)
    ws = np.clip(cq - WIN_W // 2, 0, GRID_W - WIN_W)
    ok_col = (cq[None, :] >= ws[:, None]) & (cq[None, :] < ws[:, None] + WIN_W)
    dcol = np.clip(cq[None, :] - cq[:, None] + WIN_W - 1, 0, 2 * WIN_W - 2)
    oh_col = (dcol[..., None] == np.arange(2 * WIN_W - 1)).astype(np.float32)
    tabs = []
    for i in (0, 1, n_i - 1):
        start = int(np.clip(NA_QROWS * i - WIN_H // 2, 0, rows - NA_KROWS))
        r = NA_QROWS * i + a
        krow = start + kr
        rs = np.clip(r - WIN_H // 2, 0, rows - WIN_H)
        ok_row = (krow[None, :] >= rs[:, None]) & (krow[None, :] < rs[:, None] + WIN_H)
        drow = np.clip(krow[None, :] - r[:, None] + WIN_H - 1, 0, 2 * WIN_H - 2)
        oh_row = (drow[..., None] == np.arange(2 * WIN_H - 1)).astype(np.float32)
        by_row = jnp.einsum('hrc,akr->hakc', rpb, oh_row, precision=HIGHEST)
        bias = jnp.einsum('hakc,qjc->haqkj', by_row, oh_col, precision=HIGHEST)
        ok = ok_row[:, None, :, None] & ok_col[None, :, None, :]
        tabs.append(jnp.where(ok[None], bias, NEG_INF).reshape(heads, TOK_TILE, NA_KROWS * GRID_W))
    tabs.append(jnp.full_like(tabs[0], NEG_INF))
    return jnp.stack(tabs).astype(F32)


def _na_kernel(pool_args, q_ref, k0_ref, k1_ref, k2_ref, kc_ref, v0_ref, v1_ref, v2_ref, vc_ref, bias_ref,
               uprev_ref, ucur_ref, unext_ref, pw_ref, ps_ref, o_ref, po_ref, halo_ref):
    _pool_tile(*pool_args, uprev_ref, ucur_ref, unext_ref, pw_ref, ps_ref, po_ref, halo_ref)
    is_ctx = pl.program_id(1) == pool_args[0]

    @pl.when(jnp.logical_not(is_ctx))
    def _():
        _na_attend(q_ref, (k0_ref, k1_ref, k2_ref, kc_ref), (v0_ref, v1_ref, v2_ref, vc_ref), 3, bias_ref, o_ref)

    @pl.when(is_ctx)
    def _():
        _na_attend(q_ref, (kc_ref,), (vc_ref,), 0, bias_ref, o_ref)


def _na_attend(q_ref, k_refs, v_refs, n_band, bias_ref, o_ref):
    lane = lax.broadcasted_iota(jnp.int32, (TOK_TILE, LANES), 1)
    first = lane < NA_HEAD_DIM
    for pair in range(NA_PAIRS_PER_STEP):
        cols = slice(pair * LANES, (pair + 1) * LANES)
        q = q_ref[0, :, cols]
        outs = []
        for hh in range(2):
            mine = first if hh == 0 else jnp.logical_not(first)
            qm = jnp.where(mine, q, jnp.zeros_like(q)) * NA_HEAD_DIM ** -0.5
            scores = []
            for j, k_ref in enumerate(k_refs):
                s = lax.dot_general(qm, k_ref[0, :, cols], (((1,), (1,)), ((), ())), preferred_element_type=F32)
                if j < n_band:
                    s = s + bias_ref[0, 2 * pair + hh, :, j * TOK_TILE:(j + 1) * TOK_TILE]
                scores.append(s)
            m = functools.reduce(jnp.maximum, [jnp.max(s, axis=-1, keepdims=True) for s in scores])
            l = 0.0
            o = 0.0
            for s, v_ref in zip(scores, v_refs):
                p = jnp.exp(s - m)
                l = l + jnp.sum(p, axis=-1, keepdims=True)
                o = o + jnp.dot(p.astype(BF16), v_ref[0, :, cols], preferred_element_type=F32)
            outs.append(o / l)
        o_ref[0, :, cols] = jnp.where(first, outs[0], outs[1]).astype(o_ref.dtype)


def _na_pool(qkv, bias, u, pool_w, pool_scale, n_lat, t_lat, t_ctx):
    b, l, _ = qkv.shape
    pw = u.shape[-1]
    tpb = l // TOK_TILE
    n_groups = NA_WIDTH // (LANES * NA_PAIRS_PER_STEP)
    assert n_groups == 1
    blk = (1, TOK_TILE, LANES * NA_PAIRS_PER_STEP)
    ublk = (1, TOK_TILE, pw)

    def kstart(i):
        return jnp.clip(i - 1, 0, n_lat - NA_KROWS // NA_QROWS)

    def btype(i):
        return jnp.where(i == 0, 0, jnp.where(i == n_lat - 1, 2, jnp.where(i == n_lat, 3, 1)))

    def kv_spec(col0, j):
        return pl.BlockSpec(blk, lambda hp, i, bi: (bi, kstart(i) + j, col0 + hp))

    def ctx_spec(col0):
        return pl.BlockSpec(blk, lambda hp, i, bi: (bi, n_lat, col0 + hp))

    nk = bias.shape[-1]
    return pl.pallas_call(
        functools.partial(_na_kernel, (n_lat, t_lat, t_ctx)),
        grid=(n_groups, tpb, b),
        in_specs=[pl.BlockSpec(blk, lambda hp, i, bi: (bi, i, hp)),
                  kv_spec(n_groups, 0), kv_spec(n_groups, 1), kv_spec(n_groups, 2), ctx_spec(n_groups),
                  kv_spec(2 * n_groups, 0), kv_spec(2 * n_groups, 1), kv_spec(2 * n_groups, 2),
                  ctx_spec(2 * n_groups),
                  pl.BlockSpec((1, 2 * NA_PAIRS_PER_STEP, TOK_TILE, nk), lambda hp, i, bi: (btype(i), hp, 0, 0)),
                  pl.BlockSpec(ublk, lambda hp, i, bi: (bi, jnp.maximum(i - 1, 0), 0)),
                  pl.BlockSpec(ublk, lambda hp, i, bi: (bi, i, 0)),
                  pl.BlockSpec(ublk, lambda hp, i, bi: (bi, jnp.minimum(i + 1, tpb - 1), 0)),
                  pl.BlockSpec((pw, pw), lambda hp, i, bi: (0, 0)),
                  pl.BlockSpec((1, pw), lambda hp, i, bi: (0, 0))],
        out_specs=[pl.BlockSpec(blk, lambda hp, i, bi: (bi, i, hp)),
                   pl.BlockSpec(ublk, lambda hp, i, bi: (bi, i, 0))],
        out_shape=[jax.ShapeDtypeStruct((b, l, NA_WIDTH), BF16), jax.ShapeDtypeStruct((b, l, pw), BF16)],
        scratch_shapes=[pltpu.VMEM((TOK_TILE + 2 * POOL_HALO, pw), F32)],
        compiler_params=_cparams(("arbitrary", "arbitrary", "arbitrary")),
        name="na_pool",
    )(qkv, qkv, qkv, qkv, qkv, qkv, qkv, qkv, qkv, bias, u, u, u, pool_w, pool_scale)


def _log_sigmoid(z):
    return jnp.minimum(z, 0.0) - jnp.log(1.0 + jnp.exp(-jnp.abs(z)))


def _gla_prep_kernel(q_ref, k_ref, g_ref, cos_ref, sin_ref, wg_ref, bg_ref, tri_ref,
                     qe_ref, kd_ref, a_ref, gd_ref, b_scr, qr_scr, kr_scr):
    tg = q_ref.shape[0]
    head_cols = [slice(hd * GLA_DK, (hd + 1) * GLA_DK) for hd in range(GLA_HEADS)]
    cosv = cos_ref[...]
    sinv = sin_ref[...]
    rotated = []
    for hd, cols in enumerate(head_cols):
        q = q_ref[:, cols]
        k = k_ref[:, cols]
        qr = (q * cosv + pltpu.roll(q, GLA_DK // 2, 1) * sinv) * GLA_DK ** -0.5
        kr = k * cosv + pltpu.roll(k, GLA_DK // 2, 1) * sinv
        qr_scr[hd] = qr
        kr_scr[hd] = kr
        rotated.append((qr, kr))
    for rev in (0, 1):
        gg = g_ref[:, rev * GATE_RANK:(rev + 1) * GATE_RANK]
        z = _dot_bf16x3(gg, wg_ref[rev], ((1,), (0,))) + bg_ref[rev]
        la = _log_sigmoid(z) * (1.0 / GATE_NORM)
        tri = tri_ref[rev]
        b = jnp.zeros_like(la)
        rest = la
        for _ in range(3):
            piece = rest.astype(BF16)
            b = b + jnp.dot(tri, piece, preferred_element_type=F32)
            rest = rest - piece.astype(F32)
        for hd, cols in enumerate(head_cols):
            qr, kr = rotated[hd]
            bh = b[:, cols]
            b_scr[rev, hd] = bh * LOG2_E
            qe_ref[rev, :, cols] = (qr * jnp.exp(bh)).astype(qe_ref.dtype)
            for c in range(tg // GLA_CHUNK):
                r0 = c * GLA_CHUNK
                last = r0 if rev else r0 + GLA_CHUNK - 1
                tot = bh[last:last + 1]
                kd_ref[rev, r0:r0 + GLA_CHUNK, cols] = (
                    kr[r0:r0 + GLA_CHUNK] * jnp.exp(tot - bh[r0:r0 + GLA_CHUNK])).astype(kd_ref.dtype)
                gd_ref[rev, hd, 0, c:c + 1] = jnp.exp(tot)

    n_strip = GLA_CHUNK // GLA_STRIP
    colio = lax.broadcasted_iota(jnp.int32, (GLA_STRIP, GLA_CHUNK), 1)
    rowio = lax.broadcasted_iota(jnp.int32, (GLA_STRIP, GLA_CHUNK), 0)

    def strip_scores(rev, hd, c0, bch, qch, kch, u):
        lo, hi = u * GLA_STRIP, (u + 1) * GLA_STRIP
        bu, qu = bch[lo:hi], qch[lo:hi]
        krows, ref = (slice(hi, GLA_CHUNK), hi) if rev else (slice(0, lo), lo - 1)
        if krows.stop > krows.start:
            rb = b_scr[rev, hd, pl.ds(c0 + ref, 1), :]
            qt = qu * jnp.exp2(jnp.minimum(bu - rb, 0.0))
            kt = kch[krows] * jnp.exp2(jnp.minimum(rb - bch[krows], 0.0))
            pad = jnp.zeros((GLA_CHUNK - kt.shape[0], GLA_DK), F32)
            kt = jnp.concatenate([pad, kt] if rev else [kt, pad], axis=0)
            acc = lax.dot_general(qt.astype(BF16), kt.astype(BF16), (((1,), (1,)), ((), ())),
                                  preferred_element_type=F32)
        else:
            acc = jnp.zeros((GLA_STRIP, GLA_CHUNK), F32)
        for s in range(GLA_STRIP):
            ks = kr_scr[hd, pl.ds(c0 + lo + s, 1), :]
            bs = b_scr[rev, hd, pl.ds(c0 + lo + s, 1), :]
            col = jnp.sum(qu * ks * jnp.exp2(bu - bs), axis=1, keepdims=True)
            causal = (rowio <= s) if rev else (rowio >= s)
            acc = jnp.where(jnp.logical_and(colio == lo + s, causal), col, acc)
        return acc

    def chunk_scores(c, carry):
        c0 = pl.multiple_of(c * GLA_CHUNK, GLA_CHUNK)
        for hd in range(GLA_HEADS):
            qch = qr_scr[hd, pl.ds(c0, GLA_CHUNK), :]
            kch = kr_scr[hd, pl.ds(c0, GLA_CHUNK), :]
            for rev in (0, 1):
                bch = b_scr[rev, hd, pl.ds(c0, GLA_CHUNK), :]
                for u in range(0, n_strip, 2):
                    pair = jnp.concatenate([strip_scores(rev, hd, c0, bch, qch, kch, u),
                                            strip_scores(rev, hd, c0, bch, qch, kch, u + 1)], axis=0)
                    a_ref[rev, hd, pl.ds(c0 + u * GLA_STRIP, 2 * GLA_STRIP), :] = pair.astype(a_ref.dtype)
        return carry

    lax.fori_loop(0, tg // GLA_CHUNK, chunk_scores, 0)


def _gla_prep(qk, g, cos2, sin2, w_gate, b_gate, tri, tpb):
    n = qk.shape[0]
    tg = TOK_TILE
    nt = n // tg
    fixed3 = lambda t: (0, 0, 0)
    return pl.pallas_call(
        _gla_prep_kernel,
        grid=(nt,),
        in_specs=[pl.BlockSpec((tg, GLA_QK), lambda t: (t, 0)),
                  pl.BlockSpec((tg, GLA_QK), lambda t: (t, 1)),
                  pl.BlockSpec((tg, 2 * GATE_RANK), lambda t: (t, 0)),
                  pl.BlockSpec((tg, GLA_DK), lambda t: (t % tpb, 0)),
                  pl.BlockSpec((tg, GLA_DK), lambda t: (t % tpb, 0)),
                  pl.BlockSpec((2, GATE_RANK, GLA_QK), fixed3),
                  pl.BlockSpec((2, 1, GLA_QK), fixed3),
                  pl.BlockSpec((2, tg, tg), fixed3)],
        out_specs=[pl.BlockSpec((2, tg, GLA_QK), lambda t: (0, t, 0)),
                   pl.BlockSpec((2, tg, GLA_QK), lambda t: (0, t, 0)),
                   pl.BlockSpec((2, GLA_HEADS, tg, GLA_CHUNK), lambda t: (0, 0, t, 0)),
                   pl.BlockSpec((2, GLA_HEADS, 1, tg // GLA_CHUNK, GLA_DK), lambda t: (0, 0, t, 0, 0))],
        out_shape=[jax.ShapeDtypeStruct((2, n, GLA_QK), BF16),
                   jax.ShapeDtypeStruct((2, n, GLA_QK), BF16),
                   jax.ShapeDtypeStruct((2, GLA_HEADS, n, GLA_CHUNK), BF16),
                   jax.ShapeDtypeStruct((2, GLA_HEADS, nt, tg // GLA_CHUNK, GLA_DK), F32)],
        scratch_shapes=[pltpu.VMEM((2, GLA_HEADS, tg, GLA_DK), F32),
                        pltpu.VMEM((GLA_HEADS, tg, GLA_DK), F32), pltpu.VMEM((GLA_HEADS, tg, GLA_DK), F32)],
        compiler_params=_cparams(("arbitrary",)),
        name="gla_prep",
    )(qk, qk, g, cos2, sin2, w_gate, b_gate, tri)


def _gla_scan_kernel(*refs):
    dirs = (refs[0:5], refs[5:10])
    o_refs = refs[10:12]
    st_ref = refs[12]

    @pl.when(pl.program_id(1) == 0)
    def _():
        st_ref[...] = jnp.zeros_like(st_ref)

    group = refs[0].shape[0]
    n_chunks = refs[0].shape[1] // GLA_CHUNK
    lanes = [(rev, p) for rev in (0, 1) for p in range(group)]
    for hd in range(GLA_HEADS):
        kcols = slice(hd * GLA_DK, (hd + 1) * GLA_DK)
        vcols = slice(hd * GLA_DV, (hd + 1) * GLA_DV)
        states = {lane: st_ref[lane[0], lane[1], hd] for lane in lanes}
        for cc in range(n_chunks):
            for rev, p in lanes:
                qe_ref, kd_ref, a_ref, gd_ref, v_ref = dirs[rev]
                c = n_chunks - 1 - cc if rev else cc
                rows = slice(c * GLA_CHUNK, (c + 1) * GLA_CHUNK)
                st = states[rev, p]
                v_c = v_ref[p, rows, vcols]
                o = lax.dot_general(qe_ref[p, rows, kcols], st.astype(BF16), (((1,), (1,)), ((), ())),
                                    preferred_element_type=F32)
                o = o + jnp.dot(a_ref[hd, p, rows, :], v_c, preferred_element_type=F32)
                o_refs[rev][p, rows, vcols] = o
                upd = lax.dot_general(v_c, kd_ref[p, rows, kcols], (((0,), (0,)), ((), ())),
                                      preferred_element_type=F32)
                states[rev, p] = st * gd_ref[hd, p, 0, c:c + 1, :] + upd
        for rev, p in lanes:
            st_ref[rev, p, hd] = states[rev, p]


def _gla_scan(prep, v, batch, tpb):
    n = v.shape[0]
    tg = TOK_TILE
    n_lat = tpb - 1
    l = tpb * tg
    group = SCAN_GROUP if batch % SCAN_GROUP == 0 else 1
    nb = batch // group
    qe, kd, a, gd = prep
    qe = qe.reshape(2, nb, group, l, GLA_QK)
    kd = kd.reshape(2, nb, group, l, GLA_QK)
    a = a.reshape(2, GLA_HEADS, nb, group, l, GLA_CHUNK)
    gd = gd.reshape(2, GLA_HEADS, nb, group, tpb, tg // GLA_CHUNK, GLA_DK)
    v = v.reshape(nb, group, l, GLA_V)

    def specs(rev):
        def tile(s):
            return jnp.where(s == 0, n_lat, n_lat - s if rev else s - 1)

        ins = [pl.BlockSpec((None, None, group, tg, GLA_QK), lambda bi, s: (rev, bi, 0, tile(s), 0)),
               pl.BlockSpec((None, None, group, tg, GLA_QK), lambda bi, s: (rev, bi, 0, tile(s), 0)),
               pl.BlockSpec((None, GLA_HEADS, None, group, tg, GLA_CHUNK),
                            lambda bi, s: (rev, 0, bi, 0, tile(s), 0)),
               pl.BlockSpec((None, GLA_HEADS, None, group, 1, tg // GLA_CHUNK, GLA_DK),
                            lambda bi, s: (rev, 0, bi, 0, tile(s), 0, 0)),
               pl.BlockSpec((None, group, tg, GLA_V), lambda bi, s: (bi, 0, tile(s), 0))]
        return ins, pl.BlockSpec((None, group, tg, GLA_V), lambda bi, s: (bi, 0, tile(s), 0))

    in_f, out_f = specs(0)
    in_b, out_b = specs(1)
    o_f, o_b = pl.pallas_call(
        _gla_scan_kernel,
        grid=(nb, tpb),
        in_specs=in_f + in_b,
        out_specs=[out_f, out_b],
        out_shape=[jax.ShapeDtypeStruct((nb, group, l, GLA_V), F32)] * 2,
        scratch_shapes=[pltpu.VMEM((2, group, GLA_HEADS, GLA_DV, GLA_DK), F32)],
        compiler_params=_cparams(("arbitrary", "arbitrary")),
        name="gla_scan",
    )(qe, kd, a, gd, v, qe, kd, a, gd, v)
    return o_f.reshape(n, GLA_V), o_b.reshape(n, GLA_V)


def _pack_bf16(x):
    half = x.shape[1] // 2
    lo = lax.bitcast_convert_type(x[:, :half].astype(BF16).astype(F32), jnp.uint32)
    hi = lax.bitcast_convert_type(x[:, half:].astype(BF16).astype(F32), jnp.uint32)
    return (lo >> 16) | (hi & jnp.uint32(0xFFFF0000))


def _unpack_bf16(p):
    lo = lax.bitcast_convert_type(p << 16, F32)
    hi = lax.bitcast_convert_type(p & jnp.uint32(0xFFFF0000), F32)
    return lo, hi


def _residual_ln(x, a, t, ln, alpha):
    y = alpha * x + t[0:1] * a
    mu = jnp.mean(y, axis=-1, keepdims=True)
    yc = y - mu
    var = jnp.mean(yc * yc, axis=-1, keepdims=True)
    xn = yc * lax.rsqrt(var + LN_EPS) * ln[0:1] + ln[1:2]
    return xn, xn * (1.0 + t[1:2]) + t[2:3]


def _top4_softmax(lt):
    e = lt.shape[0]
    io = lax.broadcasted_iota(jnp.int32, lt.shape, 0)
    work = lt
    idxs, vals = [], []
    for _ in range(TOP_K):
        m = jnp.max(work, axis=0, keepdims=True)
        ik = jnp.min(jnp.where(work == m, io, e), axis=0, keepdims=True)
        idxs.append(ik)
        vals.append(m)
        work = jnp.where(io == ik, -jnp.inf, work)
    ex = [jnp.exp(v - vals[0]) for v in vals]
    den = ex[0] + ex[1] + ex[2] + ex[3]
    return jnp.concatenate(idxs, axis=0), jnp.concatenate([x / den for x in ex], axis=0)


def _gla_gated_norm(o, r, gn):
    gate = r * jax.nn.sigmoid(r)
    heads = []
    for hd in range(GLA_HEADS):
        cols = slice(hd * GLA_DV, (hd + 1) * GLA_DV)
        oh = o[:, cols]
        ms = jnp.mean(oh * oh, axis=-1, keepdims=True)
        heads.append((oh * lax.rsqrt(ms + RMS_EPS) * gn * gate[:, cols]).astype(BF16))
    return jnp.concatenate(heads, axis=1)


def _post_kernel(gla, n_act, tpb, alpha, *refs):
    if gla:
        of_ref, ob_ref, r_ref, gn_ref, w_ref = refs[:5]
        rest = refs[5:]
        act = _gla_gated_norm(of_ref[...] + ob_ref[...], r_ref[...], gn_ref[...])
        a = jnp.dot(act, w_ref[...], preferred_element_type=F32)
    else:
        acts = refs[:n_act]
        ws = refs[n_act:2 * n_act]
        rest = refs[2 * n_act:]
        a = jnp.dot(acts[0][...], ws[0][...], preferred_element_type=F32)
        for k in range(1, n_act):
            a = a + jnp.dot(acts[k][...], ws[k][...], preferred_element_type=F32)
    x_ref, tab_ref, ln_ref, rw_ref, rb_ref, xo_ref, h_ref, idx_ref, gate_ref = rest
    tm = x_ref.shape[0]
    ln = ln_ref[...]
    for s in range(tm // TOK_TILE):
        rows = slice(s * TOK_TILE, (s + 1) * TOK_TILE)
        t = tab_ref[_tab_row(pl.program_id(0) * (tm // TOK_TILE) + s, tpb)]
        xn, h = _residual_ln(x_ref[rows], a[rows], t, ln, alpha)
        xo_ref[rows] = xn
        h_ref[rows] = _pack_bf16(h)
        lt = _dot_bf16x3(rw_ref[...], h, ((1,), (1,))) + rb_ref[...]
        idx, gates = _top4_softmax(lt)
        idx_ref[:, rows] = idx
        gate_ref[:, rows] = gates


def _post(acts, ws, x, tab, ln, rw_t, rb, tpb, alpha, gla=False):
    n, d = x.shape
    tm = _row_tile(n)
    e = rw_t.shape[0]
    row = lambda i: (i, 0)
    fixed = lambda i: (0, 0)
    return pl.pallas_call(
        functools.partial(_post_kernel, gla, len(acts), tpb, alpha),
        grid=(n // tm,),
        in_specs=([pl.BlockSpec((tm, a.shape[1]), row) for a in acts]
                  + [pl.BlockSpec(w.shape, fixed) for w in ws]
                  + [pl.BlockSpec((tm, d), row),
                     pl.BlockSpec(tab.shape, lambda i: (0, 0, 0)),
                     pl.BlockSpec(ln.shape, fixed),
                     pl.BlockSpec((e, d), fixed),
                     pl.BlockSpec((e, 1), fixed)]),
        out_specs=[pl.BlockSpec((tm, d), row), pl.BlockSpec((tm, d // 2), row),
                   pl.BlockSpec((TOP_K, tm), lambda i: (0, i)), pl.BlockSpec((TOP_K, tm), lambda i: (0, i))],
        out_shape=[jax.ShapeDtypeStruct((n, d), F32), jax.ShapeDtypeStruct((n, d // 2), jnp.uint32),
                   jax.ShapeDtypeStruct((TOP_K, n), jnp.int32), jax.ShapeDtypeStruct((TOP_K, n), F32)],
        compiler_params=_cparams(("arbitrary",)),
        name="post",
    )(*acts, *ws, x, tab, ln, rw_t, rb)


def _rank_kernel(idx_ref, tri_ref, rank_ref, cnt_ref, carry_ref):
    @pl.when(pl.program_id(0) == 0)
    def _():
        carry_ref[...] = jnp.zeros_like(carry_ref)

    idx = idx_ref[...]
    e = carry_ref.shape[0]
    tr = idx.shape[1]
    io = lax.broadcasted_iota(jnp.int32, (e, tr), 0)
    chosen = jnp.zeros((e, tr), F32)
    for k in range(TOP_K):
        chosen = chosen + (idx[k:k + 1] == io).astype(F32)
    cum = jnp.dot(chosen.astype(BF16), tri_ref[...], preferred_element_type=F32)
    base = carry_ref[:, 0:1]
    excl = base + cum - chosen
    ranks = [jnp.sum(jnp.where(idx[k:k + 1] == io, excl, 0.0), axis=0, keepdims=True) for k in range(TOP_K)]
    rank_ref[...] = jnp.concatenate(ranks, axis=0).astype(jnp.int32)
    carry_ref[...] = carry_ref[...] + jnp.sum(chosen, axis=1, keepdims=True)
    cnt_ref[...] = carry_ref[...]


def _rank(idx_t, n_experts):
    n = idx_t.shape[1]
    tr = _row_tile(n)
    tri = (np.arange(tr)[:, None] <= np.arange(tr)[None, :]).astype(np.float32)
    return pl.pallas_call(
        _rank_kernel,
        grid=(n // tr,),
        in_specs=[pl.BlockSpec((TOP_K, tr), lambda i: (0, i)),
                  pl.BlockSpec((tr, tr), lambda i: (0, 0))],
        out_specs=[pl.BlockSpec((TOP_K, tr), lambda i: (0, i)),
                   pl.BlockSpec((n_experts, LANES), lambda i: (0, 0))],
        out_shape=[jax.ShapeDtypeStruct((TOP_K, n), jnp.int32),
                   jax.ShapeDtypeStruct((n_experts, LANES), F32)],
        scratch_shapes=[pltpu.VMEM((n_experts, LANES), F32)],
        compiler_params=_cparams(("arbitrary",)),
        name="rank",
    )(idx_t, jnp.asarray(tri, BF16))


def _expert_kernel(be_ref, rows_ref, x_ref, w1_ref, b1_ref, w2_ref, b2_ref, o_ref, w1b_ref, w2b_ref):
    i = pl.program_id(0)
    used = rows_ref[i] > 0
    new_expert = jnp.logical_or(i == 0, be_ref[i] != be_ref[jnp.maximum(i - 1, 0)])

    @pl.when(jnp.logical_and(used, new_expert))
    def _():
        w1b_ref[...] = w1_ref[0].astype(BF16)
        w2b_ref[...] = w2_ref[0].astype(BF16)

    @pl.when(used)
    def _():
        x_lo, x_hi = _unpack_bf16(x_ref[...])
        kh = x_lo.shape[1]
        hid = (jnp.dot(x_lo.astype(BF16), w1b_ref[:kh], preferred_element_type=F32)
               + jnp.dot(x_hi.astype(BF16), w1b_ref[kh:], preferred_element_type=F32) + b1_ref[0])
        half = hid.shape[1] // 2
        glu = jnp.minimum(hid[:, :half], SWIGLU_LIMIT)
        lin = jnp.clip(hid[:, half:], -SWIGLU_LIMIT, SWIGLU_LIMIT)
        act = glu * jax.nn.sigmoid(SWIGLU_ALPHA * glu) * (lin + 1.0)
        y = jnp.dot(act.astype(BF16), w2b_ref[...], preferred_element_type=F32) + b2_ref[0]
        o_ref[...] = _pack_bf16(y)

    @pl.when(jnp.logical_not(used))
    def _():
        o_ref[...] = jnp.zeros_like(o_ref)


def _experts(block_expert, block_rows, x_pad, layer, w1, b1, w2, b2):
    n_pad, dp = x_pad.shape
    depth, e, d, dh2 = w1.shape
    n_blocks = n_pad // EXPERT_BLOCK
    grid_spec = pltpu.PrefetchScalarGridSpec(
        num_scalar_prefetch=2,
        grid=(n_blocks,),
        in_specs=[pl.BlockSpec((EXPERT_BLOCK, dp), lambda i, be, nb: (i, 0)),
                  pl.BlockSpec((None, 1, d, dh2), lambda i, be, nb: (layer, be[i], 0, 0)),
                  pl.BlockSpec((None, 1, 1, dh2), lambda i, be, nb: (layer, be[i], 0, 0)),
                  pl.BlockSpec((None, 1, dh2 // 2, d), lambda i, be, nb: (layer, be[i], 0, 0)),
                  pl.BlockSpec((None, 1, 1, d), lambda i, be, nb: (layer, be[i], 0, 0))],
        out_specs=pl.BlockSpec((EXPERT_BLOCK, dp), lambda i, be, nb: (i, 0)),
        scratch_shapes=[pltpu.VMEM((d, dh2), BF16), pltpu.VMEM((dh2 // 2, d), BF16)],
    )
    return pl.pallas_call(
        _expert_kernel,
        grid_spec=grid_spec,
        out_shape=jax.ShapeDtypeStruct((n_pad, dp), jnp.uint32),
        compiler_params=pltpu.CompilerParams(dimension_semantics=("arbitrary",),
                                             vmem_limit_bytes=EXPERT_VMEM_LIMIT),
        name="experts",
    )(block_expert, block_rows, x_pad, w1, b1.reshape(depth, e, 1, dh2), w2, b2.reshape(depth, e, 1, d))


def _stream_tile(i, tpb, latent_only):
    return (i // (tpb - 1)) * tpb + i % (tpb - 1) if latent_only else i


def _combine_kernel(tpb, alpha, splits, y_ref, gate_ref, x_ref, tab_ref, ln_ref, *refs):
    latent_only = splits is None
    g = gate_ref[...]
    y_lo, y_hi = None, None
    for k in range(TOP_K):
        lo, hi = _unpack_bf16(y_ref[k])
        gk = g[:, k:k + 1]
        y_lo = lo * gk if y_lo is None else y_lo + lo * gk
        y_hi = hi * gk if y_hi is None else y_hi + hi * gk
    y = jnp.concatenate([y_lo, y_hi], axis=1)
    ln = ln_ref[...]
    n_sub = x_ref.shape[0] // TOK_TILE
    xns, hs = [], []
    for s in range(n_sub):
        rows = slice(s * TOK_TILE, (s + 1) * TOK_TILE)
        t = tab_ref[_tab_row(_stream_tile(pl.program_id(0) * n_sub + s, tpb, latent_only), tpb)]
        xn, h = _residual_ln(x_ref[rows], y[rows], t, ln, alpha)
        xns.append(xn)
        hs.append(h)
    xn = jnp.concatenate(xns, axis=0)
    if latent_only:
        refs[0][...] = xn
    else:
        w_ref, xo_ref = refs[:2]
        xo_ref[...] = xn
        _emit_proj(jnp.concatenate(hs, axis=0), w_ref, splits, refs[2:])


def _combine(y_g, gates, x, tab, ln, tpb, alpha, plan):
    n, d = x.shape
    latent_only = plan is None
    tm = TOK_TILE if latent_only else _row_tile(n)
    tiles = n // tm
    steps = tiles // tpb * (tpb - 1) if latent_only else tiles
    src = lambda i: (_stream_tile(i, tpb, latent_only), 0)
    in_specs = [pl.BlockSpec((TOP_K, tm, d // 2), lambda i: (0, _stream_tile(i, tpb, latent_only), 0)),
                pl.BlockSpec((tm, TOP_K), src),
                pl.BlockSpec((tm, d), src),
                pl.BlockSpec(tab.shape, lambda i: (0, 0, 0)),
                pl.BlockSpec(ln.shape, lambda i: (0, 0))]
    out_specs = [pl.BlockSpec((tm, d), lambda i: (i, 0))]
    out_shape = [jax.ShapeDtypeStruct((steps * tm, d), F32)]
    operands = [y_g, gates, x, tab, ln]
    if not latent_only:
        w_spec, p_specs, p_shape = _proj_specs(n, tm, plan)
        in_specs.append(w_spec)
        out_specs += p_specs
        out_shape += p_shape
        operands.append(plan[0])
    return pl.pallas_call(
        functools.partial(_combine_kernel, tpb, alpha, None if latent_only else tuple(plan[1])),
        grid=(steps,),
        in_specs=in_specs,
        out_specs=out_specs,
        out_shape=out_shape,
        compiler_params=_cparams(("arbitrary",)),
        name="combine",
    )(*operands)


def _sc_mesh():
    return plsc.VectorSubcoreMesh(core_axis_name="c", subcore_axis_name="s")


def _sc_split(rows, mesh):
    workers = mesh.num_cores * mesh.num_subcores
    per = rows // workers
    assert per * workers == rows
    chunk = SC_CHUNK if per % SC_CHUNK == 0 else 8
    assert per % chunk == 0
    return per, chunk


def _sc_scatter_rows(x, idx, n_out):
    r, c = x.shape
    mesh = _sc_mesh()
    per, chunk = _sc_split(r, mesh)

    n_chunks = per // chunk

    @functools.partial(pl.kernel, out_type=jax.ShapeDtypeStruct((n_out, c), x.dtype), mesh=mesh,
                       scratch_types=[pltpu.VMEM((chunk, c), x.dtype), pltpu.SemaphoreType.DMA] * 2
                       + [pltpu.VMEM((chunk,), jnp.int32)] * TOP_K + [pltpu.SemaphoreType.DMA])
    def scatter(x_hbm, i_hbm, o_hbm, rows_a, sem_a, rows_b, sem_b, *rest):
        idx_vs, sem_s = rest[:TOP_K], rest[TOP_K]
        base = (lax.axis_index("s") * mesh.num_cores + lax.axis_index("c")) * per
        slots = ((rows_a, sem_a), (rows_b, sem_b))

        def load(j, slot):
            rows_v, sem = slot
            off = pl.multiple_of(base + j * chunk, chunk)
            return pltpu.make_async_copy(x_hbm.at[pl.ds(off, chunk)], rows_v, sem)

        def scatter_chunk(j, slot, prefetch):
            rows_v, _ = slot
            off = pl.multiple_of(base + j * chunk, chunk)
            load(j, slot).wait()
            for k in range(TOP_K):
                pltpu.sync_copy(i_hbm.at[pl.ds(k * r + off, chunk)], idx_vs[k])
            for k in range(TOP_K):
                pltpu.async_copy(rows_v, o_hbm.at[idx_vs[k]], sem_s)
            prefetch()
            for k in range(TOP_K):
                pltpu.make_async_copy(rows_v, o_hbm.at[idx_vs[k]], sem_s).wait()

        load(0, slots[0]).start()

        @pl.loop(0, n_chunks // 2)
        def _(p):
            j = 2 * p
            scatter_chunk(j, slots[0], lambda: load(j + 1, slots[1]).start())

            def next_even():
                @pl.when(j + 2 < n_chunks)
                def _():
                    load(j + 2, slots[0]).start()

            scatter_chunk(j + 1, slots[1], next_even)

        if n_chunks % 2:
            scatter_chunk(n_chunks - 1, slots[0], lambda: None)

    return scatter(x, idx)


def _sc_gather_rows(table, idx):
    m = idx.shape[0]
    c = table.shape[1]
    mesh = _sc_mesh()
    per, chunk = _sc_split(m, mesh)

    n_chunks = per // chunk
    slot_types = [pltpu.VMEM((chunk,), jnp.int32), pltpu.VMEM((chunk, c), table.dtype), pltpu.SemaphoreType.DMA]

    @functools.partial(pl.kernel, out_type=jax.ShapeDtypeStruct((m, c), table.dtype), mesh=mesh,
                       scratch_types=slot_types * 2)
    def gather(t_hbm, i_hbm, o_hbm, idx_a, rows_a, sem_a, idx_b, rows_b, sem_b):
        base = (lax.axis_index("s") * mesh.num_cores + lax.axis_index("c")) * per
        slots = ((idx_a, rows_a, sem_a), (idx_b, rows_b, sem_b))

        def start(j, slot):
            idx_v, rows_v, sem = slot
            off = pl.multiple_of(base + j * chunk, chunk)
            pltpu.sync_copy(i_hbm.at[pl.ds(off, chunk)], idx_v)
            pltpu.async_copy(t_hbm.at[idx_v], rows_v, sem)

        def finish(j, slot):
            idx_v, rows_v, sem = slot
            off = pl.multiple_of(base + j * chunk, chunk)
            pltpu.make_async_copy(t_hbm.at[idx_v], rows_v, sem).wait()
            pltpu.sync_copy(rows_v, o_hbm.at[pl.ds(off, chunk)])

        start(0, slots[0])

        @pl.loop(0, n_chunks // 2)
        def _(p):
            j = 2 * p
            start(j + 1, slots[1])
            finish(j, slots[0])

            @pl.when(j + 2 < n_chunks)
            def _():
                start(j + 2, slots[0])

            finish(j + 1, slots[1])

        if n_chunks % 2:
            finish(n_chunks - 1, slots[0])

    return gather(table, idx)


def _moe(h, idx_t, layer, w1, b1, w2, b2):
    n, dp = h.shape
    e = w1.shape[1]
    m = n * TOP_K
    rank_t, cnt = _rank(idx_t, e)
    sizes = cnt[:, 0].astype(jnp.int32)
    padded = (sizes + EXPERT_BLOCK - 1) // EXPERT_BLOCK * EXPERT_BLOCK
    pad_ends = jnp.cumsum(padded)
    pad_starts = pad_ends - padded
    ids = jnp.arange(e, dtype=jnp.int32)[:, None, None]
    dest_t = jnp.sum(jnp.where(idx_t[None] == ids, pad_starts[:, None, None], 0), axis=0) + rank_t
    dest = dest_t.reshape(-1)
    n_blocks = (m + e * (EXPERT_BLOCK - 1)) // EXPERT_BLOCK + 1
    n_pad = n_blocks * EXPERT_BLOCK
    block_start = jnp.arange(n_blocks, dtype=jnp.int32) * EXPERT_BLOCK
    block_expert = jnp.minimum(jnp.sum((pad_ends[None, :] <= block_start[:, None]).astype(jnp.int32), axis=1), e - 1)
    group_end = jnp.sum(jnp.where(block_expert[:, None] == jnp.arange(e, dtype=jnp.int32)[None, :],
                                  (pad_starts + sizes)[None, :], 0), axis=1)
    block_rows = jnp.clip(group_end - block_start, 0, EXPERT_BLOCK).astype(jnp.int32)
    x_pad = _sc_scatter_rows(h, dest, n_pad)
    y_pad = _experts(block_expert, block_rows, x_pad, layer, w1, b1, w2, b2)
    return _sc_gather_rows(y_pad, dest).reshape(TOP_K, n, dp)


def _rope_tables(t_lat, n_ctx):
    t = jnp.arange(t_lat)
    row = (t // GRID_W).astype(F32)
    col = (t % GRID_W).astype(F32)
    nf = GLA_DK // 4
    freqs = ROPE_BASE ** (-jnp.arange(nf, dtype=F32) / nf)
    ang = jnp.concatenate([row[:, None] * freqs, col[:, None] * freqs], axis=-1)
    cos, sin = jnp.cos(ang), jnp.sin(ang)
    cos2 = jnp.concatenate([cos, cos], axis=-1)
    sin2 = jnp.concatenate([-sin, sin], axis=-1)
    return (jnp.concatenate([cos2, jnp.ones((n_ctx, GLA_DK), F32)], axis=0),
            jnp.concatenate([sin2, jnp.zeros((n_ctx, GLA_DK), F32)], axis=0))


def _chunk_tri(tg, rev):
    t = np.arange(tg)
    same = (t[:, None] // GLA_CHUNK) == (t[None, :] // GLA_CHUNK)
    side = (t[None, :] >= t[:, None]) if rev else (t[None, :] <= t[:, None])
    return jnp.asarray((same & side).astype(np.float32), BF16)


def _table(mods, rows, batch):
    lat = jnp.stack([mods[:batch, r] for r in rows], axis=1)
    ctx = jnp.broadcast_to(jnp.stack([mods[batch, r] for r in rows], axis=0)[None], lat.shape)
    tab = jnp.stack([lat, ctx], axis=1).reshape(2 * batch, len(rows), -1)
    return jnp.pad(tab, ((0, 0), (0, 8 - len(rows)), (0, 0)))


@jax.jit
def _forward(x, c, ctx, c_ctx, ada_w, ada_b, ln_g, ln_b, ab_w_in, ab_pool_w, ab_pool_scale, ab_rpb,
             ab_w_out, gla_w_in, gla_w_gate, gla_b_gate, gla_norm_g, gla_w_out, router_w, router_b,
             exp_w1, exp_b1, exp_w2, exp_b2):
    batch, t_lat, d = x.shape
    n_ctx = ctx.shape[1]
    depth = ada_w.shape[0]
    assert d == D_MODEL and n_ctx == TOK_TILE and t_lat % TOK_TILE == 0
    rows = t_lat // GRID_W
    assert rows % NA_QROWS == 0 and rows >= NA_KROWS + NA_QROWS
    n_lat = t_lat // TOK_TILE
    tpb = n_lat + 1
    l = t_lat + n_ctx
    n = batch * l
    alpha = (2.0 * depth) ** 0.25

    cc = jnp.concatenate([c, c_ctx[None], jnp.zeros((16 - batch - 1, d), F32)], axis=0)
    mods = _mods(cc, ada_w, ada_b).reshape(depth, 16, N_MOD, d)

    def in_proj_plan(i):
        if i % 2 == 0:
            return (ab_w_in[i // 2].astype(BF16),
                    [(0, POOL_WIDTH), (POOL_WIDTH, POOL_WIDTH + 3 * NA_WIDTH)], [F32, BF16])
        edges = (0, 2 * GLA_QK, 2 * GLA_QK + GLA_V, 2 * GLA_QK + 2 * GLA_V, 2 * GLA_QK + 2 * GLA_V + 2 * GATE_RANK)
        return gla_w_in[i // 2].astype(BF16), list(zip(edges[:-1], edges[1:])), [F32, BF16, F32, F32]

    z, *projected = _modulate(x.reshape(batch * t_lat, d), ctx.reshape(batch * n_ctx, d),
                              _table(mods[0], (1, 0), batch), tpb, in_proj_plan(0))
    cos2, sin2 = _rope_tables(t_lat, n_ctx)

    for i in range(depth):
        j = i // 2
        last = i == depth - 1
        tab1 = _table(mods[i], (2, 4, 3), batch)
        ln1 = jnp.stack([ln_g[i, 0], ln_b[i, 0]])
        ln2 = jnp.stack([ln_g[i, 1], ln_b[i, 1]])
        rw_t = router_w[i].T
        rb = router_b[i][:, None]
        if i % 2 == 0:
            u, qkv = projected
            w_blk = jax.scipy.linalg.block_diag(*[ab_pool_w[j, g] for g in range(len(POOL_WINDOWS))])
            bias = _na_bias_tables(ab_rpb[j], rows)
            attn, pooled = _na_pool(qkv.reshape(batch, l, 3 * NA_WIDTH), bias, u.reshape(batch, l, POOL_WIDTH),
                                    w_blk.astype(BF16), ab_pool_scale[j][None, :], n_lat, t_lat, n_ctx)
            w_out = ab_w_out[j].astype(BF16)
            acts = [pooled.reshape(n, POOL_WIDTH), attn.reshape(n, NA_WIDTH)]
            ws = [w_out[:POOL_WIDTH], w_out[POOL_WIDTH:]]
        else:
            qk, v, r, g = projected
            tri = jnp.stack([_chunk_tri(TOK_TILE, False), _chunk_tri(TOK_TILE, True)])
            prep = _gla_prep(qk, g, cos2, sin2, gla_w_gate[j], gla_b_gate[j][:, None, :], tri, tpb)
            o_f, o_b = _gla_scan(prep, v, batch, tpb)
            acts = [o_f, o_b, r]
            ws = [gla_norm_g[j][None, :], gla_w_out[j].astype(BF16)]
        z, h, idx_t, gates_t = _post(acts, ws, z, tab1, ln1, rw_t, rb, tpb, alpha, gla=i % 2 == 1)
        y_g = _moe(h, idx_t, i, exp_w1, exp_b1, exp_w2, exp_b2)
        nxt = mods[i + 1] if not last else mods[i]
        tab2 = _table(jnp.concatenate([mods[i][:, 5:6], nxt[:, 1:2], nxt[:, 0:1]], axis=1), (0, 1, 2), batch)
        if last:
            (out,) = _combine(y_g, gates_t.T, z, tab2, ln2, tpb, alpha, None)
            return out.reshape(batch, t_lat, d)
        z, *projected = _combine(y_g, gates_t.T, z, tab2, ln2, tpb, alpha, in_proj_plan(i + 1))


def kernel(x, c, ctx, c_ctx, ada_w, ada_b, ln_g, ln_b, ab_w_in, ab_pool_w, ab_pool_scale, ab_rpb, ab_w_out,
           gla_w_in, gla_w_gate, gla_b_gate, gla_norm_g, gla_w_out, router_w, router_b, exp_w1, exp_b1, exp_w2,
           exp_b2):
    return _forward(x, c, ctx, c_ctx, ada_w, ada_b, ln_g, ln_b, ab_w_in, ab_pool_w, ab_pool_scale, ab_rpb,
                    ab_w_out, gla_w_in, gla_w_gate, gla_b_gate, gla_norm_g, gla_w_out, router_w, router_b,
                    exp_w1, exp_b1, exp_w2, exp_b2)
```

```python
import functools
import math

import numpy as np
import jax
import jax.numpy as jnp
from jax import lax
from jax.experimental import pallas as pl
from jax.experimental.pallas import tpu as pltpu
from jax.experimental.pallas import tpu_sc as plsc

F32 = jnp.float32
BF16 = jnp.bfloat16
HIGHEST = lax.Precision.HIGHEST

D_MODEL = 1024
GRID_W = 64
N_MOD = 6
POOL_WINDOWS = (2, 4, 8, 16)
POOL_WIDTH = D_MODEL // 4
POOL_GROUP_DIM = POOL_WIDTH // len(POOL_WINDOWS)
POOL_HALO = max(POOL_WINDOWS) // 2
NA_HEAD_DIM = 64
NA_HEADS = (D_MODEL - POOL_WIDTH) // NA_HEAD_DIM
NA_WIDTH = NA_HEADS * NA_HEAD_DIM
WIN_H = 8
WIN_W = 16
GLA_HEADS = 4
GLA_DK = D_MODEL // 2 // GLA_HEADS
GLA_DV = D_MODEL // GLA_HEADS
GATE_RANK = 16
GATE_NORM = 16.0
GLA_CHUNK = 64
GLA_STRIP = 8
GLA_QK = GLA_HEADS * GLA_DK
GLA_V = GLA_HEADS * GLA_DV
ROPE_BASE = 10000.0
TOP_K = 4
SWIGLU_LIMIT = 7.0
SWIGLU_ALPHA = 1.702
LN_EPS = 1e-5
RMS_EPS = 1e-6
NEG_INF = -1e30
LOG2_E = math.log2(math.e)

LANES = 128
TOK_TILE = 256
NA_QROWS = 4
NA_KROWS = 12
NA_PAIRS_PER_STEP = 6
SCAN_GROUP = 4
VMEM_LIMIT = 48 * 1024 * 1024
EXPERT_VMEM_LIMIT = 56 * 1024 * 1024
EXPERT_BLOCK = 512
SC_CHUNK = 64


def _cparams(sem):
    return pltpu.CompilerParams(dimension_semantics=sem, vmem_limit_bytes=VMEM_LIMIT)


def _dot_bf16x3(a, b, dims):
    a_hi = a.astype(BF16)
    a_lo = (a - a_hi.astype(F32)).astype(BF16)
    b_hi = b.astype(BF16)
    b_lo = (b - b_hi.astype(F32)).astype(BF16)
    dg = functools.partial(lax.dot_general, dimension_numbers=(dims, ((), ())), preferred_element_type=F32)
    return dg(a_hi, b_hi) + dg(a_hi, b_lo) + dg(a_lo, b_hi)


def _row_tile(n):
    return 2 * TOK_TILE if n % (2 * TOK_TILE) == 0 else TOK_TILE


def _mods_kernel(c_ref, w_ref, b_ref, o_ref):
    cv = c_ref[...]
    sc = cv * jax.nn.sigmoid(cv)
    o_ref[0] = jnp.dot(sc, w_ref[0], precision=HIGHEST, preferred_element_type=F32) + b_ref[0]


def _mods(cc, ada_w, ada_b):
    depth, d, n = ada_w.shape
    r = cc.shape[0]
    tn = n // 4
    return pl.pallas_call(
        _mods_kernel,
        grid=(depth, n // tn),
        in_specs=[pl.BlockSpec((r, d), lambda i, j: (0, 0)),
                  pl.BlockSpec((1, d, tn), lambda i, j: (i, 0, j)),
                  pl.BlockSpec((1, 1, tn), lambda i, j: (i, 0, j))],
        out_specs=pl.BlockSpec((1, r, tn), lambda i, j: (i, 0, j)),
        out_shape=jax.ShapeDtypeStruct((depth, r, n), F32),
        compiler_params=_cparams(("arbitrary", "arbitrary")),
        name="mods",
    )(cc, ada_w, ada_b.reshape(depth, 1, n))


def _tab_row(g, tpb):
    return (g // tpb) * 2 + (g % tpb == tpb - 1).astype(jnp.int32)


def _modulate_kernel(tpb, splits, x_ref, ctx_ref, tab_ref, w_ref, z_ref, *out_refs):
    i = pl.program_id(0)
    is_ctx = i % tpb == tpb - 1
    z = jnp.where(is_ctx, ctx_ref[...], x_ref[...])
    t = tab_ref[_tab_row(i, tpb)]
    z_ref[...] = z
    _emit_proj(z * (1.0 + t[0:1]) + t[1:2], w_ref, splits, out_refs)


def _modulate(x, ctx, tab, tpb, plan):
    d = x.shape[1]
    n_lat = tpb - 1
    n = x.shape[0] + ctx.shape[0]
    w_spec, p_specs, p_shape = _proj_specs(n, TOK_TILE, plan)
    return pl.pallas_call(
        functools.partial(_modulate_kernel, tpb, tuple(plan[1])),
        grid=(n // TOK_TILE,),
        in_specs=[pl.BlockSpec((TOK_TILE, d), lambda i: ((i // tpb) * n_lat + jnp.minimum(i % tpb, n_lat - 1), 0)),
                  pl.BlockSpec((TOK_TILE, d), lambda i: (i // tpb, 0)),
                  pl.BlockSpec(tab.shape, lambda i: (0, 0, 0)),
                  w_spec],
        out_specs=[pl.BlockSpec((TOK_TILE, d), lambda i: (i, 0))] + p_specs,
        out_shape=[jax.ShapeDtypeStruct((n, d), F32)] + p_shape,
        compiler_params=_cparams(("arbitrary",)),
        name="modulate",
    )(x, ctx, tab, plan[0])


def _emit_proj(h, w_ref, splits, out_refs):
    hb = h.astype(BF16)
    for (a, b), o_ref in zip(splits, out_refs):
        o_ref[...] = jnp.dot(hb, w_ref[:, a:b], preferred_element_type=F32).astype(o_ref.dtype)


def _proj_specs(n, tm, plan):
    w, splits, dtypes = plan
    w_spec = pl.BlockSpec(w.shape, lambda i: (0, 0), pipeline_mode=pl.Buffered(1))
    out_specs = [pl.BlockSpec((tm, b - a), lambda i: (i, 0)) for a, b in splits]
    out_shape = [jax.ShapeDtypeStruct((n, b - a), dt) for (a, b), dt in zip(splits, dtypes)]
    return w_spec, out_specs, out_shape


def _pool_tile(n_lat, t_lat, t_ctx, prev_ref, cur_ref, next_ref, w_ref, scale_ref, o_ref, halo_ref):
    j = pl.program_id(1)
    is_ctx = j == n_lat
    has_prev = jnp.logical_and(j > 0, jnp.logical_not(is_ctx))
    has_next = j < n_lat - 1
    cur = cur_ref[0]
    hl = POOL_HALO
    halo_ref[0:hl] = jnp.where(has_prev, prev_ref[0, TOK_TILE - hl:TOK_TILE], 0.0)
    halo_ref[hl:hl + TOK_TILE] = cur
    halo_ref[hl + TOK_TILE:2 * hl + TOK_TILE] = jnp.where(has_next, next_ref[0, 0:hl], 0.0)

    shape = cur.shape
    lane = lax.broadcasted_iota(jnp.int32, shape, 1)
    group = lane // POOL_GROUP_DIM
    half = jnp.ones(shape, jnp.int32)
    for gi, wdw in enumerate(POOL_WINDOWS):
        half = jnp.where(group == gi, wdw // 2, half)
    acc = jnp.zeros(shape, F32)
    for off in range(-hl, hl):
        v = halo_ref[hl + off:hl + off + TOK_TILE]
        inside = (half >= -off) if off < 0 else (half > off)
        acc = acc + jnp.where(inside, v, 0.0)
    pos0 = jnp.where(is_ctx, 0, j * TOK_TILE)
    seq = jnp.where(is_ctx, t_ctx, t_lat)
    t = pos0 + lax.broadcasted_iota(jnp.int32, shape, 0)
    cnt = jnp.minimum(t + half, seq) - jnp.maximum(t - half, 0)
    pooled = acc / cnt.astype(F32) - cur
    y = jnp.dot(pooled.astype(BF16), w_ref[...], preferred_element_type=F32) * scale_ref[...]
    o_ref[0] = y.astype(o_ref.dtype)


def _na_bias_tables(rpb, rows):
    n_i = rows // NA_QROWS
    heads = rpb.shape[0]
    a = np.arange(NA_QROWS)
    kr = np.arange(NA_KROWS)
    cq = np.arange(GRID_W)
    ws = np.clip(cq - WIN_W // 2, 0, GRID_W - WIN_W)
    ok_col = (cq[None, :] >= ws[:, None]) & (cq[None, :] < ws[:, None] + WIN_W)
    dcol = np.clip(cq[None, :] - cq[:, None] + WIN_W - 1, 0, 2 * WIN_W - 2)
    oh_col = (dcol[..., None] == np.arange(2 * WIN_W - 1)).astype(np.float32)
    tabs = []
    for i in (0, 1, n_i - 1):
        start = int(np.clip(NA_QROWS * i - WIN_H // 2, 0, rows - NA_KROWS))
        r = NA_QROWS * i + a
        krow = start + kr
        rs = np.clip(r - WIN_H // 2, 0, rows - WIN_H)
        ok_row = (krow[None, :] >= rs[:, None]) & (krow[None, :] < rs[:, None] + WIN_H)
        drow = np.clip(krow[None, :] - r[:, None] + WIN_H - 1, 0, 2 * WIN_H - 2)
        oh_row = (drow[..., None] == np.arange(2 * WIN_H - 1)).astype(np.float32)
        by_row = jnp.einsum('hrc,akr->hakc', rpb, oh_row, precision=HIGHEST)
        bias = jnp.einsum('hakc,qjc->haqkj', by_row, oh_col, precision=HIGHEST)
        ok = ok_row[:, None, :, None] & ok_col[None, :, None, :]
        tabs.append(jnp.where(ok[None], bias, NEG_INF).reshape(heads, TOK_TILE, NA_KROWS * GRID_W))
    tabs.append(jnp.full_like(tabs[0], NEG_INF))
    return jnp.stack(tabs).astype(F32)


def _na_kernel(pool_args, q_ref, k0_ref, k1_ref, k2_ref, kc_ref, v0_ref, v1_ref, v2_ref, vc_ref, bias_ref,
               uprev_ref, ucur_ref, unext_ref, pw_ref, ps_ref, o_ref, po_ref, halo_ref):
    _pool_tile(*pool_args, uprev_ref, ucur_ref, unext_ref, pw_ref, ps_ref, po_ref, halo_ref)
    k_refs = (k0_ref, k1_ref, k2_ref, kc_ref)
    v_refs = (v0_ref, v1_ref, v2_ref, vc_ref)
    n_band = len(k_refs) - 1
    lane = lax.broadcasted_iota(jnp.int32, (TOK_TILE, LANES), 1)
    first = lane < NA_HEAD_DIM
    for pair in range(NA_PAIRS_PER_STEP):
        cols = slice(pair * LANES, (pair + 1) * LANES)
        q = q_ref[0, :, cols]
        outs = []
        for hh in range(2):
            mine = first if hh == 0 else jnp.logical_not(first)
            qm = jnp.where(mine, q, jnp.zeros_like(q)) * NA_HEAD_DIM ** -0.5
            scores = []
            for j, k_ref in enumerate(k_refs):
                s = lax.dot_general(qm, k_ref[0, :, cols], (((1,), (1,)), ((), ())), preferred_element_type=F32)
                if j < n_band:
                    s = s + bias_ref[0, 2 * pair + hh, :, j * TOK_TILE:(j + 1) * TOK_TILE]
                scores.append(s)
            m = functools.reduce(jnp.maximum, [jnp.max(s, axis=-1, keepdims=True) for s in scores])
            l = 0.0
            o = 0.0
            for s, v_ref in zip(scores, v_refs):
                p = jnp.exp(s - m)
                l = l + jnp.sum(p, axis=-1, keepdims=True)
                o = o + jnp.dot(p.astype(BF16), v_ref[0, :, cols], preferred_element_type=F32)
            outs.append(o / l)
        o_ref[0, :, cols] = jnp.where(first, outs[0], outs[1]).astype(o_ref.dtype)


def _na_pool(qkv, bias, u, pool_w, pool_scale, n_lat, t_lat, t_ctx):
    b, l, _ = qkv.shape
    pw = u.shape[-1]
    tpb = l // TOK_TILE
    n_groups = NA_WIDTH // (LANES * NA_PAIRS_PER_STEP)
    assert n_groups == 1
    blk = (1, TOK_TILE, LANES * NA_PAIRS_PER_STEP)
    ublk = (1, TOK_TILE, pw)

    def kstart(i):
        return jnp.clip(i - 1, 0, n_lat - NA_KROWS // NA_QROWS)

    def btype(i):
        return jnp.where(i == 0, 0, jnp.where(i == n_lat - 1, 2, jnp.where(i == n_lat, 3, 1)))

    def kv_spec(col0, j):
        return pl.BlockSpec(blk, lambda hp, i, bi: (bi, kstart(i) + j, col0 + hp))

    def ctx_spec(col0):
        return pl.BlockSpec(blk, lambda hp, i, bi: (bi, n_lat, col0 + hp))

    nk = bias.shape[-1]
    return pl.pallas_call(
        functools.partial(_na_kernel, (n_lat, t_lat, t_ctx)),
        grid=(n_groups, tpb, b),
        in_specs=[pl.BlockSpec(blk, lambda hp, i, bi: (bi, i, hp)),
                  kv_spec(n_groups, 0), kv_spec(n_groups, 1), kv_spec(n_groups, 2), ctx_spec(n_groups),
                  kv_spec(2 * n_groups, 0), kv_spec(2 * n_groups, 1), kv_spec(2 * n_groups, 2),
                  ctx_spec(2 * n_groups),
                  pl.BlockSpec((1, 2 * NA_PAIRS_PER_STEP, TOK_TILE, nk), lambda hp, i, bi: (btype(i), hp, 0, 0)),
                  pl.BlockSpec(ublk, lambda hp, i, bi: (bi, jnp.maximum(i - 1, 0), 0)),
                  pl.BlockSpec(ublk, lambda hp, i, bi: (bi, i, 0)),
                  pl.BlockSpec(ublk, lambda hp, i, bi: (bi, jnp.minimum(i + 1, tpb - 1), 0)),
                  pl.BlockSpec((pw, pw), lambda hp, i, bi: (0, 0)),
                  pl.BlockSpec((1, pw), lambda hp, i, bi: (0, 0))],
        out_specs=[pl.BlockSpec(blk, lambda hp, i, bi: (bi, i, hp)),
                   pl.BlockSpec(ublk, lambda hp, i, bi: (bi, i, 0))],
        out_shape=[jax.ShapeDtypeStruct((b, l, NA_WIDTH), BF16), jax.ShapeDtypeStruct((b, l, pw), BF16)],
        scratch_shapes=[pltpu.VMEM((TOK_TILE + 2 * POOL_HALO, pw), F32)],
        compiler_params=_cparams(("arbitrary", "arbitrary", "arbitrary")),
        name="na_pool",
    )(qkv, qkv, qkv, qkv, qkv, qkv, qkv, qkv, qkv, bias, u, u, u, pool_w, pool_scale)


def _log_sigmoid(z):
    return jnp.minimum(z, 0.0) - jnp.log(1.0 + jnp.exp(-jnp.abs(z)))


def _gla_prep_kernel(q_ref, k_ref, g_ref, cos_ref, sin_ref, wg_ref, bg_ref, tri_ref,
                     qe_ref, kd_ref, a_ref, gd_ref, b_scr, qr_scr, kr_scr):
    tg = q_ref.shape[0]
    head_cols = [slice(hd * GLA_DK, (hd + 1) * GLA_DK) for hd in range(GLA_HEADS)]
    cosv = cos_ref[...]
    sinv = sin_ref[...]
    rotated = []
    for hd, cols in enumerate(head_cols):
        q = q_ref[:, cols]
        k = k_ref[:, cols]
        qr = (q * cosv + pltpu.roll(q, GLA_DK // 2, 1) * sinv) * GLA_DK ** -0.5
        kr = k * cosv + pltpu.roll(k, GLA_DK // 2, 1) * sinv
        qr_scr[hd] = qr
        kr_scr[hd] = kr
        rotated.append((qr, kr))
    for rev in (0, 1):
        gg = g_ref[:, rev * GATE_RANK:(rev + 1) * GATE_RANK]
        z = _dot_bf16x3(gg, wg_ref[rev], ((1,), (0,))) + bg_ref[rev]
        la = _log_sigmoid(z) * (1.0 / GATE_NORM)
        tri = tri_ref[rev]
        b = jnp.zeros_like(la)
        rest = la
        for _ in range(3):
            piece = rest.astype(BF16)
            b = b + jnp.dot(tri, piece, preferred_element_type=F32)
            rest = rest - piece.astype(F32)
        for hd, cols in enumerate(head_cols):
            qr, kr = rotated[hd]
            bh = b[:, cols]
            b_scr[rev, hd] = bh * LOG2_E
            qe_ref[rev, :, cols] = (qr * jnp.exp(bh)).astype(qe_ref.dtype)
            for c in range(tg // GLA_CHUNK):
                r0 = c * GLA_CHUNK
                last = r0 if rev else r0 + GLA_CHUNK - 1
                tot = bh[last:last + 1]
                kd_ref[rev, r0:r0 + GLA_CHUNK, cols] = (
                    kr[r0:r0 + GLA_CHUNK] * jnp.exp(tot - bh[r0:r0 + GLA_CHUNK])).astype(kd_ref.dtype)
                gd_ref[rev, hd, 0, c:c + 1] = jnp.exp(tot)

    n_strip = GLA_CHUNK // GLA_STRIP
    colio = lax.broadcasted_iota(jnp.int32, (GLA_STRIP, GLA_CHUNK), 1)
    rowio = lax.broadcasted_iota(jnp.int32, (GLA_STRIP, GLA_CHUNK), 0)

    def strip_scores(rev, hd, c0, bch, qch, kch, u):
        lo, hi = u * GLA_STRIP, (u + 1) * GLA_STRIP
        bu, qu = bch[lo:hi], qch[lo:hi]
        krows, ref = (slice(hi, GLA_CHUNK), hi) if rev else (slice(0, lo), lo - 1)
        if krows.stop > krows.start:
            rb = b_scr[rev, hd, pl.ds(c0 + ref, 1), :]
            qt = qu * jnp.exp2(jnp.minimum(bu - rb, 0.0))
            kt = kch[krows] * jnp.exp2(jnp.minimum(rb - bch[krows], 0.0))
            pad = jnp.zeros((GLA_CHUNK - kt.shape[0], GLA_DK), F32)
            kt = jnp.concatenate([pad, kt] if rev else [kt, pad], axis=0)
            acc = lax.dot_general(qt.astype(BF16), kt.astype(BF16), (((1,), (1,)), ((), ())),
                                  preferred_element_type=F32)
        else:
            acc = jnp.zeros((GLA_STRIP, GLA_CHUNK), F32)
        for s in range(GLA_STRIP):
            ks = kr_scr[hd, pl.ds(c0 + lo + s, 1), :]
            bs = b_scr[rev, hd, pl.ds(c0 + lo + s, 1), :]
            col = jnp.sum(qu * ks * jnp.exp2(bu - bs), axis=1, keepdims=True)
            causal = (rowio <= s) if rev else (rowio >= s)
            acc = jnp.where(jnp.logical_and(colio == lo + s, causal), col, acc)
        return acc

    def chunk_scores(c, carry):
        c0 = pl.multiple_of(c * GLA_CHUNK, GLA_CHUNK)
        for hd in range(GLA_HEADS):
            qch = qr_scr[hd, pl.ds(c0, GLA_CHUNK), :]
            kch = kr_scr[hd, pl.ds(c0, GLA_CHUNK), :]
            for rev in (0, 1):
                bch = b_scr[rev, hd, pl.ds(c0, GLA_CHUNK), :]
                for u in range(0, n_strip, 2):
                    pair = jnp.concatenate([strip_scores(rev, hd, c0, bch, qch, kch, u),
                                            strip_scores(rev, hd, c0, bch, qch, kch, u + 1)], axis=0)
                    a_ref[rev, hd, pl.ds(c0 + u * GLA_STRIP, 2 * GLA_STRIP), :] = pair.astype(a_ref.dtype)
        return carry

    lax.fori_loop(0, tg // GLA_CHUNK, chunk_scores, 0)


def _gla_prep(qk, g, cos2, sin2, w_gate, b_gate, tri, tpb):
    n = qk.shape[0]
    tg = TOK_TILE
    nt = n // tg
    fixed3 = lambda t: (0, 0, 0)
    return pl.pallas_call(
        _gla_prep_kernel,
        grid=(nt,),
        in_specs=[pl.BlockSpec((tg, GLA_QK), lambda t: (t, 0)),
                  pl.BlockSpec((tg, GLA_QK), lambda t: (t, 1)),
                  pl.BlockSpec((tg, 2 * GATE_RANK), lambda t: (t, 0)),
                  pl.BlockSpec((tg, GLA_DK), lambda t: (t % tpb, 0)),
                  pl.BlockSpec((tg, GLA_DK), lambda t: (t % tpb, 0)),
                  pl.BlockSpec((2, GATE_RANK, GLA_QK), fixed3),
                  pl.BlockSpec((2, 1, GLA_QK), fixed3),
                  pl.BlockSpec((2, tg, tg), fixed3)],
        out_specs=[pl.BlockSpec((2, tg, GLA_QK), lambda t: (0, t, 0)),
                   pl.BlockSpec((2, tg, GLA_QK), lambda t: (0, t, 0)),
                   pl.BlockSpec((2, GLA_HEADS, tg, GLA_CHUNK), lambda t: (0, 0, t, 0)),
                   pl.BlockSpec((2, GLA_HEADS, 1, tg // GLA_CHUNK, GLA_DK), lambda t: (0, 0, t, 0, 0))],
        out_shape=[jax.ShapeDtypeStruct((2, n, GLA_QK), BF16),
                   jax.ShapeDtypeStruct((2, n, GLA_QK), BF16),
                   jax.ShapeDtypeStruct((2, GLA_HEADS, n, GLA_CHUNK), BF16),
                   jax.ShapeDtypeStruct((2, GLA_HEADS, nt, tg // GLA_CHUNK, GLA_DK), F32)],
        scratch_shapes=[pltpu.VMEM((2, GLA_HEADS, tg, GLA_DK), F32),
                        pltpu.VMEM((GLA_HEADS, tg, GLA_DK), F32), pltpu.VMEM((GLA_HEADS, tg, GLA_DK), F32)],
        compiler_params=_cparams(("arbitrary",)),
        name="gla_prep",
    )(qk, qk, g, cos2, sin2, w_gate, b_gate, tri)


def _gla_scan_kernel(*refs):
    dirs = (refs[0:5], refs[5:10])
    o_refs = refs[10:12]
    st_ref = refs[12]

    @pl.when(pl.program_id(1) == 0)
    def _():
        st_ref[...] = jnp.zeros_like(st_ref)

    group = refs[0].shape[0]
    n_chunks = refs[0].shape[1] // GLA_CHUNK
    lanes = [(rev, p) for rev in (0, 1) for p in range(group)]
    for hd in range(GLA_HEADS):
        kcols = slice(hd * GLA_DK, (hd + 1) * GLA_DK)
        vcols = slice(hd * GLA_DV, (hd + 1) * GLA_DV)
        states = {lane: st_ref[lane[0], lane[1], hd] for lane in lanes}
        for cc in range(n_chunks):
            for rev, p in lanes:
                qe_ref, kd_ref, a_ref, gd_ref, v_ref = dirs[rev]
                c = n_chunks - 1 - cc if rev else cc
                rows = slice(c * GLA_CHUNK, (c + 1) * GLA_CHUNK)
                st = states[rev, p]
                v_c = v_ref[p, rows, vcols]
                o = lax.dot_general(qe_ref[p, rows, kcols], st.astype(BF16), (((1,), (1,)), ((), ())),
                                    preferred_element_type=F32)
                o = o + jnp.dot(a_ref[hd, p, rows, :], v_c, preferred_element_type=F32)
                o_refs[rev][p, rows, vcols] = o
                upd = lax.dot_general(v_c, kd_ref[p, rows, kcols], (((0,), (0,)), ((), ())),
                                      preferred_element_type=F32)
                states[rev, p] = st * gd_ref[hd, p, 0, c:c + 1, :] + upd
        for rev, p in lanes:
            st_ref[rev, p, hd] = states[rev, p]


def _gla_scan(prep, v, batch, tpb):
    n = v.shape[0]
    tg = TOK_TILE
    n_lat = tpb - 1
    l = tpb * tg
    group = SCAN_GROUP if batch % SCAN_GROUP == 0 else 1
    nb = batch // group
    qe, kd, a, gd = prep
    qe = qe.reshape(2, nb, group, l, GLA_QK)
    kd = kd.reshape(2, nb, group, l, GLA_QK)
    a = a.reshape(2, GLA_HEADS, nb, group, l, GLA_CHUNK)
    gd = gd.reshape(2, GLA_HEADS, nb, group, tpb, tg // GLA_CHUNK, GLA_DK)
    v = v.reshape(nb, group, l, GLA_V)

    def specs(rev):
        def tile(s):
            return jnp.where(s == 0, n_lat, n_lat - s if rev else s - 1)

        ins = [pl.BlockSpec((None, None, group, tg, GLA_QK), lambda bi, s: (rev, bi, 0, tile(s), 0)),
               pl.BlockSpec((None, None, group, tg, GLA_QK), lambda bi, s: (rev, bi, 0, tile(s), 0)),
               pl.BlockSpec((None, GLA_HEADS, None, group, tg, GLA_CHUNK),
                            lambda bi, s: (rev, 0, bi, 0, tile(s), 0)),
               pl.BlockSpec((None, GLA_HEADS, None, group, 1, tg // GLA_CHUNK, GLA_DK),
                            lambda bi, s: (rev, 0, bi, 0, tile(s), 0, 0)),
               pl.BlockSpec((None, group, tg, GLA_V), lambda bi, s: (bi, 0, tile(s), 0))]
        return ins, pl.BlockSpec((None, group, tg, GLA_V), lambda bi, s: (bi, 0, tile(s), 0))

    in_f, out_f = specs(0)
    in_b, out_b = specs(1)
    o_f, o_b = pl.pallas_call(
        _gla_scan_kernel,
        grid=(nb, tpb),
        in_specs=in_f + in_b,
        out_specs=[out_f, out_b],
        out_shape=[jax.ShapeDtypeStruct((nb, group, l, GLA_V), F32)] * 2,
        scratch_shapes=[pltpu.VMEM((2, group, GLA_HEADS, GLA_DV, GLA_DK), F32)],
        compiler_params=_cparams(("arbitrary", "arbitrary")),
        name="gla_scan",
    )(qe, kd, a, gd, v, qe, kd, a, gd, v)
    return o_f.reshape(n, GLA_V), o_b.reshape(n, GLA_V)


def _pack_bf16(x):
    half = x.shape[1] // 2
    lo = lax.bitcast_convert_type(x[:, :half].astype(BF16).astype(F32), jnp.uint32)
    hi = lax.bitcast_convert_type(x[:, half:].astype(BF16).astype(F32), jnp.uint32)
    return (lo >> 16) | (hi & jnp.uint32(0xFFFF0000))


def _unpack_bf16(p):
    lo = lax.bitcast_convert_type(p << 16, F32)
    hi = lax.bitcast_convert_type(p & jnp.uint32(0xFFFF0000), F32)
    return lo, hi


def _residual_ln(x, a, t, ln, alpha):
    y = alpha * x + t[0:1] * a
    mu = jnp.mean(y, axis=-1, keepdims=True)
    yc = y - mu
    var = jnp.mean(yc * yc, axis=-1, keepdims=True)
    xn = yc * lax.rsqrt(var + LN_EPS) * ln[0:1] + ln[1:2]
    return xn, xn * (1.0 + t[1:2]) + t[2:3]


def _top4_softmax(lt):
    e = lt.shape[0]
    io = lax.broadcasted_iota(jnp.int32, lt.shape, 0)
    work = lt
    idxs, vals = [], []
    for _ in range(TOP_K):
        m = jnp.max(work, axis=0, keepdims=True)
        ik = jnp.min(jnp.where(work == m, io, e), axis=0, keepdims=True)
        idxs.append(ik)
        vals.append(m)
        work = jnp.where(io == ik, -jnp.inf, work)
    ex = [jnp.exp(v - vals[0]) for v in vals]
    den = ex[0] + ex[1] + ex[2] + ex[3]
    return jnp.concatenate(idxs, axis=0), jnp.concatenate([x / den for x in ex], axis=0)


def _gla_gated_norm(o, r, gn):
    gate = r * jax.nn.sigmoid(r)
    heads = []
    for hd in range(GLA_HEADS):
        cols = slice(hd * GLA_DV, (hd + 1) * GLA_DV)
        oh = o[:, cols]
        ms = jnp.mean(oh * oh, axis=-1, keepdims=True)
        heads.append((oh * lax.rsqrt(ms + RMS_EPS) * gn * gate[:, cols]).astype(BF16))
    return jnp.concatenate(heads, axis=1)


def _post_kernel(gla, n_act, tpb, alpha, *refs):
    if gla:
        of_ref, ob_ref, r_ref, gn_ref, w_ref = refs[:5]
        rest = refs[5:]
        act = _gla_gated_norm(of_ref[...] + ob_ref[...], r_ref[...], gn_ref[...])
        a = jnp.dot(act, w_ref[...], preferred_element_type=F32)
    else:
        acts = refs[:n_act]
        ws = refs[n_act:2 * n_act]
        rest = refs[2 * n_act:]
        a = jnp.dot(acts[0][...], ws[0][...], preferred_element_type=F32)
        for k in range(1, n_act):
            a = a + jnp.dot(acts[k][...], ws[k][...], preferred_element_type=F32)
    x_ref, tab_ref, ln_ref, rw_ref, rb_ref, xo_ref, h_ref, idx_ref, gate_ref = rest
    tm = x_ref.shape[0]
    ln = ln_ref[...]
    for s in range(tm // TOK_TILE):
        rows = slice(s * TOK_TILE, (s + 1) * TOK_TILE)
        t = tab_ref[_tab_row(pl.program_id(0) * (tm // TOK_TILE) + s, tpb)]
        xn, h = _residual_ln(x_ref[rows], a[rows], t, ln, alpha)
        xo_ref[rows] = xn
        h_ref[rows] = _pack_bf16(h)
        lt = _dot_bf16x3(rw_ref[...], h, ((1,), (1,))) + rb_ref[...]
        idx, gates = _top4_softmax(lt)
        idx_ref[:, rows] = idx
        gate_ref[:, rows] = gates


def _post(acts, ws, x, tab, ln, rw_t, rb, tpb, alpha, gla=False):
    n, d = x.shape
    tm = _row_tile(n)
    e = rw_t.shape[0]
    row = lambda i: (i, 0)
    fixed = lambda i: (0, 0)
    return pl.pallas_call(
        functools.partial(_post_kernel, gla, len(acts), tpb, alpha),
        grid=(n // tm,),
        in_specs=([pl.BlockSpec((tm, a.shape[1]), row) for a in acts]
                  + [pl.BlockSpec(w.shape, fixed) for w in ws]
                  + [pl.BlockSpec((tm, d), row),
                     pl.BlockSpec(tab.shape, lambda i: (0, 0, 0)),
                     pl.BlockSpec(ln.shape, fixed),
                     pl.BlockSpec((e, d), fixed),
                     pl.BlockSpec((e, 1), fixed)]),
        out_specs=[pl.BlockSpec((tm, d), row), pl.BlockSpec((tm, d // 2), row),
                   pl.BlockSpec((TOP_K, tm), lambda i: (0, i)), pl.BlockSpec((TOP_K, tm), lambda i: (0, i))],
        out_shape=[jax.ShapeDtypeStruct((n, d), F32), jax.ShapeDtypeStruct((n, d // 2), jnp.uint32),
                   jax.ShapeDtypeStruct((TOP_K, n), jnp.int32), jax.ShapeDtypeStruct((TOP_K, n), F32)],
        compiler_params=_cparams(("arbitrary",)),
        name="post",
    )(*acts, *ws, x, tab, ln, rw_t, rb)


def _rank_kernel(idx_ref, tri_ref, rank_ref, cnt_ref, carry_ref):
    @pl.when(pl.program_id(0) == 0)
    def _():
        carry_ref[...] = jnp.zeros_like(carry_ref)

    idx = idx_ref[...]
    e = carry_ref.shape[0]
    tr = idx.shape[1]
    io = lax.broadcasted_iota(jnp.int32, (e, tr), 0)
    chosen = jnp.zeros((e, tr), F32)
    for k in range(TOP_K):
        chosen = chosen + (idx[k:k + 1] == io).astype(F32)
    cum = jnp.dot(chosen.astype(BF16), tri_ref[...], preferred_element_type=F32)
    base = carry_ref[:, 0:1]
    excl = base + cum - chosen
    ranks = [jnp.sum(jnp.where(idx[k:k + 1] == io, excl, 0.0), axis=0, keepdims=True) for k in range(TOP_K)]
    rank_ref[...] = jnp.concatenate(ranks, axis=0).astype(jnp.int32)
    carry_ref[...] = carry_ref[...] + jnp.sum(chosen, axis=1, keepdims=True)
    cnt_ref[...] = carry_ref[...]


def _rank(idx_t, n_experts):
    n = idx_t.shape[1]
    tr = _row_tile(n)
    tri = (np.arange(tr)[:, None] <= np.arange(tr)[None, :]).astype(np.float32)
    return pl.pallas_call(
        _rank_kernel,
        grid=(n // tr,),
        in_specs=[pl.BlockSpec((TOP_K, tr), lambda i: (0, i)),
                  pl.BlockSpec((tr, tr), lambda i: (0, 0))],
        out_specs=[pl.BlockSpec((TOP_K, tr), lambda i: (0, i)),
                   pl.BlockSpec((n_experts, LANES), lambda i: (0, 0))],
        out_shape=[jax.ShapeDtypeStruct((TOP_K, n), jnp.int32),
                   jax.ShapeDtypeStruct((n_experts, LANES), F32)],
        scratch_shapes=[pltpu.VMEM((n_experts, LANES), F32)],
        compiler_params=_cparams(("arbitrary",)),
        name="rank",
    )(idx_t, jnp.asarray(tri, BF16))


def _expert_kernel(be_ref, rows_ref, x_ref, w1_ref, b1_ref, w2_ref, b2_ref, o_ref, w1b_ref, w2b_ref):
    i = pl.program_id(0)
    used = rows_ref[i] > 0
    new_expert = jnp.logical_or(i == 0, be_ref[i] != be_ref[jnp.maximum(i - 1, 0)])

    @pl.when(jnp.logical_and(used, new_expert))
    def _():
        w1b_ref[...] = w1_ref[0].astype(BF16)
        w2b_ref[...] = w2_ref[0].astype(BF16)

    @pl.when(used)
    def _():
        x_lo, x_hi = _unpack_bf16(x_ref[...])
        kh = x_lo.shape[1]
        hid = (jnp.dot(x_lo.astype(BF16), w1b_ref[:kh], preferred_element_type=F32)
               + jnp.dot(x_hi.astype(BF16), w1b_ref[kh:], preferred_element_type=F32) + b1_ref[0])
        half = hid.shape[1] // 2
        glu = jnp.minimum(hid[:, :half], SWIGLU_LIMIT)
        lin = jnp.clip(hid[:, half:], -SWIGLU_LIMIT, SWIGLU_LIMIT)
        act = glu * jax.nn.sigmoid(SWIGLU_ALPHA * glu) * (lin + 1.0)
        y = jnp.dot(act.astype(BF16), w2b_ref[...], preferred_element_type=F32) + b2_ref[0]
        o_ref[...] = _pack_bf16(y)

    @pl.when(jnp.logical_not(used))
    def _():
        o_ref[...] = jnp.zeros_like(o_ref)


def _experts(block_expert, block_rows, x_pad, layer, w1, b1, w2, b2):
    n_pad, dp = x_pad.shape
    depth, e, d, dh2 = w1.shape
    n_blocks = n_pad // EXPERT_BLOCK
    grid_spec = pltpu.PrefetchScalarGridSpec(
        num_scalar_prefetch=2,
        grid=(n_blocks,),
        in_specs=[pl.BlockSpec((EXPERT_BLOCK, dp), lambda i, be, nb: (i, 0)),
                  pl.BlockSpec((None, 1, d, dh2), lambda i, be, nb: (layer, be[i], 0, 0)),
                  pl.BlockSpec((None, 1, 1, dh2), lambda i, be, nb: (layer, be[i], 0, 0)),
                  pl.BlockSpec((None, 1, dh2 // 2, d), lambda i, be, nb: (layer, be[i], 0, 0)),
                  pl.BlockSpec((None, 1, 1, d), lambda i, be, nb: (layer, be[i], 0, 0))],
        out_specs=pl.BlockSpec((EXPERT_BLOCK, dp), lambda i, be, nb: (i, 0)),
        scratch_shapes=[pltpu.VMEM((d, dh2), BF16), pltpu.VMEM((dh2 // 2, d), BF16)],
    )
    return pl.pallas_call(
        _expert_kernel,
        grid_spec=grid_spec,
        out_shape=jax.ShapeDtypeStruct((n_pad, dp), jnp.uint32),
        compiler_params=pltpu.CompilerParams(dimension_semantics=("arbitrary",),
                                             vmem_limit_bytes=EXPERT_VMEM_LIMIT),
        name="experts",
    )(block_expert, block_rows, x_pad, w1, b1.reshape(depth, e, 1, dh2), w2, b2.reshape(depth, e, 1, d))


def _stream_tile(i, tpb, latent_only):
    return (i // (tpb - 1)) * tpb + i % (tpb - 1) if latent_only else i


def _combine_kernel(tpb, alpha, splits, y_ref, gate_ref, x_ref, tab_ref, ln_ref, *refs):
    latent_only = splits is None
    g = gate_ref[...]
    y_lo, y_hi = None, None
    for k in range(TOP_K):
        lo, hi = _unpack_bf16(y_ref[k])
        gk = g[:, k:k + 1]
        y_lo = lo * gk if y_lo is None else y_lo + lo * gk
        y_hi = hi * gk if y_hi is None else y_hi + hi * gk
    y = jnp.concatenate([y_lo, y_hi], axis=1)
    ln = ln_ref[...]
    n_sub = x_ref.shape[0] // TOK_TILE
    xns, hs = [], []
    for s in range(n_sub):
        rows = slice(s * TOK_TILE, (s + 1) * TOK_TILE)
        t = tab_ref[_tab_row(_stream_tile(pl.program_id(0) * n_sub + s, tpb, latent_only), tpb)]
        xn, h = _residual_ln(x_ref[rows], y[rows], t, ln, alpha)
        xns.append(xn)
        hs.append(h)
    xn = jnp.concatenate(xns, axis=0)
    if latent_only:
        refs[0][...] = xn
    else:
        w_ref, xo_ref = refs[:2]
        xo_ref[...] = xn
        _emit_proj(jnp.concatenate(hs, axis=0), w_ref, splits, refs[2:])


def _combine(y_g, gates, x, tab, ln, tpb, alpha, plan):
    n, d = x.shape
    latent_only = plan is None
    tm = TOK_TILE if latent_only else _row_tile(n)
    tiles = n // tm
    steps = tiles // tpb * (tpb - 1) if latent_only else tiles
    src = lambda i: (_stream_tile(i, tpb, latent_only), 0)
    in_specs = [pl.BlockSpec((TOP_K, tm, d // 2), lambda i: (0, _stream_tile(i, tpb, latent_only), 0)),
                pl.BlockSpec((tm, TOP_K), src),
                pl.BlockSpec((tm, d), src),
                pl.BlockSpec(tab.shape, lambda i: (0, 0, 0)),
                pl.BlockSpec(ln.shape, lambda i: (0, 0))]
    out_specs = [pl.BlockSpec((tm, d), lambda i: (i, 0))]
    out_shape = [jax.ShapeDtypeStruct((steps * tm, d), F32)]
    operands = [y_g, gates, x, tab, ln]
    if not latent_only:
        w_spec, p_specs, p_shape = _proj_specs(n, tm, plan)
        in_specs.append(w_spec)
        out_specs += p_specs
        out_shape += p_shape
        operands.append(plan[0])
    return pl.pallas_call(
        functools.partial(_combine_kernel, tpb, alpha, None if latent_only else tuple(plan[1])),
        grid=(steps,),
        in_specs=in_specs,
        out_specs=out_specs,
        out_shape=out_shape,
        compiler_params=_cparams(("arbitrary",)),
        name="combine",
    )(*operands)


def _sc_mesh():
    return plsc.VectorSubcoreMesh(core_axis_name="c", subcore_axis_name="s")


def _sc_split(rows, mesh):
    workers = mesh.num_cores * mesh.num_subcores
    per = rows // workers
    assert per * workers == rows
    chunk = SC_CHUNK if per % SC_CHUNK == 0 else 8
    assert per % chunk == 0
    return per, chunk


def _sc_scatter_rows(x, idx, n_out):
    r, c = x.shape
    mesh = _sc_mesh()
    per, chunk = _sc_split(r, mesh)

    n_chunks = per // chunk

    @functools.partial(pl.kernel, out_type=jax.ShapeDtypeStruct((n_out, c), x.dtype), mesh=mesh,
                       scratch_types=[pltpu.VMEM((chunk, c), x.dtype), pltpu.SemaphoreType.DMA] * 2
                       + [pltpu.VMEM((chunk,), jnp.int32)] * TOP_K + [pltpu.SemaphoreType.DMA])
    def scatter(x_hbm, i_hbm, o_hbm, rows_a, sem_a, rows_b, sem_b, *rest):
        idx_vs, sem_s = rest[:TOP_K], rest[TOP_K]
        base = (lax.axis_index("s") * mesh.num_cores + lax.axis_index("c")) * per
        slots = ((rows_a, sem_a), (rows_b, sem_b))

        def load(j, slot):
            rows_v, sem = slot
            off = pl.multiple_of(base + j * chunk, chunk)
            return pltpu.make_async_copy(x_hbm.at[pl.ds(off, chunk)], rows_v, sem)

        def scatter_chunk(j, slot, prefetch):
            rows_v, _ = slot
            off = pl.multiple_of(base + j * chunk, chunk)
            load(j, slot).wait()
            for k in range(TOP_K):
                pltpu.sync_copy(i_hbm.at[pl.ds(k * r + off, chunk)], idx_vs[k])
            for k in range(TOP_K):
                pltpu.async_copy(rows_v, o_hbm.at[idx_vs[k]], sem_s)
            prefetch()
            for k in range(TOP_K):
                pltpu.make_async_copy(rows_v, o_hbm.at[idx_vs[k]], sem_s).wait()

        load(0, slots[0]).start()

        @pl.loop(0, n_chunks // 2)
        def _(p):
            j = 2 * p
            scatter_chunk(j, slots[0], lambda: load(j + 1, slots[1]).start())

            def next_even():
                @pl.when(j + 2 < n_chunks)
                def _():
                    load(j + 2, slots[0]).start()

            scatter_chunk(j + 1, slots[1], next_even)

        if n_chunks % 2:
            scatter_chunk(n_chunks - 1, slots[0], lambda: None)

    return scatter(x, idx)


def _sc_gather_rows(table, idx):
    m = idx.shape[0]
    c = table.shape[1]
    mesh = _sc_mesh()
    per, chunk = _sc_split(m, mesh)

    n_chunks = per // chunk
    slot_types = [pltpu.VMEM((chunk,), jnp.int32), pltpu.VMEM((chunk, c), table.dtype), pltpu.SemaphoreType.DMA]

    @functools.partial(pl.kernel, out_type=jax.ShapeDtypeStruct((m, c), table.dtype), mesh=mesh,
                       scratch_types=slot_types * 2)
    def gather(t_hbm, i_hbm, o_hbm, idx_a, rows_a, sem_a, idx_b, rows_b, sem_b):
        base = (lax.axis_index("s") * mesh.num_cores + lax.axis_index("c")) * per
        slots = ((idx_a, rows_a, sem_a), (idx_b, rows_b, sem_b))

        def start(j, slot):
            idx_v, rows_v, sem = slot
            off = pl.multiple_of(base + j * chunk, chunk)
            pltpu.sync_copy(i_hbm.at[pl.ds(off, chunk)], idx_v)
            pltpu.async_copy(t_hbm.at[idx_v], rows_v, sem)

        def finish(j, slot):
            idx_v, rows_v, sem = slot
            off = pl.multiple_of(base + j * chunk, chunk)
            pltpu.make_async_copy(t_hbm.at[idx_v], rows_v, sem).wait()
            pltpu.sync_copy(rows_v, o_hbm.at[pl.ds(off, chunk)])

        start(0, slots[0])

        @pl.loop(0, n_chunks // 2)
        def _(p):
            j = 2 * p
            start(j + 1, slots[1])
            finish(j, slots[0])

            @pl.when(j + 2 < n_chunks)
            def _():
                start(j + 2, slots[0])

            finish(j + 1, slots[1])

        if n_chunks % 2:
            finish(n_chunks - 1, slots[0])

    return gather(table, idx)


def _moe(h, idx_t, layer, w1, b1, w2, b2):
    n, dp = h.shape
    e = w1.shape[1]
    m = n * TOP_K
    rank_t, cnt = _rank(idx_t, e)
    sizes = cnt[:, 0].astype(jnp.int32)
    padded = (sizes + EXPERT_BLOCK - 1) // EXPERT_BLOCK * EXPERT_BLOCK
    pad_ends = jnp.cumsum(padded)
    pad_starts = pad_ends - padded
    ids = jnp.arange(e, dtype=jnp.int32)[:, None, None]
    dest_t = jnp.sum(jnp.where(idx_t[None] == ids, pad_starts[:, None, None], 0), axis=0) + rank_t
    dest = dest_t.reshape(-1)
    n_blocks = (m + e * (EXPERT_BLOCK - 1)) // EXPERT_BLOCK + 1
    n_pad = n_blocks * EXPERT_BLOCK
    block_start = jnp.arange(n_blocks, dtype=jnp.int32) * EXPERT_BLOCK
    block_expert = jnp.minimum(jnp.sum((pad_ends[None, :] <= block_start[:, None]).astype(jnp.int32), axis=1), e - 1)
    group_end = jnp.sum(jnp.where(block_expert[:, None] == jnp.arange(e, dtype=jnp.int32)[None, :],
                                  (pad_starts + sizes)[None, :], 0), axis=1)
    block_rows = jnp.clip(group_end - block_start, 0, EXPERT_BLOCK).astype(jnp.int32)
    x_pad = _sc_scatter_rows(h, dest, n_pad)
    y_pad = _experts(block_expert, block_rows, x_pad, layer, w1, b1, w2, b2)
    return _sc_gather_rows(y_pad, dest).reshape(TOP_K, n, dp)


def _rope_tables(t_lat, n_ctx):
    t = jnp.arange(t_lat)
    row = (t // GRID_W).astype(F32)
    col = (t % GRID_W).astype(F32)
    nf = GLA_DK // 4
    freqs = ROPE_BASE ** (-jnp.arange(nf, dtype=F32) / nf)
    ang = jnp.concatenate([row[:, None] * freqs, col[:, None] * freqs], axis=-1)
    cos, sin = jnp.cos(ang), jnp.sin(ang)
    cos2 = jnp.concatenate([cos, cos], axis=-1)
    sin2 = jnp.concatenate([-sin, sin], axis=-1)
    return (jnp.concatenate([cos2, jnp.ones((n_ctx, GLA_DK), F32)], axis=0),
            jnp.concatenate([sin2, jnp.zeros((n_ctx, GLA_DK), F32)], axis=0))


def _chunk_tri(tg, rev):
    t = np.arange(tg)
    same = (t[:, None] // GLA_CHUNK) == (t[None, :] // GLA_CHUNK)
    side = (t[None, :] >= t[:, None]) if rev else (t[None, :] <= t[:, None])
    return jnp.asarray((same & side).astype(np.float32), BF16)


def _table(mods, rows, batch):
    lat = jnp.stack([mods[:batch, r] for r in rows], axis=1)
    ctx = jnp.broadcast_to(jnp.stack([mods[batch, r] for r in rows], axis=0)[None], lat.shape)
    tab = jnp.stack([lat, ctx], axis=1).reshape(2 * batch, len(rows), -1)
    return jnp.pad(tab, ((0, 0), (0, 8 - len(rows)), (0, 0)))


@jax.jit
def _forward(x, c, ctx, c_ctx, ada_w, ada_b, ln_g, ln_b, ab_w_in, ab_pool_w, ab_pool_scale, ab_rpb,
             ab_w_out, gla_w_in, gla_w_gate, gla_b_gate, gla_norm_g, gla_w_out, router_w, router_b,
             exp_w1, exp_b1, exp_w2, exp_b2):
    batch, t_lat, d = x.shape
    n_ctx = ctx.shape[1]
    depth = ada_w.shape[0]
    assert d == D_MODEL and n_ctx == TOK_TILE and t_lat % TOK_TILE == 0
    rows = t_lat // GRID_W
    assert rows % NA_QROWS == 0 and rows >= NA_KROWS + NA_QROWS
    n_lat = t_lat // TOK_TILE
    tpb = n_lat + 1
    l = t_lat + n_ctx
    n = batch * l
    alpha = (2.0 * depth) ** 0.25

    cc = jnp.concatenate([c, c_ctx[None], jnp.zeros((16 - batch - 1, d), F32)], axis=0)
    mods = _mods(cc, ada_w, ada_b).reshape(depth, 16, N_MOD, d)

    def in_proj_plan(i):
        if i % 2 == 0:
            return (ab_w_in[i // 2].astype(BF16),
                    [(0, POOL_WIDTH), (POOL_WIDTH, POOL_WIDTH + 3 * NA_WIDTH)], [F32, BF16])
        edges = (0, 2 * GLA_QK, 2 * GLA_QK + GLA_V, 2 * GLA_QK + 2 * GLA_V, 2 * GLA_QK + 2 * GLA_V + 2 * GATE_RANK)
        return gla_w_in[i // 2].astype(BF16), list(zip(edges[:-1], edges[1:])), [F32, BF16, F32, F32]

    z, *projected = _modulate(x.reshape(batch * t_lat, d), ctx.reshape(batch * n_ctx, d),
                              _table(mods[0], (1, 0), batch), tpb, in_proj_plan(0))
    cos2, sin2 = _rope_tables(t_lat, n_ctx)

    for i in range(depth):
        j = i // 2
        last = i == depth - 1
        tab1 = _table(mods[i], (2, 4, 3), batch)
        ln1 = jnp.stack([ln_g[i, 0], ln_b[i, 0]])
        ln2 = jnp.stack([ln_g[i, 1], ln_b[i, 1]])
        rw_t = router_w[i].T
        rb = router_b[i][:, None]
        if i % 2 == 0:
            u, qkv = projected
            w_blk = jax.scipy.linalg.block_diag(*[ab_pool_w[j, g] for g in range(len(POOL_WINDOWS))])
            bias = _na_bias_tables(ab_rpb[j], rows)
            attn, pooled = _na_pool(qkv.reshape(batch, l, 3 * NA_WIDTH), bias, u.reshape(batch, l, POOL_WIDTH),
                                    w_blk.astype(BF16), ab_pool_scale[j][None, :], n_lat, t_lat, n_ctx)
            w_out = ab_w_out[j].astype(BF16)
            acts = [pooled.reshape(n, POOL_WIDTH), attn.reshape(n, NA_WIDTH)]
            ws = [w_out[:POOL_WIDTH], w_out[POOL_WIDTH:]]
        else:
            qk, v, r, g = projected
            tri = jnp.stack([_chunk_tri(TOK_TILE, False), _chunk_tri(TOK_TILE, True)])
            prep = _gla_prep(qk, g, cos2, sin2, gla_w_gate[j], gla_b_gate[j][:, None, :], tri, tpb)
            o_f, o_b = _gla_scan(prep, v, batch, tpb)
            acts = [o_f, o_b, r]
            ws = [gla_norm_g[j][None, :], gla_w_out[j].astype(BF16)]
        z, h, idx_t, gates_t = _post(acts, ws, z, tab1, ln1, rw_t, rb, tpb, alpha, gla=i % 2 == 1)
        y_g = _moe(h, idx_t, i, exp_w1, exp_b1, exp_w2, exp_b2)
        nxt = mods[i + 1] if not last else mods[i]
        tab2 = _table(jnp.concatenate([mods[i][:, 5:6], nxt[:, 1:2], nxt[:, 0:1]], axis=1), (0, 1, 2), batch)
        if last:
            (out,) = _combine(y_g, gates_t.T, z, tab2, ln2, tpb, alpha, None)
            return out.reshape(batch, t_lat, d)
        z, *projected = _combine(y_g, gates_t.T, z, tab2, ln2, tpb, alpha, in_proj_plan(i + 1))


def kernel(x, c, ctx, c_ctx, ada_w, ada_b, ln_g, ln_b, ab_w_in, ab_pool_w, ab_pool_scale, ab_rpb, ab_w_out,
           gla_w_in, gla_w_gate, gla_b_gate, gla_norm_g, gla_w_out, router_w, router_b, exp_w1, exp_b1, exp_w2,
           exp_b2):
    return _forward(x, c, ctx, c_ctx, ada_w, ada_b, ln_g, ln_b, ab_w_in, ab_pool_w, ab_pool_scale, ab_rpb,
                    ab_w_out, gla_w_in, gla_w_gate, gla_b_gate, gla_norm_g, gla_w_out, router_w, router_b,
                    exp_w1, exp_b1, exp_w2, exp_b2)
```
